```python
import math
import jax, jax.numpy as jnp
from jax import lax
import numpy as np

D_MODEL = 1024
BATCH = 8
SEQ = 2048
DEPTH = 4
DEC_BATCH = 128
DEC_SEQ = 8
PAST_LEN = 16384
PAGE_SIZE = 128

N_MIXERS = 2
N_HGRN = (DEPTH + 1) // 2
N_LRU = DEPTH // 2
N_DENSE = (DEPTH + 1) // 2
N_MOE = DEPTH // 2
HG_DK = 128
HG_HEADS = D_MODEL // HG_DK
HG_DV = D_MODEL // HG_HEADS
HG_CHUNK = 64
D_RNN = D_MODEL
LRU_BLOCKS = 4
LRU_BW = D_RNN // LRU_BLOCKS
CONV_W = 4
RG_C = 8.0
D_FF = 2816
N_EXPERTS = 8
TOP_K = 2
D_FF_EXPERT = 3584
MOE_BLOCK = 128
ALPHA = (2 * DEPTH) ** 0.25
BETA = (8 * DEPTH) ** -0.25
LN_EPS = 1e-5
F_FLOOR = 1e-20
F32 = jnp.float32

kernel_name = 'hgrn2_rglru_hybrid_step'


def layer_norm(x, g, b):
    xf = x.astype(F32)
    mu = jnp.mean(xf, -1, keepdims=True)
    xc = xf - mu
    var = jnp.mean(xc * xc, -1, keepdims=True)
    return (xc * lax.rsqrt(var + LN_EPS) * g.astype(F32) + b.astype(F32)).astype(x.dtype)


def hgrn2_lower_bounds(lb_logits):
    p = jax.nn.softmax(lb_logits.astype(F32), axis=0)
    return jnp.cumsum(p, axis=0) - p[0]


def hgrn2_recurrence(q, k, v, logf, s0, chunk):
    B, T, H, _ = q.shape
    n = T // chunk

    def to_chunks(a):
        return a.reshape(B, n, chunk, H, a.shape[-1]).transpose(1, 0, 3, 2, 4)

    causal = jnp.tril(jnp.ones((chunk, chunk), dtype=bool))[:, :, None]
    causal_f = causal.astype(F32)

    def step(s, inp):
        qc, kc, vc, gc = inp
        G = jnp.cumsum(gc, axis=2)
        diff = G[:, :, :, None, :] - G[:, :, None, :, :]
        decay = jnp.exp(jnp.where(causal, diff, 0.0)) * causal_f
        scores = jnp.einsum('bhtk,bhtsk,bhsk->bhts', qc, decay, kc)
        o = (jnp.einsum('bhts,bhsv->bhtv', scores, vc)
             + jnp.einsum('bhtk,bhkv->bhtv', qc * jnp.exp(G), s))
        g_end = G[:, :, -1:, :]
        s_new = (jnp.exp(g_end[:, :, 0, :])[..., None] * s
                 + jnp.einsum('bhsk,bhsv->bhkv', kc * jnp.exp(g_end - G), vc))
        return s_new, o

    s_fin, o = lax.scan(step, s0, (to_chunks(q), to_chunks(k), to_chunks(v), to_chunks(logf)))
    return o.transpose(1, 0, 3, 2, 4).reshape(B, T, H, -1), s_fin


def hgrn2_mixer(x, s0, lb, w_in, norm_g, w_out):
    B, T, _ = x.shape
    hk, hv = HG_HEADS * HG_DK, HG_HEADS * HG_DV
    proj = jnp.einsum('btd,de->bte', x, w_in).astype(F32)
    q, f_pre, v, g = jnp.split(proj, [hk, 2 * hk, 2 * hk + hv], axis=-1)
    q = jax.nn.silu(q) * (HG_DK ** -0.5)
    f = lb + (1.0 - lb) * jax.nn.sigmoid(f_pre)
    logf = jnp.log(jnp.maximum(f, F_FLOOR))
    k = (1.0 - lb) * jax.nn.sigmoid(-f_pre)
    o, s_fin = hgrn2_recurrence(q.reshape(B, T, HG_HEADS, HG_DK), k.reshape(B, T, HG_HEADS, HG_DK),
                                v.reshape(B, T, HG_HEADS, HG_DV), logf.reshape(B, T, HG_HEADS, HG_DK),
                                s0.astype(F32), math.gcd(T, HG_CHUNK))
    o = o * lax.rsqrt(jnp.mean(o * o, -1, keepdims=True) + LN_EPS) * norm_g.astype(F32).reshape(HG_HEADS, HG_DV)
    o = o.reshape(B, T, hv) * jax.nn.silu(g)
    y = jnp.einsum('bte,ed->btd', o.astype(x.dtype), w_out)
    return y.astype(x.dtype), s_fin


def lru_combine(left, right):
    return (left[0] * right[0], right[0] * left[1] + right[1])


def rglru_mixer(x, h0, conv0, w_in, conv_w, conv_b, w_r, b_r, w_i, b_i, lam, w_out):
    B, T, _ = x.shape
    proj = jnp.einsum('btd,de->bte', x, w_in).astype(F32)
    xb, yb = jnp.split(proj, 2, axis=-1)
    gate = jax.nn.gelu(yb, approximate=True)
    xp = jnp.concatenate([conv0.astype(F32), xb], axis=1)
    cw = conv_w.astype(F32)
    xc = conv_b.astype(F32) + sum(xp[:, j:j + T] * cw[j] for j in range(CONV_W))
    new_conv = xp[:, T:]
    xh = xc.reshape(B, T, LRU_BLOCKS, LRU_BW)
    r = jax.nn.sigmoid(jnp.einsum('btnc,ncd->btnd', xh, w_r.astype(F32))
                       + b_r.astype(F32).reshape(LRU_BLOCKS, LRU_BW)).reshape(B, T, D_RNN)
    i = jax.nn.sigmoid(jnp.einsum('btnc,ncd->btnd', xh, w_i.astype(F32))
                       + b_i.astype(F32).reshape(LRU_BLOCKS, LRU_BW)).reshape(B, T, D_RNN)
    log_a = -RG_C * r * jax.nn.softplus(-lam.astype(F32))
    a = jnp.exp(log_a)
    u = jnp.sqrt(-jnp.expm1(2.0 * log_a)) * (i * xc)
    u = u.at[:, 0].add(a[:, 0] * h0.astype(F32))
    _, h = lax.associative_scan(lru_combine, (a, u), axis=1)
    y = jnp.einsum('bte,ed->btd', (h * gate).astype(x.dtype), w_out)
    return y.astype(x.dtype), h[:, -1], new_conv


def swiglu_ffn(x, w_gu, w_down):
    g, u = jnp.split(jnp.einsum('btd,df->btf', x, w_gu), 2, axis=-1)
    return jnp.einsum('btf,fd->btd', jax.nn.silu(g) * u, w_down).astype(x.dtype)


def moe_ffn(x, w_router, w_gu, w_down):
    B, T, D = x.shape
    xt = x.reshape(-1, D)
    N = xt.shape[0]
    probs = jax.nn.softmax((xt @ w_router).astype(F32), axis=-1)
    top_p, top_e = lax.top_k(probs, TOP_K)
    top_p = top_p / jnp.sum(top_p, -1, keepdims=True)
    nk = N * TOP_K
    flat_e = top_e.reshape(-1)
    order = jnp.argsort(flat_e)
    sorted_e = flat_e[order]
    token_of = order // TOP_K
    counts = jnp.bincount(flat_e, length=N_EXPERTS)
    padded = (counts + MOE_BLOCK - 1) // MOE_BLOCK * MOE_BLOCK
    pad_end = jnp.cumsum(padded)
    pad_start = pad_end - padded
    grp_start = jnp.cumsum(counts) - counts
    dest = pad_start[sorted_e] + jnp.arange(nk) - grp_start[sorted_e]
    n_blocks = -(-nk // MOE_BLOCK) + N_EXPERTS
    n_rows = n_blocks * MOE_BLOCK
    xs = jnp.zeros((n_rows, D), xt.dtype).at[dest].set(xt[token_of])
    block_e = jnp.minimum(jnp.searchsorted(pad_end, jnp.arange(n_blocks) * MOE_BLOCK, side='right'),
                          N_EXPERTS - 1)

    def expert_block(args):
        xblk, e = args
        g, u = jnp.split(xblk @ w_gu[e], 2, axis=-1)
        return (jax.nn.silu(g) * u) @ w_down[e]

    ys = lax.map(expert_block, (xs.reshape(n_blocks, MOE_BLOCK, D), block_e)).reshape(n_rows, D)
    contrib = ys[dest] * top_p.reshape(-1)[order][:, None].astype(ys.dtype)
    y = jnp.zeros((N, D), ys.dtype).at[token_of].add(contrib)
    return y.reshape(B, T, D).astype(x.dtype)


def trunk(x, s_hgrn, s_lru_h, s_lru_conv, lb, params):
    (ln_mix_g, ln_mix_b, ln_ffn_g, ln_ffn_b,
     w_hgrn_in, hgrn_norm_g, w_hgrn_out,
     w_lru_in, lru_conv_w, lru_conv_b, w_lru_rgate, b_lru_rgate, w_lru_igate, b_lru_igate,
     lru_lambda, w_lru_out,
     w_ffn_gu, w_ffn_down, w_router, w_moe_gu, w_moe_down) = params
    new_hg, new_h, new_c = [], [], []
    for layer in range(DEPTH):
        j = layer // N_MIXERS
        if layer % N_MIXERS == 0:
            mix, s = hgrn2_mixer(x, s_hgrn[j], lb[j], w_hgrn_in[j], hgrn_norm_g[j], w_hgrn_out[j])
            new_hg.append(s)
        else:
            mix, h, c = rglru_mixer(x, s_lru_h[j], s_lru_conv[j], w_lru_in[j], lru_conv_w[j], lru_conv_b[j],
                                    w_lru_rgate[j], b_lru_rgate[j], w_lru_igate[j], b_lru_igate[j],
                                    lru_lambda[j], w_lru_out[j])
            new_h.append(h)
            new_c.append(c)
        x = layer_norm(ALPHA * x + mix, ln_mix_g[layer], ln_mix_b[layer])
        if layer % 2 == 0:
            ff = swiglu_ffn(x, w_ffn_gu[layer // 2], w_ffn_down[layer // 2])
        else:
            ff = moe_ffn(x, w_router[layer // 2], w_moe_gu[layer // 2], w_moe_down[layer // 2])
        x = layer_norm(ALPHA * x + ff, ln_ffn_g[layer], ln_ffn_b[layer])
    return x, jnp.stack(new_hg), jnp.stack(new_h), jnp.stack(new_c)


def setup_inputs(seed: int = 0) -> dict:
    key = jax.random.key(seed)
    ks = iter(jax.random.split(key, 40))

    def nrm(shape, scale):
        return jax.random.normal(next(ks), shape, F32) * scale

    hk, hv = HG_HEADS * HG_DK, HG_HEADS * HG_DV
    a_c = jax.random.uniform(next(ks), (N_LRU, D_RNN), F32, 0.9, 0.999)
    s_l = a_c ** (1.0 / RG_C)
    return {
        'x_prompt': nrm((BATCH, SEQ, D_MODEL), 1.0),
        'x_sample': nrm((DEC_BATCH, DEC_SEQ, D_MODEL), 1.0),
        'state_hgrn': nrm((N_HGRN, DEC_BATCH, HG_HEADS, HG_DK, HG_DV), 0.5),
        'state_lru_h': nrm((N_LRU, DEC_BATCH, D_RNN), 0.5),
        'state_lru_conv': nrm((N_LRU, DEC_BATCH, CONV_W - 1, D_RNN), 1.0),
        'ln_mix_g': 1.0 + nrm((DEPTH, D_MODEL), 0.01),
        'ln_mix_b': nrm((DEPTH, D_MODEL), 0.01),
        'ln_ffn_g': 1.0 + nrm((DEPTH, D_MODEL), 0.01),
        'ln_ffn_b': nrm((DEPTH, D_MODEL), 0.01),
        'w_hgrn_in': nrm((N_HGRN, D_MODEL, 2 * hk + 2 * hv), D_MODEL ** -0.5),
        'hgrn_lb_logits': nrm((N_HGRN, hk), 0.5),
        'hgrn_norm_g': 1.0 + nrm((N_HGRN, hv), 0.01),
        'w_hgrn_out': nrm((N_HGRN, hv, D_MODEL), hv ** -0.5 * BETA),
        'w_lru_in': nrm((N_LRU, D_MODEL, 2 * D_RNN), D_MODEL ** -0.5),
        'lru_conv_w': nrm((N_LRU, CONV_W, D_RNN), CONV_W ** -0.5),
        'lru_conv_b': nrm((N_LRU, D_RNN), 0.01),
        'w_lru_rgate': nrm((N_LRU, LRU_BLOCKS, LRU_BW, LRU_BW), LRU_BW ** -0.5),
        'b_lru_rgate': nrm((N_LRU, D_RNN), 0.01),
        'w_lru_igate': nrm((N_LRU, LRU_BLOCKS, LRU_BW, LRU_BW), LRU_BW ** -0.5),
        'b_lru_igate': nrm((N_LRU, D_RNN), 0.01),
        'lru_lambda': jnp.log(s_l) - jnp.log1p(-s_l),
        'w_lru_out': nrm((N_LRU, D_RNN, D_MODEL), D_RNN ** -0.5 * BETA),
        'w_ffn_gu': nrm((N_DENSE, D_MODEL, 2 * D_FF), D_MODEL ** -0.5),
        'w_ffn_down': nrm((N_DENSE, D_FF, D_MODEL), D_FF ** -0.5 * BETA),
        'w_router': nrm((N_MOE, D_MODEL, N_EXPERTS), D_MODEL ** -0.5),
        'w_moe_gu': nrm((N_MOE, N_EXPERTS, D_MODEL, 2 * D_FF_EXPERT), D_MODEL ** -0.5),
        'w_moe_down': nrm((N_MOE, N_EXPERTS, D_FF_EXPERT, D_MODEL), D_FF_EXPERT ** -0.5 * BETA),
    }


def reference(x_prompt, x_sample, state_hgrn, state_lru_h, state_lru_conv,
              ln_mix_g, ln_mix_b, ln_ffn_g, ln_ffn_b,
              w_hgrn_in, hgrn_lb_logits, hgrn_norm_g, w_hgrn_out,
              w_lru_in, lru_conv_w, lru_conv_b, w_lru_rgate, b_lru_rgate, w_lru_igate, b_lru_igate,
              lru_lambda, w_lru_out,
              w_ffn_gu, w_ffn_down, w_router, w_moe_gu, w_moe_down):
    params = (ln_mix_g, ln_mix_b, ln_ffn_g, ln_ffn_b,
              w_hgrn_in, hgrn_norm_g, w_hgrn_out,
              w_lru_in, lru_conv_w, lru_conv_b, w_lru_rgate, b_lru_rgate, w_lru_igate, b_lru_igate,
              lru_lambda, w_lru_out,
              w_ffn_gu, w_ffn_down, w_router, w_moe_gu, w_moe_down)
    lb = hgrn2_lower_bounds(hgrn_lb_logits)
    B = x_prompt.shape[0]
    zero_hgrn = jnp.zeros((N_HGRN, B, HG_HEADS, HG_DK, HG_DV), F32)
    zero_h = jnp.zeros((N_LRU, B, D_RNN), F32)
    zero_conv = jnp.zeros((N_LRU, B, CONV_W - 1, D_RNN), F32)
    y_prompt, hg_p, h_p, c_p = trunk(x_prompt, zero_hgrn, zero_h, zero_conv, lb, params)
    y_sample, hg_s, h_s, c_s = trunk(x_sample, state_hgrn, state_lru_h, state_lru_conv, lb, params)
    return (y_prompt, y_sample, hg_p, hg_s, h_p, h_s, c_p, c_s)
```

```python
import functools
import math

import jax
import jax.numpy as jnp
from jax import lax
from jax.experimental import pallas as pl
from jax.experimental.pallas import tpu as pltpu

F32 = jnp.float32
BF16 = jnp.bfloat16
I32 = jnp.int32

D_MODEL = 1024
DEPTH = 4
HG_DK = 128
HG_HEADS = 8
HG_DV = 128
CONV_W = 4
RG_C = 8.0
LRU_BLOCKS = 4
LRU_BW = 256
D_FF = 2816
N_EXPERTS = 8
D_FF_EXPERT = 3584
ALPHA = (2 * DEPTH) ** 0.25
LN_EPS = 1e-5
F_FLOOR = 1e-20

LANES = 128
SUBLANES = 8
VMEM_LIMIT = 56 * 1024 * 1024

HG_CHUNK_PROMPT = 16
MOE_TILE = 512


def _cparams(sem):
    return pltpu.CompilerParams(dimension_semantics=sem, vmem_limit_bytes=VMEM_LIMIT)


def _sigmoid(x):
    return 1.0 / (1.0 + jnp.exp(-x))


def _layer_norm_rows(z, g, b):
    mu = jnp.mean(z, axis=-1, keepdims=True)
    zc = z - mu
    var = jnp.mean(zc * zc, axis=-1, keepdims=True)
    return zc * lax.rsqrt(var + LN_EPS) * g + b


def _hgrn_in_kernel(layer_j, x_ref, w_ref, lbl_ref, q_ref, k_ref, lf_ref, v_ref, g_ref):
    xb = x_ref[...].astype(BF16)
    l = lbl_ref[...]
    e = jnp.exp(l - jnp.max(l, axis=0, keepdims=True))
    p = e / jnp.sum(e, axis=0, keepdims=True)
    cs = p[0:1]
    for i in range(1, layer_j + 1):
        cs = cs + p[i:i + 1]
    lb = cs - p[0:1]

    def seg(i):
        return jnp.dot(xb, w_ref[:, i * D_MODEL:(i + 1) * D_MODEL], preferred_element_type=F32)

    q = seg(0)
    q_ref[...] = q * _sigmoid(q) * (HG_DK ** -0.5)
    fp = seg(1)
    f = lb + (1.0 - lb) * _sigmoid(fp)
    lf_ref[...] = jnp.log(jnp.maximum(f, F_FLOOR))
    k_ref[...] = (1.0 - lb) * _sigmoid(-fp)
    v_ref[...] = seg(2)
    g = seg(3)
    g_ref[...] = g * _sigmoid(g)


def hgrn_in(x, w_in_bf, lb_logits, layer_j, tm=256):
    n = x.shape[0]
    row = pl.BlockSpec((tm, D_MODEL), lambda i: (i, 0))
    out = jax.ShapeDtypeStruct((n, D_MODEL), F32)
    return pl.pallas_call(
        functools.partial(_hgrn_in_kernel, layer_j),
        grid=(n // tm,),
        in_specs=[row,
                  pl.BlockSpec((D_MODEL, 4 * D_MODEL), lambda i: (0, 0)),
                  pl.BlockSpec(lb_logits.shape, lambda i: (0, 0))],
        out_specs=[row] * 5,
        out_shape=[out] * 5,
        compiler_params=_cparams(("arbitrary",)),
        name="hgrn_in",
    )(x, w_in_bf, lb_logits)


def _hgrn_chunk_head(C, q, k, lf, v, S):
    row = lax.broadcasted_iota(I32, (C, LANES), 0)
    G = lf
    d = 1
    while d < C:
        G = G + jnp.where(row >= d, pltpu.roll(G, d, 0), 0.0)
        d *= 2
    gend = G[C - 1:C, :]
    qt = q * jnp.exp(G)
    kt = k * jnp.exp(gend - G)
    o = jnp.dot(qt.astype(BF16), S.astype(BF16), preferred_element_type=F32)

    row8 = lax.broadcasted_iota(I32, (SUBLANES, LANES), 0)
    parts = []
    for gi in range(C // SUBLANES):
        lo = gi * SUBLANES
        Gg = G[lo:lo + SUBLANES]
        qg = q[lo:lo + SUBLANES]
        acc = jnp.zeros((SUBLANES, LANES), F32)
        for s in range(lo + SUBLANES):
            dec = jnp.exp(Gg - G[s:s + 1])
            if s >= lo:
                dec = jnp.where(row8 >= (s - lo), dec, 0.0)
            w = jnp.sum(qg * dec * k[s:s + 1], axis=-1, keepdims=True)
            acc = acc + w * v[s:s + 1]
        parts.append(acc)
    o = o + (parts[0] if len(parts) == 1 else jnp.concatenate(parts, axis=0))

    ecol = jnp.transpose(jnp.broadcast_to(jnp.exp(gend), (HG_DK, LANES)))
    upd = lax.dot_general(kt.astype(BF16), v.astype(BF16), (((0,), (0,)), ((), ())),
                          preferred_element_type=F32)
    return o, ecol * S + upd


def _hgrn_rec_kernel(C, cps, zero_init, *refs):
    if zero_init:
        q_ref, k_ref, lf_ref, v_ref, g_ref, ng_ref, o_ref, s_ref = refs

        @pl.when(pl.program_id(1) == 0)
        def _():
            s_ref[...] = jnp.zeros(s_ref.shape, F32)
    else:
        q_ref, k_ref, lf_ref, v_ref, g_ref, ng_ref, s0_ref, o_ref, s_ref = refs
        s_ref[...] = s0_ref[...]

    n_chunks = q_ref.shape[0] // C

    def body(i, carry):
        r0 = pl.multiple_of(i * C, C)
        si = i // cps
        rows = pl.ds(r0, C)
        for h in range(HG_HEADS):
            sl = slice(h * LANES, (h + 1) * LANES)
            o, s_new = _hgrn_chunk_head(C, q_ref[rows, sl], k_ref[rows, sl], lf_ref[rows, sl],
                                        v_ref[rows, sl], s_ref[si, h])
            ms = jnp.mean(o * o, axis=-1, keepdims=True)
            o_ref[rows, sl] = o * lax.rsqrt(ms + LN_EPS) * ng_ref[:, sl] * g_ref[rows, sl]
            s_ref[si, h] = s_new
        return carry

    lax.fori_loop(0, n_chunks, body, 0)


def hgrn_rec_prompt(q, k, lf, v, g, norm_g, s_all, layer_j, batch, seq, tt=256):
    n = q.shape[0]
    nt = seq // tt
    C = HG_CHUNK_PROMPT
    row = pl.BlockSpec((tt, D_MODEL), lambda b, t: (b * nt + t, 0))
    s_spec = pl.BlockSpec((None, 1, HG_HEADS, HG_DK, HG_DV), lambda b, t: (layer_j, b, 0, 0, 0))
    in_specs = [row] * 5 + [pl.BlockSpec((1, D_MODEL), lambda b, t: (0, 0))]
    args = [q, k, lf, v, g, norm_g]
    aliases = {}
    kern = functools.partial(_hgrn_rec_kernel, C, tt // C, True)
    if s_all is not None:
        def kern(*refs):
            _hgrn_rec_kernel(C, tt // C, True, *refs[:6], *refs[7:])
        in_specs.append(pl.BlockSpec(memory_space=pl.ANY))
        args.append(s_all)
        aliases = {6: 1}
    return pl.pallas_call(
        kern,
        grid=(batch, nt),
        in_specs=in_specs,
        out_specs=[row, s_spec],
        out_shape=[jax.ShapeDtypeStruct((n, D_MODEL), F32),
                   jax.ShapeDtypeStruct((2, batch, HG_HEADS, HG_DK, HG_DV), F32)],
        input_output_aliases=aliases,
        compiler_params=_cparams(("arbitrary", "arbitrary")),
        name="hgrn_rec_prompt",
    )(*args)


def hgrn_rec_sample(q, k, lf, v, g, norm_g, state, o_prev, s_all, layer_j, row0, nseq, seq, sg=8):
    n = q.shape[0]
    rt = sg * seq
    base = row0 // rt
    row = pl.BlockSpec((rt, D_MODEL), lambda i: (base + i, 0))
    s_in = pl.BlockSpec((None, sg, HG_HEADS, HG_DK, HG_DV), lambda i: (layer_j, i, 0, 0, 0))
    anyspec = pl.BlockSpec(memory_space=pl.ANY)
    in_specs = [row] * 5 + [pl.BlockSpec((1, D_MODEL), lambda i: (0, 0)), s_in, anyspec]
    args = [q, k, lf, v, g, norm_g, state, o_prev]
    aliases = {7: 0}
    if s_all is not None:
        in_specs.append(anyspec)
        args.append(s_all)
        aliases[8] = 1

    def kern(*refs):
        _hgrn_rec_kernel(seq, 1, False, *refs[:7], *refs[len(args):])

    return pl.pallas_call(
        kern,
        grid=(nseq // sg,),
        in_specs=in_specs,
        out_specs=[row, s_in],
        out_shape=[jax.ShapeDtypeStruct((n, D_MODEL), F32),
                   jax.ShapeDtypeStruct((2, nseq, HG_HEADS, HG_DK, HG_DV), F32)],
        input_output_aliases=aliases,
        compiler_params=_cparams(("arbitrary",)),
        name="hgrn_rec_sample",
    )(*args)


def _proj_ln_kernel(a_ref, w_ref, x_ref, g_ref, b_ref, o_ref):
    y = jnp.dot(a_ref[...].astype(BF16), w_ref[...], preferred_element_type=F32)
    o_ref[...] = _layer_norm_rows(ALPHA * x_ref[...] + y, g_ref[...], b_ref[...])


def proj_ln(a, w_bf, x, g, b, tm=512):
    n = x.shape[0]
    row = pl.BlockSpec((tm, D_MODEL), lambda i: (i, 0))
    vec = pl.BlockSpec((1, D_MODEL), lambda i: (0, 0))
    return pl.pallas_call(
        _proj_ln_kernel,
        grid=(n // tm,),
        in_specs=[row, pl.BlockSpec((D_MODEL, D_MODEL), lambda i: (0, 0)), row, vec, vec],
        out_specs=row,
        out_shape=jax.ShapeDtypeStruct((n, D_MODEL), F32),
        compiler_params=_cparams(("arbitrary",)),
        name="proj_ln",
    )(a, w_bf, x, g, b)


def _ffn_kernel(x_ref, wg_ref, wu_ref, wd_ref, g_ref, b_ref, o_ref, acc_ref):
    f = pl.program_id(1)
    xb = x_ref[...].astype(BF16)
    gg = jnp.dot(xb, wg_ref[...], preferred_element_type=F32)
    uu = jnp.dot(xb, wu_ref[...], preferred_element_type=F32)
    h = (gg * _sigmoid(gg) * uu).astype(BF16)
    part = jnp.dot(h, wd_ref[...], preferred_element_type=F32)

    @pl.when(f == 0)
    def _():
        acc_ref[...] = part

    @pl.when(f > 0)
    def _():
        acc_ref[...] += part

    @pl.when(f == pl.num_programs(1) - 1)
    def _():
        o_ref[...] = _layer_norm_rows(ALPHA * x_ref[...] + acc_ref[...], g_ref[...], b_ref[...])


def ffn_ln(x, w_gu_bf, w_down_bf, g, b, tm=512, tf=1408):
    n = x.shape[0]
    nf = D_FF // tf
    row = pl.BlockSpec((tm, D_MODEL), lambda i, f: (i, 0))
    vec = pl.BlockSpec((1, D_MODEL), lambda i, f: (0, 0))
    return pl.pallas_call(
        _ffn_kernel,
        grid=(n // tm, nf),
        in_specs=[row,
                  pl.BlockSpec((D_MODEL, tf), lambda i, f: (0, f)),
                  pl.BlockSpec((D_MODEL, tf), lambda i, f: (0, nf + f)),
                  pl.BlockSpec((tf, D_MODEL), lambda i, f: (f, 0)),
                  vec, vec],
        out_specs=row,
        out_shape=jax.ShapeDtypeStruct((n, D_MODEL), F32),
        scratch_shapes=[pltpu.VMEM((tm, D_MODEL), F32)],
        compiler_params=_cparams(("arbitrary", "arbitrary")),
        name="ffn_ln",
    )(x, w_gu_bf, w_gu_bf, w_down_bf, g, b)


def _lru_in_kernel(x_ref, w_ref, xb_ref, gate_ref):
    xb = x_ref[...].astype(BF16)
    xb_ref[...] = jnp.dot(xb, w_ref[:, :D_MODEL], preferred_element_type=F32)
    y = jnp.dot(xb, w_ref[:, D_MODEL:], preferred_element_type=F32)
    c = math.sqrt(2.0 / math.pi)
    gate_ref[...] = 0.5 * y * (1.0 + jnp.tanh(c * (y + 0.044715 * (y * y * y))))


def lru_in(x, w_in_bf, tm=512):
    n = x.shape[0]
    row = pl.BlockSpec((tm, D_MODEL), lambda i: (i, 0))
    out = jax.ShapeDtypeStruct((n, D_MODEL), F32)
    return pl.pallas_call(
        _lru_in_kernel,
        grid=(n // tm,),
        in_specs=[row, pl.BlockSpec((D_MODEL, 2 * D_MODEL), lambda i: (0, 0))],
        out_specs=[row, row],
        out_shape=[out, out],
        compiler_params=_cparams(("arbitrary",)),
        name="lru_in",
    )(x, w_in_bf)


def _lru_rec_kernel(L, carry, *refs):
    if carry:
        (xb_ref, gate_ref, cw_ref, cb_ref, wr_ref, br_ref, wi_ref, bi_ref, lam_ref,
         hg_ref, hn_ref, cn_ref, xp_ref, hs_ref, hc_ref) = refs
    else:
        (xb_ref, gate_ref, cc_ref, h0_ref, cw_ref, cb_ref, wr_ref, br_ref, wi_ref, bi_ref, lam_ref,
         _, hg_ref, hn_ref, cn_ref, xp_ref, hs_ref) = refs
    R = xb_ref.shape[0]
    S = R // L
    PADR = SUBLANES
    t = pl.program_id(1) if carry else 0

    if carry:
        @pl.when(t == 0)
        def _():
            xp_ref[:, 0:PADR, :] = jnp.zeros((S, PADR, D_MODEL), F32)
            hc_ref[...] = jnp.zeros(hc_ref.shape, F32)

        @pl.when(t > 0)
        def _():
            xp_ref[:, 0:PADR, :] = xp_ref[:, L:L + PADR, :]
    else:
        xp_ref[:, 0:PADR, :] = cc_ref[...].reshape(S, PADR, D_MODEL)
    xp_ref[:, PADR:PADR + L, :] = xb_ref[...].reshape(S, L, D_MODEL)

    xc = cb_ref[...].reshape(1, 1, D_MODEL) + xp_ref[:, PADR:PADR + L, :] * cw_ref[CONV_W - 1:CONV_W, :].reshape(1, 1, D_MODEL)
    for j in range(CONV_W - 1):
        off = PADR - (CONV_W - 1) + j
        xc = xc + xp_ref[:, off:off + L, :] * cw_ref[j:j + 1, :].reshape(1, 1, D_MODEL)
    xc = xc.reshape(R, D_MODEL)
    cn_ref[...] = xp_ref[:, L + PADR - (CONV_W - 1):L + PADR, :].reshape(cn_ref.shape)

    assert L & (L - 1) == 0
    pos = jnp.bitwise_and(lax.broadcasted_iota(I32, (R, LRU_BW), 0), L - 1)
    for nb in range(LRU_BLOCKS):
        sl = slice(nb * LRU_BW, (nb + 1) * LRU_BW)
        xcb = xc[:, sl]
        xh = xcb.astype(BF16)
        r = _sigmoid(jnp.dot(xh, wr_ref[nb], preferred_element_type=F32) + br_ref[:, sl])
        ig = _sigmoid(jnp.dot(xh, wi_ref[nb], preferred_element_type=F32) + bi_ref[:, sl])
        z = -lam_ref[:, sl]
        softplus = jnp.maximum(z, 0.0) + jnp.log1p(jnp.exp(-jnp.abs(z)))
        log_a = (-RG_C) * r * softplus
        a = jnp.exp(log_a)
        th = jnp.tanh(-log_a)
        u = jnp.sqrt(2.0 * th / (1.0 + th)) * (ig * xcb)
        if carry:
            h0rows = jnp.where(pos == 0, hc_ref[:, sl], 0.0)
        else:
            h0rows = h0_ref[:, sl]
        u = u + a * h0rows
        d = 1
        while d < L:
            m = pos >= d
            u = jnp.where(m, a * pltpu.roll(u, d, 0) + u, u)
            a = jnp.where(m, a * pltpu.roll(a, d, 0), a)
            d *= 2
        hg_ref[:, sl] = u * gate_ref[:, sl]
        hs_ref[:, :, sl] = u.reshape(S, L, LRU_BW)

    hlast = hs_ref[:, L - 1:L, :]
    if carry:
        hc_ref[...] = hlast.reshape(1, D_MODEL)
        hn_ref[...] = hlast.reshape(hn_ref.shape)
    else:
        hn_ref[...] = hlast.reshape(hn_ref.shape)


def _lru_weight_specs(nidx):
    def c2(*_):
        return (0, 0)

    def c3(*_):
        return (0, 0, 0)
    return [pl.BlockSpec((CONV_W, D_MODEL), c2), pl.BlockSpec((1, D_MODEL), c2),
            pl.BlockSpec((LRU_BLOCKS, LRU_BW, LRU_BW), c3), pl.BlockSpec((1, D_MODEL), c2),
            pl.BlockSpec((LRU_BLOCKS, LRU_BW, LRU_BW), c3), pl.BlockSpec((1, D_MODEL), c2),
            pl.BlockSpec((1, D_MODEL), c2)]


def lru_rec_prompt(xb, gate, wts, batch, seq, tt=256):
    n = xb.shape[0]
    nt = seq // tt
    row = pl.BlockSpec((tt, D_MODEL), lambda b, t: (b * nt + t, 0))
    return pl.pallas_call(
        functools.partial(_lru_rec_kernel, tt, True),
        grid=(batch, nt),
        in_specs=[row, row] + _lru_weight_specs(2),
        out_specs=[row,
                   pl.BlockSpec((1, 1, D_MODEL), lambda b, t: (b, 0, 0)),
                   pl.BlockSpec((1, CONV_W - 1, D_MODEL), lambda b, t: (b, 0, 0))],
        out_shape=[jax.ShapeDtypeStruct((n, D_MODEL), F32),
                   jax.ShapeDtypeStruct((batch, 1, D_MODEL), F32),
                   jax.ShapeDtypeStruct((batch, CONV_W - 1, D_MODEL), F32)],
        scratch_shapes=[pltpu.VMEM((1, tt + SUBLANES, D_MODEL), F32),
                        pltpu.VMEM((1, tt, D_MODEL), F32),
                        pltpu.VMEM((1, D_MODEL), F32)],
        compiler_params=_cparams(("arbitrary", "arbitrary")),
        name="lru_rec_prompt",
    )(xb, gate, *wts)


def lru_rec_sample(xb, gate, cc_rows, h0_rows, wts, hg_prev, row0, nseq, seq, sg=32):
    n = xb.shape[0]
    rt = sg * seq
    base = row0 // rt
    row = pl.BlockSpec((rt, D_MODEL), lambda i: (base + i, 0))
    return pl.pallas_call(
        functools.partial(_lru_rec_kernel, seq, False),
        grid=(nseq // sg,),
        in_specs=[row, row,
                  pl.BlockSpec((sg * SUBLANES, D_MODEL), lambda i: (i, 0)),
                  pl.BlockSpec((rt, D_MODEL), lambda i: (i, 0))]
                 + _lru_weight_specs(1) + [pl.BlockSpec(memory_space=pl.ANY)],
        out_specs=[row,
                   pl.BlockSpec((sg, 1, D_MODEL), lambda i: (i, 0, 0)),
                   pl.BlockSpec((sg, CONV_W - 1, D_MODEL), lambda i: (i, 0, 0))],
        out_shape=[jax.ShapeDtypeStruct((n, D_MODEL), F32),
                   jax.ShapeDtypeStruct((nseq, 1, D_MODEL), F32),
                   jax.ShapeDtypeStruct((nseq, CONV_W - 1, D_MODEL), F32)],
        scratch_shapes=[pltpu.VMEM((sg, seq + SUBLANES, D_MODEL), F32),
                        pltpu.VMEM((sg, seq, D_MODEL), F32)],
        input_output_aliases={11: 0},
        compiler_params=_cparams(("arbitrary",)),
        name="lru_rec_sample",
    )(xb, gate, cc_rows, h0_rows, *wts, hg_prev)


def _router_kernel(x_ref, w_ref, mi_ref, mf_ref, cnt_ref, run_ref):
    i = pl.program_id(0)

    @pl.when(i == 0)
    def _():
        run_ref[...] = jnp.zeros(run_ref.shape, F32)

    tm = x_ref.shape[0]
    logits = jnp.dot(x_ref[...], w_ref[...], preferred_element_type=F32,
                     precision=lax.Precision.HIGHEST)
    lane_i = lax.broadcasted_iota(I32, (tm, LANES), 1)
    lane = lane_i.astype(F32)
    neg = jnp.float32(-jnp.inf)
    logits = jnp.where(lane_i < N_EXPERTS, logits, neg)
    m1 = jnp.max(logits, axis=-1, keepdims=True)
    i1 = jnp.min(jnp.where(logits == m1, lane, float(LANES)), axis=-1, keepdims=True)
    l2 = jnp.where(lane == i1, neg, logits)
    m2 = jnp.max(l2, axis=-1, keepdims=True)
    i2 = jnp.min(jnp.where(l2 == m2, lane, float(LANES)), axis=-1, keepdims=True)
    e2 = jnp.exp(m2 - m1)
    p1 = 1.0 / (1.0 + e2)
    p2 = e2 * p1

    hit1 = lane == i1
    hit2 = lane == i2
    onehot = jnp.where(hit1 | hit2, 1.0, 0.0)
    r_i = lax.broadcasted_iota(I32, (tm, tm), 0)
    c_i = lax.broadcasted_iota(I32, (tm, tm), 1)
    tri = jnp.where(c_i < r_i, 1.0, 0.0).astype(BF16)
    rank = jnp.dot(tri, onehot.astype(BF16), preferred_element_type=F32) + run_ref[0:1, :]
    r1 = jnp.sum(jnp.where(hit1, rank, 0.0), axis=-1, keepdims=True)
    r2 = jnp.sum(jnp.where(hit2, rank, 0.0), axis=-1, keepdims=True)
    total = run_ref[0:1, :] + jnp.sum(onehot, axis=0, keepdims=True)
    run_ref[...] = jnp.broadcast_to(total, run_ref.shape)
    cnt_ref[...] = jnp.broadcast_to(total, cnt_ref.shape).astype(I32)

    mi = jnp.where(lane_i == 0, i1, jnp.where(lane_i == 1, i2, 0.0))
    mi = jnp.where(lane_i == 2, r1, jnp.where(lane_i == 3, r2, mi))
    mi_ref[...] = mi.astype(I32)
    mf_ref[...] = jnp.where(lane_i == 0, p1, jnp.where(lane_i == 1, p2, 0.0))


def moe_router(x, w_router_pad, tm=512):
    n = x.shape[0]
    row = pl.BlockSpec((tm, LANES), lambda i: (i, 0))
    return pl.pallas_call(
        _router_kernel,
        grid=(n // tm,),
        in_specs=[pl.BlockSpec((tm, D_MODEL), lambda i: (i, 0)),
                  pl.BlockSpec((D_MODEL, LANES), lambda i: (0, 0))],
        out_specs=[row, row, pl.BlockSpec((SUBLANES, LANES), lambda i: (0, 0))],
        out_shape=[jax.ShapeDtypeStruct((n, LANES), I32),
                   jax.ShapeDtypeStruct((n, LANES), F32),
                   jax.ShapeDtypeStruct((SUBLANES, LANES), I32)],
        scratch_shapes=[pltpu.VMEM((SUBLANES, LANES), F32)],
        compiler_params=_cparams(("arbitrary",)),
        name="moe_router",
    )(x, w_router_pad)


DISPATCH_BATCH = 256


def _dispatch_kernel(dest_ref, x_hbm, xs_in, xs_hbm, sem):
    del xs_in
    n = x_hbm.shape[0]
    nb = n // DISPATCH_BATCH

    def wait_batch():
        pltpu.make_async_copy(x_hbm.at[pl.ds(0, 2 * DISPATCH_BATCH)],
                              xs_hbm.at[pl.ds(0, 2 * DISPATCH_BATCH)], sem).wait()

    def batch(b, carry):
        def tok(j, c):
            t = b * DISPATCH_BATCH + j
            src = x_hbm.at[pl.ds(t, 1)]
            pltpu.make_async_copy(src, xs_hbm.at[pl.ds(dest_ref[2 * t], 1)], sem).start()
            pltpu.make_async_copy(src, xs_hbm.at[pl.ds(dest_ref[2 * t + 1], 1)], sem).start()
            return c

        lax.fori_loop(0, DISPATCH_BATCH, tok, 0)

        @pl.when(b > 0)
        def _():
            wait_batch()
        return carry

    lax.fori_loop(0, nb, batch, 0)
    wait_batch()


def moe_dispatch(dest, x, n_rows):
    xs0 = jnp.zeros((n_rows, D_MODEL), F32)
    anyspec = pl.BlockSpec(memory_space=pl.ANY)
    return pl.pallas_call(
        _dispatch_kernel,
        grid_spec=pltpu.PrefetchScalarGridSpec(
            num_scalar_prefetch=1, grid=(1,),
            in_specs=[anyspec, anyspec], out_specs=anyspec,
            scratch_shapes=[pltpu.SemaphoreType.DMA(())]),
        out_shape=jax.ShapeDtypeStruct((n_rows, D_MODEL), F32),
        input_output_aliases={2: 0},
        compiler_params=pltpu.CompilerParams(dimension_semantics=("arbitrary",),
                                             has_side_effects=True),
        name="moe_dispatch",
    )(dest, x, xs0)


def _expert_kernel(te_ref, nu_ref, xs_ref, wg_ref, wu_ref, wd_ref, ys_ref, xb_ref, acc_ref):
    i = pl.program_id(0)
    f = pl.program_id(1)
    used = i < nu_ref[0]

    @pl.when(used & (f == 0))
    def _():
        xb_ref[...] = xs_ref[...].astype(BF16)

    @pl.when(used)
    def _():
        xb = xb_ref[...]
        gg = jnp.dot(xb, wg_ref[...], preferred_element_type=F32)
        uu = jnp.dot(xb, wu_ref[...], preferred_element_type=F32)
        h = (gg * _sigmoid(gg) * uu).astype(BF16)
        part = jnp.dot(h, wd_ref[...], preferred_element_type=F32)

        @pl.when(f == 0)
        def _():
            acc_ref[...] = part

        @pl.when(f > 0)
        def _():
            acc_ref[...] += part

    @pl.when(f == pl.num_programs(1) - 1)
    def _():
        @pl.when(used)
        def _():
            ys_ref[...] = acc_ref[...]

        @pl.when(jnp.logical_not(used))
        def _():
            ys_ref[...] = jnp.zeros(ys_ref.shape, F32)


def moe_experts(tile_expert, n_used, xs, w_gu_bf, w_down_bf, layer_j, tf=512):
    n_rows = xs.shape[0]
    tm = MOE_TILE
    nf = D_FF_EXPERT // tf

    def xrow(i, f, te, nu):
        return (jnp.minimum(i, nu[0] - 1), 0)

    def wg(i, f, te, nu):
        return (layer_j, te[i], 0, jnp.where(i < nu[0], f, nf - 1))

    def wu(i, f, te, nu):
        return (layer_j, te[i], 0, nf + jnp.where(i < nu[0], f, nf - 1))

    def wd(i, f, te, nu):
        return (layer_j, te[i], jnp.where(i < nu[0], f, nf - 1), 0)

    return pl.pallas_call(
        _expert_kernel,
        grid_spec=pltpu.PrefetchScalarGridSpec(
            num_scalar_prefetch=2, grid=(n_rows // tm, nf),
            in_specs=[pl.BlockSpec((tm, D_MODEL), xrow),
                      pl.BlockSpec((None, None, D_MODEL, tf), wg),
                      pl.BlockSpec((None, None, D_MODEL, tf), wu),
                      pl.BlockSpec((None, None, tf, D_MODEL), wd)],
            out_specs=pl.BlockSpec((tm, D_MODEL), lambda i, f, te, nu: (i, 0)),
            scratch_shapes=[pltpu.VMEM((tm, D_MODEL), BF16), pltpu.VMEM((tm, D_MODEL), F32)]),
        out_shape=jax.ShapeDtypeStruct((n_rows, D_MODEL), F32),
        compiler_params=_cparams(("arbitrary", "arbitrary")),
        name="moe_experts",
    )(tile_expert, n_used, xs, w_gu_bf, w_gu_bf, w_down_bf)


def _combine_kernel(dest_ref, ys_hbm, mf_ref, x_ref, g_ref, b_ref, o_ref, buf_ref, sem):
    i = pl.program_id(0)
    tm = x_ref.shape[0]

    def tok(j, c):
        t = i * tm + j
        pltpu.make_async_copy(ys_hbm.at[pl.ds(dest_ref[2 * t], 1)],
                              buf_ref.at[0, pl.ds(j, 1)], sem).start()
        pltpu.make_async_copy(ys_hbm.at[pl.ds(dest_ref[2 * t + 1], 1)],
                              buf_ref.at[1, pl.ds(j, 1)], sem).start()
        return c

    lax.fori_loop(0, tm, tok, 0)
    pltpu.make_async_copy(ys_hbm.at[pl.ds(0, 2 * tm)], buf_ref.reshape(2 * tm, D_MODEL), sem).wait()
    mf = mf_ref[...]
    y = buf_ref[0] * mf[:, 0:1] + buf_ref[1] * mf[:, 1:2]
    o_ref[...] = _layer_norm_rows(ALPHA * x_ref[...] + y, g_ref[...], b_ref[...])


def moe_combine_ln(dest, ys, mf, x, g, b, tm=256):
    n = x.shape[0]
    row = pl.BlockSpec((tm, D_MODEL), lambda i, d: (i, 0))
    vec = pl.BlockSpec((1, D_MODEL), lambda i, d: (0, 0))
    return pl.pallas_call(
        _combine_kernel,
        grid_spec=pltpu.PrefetchScalarGridSpec(
            num_scalar_prefetch=1, grid=(n // tm,),
            in_specs=[pl.BlockSpec(memory_space=pl.ANY),
                      pl.BlockSpec((tm, LANES), lambda i, d: (i, 0)),
                      row, vec, vec],
            out_specs=row,
            scratch_shapes=[pltpu.VMEM((2, tm, D_MODEL), F32), pltpu.SemaphoreType.DMA(())]),
        out_shape=jax.ShapeDtypeStruct((n, D_MODEL), F32),
        compiler_params=_cparams(("arbitrary",)),
        name="moe_combine_ln",
    )(dest, ys, mf, x, g, b)


def moe_ffn_ln(x, w_router_pad, w_gu_bf, w_down_bf, layer_j, g, b):
    n = x.shape[0]
    tm = MOE_TILE
    mi, mf, cnt = moe_router(x, w_router_pad)
    counts = cnt[0, :N_EXPERTS]
    padded = (counts + tm - 1) // tm * tm
    ends = jnp.cumsum(padded)
    starts = ends - padded
    ids = mi[:, 0:2]
    ranks = mi[:, 2:4]
    dest = (starts[ids] + ranks).reshape(-1).astype(I32)
    n_tiles = (2 * n) // tm + N_EXPERTS
    tile_start = jnp.arange(n_tiles, dtype=I32) * tm
    tile_expert = jnp.minimum(jnp.searchsorted(ends, tile_start, side="right"), N_EXPERTS - 1).astype(I32)
    n_used = (ends[-1] // tm).astype(I32).reshape(1)
    tile_expert = jnp.where(tile_start < ends[-1], tile_expert, tile_expert[jnp.maximum(n_used[0] - 1, 0)])
    xs = moe_dispatch(dest, x, n_tiles * tm)
    ys = moe_experts(tile_expert, n_used, xs, w_gu_bf, w_down_bf, layer_j)
    return moe_combine_ln(dest, ys, mf, x, g, b)


def kernel(x_prompt, x_sample, state_hgrn, state_lru_h, state_lru_conv, ln_mix_g, ln_mix_b, ln_ffn_g, ln_ffn_b, w_hgrn_in, hgrn_lb_logits, hgrn_norm_g, w_hgrn_out, w_lru_in, lru_conv_w, lru_conv_b, w_lru_rgate, b_lru_rgate, w_lru_igate, b_lru_igate, lru_lambda, w_lru_out, w_ffn_gu, w_ffn_down, w_router, w_moe_gu, w_moe_down):
    bp, tp, _ = x_prompt.shape
    bs, ts, _ = x_sample.shape
    n_p = bp * tp
    x = jnp.concatenate([x_prompt.reshape(n_p, D_MODEL), x_sample.reshape(bs * ts, D_MODEL)], axis=0)

    def vec(a, i):
        return a[i].reshape(1, D_MODEL)

    w_hgrn_in_bf = w_hgrn_in.astype(BF16)
    w_hgrn_out_bf = w_hgrn_out.astype(BF16)
    w_lru_in_bf = w_lru_in.astype(BF16)
    w_lru_out_bf = w_lru_out.astype(BF16)
    w_r_bf = w_lru_rgate.astype(BF16)
    w_i_bf = w_lru_igate.astype(BF16)
    w_ffn_gu_bf = w_ffn_gu.astype(BF16)
    w_ffn_down_bf = w_ffn_down.astype(BF16)
    w_moe_gu_bf = w_moe_gu.astype(BF16)
    w_moe_down_bf = w_moe_down.astype(BF16)
    w_router_pad = jnp.pad(w_router, ((0, 0), (0, 0), (0, LANES - N_EXPERTS)))

    hg_p = hg_s = None
    h_p, h_s, c_p, c_s = [], [], [], []
    for layer in range(DEPTH):
        j = layer // 2
        if layer % 2 == 0:
            q, k, lf, v, g = hgrn_in(x, w_hgrn_in_bf[j], hgrn_lb_logits, j)
            ng = vec(hgrn_norm_g, j)
            o, hg_p = hgrn_rec_prompt(q, k, lf, v, g, ng, hg_p, j, bp, tp)
            o, hg_s = hgrn_rec_sample(q, k, lf, v, g, ng, state_hgrn, o, hg_s, j, n_p, bs, ts)
            x = proj_ln(o, w_hgrn_out_bf[j], x, vec(ln_mix_g, layer), vec(ln_mix_b, layer))
            x = ffn_ln(x, w_ffn_gu_bf[j], w_ffn_down_bf[j], vec(ln_ffn_g, layer), vec(ln_ffn_b, layer))
        else:
            xb, gate = lru_in(x, w_lru_in_bf[j])
            wts = (lru_conv_w[j], vec(lru_conv_b, j), w_r_bf[j], vec(b_lru_rgate, j),
                   w_i_bf[j], vec(b_lru_igate, j), vec(lru_lambda, j))
            hgate, hp, cp = lru_rec_prompt(xb, gate, wts, bp, tp)
            cc_rows = jnp.pad(state_lru_conv[j], ((0, 0), (SUBLANES - (CONV_W - 1), 0), (0, 0)))
            h0_rows = jnp.pad(state_lru_h[j][:, None, :], ((0, 0), (0, ts - 1), (0, 0)))
            hgate, hs, cs = lru_rec_sample(xb, gate, cc_rows.reshape(bs * SUBLANES, D_MODEL),
                                           h0_rows.reshape(bs * ts, D_MODEL), wts, hgate, n_p, bs, ts)
            h_p.append(hp.reshape(bp, D_MODEL))
            h_s.append(hs.reshape(bs, D_MODEL))
            c_p.append(cp)
            c_s.append(cs)
            x = proj_ln(hgate, w_lru_out_bf[j], x, vec(ln_mix_g, layer), vec(ln_mix_b, layer))
            x = moe_ffn_ln(x, w_router_pad[j], w_moe_gu_bf, w_moe_down_bf, j,
                           vec(ln_ffn_g, layer), vec(ln_ffn_b, layer))

    y_prompt = x[:n_p].reshape(bp, tp, D_MODEL)
    y_sample = x[n_p:].reshape(bs, ts, D_MODEL)
    return (y_prompt, y_sample, hg_p, hg_s, jnp.stack(h_p), jnp.stack(h_s), jnp.stack(c_p), jnp.stack(c_s))
```

```python
import functools
import math

import jax
import jax.numpy as jnp
from jax import lax
from jax.experimental import pallas as pl
from jax.experimental.pallas import tpu as pltpu

F32 = jnp.float32
BF16 = jnp.bfloat16
I32 = jnp.int32

D_MODEL = 1024
DEPTH = 4
HG_DK = 128
HG_HEADS = 8
HG_DV = 128
CONV_W = 4
RG_C = 8.0
LRU_BLOCKS = 4
LRU_BW = 256
D_FF = 2816
N_EXPERTS = 8
D_FF_EXPERT = 3584
ALPHA = (2 * DEPTH) ** 0.25
LN_EPS = 1e-5
F_FLOOR = 1e-20

LANES = 128
SUBLANES = 8
VMEM_LIMIT = 56 * 1024 * 1024

HG_CHUNK_PROMPT = 16
MOE_TILE = 512


def _cparams(sem):
    return pltpu.CompilerParams(dimension_semantics=sem, vmem_limit_bytes=VMEM_LIMIT)


def _sigmoid(x):
    return 1.0 / (1.0 + jnp.exp(-x))


def _layer_norm_rows(z, g, b):
    mu = jnp.mean(z, axis=-1, keepdims=True)
    zc = z - mu
    var = jnp.mean(zc * zc, axis=-1, keepdims=True)
    return zc * lax.rsqrt(var + LN_EPS) * g + b


def _hgrn_in_kernel(layer_j, x_ref, w_ref, lbl_ref, q_ref, k_ref, lf_ref, v_ref, g_ref):
    xb = x_ref[...].astype(BF16)
    l = lbl_ref[...]
    e = jnp.exp(l - jnp.max(l, axis=0, keepdims=True))
    p = e / jnp.sum(e, axis=0, keepdims=True)
    cs = p[0:1]
    for i in range(1, layer_j + 1):
        cs = cs + p[i:i + 1]
    lb = cs - p[0:1]

    def seg(i):
        return jnp.dot(xb, w_ref[:, i * D_MODEL:(i + 1) * D_MODEL], preferred_element_type=F32)

    q = seg(0)
    q_ref[...] = q * _sigmoid(q) * (HG_DK ** -0.5)
    fp = seg(1)
    f = lb + (1.0 - lb) * _sigmoid(fp)
    lf_ref[...] = jnp.log(jnp.maximum(f, F_FLOOR))
    k_ref[...] = (1.0 - lb) * _sigmoid(-fp)
    v_ref[...] = seg(2)
    g = seg(3)
    g_ref[...] = g * _sigmoid(g)


def hgrn_in(x, w_in_bf, lb_logits, layer_j, tm=256):
    n = x.shape[0]
    row = pl.BlockSpec((tm, D_MODEL), lambda i: (i, 0))
    out = jax.ShapeDtypeStruct((n, D_MODEL), F32)
    return pl.pallas_call(
        functools.partial(_hgrn_in_kernel, layer_j),
        grid=(n // tm,),
        in_specs=[row,
                  pl.BlockSpec((D_MODEL, 4 * D_MODEL), lambda i: (0, 0)),
                  pl.BlockSpec(lb_logits.shape, lambda i: (0, 0))],
        out_specs=[row] * 5,
        out_shape=[out] * 5,
        compiler_params=_cparams(("arbitrary",)),
        name="hgrn_in",
    )(x, w_in_bf, lb_logits)


def _hgrn_chunk_head(C, q, k, lf, v, S):
    row = lax.broadcasted_iota(I32, (C, LANES), 0)
    G = lf
    d = 1
    while d < C:
        G = G + jnp.where(row >= d, pltpu.roll(G, d, 0), 0.0)
        d *= 2
    gend = G[C - 1:C, :]
    qt = q * jnp.exp(G)
    kt = k * jnp.exp(gend - G)
    o = jnp.dot(qt.astype(BF16), S.astype(BF16), preferred_element_type=F32)

    row8 = lax.broadcasted_iota(I32, (SUBLANES, LANES), 0)
    parts = []
    for gi in range(C // SUBLANES):
        lo = gi * SUBLANES
        Gg = G[lo:lo + SUBLANES]
        qg = q[lo:lo + SUBLANES]
        acc = jnp.zeros((SUBLANES, LANES), F32)
        for s in range(lo + SUBLANES):
            dec = jnp.exp(Gg - G[s:s + 1])
            if s >= lo:
                dec = jnp.where(row8 >= (s - lo), dec, 0.0)
            w = jnp.sum(qg * dec * k[s:s + 1], axis=-1, keepdims=True)
            acc = acc + w * v[s:s + 1]
        parts.append(acc)
    o = o + (parts[0] if len(parts) == 1 else jnp.concatenate(parts, axis=0))

    ecol = jnp.transpose(jnp.broadcast_to(jnp.exp(gend), (HG_DK, LANES)))
    upd = lax.dot_general(kt.astype(BF16), v.astype(BF16), (((0,), (0,)), ((), ())),
                          preferred_element_type=F32)
    return o, ecol * S + upd


def _hgrn_rec_kernel(C, cps, zero_init, *refs):
    if zero_init:
        q_ref, k_ref, lf_ref, v_ref, g_ref, ng_ref, o_ref, s_ref = refs

        @pl.when(pl.program_id(1) == 0)
        def _():
            s_ref[...] = jnp.zeros(s_ref.shape, F32)
    else:
        q_ref, k_ref, lf_ref, v_ref, g_ref, ng_ref, s0_ref, o_ref, s_ref = refs
        s_ref[...] = s0_ref[...]

    n_chunks = q_ref.shape[0] // C

    def body(i, carry):
        r0 = pl.multiple_of(i * C, C)
        si = i // cps
        rows = pl.ds(r0, C)
        for h in range(HG_HEADS):
            sl = slice(h * LANES, (h + 1) * LANES)
            o, s_new = _hgrn_chunk_head(C, q_ref[rows, sl], k_ref[rows, sl], lf_ref[rows, sl],
                                        v_ref[rows, sl], s_ref[si, h])
            ms = jnp.mean(o * o, axis=-1, keepdims=True)
            o_ref[rows, sl] = o * lax.rsqrt(ms + LN_EPS) * ng_ref[:, sl] * g_ref[rows, sl]
            s_ref[si, h] = s_new
        return carry

    lax.fori_loop(0, n_chunks, body, 0)


def hgrn_rec_prompt(q, k, lf, v, g, norm_g, s_all, layer_j, batch, seq, tt=256):
    n = q.shape[0]
    nt = seq // tt
    C = HG_CHUNK_PROMPT
    row = pl.BlockSpec((tt, D_MODEL), lambda b, t: (b * nt + t, 0))
    s_spec = pl.BlockSpec((None, 1, HG_HEADS, HG_DK, HG_DV), lambda b, t: (layer_j, b, 0, 0, 0))
    in_specs = [row] * 5 + [pl.BlockSpec((1, D_MODEL), lambda b, t: (0, 0))]
    args = [q, k, lf, v, g, norm_g]
    aliases = {}
    kern = functools.partial(_hgrn_rec_kernel, C, tt // C, True)
    if s_all is not None:
        def kern(*refs):
            _hgrn_rec_kernel(C, tt // C, True, *refs[:6], *refs[7:])
        in_specs.append(pl.BlockSpec(memory_space=pl.ANY))
        args.append(s_all)
        aliases = {6: 1}
    return pl.pallas_call(
        kern,
        grid=(batch, nt),
        in_specs=in_specs,
        out_specs=[row, s_spec],
        out_shape=[jax.ShapeDtypeStruct((n, D_MODEL), F32),
                   jax.ShapeDtypeStruct((2, batch, HG_HEADS, HG_DK, HG_DV), F32)],
        input_output_aliases=aliases,
        compiler_params=_cparams(("arbitrary", "arbitrary")),
        name="hgrn_rec_prompt",
    )(*args)


def hgrn_rec_sample(q, k, lf, v, g, norm_g, state, o_prev, s_all, layer_j, row0, nseq, seq, sg=8):
    n = q.shape[0]
    rt = sg * seq
    base = row0 // rt
    row = pl.BlockSpec((rt, D_MODEL), lambda i: (base + i, 0))
    s_in = pl.BlockSpec((None, sg, HG_HEADS, HG_DK, HG_DV), lambda i: (layer_j, i, 0, 0, 0))
    anyspec = pl.BlockSpec(memory_space=pl.ANY)
    in_specs = [row] * 5 + [pl.BlockSpec((1, D_MODEL), lambda i: (0, 0)), s_in, anyspec]
    args = [q, k, lf, v, g, norm_g, state, o_prev]
    aliases = {7: 0}
    if s_all is not None:
        in_specs.append(anyspec)
        args.append(s_all)
        aliases[8] = 1

    def kern(*refs):
        _hgrn_rec_kernel(seq, 1, False, *refs[:7], *refs[len(args):])

    return pl.pallas_call(
        kern,
        grid=(nseq // sg,),
        in_specs=in_specs,
        out_specs=[row, s_in],
        out_shape=[jax.ShapeDtypeStruct((n, D_MODEL), F32),
                   jax.ShapeDtypeStruct((2, nseq, HG_HEADS, HG_DK, HG_DV), F32)],
        input_output_aliases=aliases,
        compiler_params=_cparams(("arbitrary",)),
        name="hgrn_rec_sample",
    )(*args)


def _proj_ln_kernel(a_ref, w_ref, x_ref, g_ref, b_ref, o_ref):
    y = jnp.dot(a_ref[...].astype(BF16), w_ref[...], preferred_element_type=F32)
    o_ref[...] = _layer_norm_rows(ALPHA * x_ref[...] + y, g_ref[...], b_ref[...])


def proj_ln(a, w_bf, x, g, b, tm=512):
    n = x.shape[0]
    row = pl.BlockSpec((tm, D_MODEL), lambda i: (i, 0))
    vec = pl.BlockSpec((1, D_MODEL), lambda i: (0, 0))
    return pl.pallas_call(
        _proj_ln_kernel,
        grid=(n // tm,),
        in_specs=[row, pl.BlockSpec((D_MODEL, D_MODEL), lambda i: (0, 0)), row, vec, vec],
        out_specs=row,
        out_shape=jax.ShapeDtypeStruct((n, D_MODEL), F32),
        compiler_params=_cparams(("arbitrary",)),
        name="proj_ln",
    )(a, w_bf, x, g, b)


def _ffn_kernel(x_ref, wg_ref, wu_ref, wd_ref, g_ref, b_ref, o_ref, acc_ref):
    f = pl.program_id(1)
    xb = x_ref[...].astype(BF16)
    gg = jnp.dot(xb, wg_ref[...], preferred_element_type=F32)
    uu = jnp.dot(xb, wu_ref[...], preferred_element_type=F32)
    h = (gg * _sigmoid(gg) * uu).astype(BF16)
    part = jnp.dot(h, wd_ref[...], preferred_element_type=F32)

    @pl.when(f == 0)
    def _():
        acc_ref[...] = part

    @pl.when(f > 0)
    def _():
        acc_ref[...] += part

    @pl.when(f == pl.num_programs(1) - 1)
    def _():
        o_ref[...] = _layer_norm_rows(ALPHA * x_ref[...] + acc_ref[...], g_ref[...], b_ref[...])


def ffn_ln(x, w_gu_bf, w_down_bf, g, b, tm=512, tf=1408):
    n = x.shape[0]
    nf = D_FF // tf
    row = pl.BlockSpec((tm, D_MODEL), lambda i, f: (i, 0))
    vec = pl.BlockSpec((1, D_MODEL), lambda i, f: (0, 0))
    return pl.pallas_call(
        _ffn_kernel,
        grid=(n // tm, nf),
        in_specs=[row,
                  pl.BlockSpec((D_MODEL, tf), lambda i, f: (0, f)),
                  pl.BlockSpec((D_MODEL, tf), lambda i, f: (0, nf + f)),
                  pl.BlockSpec((tf, D_MODEL), lambda i, f: (f, 0)),
                  vec, vec],
        out_specs=row,
        out_shape=jax.ShapeDtypeStruct((n, D_MODEL), F32),
        scratch_shapes=[pltpu.VMEM((tm, D_MODEL), F32)],
        compiler_params=_cparams(("arbitrary", "arbitrary")),
        name="ffn_ln",
    )(x, w_gu_bf, w_gu_bf, w_down_bf, g, b)


def _lru_in_kernel(x_ref, w_ref, xb_ref, gate_ref):
    xb = x_ref[...].astype(BF16)
    xb_ref[...] = jnp.dot(xb, w_ref[:, :D_MODEL], preferred_element_type=F32)
    y = jnp.dot(xb, w_ref[:, D_MODEL:], preferred_element_type=F32)
    c = math.sqrt(2.0 / math.pi)
    gate_ref[...] = 0.5 * y * (1.0 + jnp.tanh(c * (y + 0.044715 * (y * y * y))))


def lru_in(x, w_in_bf, tm=512):
    n = x.shape[0]
    row = pl.BlockSpec((tm, D_MODEL), lambda i: (i, 0))
    out = jax.ShapeDtypeStruct((n, D_MODEL), F32)
    return pl.pallas_call(
        _lru_in_kernel,
        grid=(n // tm,),
        in_specs=[row, pl.BlockSpec((D_MODEL, 2 * D_MODEL), lambda i: (0, 0))],
        out_specs=[row, row],
        out_shape=[out, out],
        compiler_params=_cparams(("arbitrary",)),
        name="lru_in",
    )(x, w_in_bf)


def _lru_rec_kernel(L, carry, *refs):
    if carry:
        (xb_ref, gate_ref, cw_ref, cb_ref, wr_ref, br_ref, wi_ref, bi_ref, lam_ref,
         hg_ref, hn_ref, cn_ref, xp_ref, hs_ref, hc_ref) = refs
    else:
        (xb_ref, gate_ref, cc_ref, h0_ref, cw_ref, cb_ref, wr_ref, br_ref, wi_ref, bi_ref, lam_ref,
         _, hg_ref, hn_ref, cn_ref, xp_ref, hs_ref) = refs
    R = xb_ref.shape[0]
    S = R // L
    PADR = SUBLANES
    t = pl.program_id(1) if carry else 0

    if carry:
        @pl.when(t == 0)
        def _():
            xp_ref[:, 0:PADR, :] = jnp.zeros((S, PADR, D_MODEL), F32)
            hc_ref[...] = jnp.zeros(hc_ref.shape, F32)

        @pl.when(t > 0)
        def _():
            xp_ref[:, 0:PADR, :] = xp_ref[:, L:L + PADR, :]
    else:
        xp_ref[:, 0:PADR, :] = cc_ref[...].reshape(S, PADR, D_MODEL)
    xp_ref[:, PADR:PADR + L, :] = xb_ref[...].reshape(S, L, D_MODEL)

    xc = cb_ref[...].reshape(1, 1, D_MODEL) + xp_ref[:, PADR:PADR + L, :] * cw_ref[CONV_W - 1:CONV_W, :].reshape(1, 1, D_MODEL)
    for j in range(CONV_W - 1):
        off = PADR - (CONV_W - 1) + j
        xc = xc + xp_ref[:, off:off + L, :] * cw_ref[j:j + 1, :].reshape(1, 1, D_MODEL)
    xc = xc.reshape(R, D_MODEL)
    cn_ref[...] = xp_ref[:, L + PADR - (CONV_W - 1):L + PADR, :].reshape(cn_ref.shape)

    assert L & (L - 1) == 0
    pos = jnp.bitwise_and(lax.broadcasted_iota(I32, (R, LRU_BW), 0), L - 1)
    for nb in range(LRU_BLOCKS):
        sl = slice(nb * LRU_BW, (nb + 1) * LRU_BW)
        xcb = xc[:, sl]
        xh = xcb.astype(BF16)
        r = _sigmoid(jnp.dot(xh, wr_ref[nb], preferred_element_type=F32) + br_ref[:, sl])
        ig = _sigmoid(jnp.dot(xh, wi_ref[nb], preferred_element_type=F32) + bi_ref[:, sl])
        z = -lam_ref[:, sl]
        softplus = jnp.maximum(z, 0.0) + jnp.log1p(jnp.exp(-jnp.abs(z)))
        log_a = (-RG_C) * r * softplus
        a = jnp.exp(log_a)
        th = jnp.tanh(-log_a)
        u = jnp.sqrt(2.0 * th / (1.0 + th)) * (ig * xcb)
        if carry:
            h0rows = jnp.where(pos == 0, hc_ref[:, sl], 0.0)
        else:
            h0rows = h0_ref[:, sl]
        u = u + a * h0rows
        d = 1
        while d < L:
            m = pos >= d
            u = jnp.where(m, a * pltpu.roll(u, d, 0) + u, u)
            a = jnp.where(m, a * pltpu.roll(a, d, 0), a)
            d *= 2
        hg_ref[:, sl] = u * gate_ref[:, sl]
        hs_ref[:, :, sl] = u.reshape(S, L, LRU_BW)

    hlast = hs_ref[:, L - 1:L, :]
    if carry:
        hc_ref[...] = hlast.reshape(1, D_MODEL)
        hn_ref[...] = hlast.reshape(hn_ref.shape)
    else:
        hn_ref[...] = hlast.reshape(hn_ref.shape)


def _lru_weight_specs(nidx):
    def c2(*_):
        return (0, 0)

    def c3(*_):
        return (0, 0, 0)
    return [pl.BlockSpec((CONV_W, D_MODEL), c2), pl.BlockSpec((1, D_MODEL), c2),
            pl.BlockSpec((LRU_BLOCKS, LRU_BW, LRU_BW), c3), pl.BlockSpec((1, D_MODEL), c2),
            pl.BlockSpec((LRU_BLOCKS, LRU_BW, LRU_BW), c3), pl.BlockSpec((1, D_MODEL), c2),
            pl.BlockSpec((1, D_MODEL), c2)]


def lru_rec_prompt(xb, gate, wts, batch, seq, tt=256):
    n = xb.shape[0]
    nt = seq // tt
    row = pl.BlockSpec((tt, D_MODEL), lambda b, t: (b * nt + t, 0))
    return pl.pallas_call(
        functools.partial(_lru_rec_kernel, tt, True),
        grid=(batch, nt),
        in_specs=[row, row] + _lru_weight_specs(2),
        out_specs=[row,
                   pl.BlockSpec((1, 1, D_MODEL), lambda b, t: (b, 0, 0)),
                   pl.BlockSpec((1, CONV_W - 1, D_MODEL), lambda b, t: (b, 0, 0))],
        out_shape=[jax.ShapeDtypeStruct((n, D_MODEL), F32),
                   jax.ShapeDtypeStruct((batch, 1, D_MODEL), F32),
                   jax.ShapeDtypeStruct((batch, CONV_W - 1, D_MODEL), F32)],
        scratch_shapes=[pltpu.VMEM((1, tt + SUBLANES, D_MODEL), F32),
                        pltpu.VMEM((1, tt, D_MODEL), F32),
                        pltpu.VMEM((1, D_MODEL), F32)],
        compiler_params=_cparams(("arbitrary", "arbitrary")),
        name="lru_rec_prompt",
    )(xb, gate, *wts)


def lru_rec_sample(xb, gate, cc_rows, h0_rows, wts, hg_prev, row0, nseq, seq, sg=32):
    n = xb.shape[0]
    rt = sg * seq
    base = row0 // rt
    row = pl.BlockSpec((rt, D_MODEL), lambda i: (base + i, 0))
    return pl.pallas_call(
        functools.partial(_lru_rec_kernel, seq, False),
        grid=(nseq // sg,),
        in_specs=[row, row,
                  pl.BlockSpec((sg * SUBLANES, D_MODEL), lambda i: (i, 0)),
                  pl.BlockSpec((rt, D_MODEL), lambda i: (i, 0))]
                 + _lru_weight_specs(1) + [pl.BlockSpec(memory_space=pl.ANY)],
        out_specs=[row,
                   pl.BlockSpec((sg, 1, D_MODEL), lambda i: (i, 0, 0)),
                   pl.BlockSpec((sg, CONV_W - 1, D_MODEL), lambda i: (i, 0, 0))],
        out_shape=[jax.ShapeDtypeStruct((n, D_MODEL), F32),
                   jax.ShapeDtypeStruct((nseq, 1, D_MODEL), F32),
                   jax.ShapeDtypeStruct((nseq, CONV_W - 1, D_MODEL), F32)],
        scratch_shapes=[pltpu.VMEM((sg, seq + SUBLANES, D_MODEL), F32),
                        pltpu.VMEM((sg, seq, D_MODEL), F32)],
        input_output_aliases={11: 0},
        compiler_params=_cparams(("arbitrary",)),
        name="lru_rec_sample",
    )(xb, gate, cc_rows, h0_rows, *wts, hg_prev)


def _router_kernel(x_ref, w_ref, mi_ref, mf_ref, cnt_ref, run_ref):
    i = pl.program_id(0)

    @pl.when(i == 0)
    def _():
        run_ref[...] = jnp.zeros(run_ref.shape, F32)

    tm = x_ref.shape[0]
    logits = jnp.dot(x_ref[...], w_ref[...], preferred_element_type=F32,
                     precision=lax.Precision.HIGHEST)
    lane_i = lax.broadcasted_iota(I32, (tm, LANES), 1)
    lane = lane_i.astype(F32)
    neg = jnp.float32(-jnp.inf)
    logits = jnp.where(lane_i < N_EXPERTS, logits, neg)
    m1 = jnp.max(logits, axis=-1, keepdims=True)
    i1 = jnp.min(jnp.where(logits == m1, lane, float(LANES)), axis=-1, keepdims=True)
    l2 = jnp.where(lane == i1, neg, logits)
    m2 = jnp.max(l2, axis=-1, keepdims=True)
    i2 = jnp.min(jnp.where(l2 == m2, lane, float(LANES)), axis=-1, keepdims=True)
    e2 = jnp.exp(m2 - m1)
    p1 = 1.0 / (1.0 + e2)
    p2 = e2 * p1

    hit1 = lane == i1
    hit2 = lane == i2
    onehot = jnp.where(hit1 | hit2, 1.0, 0.0)
    r_i = lax.broadcasted_iota(I32, (tm, tm), 0)
    c_i = lax.broadcasted_iota(I32, (tm, tm), 1)
    tri = jnp.where(c_i < r_i, 1.0, 0.0).astype(BF16)
    rank = jnp.dot(tri, onehot.astype(BF16), preferred_element_type=F32) + run_ref[0:1, :]
    r1 = jnp.sum(jnp.where(hit1, rank, 0.0), axis=-1, keepdims=True)
    r2 = jnp.sum(jnp.where(hit2, rank, 0.0), axis=-1, keepdims=True)
    total = run_ref[0:1, :] + jnp.sum(onehot, axis=0, keepdims=True)
    run_ref[...] = jnp.broadcast_to(total, run_ref.shape)
    cnt_ref[...] = jnp.broadcast_to(total, cnt_ref.shape).astype(I32)

    mi = jnp.where(lane_i == 0, i1, jnp.where(lane_i == 1, i2, 0.0))
    mi = jnp.where(lane_i == 2, r1, jnp.where(lane_i == 3, r2, mi))
    mi_ref[...] = mi.astype(I32)
    mf_ref[...] = jnp.where(lane_i == 0, p1, jnp.where(lane_i == 1, p2, 0.0))


def moe_router(x, w_router_pad, tm=512):
    n = x.shape[0]
    row = pl.BlockSpec((tm, LANES), lambda i: (i, 0))
    return pl.pallas_call(
        _router_kernel,
        grid=(n // tm,),
        in_specs=[pl.BlockSpec((tm, D_MODEL), lambda i: (i, 0)),
                  pl.BlockSpec((D_MODEL, LANES), lambda i: (0, 0))],
        out_specs=[row, row, pl.BlockSpec((SUBLANES, LANES), lambda i: (0, 0))],
        out_shape=[jax.ShapeDtypeStruct((n, LANES), I32),
                   jax.ShapeDtypeStruct((n, LANES), F32),
                   jax.ShapeDtypeStruct((SUBLANES, LANES), I32)],
        scratch_shapes=[pltpu.VMEM((SUBLANES, LANES), F32)],
        compiler_params=_cparams(("arbitrary",)),
        name="moe_router",
    )(x, w_router_pad)


def _dispatch_kernel(dest_ref, x_ref, xs_in, xs_hbm, sem):
    del xs_in
    i = pl.program_id(0)
    tm = x_ref.shape[0]

    def tok(j, c):
        t = i * tm + j
        src = x_ref.at[pl.ds(j, 1)]
        pltpu.make_async_copy(src, xs_hbm.at[pl.ds(dest_ref[2 * t], 1)], sem).start()
        pltpu.make_async_copy(src, xs_hbm.at[pl.ds(dest_ref[2 * t + 1], 1)], sem).start()
        return c

    lax.fori_loop(0, tm, tok, 0, unroll=8)
    for _ in range(2):
        pltpu.make_async_copy(x_ref, xs_hbm.at[pl.ds(0, tm)], sem).wait()


def moe_dispatch(dest, x, n_rows, tm=512):
    n = x.shape[0]
    xs0 = jnp.zeros((n_rows, D_MODEL), F32)
    anyspec = pl.BlockSpec(memory_space=pl.ANY)
    return pl.pallas_call(
        _dispatch_kernel,
        grid_spec=pltpu.PrefetchScalarGridSpec(
            num_scalar_prefetch=1, grid=(n // tm,),
            in_specs=[pl.BlockSpec((tm, D_MODEL), lambda i, d: (i, 0)), anyspec],
            out_specs=anyspec,
            scratch_shapes=[pltpu.SemaphoreType.DMA(())]),
        out_shape=jax.ShapeDtypeStruct((n_rows, D_MODEL), F32),
        input_output_aliases={2: 0},
        compiler_params=pltpu.CompilerParams(dimension_semantics=("arbitrary",),
                                             has_side_effects=True),
        name="moe_dispatch",
    )(dest, x, xs0)


def _expert_kernel(te_ref, nu_ref, xs_ref, wg_ref, wu_ref, wd_ref, ys_ref, xb_ref, acc_ref):
    i = pl.program_id(0)
    f = pl.program_id(1)
    used = i < nu_ref[0]

    @pl.when(used & (f == 0))
    def _():
        xb_ref[...] = xs_ref[...].astype(BF16)

    @pl.when(used)
    def _():
        xb = xb_ref[...]
        gg = jnp.dot(xb, wg_ref[...], preferred_element_type=F32)
        uu = jnp.dot(xb, wu_ref[...], preferred_element_type=F32)
        h = (gg * _sigmoid(gg) * uu).astype(BF16)
        part = jnp.dot(h, wd_ref[...], preferred_element_type=F32)

        @pl.when(f == 0)
        def _():
            acc_ref[...] = part

        @pl.when(f > 0)
        def _():
            acc_ref[...] += part

    @pl.when(f == pl.num_programs(1) - 1)
    def _():
        @pl.when(used)
        def _():
            ys_ref[...] = acc_ref[...]

        @pl.when(jnp.logical_not(used))
        def _():
            ys_ref[...] = jnp.zeros(ys_ref.shape, F32)


def moe_experts(tile_expert, n_used, xs, w_gu_bf, w_down_bf, layer_j, tf=1792):
    n_rows = xs.shape[0]
    tm = MOE_TILE
    nf = D_FF_EXPERT // tf

    def xrow(i, f, te, nu):
        return (jnp.minimum(i, nu[0] - 1), 0)

    def wg(i, f, te, nu):
        return (layer_j, te[i], 0, jnp.where(i < nu[0], f, nf - 1))

    def wu(i, f, te, nu):
        return (layer_j, te[i], 0, nf + jnp.where(i < nu[0], f, nf - 1))

    def wd(i, f, te, nu):
        return (layer_j, te[i], jnp.where(i < nu[0], f, nf - 1), 0)

    return pl.pallas_call(
        _expert_kernel,
        grid_spec=pltpu.PrefetchScalarGridSpec(
            num_scalar_prefetch=2, grid=(n_rows // tm, nf),
            in_specs=[pl.BlockSpec((tm, D_MODEL), xrow),
                      pl.BlockSpec((None, None, D_MODEL, tf), wg),
                      pl.BlockSpec((None, None, D_MODEL, tf), wu),
                      pl.BlockSpec((None, None, tf, D_MODEL), wd)],
            out_specs=pl.BlockSpec((tm, D_MODEL), lambda i, f, te, nu: (i, 0)),
            scratch_shapes=[pltpu.VMEM((tm, D_MODEL), BF16), pltpu.VMEM((tm, D_MODEL), F32)]),
        out_shape=jax.ShapeDtypeStruct((n_rows, D_MODEL), F32),
        compiler_params=_cparams(("arbitrary", "arbitrary")),
        name="moe_experts",
    )(tile_expert, n_used, xs, w_gu_bf, w_gu_bf, w_down_bf)


def _combine_kernel(dest_ref, ys_hbm, mf_ref, x_ref, g_ref, b_ref, o_ref, buf_ref, sem):
    i = pl.program_id(0)
    tm = x_ref.shape[0]

    def tok(j, c):
        t = i * tm + j
        pltpu.make_async_copy(ys_hbm.at[pl.ds(dest_ref[2 * t], 1)],
                              buf_ref.at[0, pl.ds(j, 1)], sem).start()
        pltpu.make_async_copy(ys_hbm.at[pl.ds(dest_ref[2 * t + 1], 1)],
                              buf_ref.at[1, pl.ds(j, 1)], sem).start()
        return c

    lax.fori_loop(0, tm, tok, 0, unroll=8)
    pltpu.make_async_copy(ys_hbm.at[pl.ds(0, 2 * tm)], buf_ref.reshape(2 * tm, D_MODEL), sem).wait()
    mf = mf_ref[...]
    y = buf_ref[0] * mf[:, 0:1] + buf_ref[1] * mf[:, 1:2]
    o_ref[...] = _layer_norm_rows(ALPHA * x_ref[...] + y, g_ref[...], b_ref[...])


def moe_combine_ln(dest, ys, mf, x, g, b, tm=256):
    n = x.shape[0]
    row = pl.BlockSpec((tm, D_MODEL), lambda i, d: (i, 0))
    vec = pl.BlockSpec((1, D_MODEL), lambda i, d: (0, 0))
    return pl.pallas_call(
        _combine_kernel,
        grid_spec=pltpu.PrefetchScalarGridSpec(
            num_scalar_prefetch=1, grid=(n // tm,),
            in_specs=[pl.BlockSpec(memory_space=pl.ANY),
                      pl.BlockSpec((tm, LANES), lambda i, d: (i, 0)),
                      row, vec, vec],
            out_specs=row,
            scratch_shapes=[pltpu.VMEM((2, tm, D_MODEL), F32), pltpu.SemaphoreType.DMA(())]),
        out_shape=jax.ShapeDtypeStruct((n, D_MODEL), F32),
        compiler_params=_cparams(("arbitrary",)),
        name="moe_combine_ln",
    )(dest, ys, mf, x, g, b)


def moe_ffn_ln(x, w_router_pad, w_gu_bf, w_down_bf, layer_j, g, b):
    n = x.shape[0]
    tm = MOE_TILE
    mi, mf, cnt = moe_router(x, w_router_pad)
    counts = cnt[0, :N_EXPERTS]
    padded = (counts + tm - 1) // tm * tm
    ends = jnp.cumsum(padded)
    starts = ends - padded
    ids = mi[:, 0:2]
    ranks = mi[:, 2:4]
    dest = (starts[ids] + ranks).reshape(-1).astype(I32)
    n_tiles = (2 * n) // tm + N_EXPERTS
    tile_start = jnp.arange(n_tiles, dtype=I32) * tm
    tile_expert = jnp.minimum(jnp.sum(tile_start[:, None] >= ends[None, :], axis=1), N_EXPERTS - 1).astype(I32)
    n_used = (ends[-1] // tm).astype(I32).reshape(1)
    tile_expert = jnp.where(tile_start < ends[-1], tile_expert, tile_expert[jnp.maximum(n_used[0] - 1, 0)])
    xs = moe_dispatch(dest, x, n_tiles * tm)
    ys = moe_experts(tile_expert, n_used, xs, w_gu_bf, w_down_bf, layer_j)
    return moe_combine_ln(dest, ys, mf, x, g, b)


def kernel(x_prompt, x_sample, state_hgrn, state_lru_h, state_lru_conv, ln_mix_g, ln_mix_b, ln_ffn_g, ln_ffn_b, w_hgrn_in, hgrn_lb_logits, hgrn_norm_g, w_hgrn_out, w_lru_in, lru_conv_w, lru_conv_b, w_lru_rgate, b_lru_rgate, w_lru_igate, b_lru_igate, lru_lambda, w_lru_out, w_ffn_gu, w_ffn_down, w_router, w_moe_gu, w_moe_down):
    bp, tp, _ = x_prompt.shape
    bs, ts, _ = x_sample.shape
    n_p = bp * tp
    x = jnp.concatenate([x_prompt.reshape(n_p, D_MODEL), x_sample.reshape(bs * ts, D_MODEL)], axis=0)

    def vec(a, i):
        return a[i].reshape(1, D_MODEL)

    w_hgrn_in_bf = w_hgrn_in.astype(BF16)
    w_hgrn_out_bf = w_hgrn_out.astype(BF16)
    w_lru_in_bf = w_lru_in.astype(BF16)
    w_lru_out_bf = w_lru_out.astype(BF16)
    w_r_bf = w_lru_rgate.astype(BF16)
    w_i_bf = w_lru_igate.astype(BF16)
    w_ffn_gu_bf = w_ffn_gu.astype(BF16)
    w_ffn_down_bf = w_ffn_down.astype(BF16)
    w_moe_gu_bf = w_moe_gu.astype(BF16)
    w_moe_down_bf = w_moe_down.astype(BF16)
    w_router_pad = jnp.pad(w_router, ((0, 0), (0, 0), (0, LANES - N_EXPERTS)))

    hg_p = hg_s = None
    h_p, h_s, c_p, c_s = [], [], [], []
    for layer in range(DEPTH):
        j = layer // 2
        if layer % 2 == 0:
            q, k, lf, v, g = hgrn_in(x, w_hgrn_in_bf[j], hgrn_lb_logits, j)
            ng = vec(hgrn_norm_g, j)
            o, hg_p = hgrn_rec_prompt(q, k, lf, v, g, ng, hg_p, j, bp, tp)
            o, hg_s = hgrn_rec_sample(q, k, lf, v, g, ng, state_hgrn, o, hg_s, j, n_p, bs, ts)
            x = proj_ln(o, w_hgrn_out_bf[j], x, vec(ln_mix_g, layer), vec(ln_mix_b, layer))
            x = ffn_ln(x, w_ffn_gu_bf[j], w_ffn_down_bf[j], vec(ln_ffn_g, layer), vec(ln_ffn_b, layer))
        else:
            xb, gate = lru_in(x, w_lru_in_bf[j])
            wts = (lru_conv_w[j], vec(lru_conv_b, j), w_r_bf[j], vec(b_lru_rgate, j),
                   w_i_bf[j], vec(b_lru_igate, j), vec(lru_lambda, j))
            hgate, hp, cp = lru_rec_prompt(xb, gate, wts, bp, tp)
            cc_rows = jnp.pad(state_lru_conv[j], ((0, 0), (SUBLANES - (CONV_W - 1), 0), (0, 0)))
            h0_rows = jnp.pad(state_lru_h[j][:, None, :], ((0, 0), (0, ts - 1), (0, 0)))
            hgate, hs, cs = lru_rec_sample(xb, gate, cc_rows.reshape(bs * SUBLANES, D_MODEL),
                                           h0_rows.reshape(bs * ts, D_MODEL), wts, hgate, n_p, bs, ts)
            h_p.append(hp.reshape(bp, D_MODEL))
            h_s.append(hs.reshape(bs, D_MODEL))
            c_p.append(cp)
            c_s.append(cs)
            x = proj_ln(hgate, w_lru_out_bf[j], x, vec(ln_mix_g, layer), vec(ln_mix_b, layer))
            x = moe_ffn_ln(x, w_router_pad[j], w_moe_gu_bf, w_moe_down_bf, j,
                           vec(ln_ffn_g, layer), vec(ln_ffn_b, layer))

    y_prompt = x[:n_p].reshape(bp, tp, D_MODEL)
    y_sample = x[n_p:].reshape(bs, ts, D_MODEL)
    return (y_prompt, y_sample, hg_p, hg_s, jnp.stack(h_p), jnp.stack(h_s), jnp.stack(c_p), jnp.stack(c_s))
```

```python
import functools
import math

import jax
import jax.numpy as jnp
from jax import lax
from jax.experimental import pallas as pl
from jax.experimental.pallas import tpu as pltpu

F32 = jnp.float32
BF16 = jnp.bfloat16
I32 = jnp.int32

D_MODEL = 1024
DEPTH = 4
HG_DK = 128
HG_HEADS = 8
HG_DV = 128
CONV_W = 4
RG_C = 8.0
LRU_BLOCKS = 4
LRU_BW = 256
D_FF = 2816
N_EXPERTS = 8
D_FF_EXPERT = 3584
ALPHA = (2 * DEPTH) ** 0.25
LN_EPS = 1e-5
F_FLOOR = 1e-20

LANES = 128
SUBLANES = 8
VMEM_LIMIT = 56 * 1024 * 1024

HG_CHUNK_PROMPT = 16
MOE_TILE = 512


def _cparams(sem):
    return pltpu.CompilerParams(dimension_semantics=sem, vmem_limit_bytes=VMEM_LIMIT)


def _sigmoid(x):
    return 1.0 / (1.0 + jnp.exp(-x))


def _layer_norm_rows(z, g, b):
    mu = jnp.mean(z, axis=-1, keepdims=True)
    zc = z - mu
    var = jnp.mean(zc * zc, axis=-1, keepdims=True)
    return zc * lax.rsqrt(var + LN_EPS) * g + b


def _hgrn_in_kernel(layer_j, x_ref, w_ref, lbl_ref, q_ref, k_ref, lf_ref, v_ref, g_ref):
    xb = x_ref[...].astype(BF16)
    l = lbl_ref[...]
    e = jnp.exp(l - jnp.max(l, axis=0, keepdims=True))
    p = e / jnp.sum(e, axis=0, keepdims=True)
    cs = p[0:1]
    for i in range(1, layer_j + 1):
        cs = cs + p[i:i + 1]
    lb = cs - p[0:1]

    def seg(i):
        return jnp.dot(xb, w_ref[:, i * D_MODEL:(i + 1) * D_MODEL], preferred_element_type=F32)

    q = seg(0)
    q_ref[...] = q * _sigmoid(q) * (HG_DK ** -0.5)
    fp = seg(1)
    f = lb + (1.0 - lb) * _sigmoid(fp)
    lf_ref[...] = jnp.log(jnp.maximum(f, F_FLOOR))
    k_ref[...] = (1.0 - lb) * _sigmoid(-fp)
    v_ref[...] = seg(2)
    g = seg(3)
    g_ref[...] = g * _sigmoid(g)


def hgrn_in(x, w_in_bf, lb_logits, layer_j, tm=256):
    n = x.shape[0]
    row = pl.BlockSpec((tm, D_MODEL), lambda i: (i, 0))
    out = jax.ShapeDtypeStruct((n, D_MODEL), F32)
    return pl.pallas_call(
        functools.partial(_hgrn_in_kernel, layer_j),
        grid=(n // tm,),
        in_specs=[row,
                  pl.BlockSpec((D_MODEL, 4 * D_MODEL), lambda i: (0, 0)),
                  pl.BlockSpec(lb_logits.shape, lambda i: (0, 0))],
        out_specs=[row] * 5,
        out_shape=[out] * 5,
        compiler_params=_cparams(("arbitrary",)),
        name="hgrn_in",
    )(x, w_in_bf, lb_logits)


def _hgrn_chunk_head(C, q, k, lf, v, S):
    row = lax.broadcasted_iota(I32, (C, LANES), 0)
    G = lf
    d = 1
    while d < C:
        G = G + jnp.where(row >= d, pltpu.roll(G, d, 0), 0.0)
        d *= 2
    gend = G[C - 1:C, :]
    qt = q * jnp.exp(G)
    kt = k * jnp.exp(gend - G)
    o = jnp.dot(qt.astype(BF16), S.astype(BF16), preferred_element_type=F32)

    row8 = lax.broadcasted_iota(I32, (SUBLANES, LANES), 0)
    parts = []
    for gi in range(C // SUBLANES):
        lo = gi * SUBLANES
        Gg = G[lo:lo + SUBLANES]
        qg = q[lo:lo + SUBLANES]
        acc = jnp.zeros((SUBLANES, LANES), F32)
        for s in range(lo + SUBLANES):
            dec = jnp.exp(Gg - G[s:s + 1])
            if s >= lo:
                dec = jnp.where(row8 >= (s - lo), dec, 0.0)
            w = jnp.sum(qg * dec * k[s:s + 1], axis=-1, keepdims=True)
            acc = acc + w * v[s:s + 1]
        parts.append(acc)
    o = o + (parts[0] if len(parts) == 1 else jnp.concatenate(parts, axis=0))

    ecol = jnp.transpose(jnp.broadcast_to(jnp.exp(gend), (HG_DK, LANES)))
    upd = lax.dot_general(kt.astype(BF16), v.astype(BF16), (((0,), (0,)), ((), ())),
                          preferred_element_type=F32)
    return o, ecol * S + upd


def _hgrn_rec_kernel(C, cps, zero_init, *refs):
    if zero_init:
        q_ref, k_ref, lf_ref, v_ref, g_ref, ng_ref, o_ref, s_ref = refs

        @pl.when(pl.program_id(1) == 0)
        def _():
            s_ref[...] = jnp.zeros(s_ref.shape, F32)
    else:
        q_ref, k_ref, lf_ref, v_ref, g_ref, ng_ref, s0_ref, o_ref, s_ref = refs
        s_ref[...] = s0_ref[...]

    n_chunks = q_ref.shape[0] // C

    def body(i, carry):
        r0 = pl.multiple_of(i * C, C)
        si = i // cps
        rows = pl.ds(r0, C)
        for h in range(HG_HEADS):
            sl = slice(h * LANES, (h + 1) * LANES)
            o, s_new = _hgrn_chunk_head(C, q_ref[rows, sl], k_ref[rows, sl], lf_ref[rows, sl],
                                        v_ref[rows, sl], s_ref[si, h])
            ms = jnp.mean(o * o, axis=-1, keepdims=True)
            o_ref[rows, sl] = o * lax.rsqrt(ms + LN_EPS) * ng_ref[:, sl] * g_ref[rows, sl]
            s_ref[si, h] = s_new
        return carry

    lax.fori_loop(0, n_chunks, body, 0)


HG_CHUNK = 64
HG_SUB = 16
HG_SAFE_RANGE = 60.0


def _hgrn_chunk_masks(tt):
    import numpy as np
    t = np.arange(tt)[:, None]
    s = np.arange(tt)[None, :]
    same = (t // HG_CHUNK) == (s // HG_CHUNK)
    tri = (s <= t) & same
    mats = [tri, (s > t) & same]
    for blk in (HG_SUB, 2 * HG_SUB, 4 * HG_SUB):
        bnd = (t // blk) * blk + (blk // 2 if blk > HG_SUB else 0)
        mats.append(tri.astype(np.int32) - ((s < bnd) & same).astype(np.int32))
    return jnp.asarray(np.concatenate([np.asarray(m, np.float32) for m in mats], axis=0), BF16)


def _hgrn_prompt_kernel(q_ref, k_ref, lf_ref, v_ref, g_ref, ng_ref, m_ref, o_ref, s_ref,
                        g_s, qi_s, ks_s, qd_s, kd_s, q1_s, k1_s, q2_s, k2_s, a_s, flag_s):
    tt = q_ref.shape[0]

    @pl.when(pl.program_id(1) == 0)
    def _():
        s_ref[...] = jnp.zeros(s_ref.shape, F32)

    lf = lf_ref[...]
    hi = lf.astype(BF16)
    lo = (lf - hi.astype(F32)).astype(BF16)

    def cum(m):
        mm = m_ref[m * tt:(m + 1) * tt, :]
        return (jnp.dot(mm, hi, preferred_element_type=F32)
                + jnp.dot(mm, lo, preferred_element_type=F32))

    q = q_ref[...]
    k = k_ref[...]
    G = cum(0)
    g_s[...] = G
    qi_s[...] = (q * jnp.exp(G)).astype(BF16)
    ks_s[...] = (k * jnp.exp(cum(1))).astype(BF16)
    e0 = cum(2)
    qd_s[...] = (q * jnp.exp(e0)).astype(BF16)
    kd_s[...] = (k * jnp.exp(-e0)).astype(BF16)
    flag_s[0] = (jnp.max(-e0) <= HG_SAFE_RANGE).astype(I32)
    e1 = cum(3)
    q1_s[...] = (q * jnp.exp(jnp.minimum(e1, 0.0))).astype(BF16)
    k1_s[...] = (k * jnp.exp(jnp.minimum(-e1, 0.0))).astype(BF16)
    e2 = cum(4)
    q2_s[...] = (q * jnp.exp(jnp.minimum(e2, 0.0))).astype(BF16)
    k2_s[...] = (k * jnp.exp(jnp.minimum(-e2, 0.0))).astype(BF16)

    def finish(rows, sl, o):
        ms = jnp.mean(o * o, axis=-1, keepdims=True)
        o_ref[rows, sl] = o * lax.rsqrt(ms + LN_EPS) * ng_ref[:, sl] * g_ref[rows, sl]

    C = HG_CHUNK
    ti = lax.broadcasted_iota(I32, (C, C), 0)
    si = lax.broadcasted_iota(I32, (C, C), 1)
    causal = si <= ti
    sub_bits = HG_SUB.bit_length() - 1
    m0 = causal & ((ti >> sub_bits) == (si >> sub_bits))
    m1 = causal & ((ti >> (sub_bits + 1)) == (si >> (sub_bits + 1))) & jnp.logical_not(m0)
    nt_dims = (((1,), (1,)), ((), ()))

    @pl.when(flag_s[0] == 1)
    def _():
        def body(c, carry):
            r0 = pl.multiple_of(c * C, C)
            rows = pl.ds(r0, C)
            heads = [slice(h * LANES, (h + 1) * LANES) for h in range(HG_HEADS)]
            for h, sl in enumerate(heads):
                a0 = lax.dot_general(qd_s[rows, sl], kd_s[rows, sl], nt_dims, preferred_element_type=F32)
                a1 = lax.dot_general(q1_s[rows, sl], k1_s[rows, sl], nt_dims, preferred_element_type=F32)
                a2 = lax.dot_general(q2_s[rows, sl], k2_s[rows, sl], nt_dims, preferred_element_type=F32)
                a = jnp.where(m0, a0, jnp.where(m1, a1, jnp.where(causal, a2, 0.0)))
                a_s[h] = a.astype(BF16)
            for h, sl in enumerate(heads):
                o = (jnp.dot(qi_s[rows, sl], s_ref[0, h].astype(BF16), preferred_element_type=F32)
                     + jnp.dot(a_s[h], v_ref[rows, sl].astype(BF16), preferred_element_type=F32))
                finish(rows, sl, o)
            for h, sl in enumerate(heads):
                gend = g_s[pl.ds(pl.multiple_of(r0 + C - SUBLANES, SUBLANES), SUBLANES), sl][SUBLANES - 1:]
                ecol = jnp.transpose(jnp.broadcast_to(jnp.exp(gend), (HG_DK, LANES)))
                upd = lax.dot_general(ks_s[rows, sl], v_ref[rows, sl].astype(BF16), (((0,), (0,)), ((), ())),
                                      preferred_element_type=F32)
                s_ref[0, h] = ecol * s_ref[0, h] + upd
            return carry

        lax.fori_loop(0, tt // C, body, 0)

    @pl.when(flag_s[0] == 0)
    def _():
        Cs = HG_CHUNK_PROMPT

        def body(i, carry):
            r0 = pl.multiple_of(i * Cs, Cs)
            rows = pl.ds(r0, Cs)
            for h in range(HG_HEADS):
                sl = slice(h * LANES, (h + 1) * LANES)
                o, s_new = _hgrn_chunk_head(Cs, q_ref[rows, sl], k_ref[rows, sl], lf_ref[rows, sl],
                                            v_ref[rows, sl], s_ref[0, h])
                finish(rows, sl, o)
                s_ref[0, h] = s_new
            return carry

        lax.fori_loop(0, tt // Cs, body, 0)


def hgrn_rec_prompt(q, k, lf, v, g, norm_g, s_all, layer_j, batch, seq, tt=256):
    n = q.shape[0]
    nt = seq // tt
    row = pl.BlockSpec((tt, D_MODEL), lambda b, t: (b * nt + t, 0))
    s_spec = pl.BlockSpec((None, 1, HG_HEADS, HG_DK, HG_DV), lambda b, t: (layer_j, b, 0, 0, 0))
    in_specs = [row] * 5 + [pl.BlockSpec((1, D_MODEL), lambda b, t: (0, 0)),
                            pl.BlockSpec((5 * tt, tt), lambda b, t: (0, 0))]
    args = [q, k, lf, v, g, norm_g, _hgrn_chunk_masks(tt)]
    aliases = {}
    kern = _hgrn_prompt_kernel
    if s_all is not None:
        def kern(*refs):
            _hgrn_prompt_kernel(*refs[:7], *refs[8:])
        in_specs.append(pl.BlockSpec(memory_space=pl.ANY))
        args.append(s_all)
        aliases = {7: 1}
    bf_tile = pltpu.VMEM((tt, D_MODEL), BF16)
    return pl.pallas_call(
        kern,
        grid=(batch, nt),
        in_specs=in_specs,
        out_specs=[row, s_spec],
        out_shape=[jax.ShapeDtypeStruct((n, D_MODEL), F32),
                   jax.ShapeDtypeStruct((2, batch, HG_HEADS, HG_DK, HG_DV), F32)],
        scratch_shapes=[pltpu.VMEM((tt, D_MODEL), F32)] + [bf_tile] * 8 + [pltpu.VMEM((HG_HEADS, HG_CHUNK, HG_CHUNK), BF16), pltpu.SMEM((1,), I32)],
        input_output_aliases=aliases,
        compiler_params=_cparams(("arbitrary", "arbitrary")),
        name="hgrn_rec_prompt",
    )(*args)


def hgrn_rec_sample(q, k, lf, v, g, norm_g, state, o_prev, s_all, layer_j, row0, nseq, seq, sg=8):
    n = q.shape[0]
    rt = sg * seq
    base = row0 // rt
    row = pl.BlockSpec((rt, D_MODEL), lambda i: (base + i, 0))
    s_in = pl.BlockSpec((None, sg, HG_HEADS, HG_DK, HG_DV), lambda i: (layer_j, i, 0, 0, 0))
    anyspec = pl.BlockSpec(memory_space=pl.ANY)
    in_specs = [row] * 5 + [pl.BlockSpec((1, D_MODEL), lambda i: (0, 0)), s_in, anyspec]
    args = [q, k, lf, v, g, norm_g, state, o_prev]
    aliases = {7: 0}
    if s_all is not None:
        in_specs.append(anyspec)
        args.append(s_all)
        aliases[8] = 1

    def kern(*refs):
        _hgrn_rec_kernel(seq, 1, False, *refs[:7], *refs[len(args):])

    return pl.pallas_call(
        kern,
        grid=(nseq // sg,),
        in_specs=in_specs,
        out_specs=[row, s_in],
        out_shape=[jax.ShapeDtypeStruct((n, D_MODEL), F32),
                   jax.ShapeDtypeStruct((2, nseq, HG_HEADS, HG_DK, HG_DV), F32)],
        input_output_aliases=aliases,
        compiler_params=_cparams(("arbitrary",)),
        name="hgrn_rec_sample",
    )(*args)


def _proj_ln_kernel(a_ref, w_ref, x_ref, g_ref, b_ref, o_ref):
    y = jnp.dot(a_ref[...].astype(BF16), w_ref[...], preferred_element_type=F32)
    o_ref[...] = _layer_norm_rows(ALPHA * x_ref[...] + y, g_ref[...], b_ref[...])


def proj_ln(a, w_bf, x, g, b, tm=512):
    n = x.shape[0]
    row = pl.BlockSpec((tm, D_MODEL), lambda i: (i, 0))
    vec = pl.BlockSpec((1, D_MODEL), lambda i: (0, 0))
    return pl.pallas_call(
        _proj_ln_kernel,
        grid=(n // tm,),
        in_specs=[row, pl.BlockSpec((D_MODEL, D_MODEL), lambda i: (0, 0)), row, vec, vec],
        out_specs=row,
        out_shape=jax.ShapeDtypeStruct((n, D_MODEL), F32),
        compiler_params=_cparams(("arbitrary",)),
        name="proj_ln",
    )(a, w_bf, x, g, b)


def _ffn_kernel(x_ref, wg_ref, wu_ref, wd_ref, g_ref, b_ref, o_ref, acc_ref):
    f = pl.program_id(1)
    xb = x_ref[...].astype(BF16)
    gg = jnp.dot(xb, wg_ref[...], preferred_element_type=F32)
    uu = jnp.dot(xb, wu_ref[...], preferred_element_type=F32)
    h = (gg * _sigmoid(gg) * uu).astype(BF16)
    part = jnp.dot(h, wd_ref[...], preferred_element_type=F32)

    @pl.when(f == 0)
    def _():
        acc_ref[...] = part

    @pl.when(f > 0)
    def _():
        acc_ref[...] += part

    @pl.when(f == pl.num_programs(1) - 1)
    def _():
        o_ref[...] = _layer_norm_rows(ALPHA * x_ref[...] + acc_ref[...], g_ref[...], b_ref[...])


def ffn_ln(x, w_gu_bf, w_down_bf, g, b, tm=512, tf=1408):
    n = x.shape[0]
    nf = D_FF // tf
    row = pl.BlockSpec((tm, D_MODEL), lambda i, f: (i, 0))
    vec = pl.BlockSpec((1, D_MODEL), lambda i, f: (0, 0))
    return pl.pallas_call(
        _ffn_kernel,
        grid=(n // tm, nf),
        in_specs=[row,
                  pl.BlockSpec((D_MODEL, tf), lambda i, f: (0, f)),
                  pl.BlockSpec((D_MODEL, tf), lambda i, f: (0, nf + f)),
                  pl.BlockSpec((tf, D_MODEL), lambda i, f: (f, 0)),
                  vec, vec],
        out_specs=row,
        out_shape=jax.ShapeDtypeStruct((n, D_MODEL), F32),
        scratch_shapes=[pltpu.VMEM((tm, D_MODEL), F32)],
        compiler_params=_cparams(("arbitrary", "arbitrary")),
        name="ffn_ln",
    )(x, w_gu_bf, w_gu_bf, w_down_bf, g, b)


def _lru_in_kernel(x_ref, w_ref, xb_ref, gate_ref):
    xb = x_ref[...].astype(BF16)
    xb_ref[...] = jnp.dot(xb, w_ref[:, :D_MODEL], preferred_element_type=F32)
    y = jnp.dot(xb, w_ref[:, D_MODEL:], preferred_element_type=F32)
    c = math.sqrt(2.0 / math.pi)
    gate_ref[...] = 0.5 * y * (1.0 + jnp.tanh(c * (y + 0.044715 * (y * y * y))))


def lru_in(x, w_in_bf, tm=512):
    n = x.shape[0]
    row = pl.BlockSpec((tm, D_MODEL), lambda i: (i, 0))
    out = jax.ShapeDtypeStruct((n, D_MODEL), F32)
    return pl.pallas_call(
        _lru_in_kernel,
        grid=(n // tm,),
        in_specs=[row, pl.BlockSpec((D_MODEL, 2 * D_MODEL), lambda i: (0, 0))],
        out_specs=[row, row],
        out_shape=[out, out],
        compiler_params=_cparams(("arbitrary",)),
        name="lru_in",
    )(x, w_in_bf)


def _lru_rec_kernel(L, carry, *refs):
    if carry:
        (xb_ref, gate_ref, cw_ref, cb_ref, wr_ref, br_ref, wi_ref, bi_ref, lam_ref,
         hg_ref, hn_ref, cn_ref, xp_ref, hs_ref, hc_ref) = refs
    else:
        (xb_ref, gate_ref, cc_ref, h0_ref, cw_ref, cb_ref, wr_ref, br_ref, wi_ref, bi_ref, lam_ref,
         _, hg_ref, hn_ref, cn_ref, xp_ref, hs_ref) = refs
    R = xb_ref.shape[0]
    S = R // L
    PADR = SUBLANES
    t = pl.program_id(1) if carry else 0

    if carry:
        @pl.when(t == 0)
        def _():
            xp_ref[:, 0:PADR, :] = jnp.zeros((S, PADR, D_MODEL), F32)
            hc_ref[...] = jnp.zeros(hc_ref.shape, F32)

        @pl.when(t > 0)
        def _():
            xp_ref[:, 0:PADR, :] = xp_ref[:, L:L + PADR, :]
    else:
        xp_ref[:, 0:PADR, :] = cc_ref[...].reshape(S, PADR, D_MODEL)
    xp_ref[:, PADR:PADR + L, :] = xb_ref[...].reshape(S, L, D_MODEL)

    xc = cb_ref[...].reshape(1, 1, D_MODEL) + xp_ref[:, PADR:PADR + L, :] * cw_ref[CONV_W - 1:CONV_W, :].reshape(1, 1, D_MODEL)
    for j in range(CONV_W - 1):
        off = PADR - (CONV_W - 1) + j
        xc = xc + xp_ref[:, off:off + L, :] * cw_ref[j:j + 1, :].reshape(1, 1, D_MODEL)
    xc = xc.reshape(R, D_MODEL)
    cn_ref[...] = xp_ref[:, L + PADR - (CONV_W - 1):L + PADR, :].reshape(cn_ref.shape)

    assert L & (L - 1) == 0
    pos = jnp.bitwise_and(lax.broadcasted_iota(I32, (R, LRU_BW), 0), L - 1)
    for nb in range(LRU_BLOCKS):
        sl = slice(nb * LRU_BW, (nb + 1) * LRU_BW)
        xcb = xc[:, sl]
        xh = xcb.astype(BF16)
        r = _sigmoid(jnp.dot(xh, wr_ref[nb], preferred_element_type=F32) + br_ref[:, sl])
        ig = _sigmoid(jnp.dot(xh, wi_ref[nb], preferred_element_type=F32) + bi_ref[:, sl])
        z = -lam_ref[:, sl]
        softplus = jnp.maximum(z, 0.0) + jnp.log1p(jnp.exp(-jnp.abs(z)))
        log_a = (-RG_C) * r * softplus
        a = jnp.exp(log_a)
        th = jnp.tanh(-log_a)
        u = jnp.sqrt(2.0 * th / (1.0 + th)) * (ig * xcb)
        if carry:
            h0rows = jnp.where(pos == 0, hc_ref[:, sl], 0.0)
        else:
            h0rows = h0_ref[:, sl]
        u = u + a * h0rows
        d = 1
        while d < L:
            m = pos >= d
            u = jnp.where(m, a * pltpu.roll(u, d, 0) + u, u)
            a = jnp.where(m, a * pltpu.roll(a, d, 0), a)
            d *= 2
        hg_ref[:, sl] = u * gate_ref[:, sl]
        hs_ref[:, :, sl] = u.reshape(S, L, LRU_BW)

    hlast = hs_ref[:, L - 1:L, :]
    if carry:
        hc_ref[...] = hlast.reshape(1, D_MODEL)
        hn_ref[...] = hlast.reshape(hn_ref.shape)
    else:
        hn_ref[...] = hlast.reshape(hn_ref.shape)


def _lru_weight_specs(nidx):
    def c2(*_):
        return (0, 0)

    def c3(*_):
        return (0, 0, 0)
    return [pl.BlockSpec((CONV_W, D_MODEL), c2), pl.BlockSpec((1, D_MODEL), c2),
            pl.BlockSpec((LRU_BLOCKS, LRU_BW, LRU_BW), c3), pl.BlockSpec((1, D_MODEL), c2),
            pl.BlockSpec((LRU_BLOCKS, LRU_BW, LRU_BW), c3), pl.BlockSpec((1, D_MODEL), c2),
            pl.BlockSpec((1, D_MODEL), c2)]


def lru_rec_prompt(xb, gate, wts, batch, seq, tt=256):
    n = xb.shape[0]
    nt = seq // tt
    row = pl.BlockSpec((tt, D_MODEL), lambda b, t: (b * nt + t, 0))
    return pl.pallas_call(
        functools.partial(_lru_rec_kernel, tt, True),
        grid=(batch, nt),
        in_specs=[row, row] + _lru_weight_specs(2),
        out_specs=[row,
                   pl.BlockSpec((1, 1, D_MODEL), lambda b, t: (b, 0, 0)),
                   pl.BlockSpec((1, CONV_W - 1, D_MODEL), lambda b, t: (b, 0, 0))],
        out_shape=[jax.ShapeDtypeStruct((n, D_MODEL), F32),
                   jax.ShapeDtypeStruct((batch, 1, D_MODEL), F32),
                   jax.ShapeDtypeStruct((batch, CONV_W - 1, D_MODEL), F32)],
        scratch_shapes=[pltpu.VMEM((1, tt + SUBLANES, D_MODEL), F32),
                        pltpu.VMEM((1, tt, D_MODEL), F32),
                        pltpu.VMEM((1, D_MODEL), F32)],
        compiler_params=_cparams(("arbitrary", "arbitrary")),
        name="lru_rec_prompt",
    )(xb, gate, *wts)


def lru_rec_sample(xb, gate, cc_rows, h0_rows, wts, hg_prev, row0, nseq, seq, sg=32):
    n = xb.shape[0]
    rt = sg * seq
    base = row0 // rt
    row = pl.BlockSpec((rt, D_MODEL), lambda i: (base + i, 0))
    return pl.pallas_call(
        functools.partial(_lru_rec_kernel, seq, False),
        grid=(nseq // sg,),
        in_specs=[row, row,
                  pl.BlockSpec((sg * SUBLANES, D_MODEL), lambda i: (i, 0)),
                  pl.BlockSpec((rt, D_MODEL), lambda i: (i, 0))]
                 + _lru_weight_specs(1) + [pl.BlockSpec(memory_space=pl.ANY)],
        out_specs=[row,
                   pl.BlockSpec((sg, 1, D_MODEL), lambda i: (i, 0, 0)),
                   pl.BlockSpec((sg, CONV_W - 1, D_MODEL), lambda i: (i, 0, 0))],
        out_shape=[jax.ShapeDtypeStruct((n, D_MODEL), F32),
                   jax.ShapeDtypeStruct((nseq, 1, D_MODEL), F32),
                   jax.ShapeDtypeStruct((nseq, CONV_W - 1, D_MODEL), F32)],
        scratch_shapes=[pltpu.VMEM((sg, seq + SUBLANES, D_MODEL), F32),
                        pltpu.VMEM((sg, seq, D_MODEL), F32)],
        input_output_aliases={11: 0},
        compiler_params=_cparams(("arbitrary",)),
        name="lru_rec_sample",
    )(xb, gate, cc_rows, h0_rows, *wts, hg_prev)


def _router_kernel(x_ref, w_ref, mi_ref, mf_ref, cnt_ref, run_ref):
    i = pl.program_id(0)

    @pl.when(i == 0)
    def _():
        run_ref[...] = jnp.zeros(run_ref.shape, F32)

    tm = x_ref.shape[0]
    logits = jnp.dot(x_ref[...], w_ref[...], preferred_element_type=F32,
                     precision=lax.Precision.HIGHEST)
    lane_i = lax.broadcasted_iota(I32, (tm, LANES), 1)
    lane = lane_i.astype(F32)
    neg = jnp.float32(-jnp.inf)
    logits = jnp.where(lane_i < N_EXPERTS, logits, neg)
    m1 = jnp.max(logits, axis=-1, keepdims=True)
    i1 = jnp.min(jnp.where(logits == m1, lane, float(LANES)), axis=-1, keepdims=True)
    l2 = jnp.where(lane == i1, neg, logits)
    m2 = jnp.max(l2, axis=-1, keepdims=True)
    i2 = jnp.min(jnp.where(l2 == m2, lane, float(LANES)), axis=-1, keepdims=True)
    e2 = jnp.exp(m2 - m1)
    p1 = 1.0 / (1.0 + e2)
    p2 = e2 * p1

    hit1 = lane == i1
    hit2 = lane == i2
    onehot = jnp.where(hit1 | hit2, 1.0, 0.0)
    r_i = lax.broadcasted_iota(I32, (tm, tm), 0)
    c_i = lax.broadcasted_iota(I32, (tm, tm), 1)
    tri = jnp.where(c_i < r_i, 1.0, 0.0).astype(BF16)
    rank = jnp.dot(tri, onehot.astype(BF16), preferred_element_type=F32) + run_ref[0:1, :]
    r1 = jnp.sum(jnp.where(hit1, rank, 0.0), axis=-1, keepdims=True)
    r2 = jnp.sum(jnp.where(hit2, rank, 0.0), axis=-1, keepdims=True)
    total = run_ref[0:1, :] + jnp.sum(onehot, axis=0, keepdims=True)
    run_ref[...] = jnp.broadcast_to(total, run_ref.shape)
    cnt_ref[...] = jnp.broadcast_to(total, cnt_ref.shape).astype(I32)

    mi = jnp.where(lane_i == 0, i1, jnp.where(lane_i == 1, i2, 0.0))
    mi = jnp.where(lane_i == 2, r1, jnp.where(lane_i == 3, r2, mi))
    mi_ref[...] = mi.astype(I32)
    mf_ref[...] = jnp.where(lane_i == 0, p1, jnp.where(lane_i == 1, p2, 0.0))


def moe_router(x, w_router_pad, tm=512):
    n = x.shape[0]
    row = pl.BlockSpec((tm, LANES), lambda i: (i, 0))
    return pl.pallas_call(
        _router_kernel,
        grid=(n // tm,),
        in_specs=[pl.BlockSpec((tm, D_MODEL), lambda i: (i, 0)),
                  pl.BlockSpec((D_MODEL, LANES), lambda i: (0, 0))],
        out_specs=[row, row, pl.BlockSpec((SUBLANES, LANES), lambda i: (0, 0))],
        out_shape=[jax.ShapeDtypeStruct((n, LANES), I32),
                   jax.ShapeDtypeStruct((n, LANES), F32),
                   jax.ShapeDtypeStruct((SUBLANES, LANES), I32)],
        scratch_shapes=[pltpu.VMEM((SUBLANES, LANES), F32)],
        compiler_params=_cparams(("arbitrary",)),
        name="moe_router",
    )(x, w_router_pad)


def _dispatch_kernel(dest_ref, x_ref, xs_in, xs_hbm, sem):
    del xs_in
    i = pl.program_id(0)
    tm = x_ref.shape[0]

    def tok(j, c):
        t = i * tm + j
        src = x_ref.at[pl.ds(j, 1)]
        pltpu.make_async_copy(src, xs_hbm.at[pl.ds(dest_ref[2 * t], 1)], sem).start()
        pltpu.make_async_copy(src, xs_hbm.at[pl.ds(dest_ref[2 * t + 1], 1)], sem).start()
        return c

    lax.fori_loop(0, tm, tok, 0, unroll=8)
    for _ in range(2):
        pltpu.make_async_copy(x_ref, xs_hbm.at[pl.ds(0, tm)], sem).wait()


def moe_dispatch(dest, x, n_rows, tm=512):
    n = x.shape[0]
    xs0 = jnp.zeros((n_rows, D_MODEL), F32)
    anyspec = pl.BlockSpec(memory_space=pl.ANY)
    return pl.pallas_call(
        _dispatch_kernel,
        grid_spec=pltpu.PrefetchScalarGridSpec(
            num_scalar_prefetch=1, grid=(n // tm,),
            in_specs=[pl.BlockSpec((tm, D_MODEL), lambda i, d: (i, 0)), anyspec],
            out_specs=anyspec,
            scratch_shapes=[pltpu.SemaphoreType.DMA(())]),
        out_shape=jax.ShapeDtypeStruct((n_rows, D_MODEL), F32),
        input_output_aliases={2: 0},
        compiler_params=pltpu.CompilerParams(dimension_semantics=("arbitrary",),
                                             has_side_effects=True),
        name="moe_dispatch",
    )(dest, x, xs0)


def _expert_kernel(te_ref, nu_ref, xs_ref, wg_ref, wu_ref, wd_ref, ys_ref, xb_ref, acc_ref):
    i = pl.program_id(0)
    f = pl.program_id(1)
    used = i < nu_ref[0]

    @pl.when(used & (f == 0))
    def _():
        xb_ref[...] = xs_ref[...].astype(BF16)

    @pl.when(used)
    def _():
        xb = xb_ref[...]
        gg = jnp.dot(xb, wg_ref[...], preferred_element_type=F32)
        uu = jnp.dot(xb, wu_ref[...], preferred_element_type=F32)
        h = (gg * _sigmoid(gg) * uu).astype(BF16)
        part = jnp.dot(h, wd_ref[...], preferred_element_type=F32)

        @pl.when(f == 0)
        def _():
            acc_ref[...] = part

        @pl.when(f > 0)
        def _():
            acc_ref[...] += part

    @pl.when(f == pl.num_programs(1) - 1)
    def _():
        @pl.when(used)
        def _():
            ys_ref[...] = acc_ref[...]

        @pl.when(jnp.logical_not(used))
        def _():
            ys_ref[...] = jnp.zeros(ys_ref.shape, F32)


def moe_experts(tile_expert, n_used, xs, w_gu_bf, w_down_bf, layer_j, tf=1792):
    n_rows = xs.shape[0]
    tm = MOE_TILE
    nf = D_FF_EXPERT // tf

    def xrow(i, f, te, nu):
        return (jnp.minimum(i, nu[0] - 1), 0)

    def wg(i, f, te, nu):
        return (layer_j, te[i], 0, jnp.where(i < nu[0], f, nf - 1))

    def wu(i, f, te, nu):
        return (layer_j, te[i], 0, nf + jnp.where(i < nu[0], f, nf - 1))

    def wd(i, f, te, nu):
        return (layer_j, te[i], jnp.where(i < nu[0], f, nf - 1), 0)

    return pl.pallas_call(
        _expert_kernel,
        grid_spec=pltpu.PrefetchScalarGridSpec(
            num_scalar_prefetch=2, grid=(n_rows // tm, nf),
            in_specs=[pl.BlockSpec((tm, D_MODEL), xrow),
                      pl.BlockSpec((None, None, D_MODEL, tf), wg),
                      pl.BlockSpec((None, None, D_MODEL, tf), wu),
                      pl.BlockSpec((None, None, tf, D_MODEL), wd)],
            out_specs=pl.BlockSpec((tm, D_MODEL), lambda i, f, te, nu: (i, 0)),
            scratch_shapes=[pltpu.VMEM((tm, D_MODEL), BF16), pltpu.VMEM((tm, D_MODEL), F32)]),
        out_shape=jax.ShapeDtypeStruct((n_rows, D_MODEL), F32),
        compiler_params=_cparams(("arbitrary", "arbitrary")),
        name="moe_experts",
    )(tile_expert, n_used, xs, w_gu_bf, w_gu_bf, w_down_bf)


def _combine_kernel(dest_ref, ys_hbm, mf_ref, x_ref, g_ref, b_ref, o_ref, buf_ref, sem):
    i = pl.program_id(0)
    tm = x_ref.shape[0]

    def tok(j, c):
        t = i * tm + j
        pltpu.make_async_copy(ys_hbm.at[pl.ds(dest_ref[2 * t], 1)],
                              buf_ref.at[0, pl.ds(j, 1)], sem).start()
        pltpu.make_async_copy(ys_hbm.at[pl.ds(dest_ref[2 * t + 1], 1)],
                              buf_ref.at[1, pl.ds(j, 1)], sem).start()
        return c

    lax.fori_loop(0, tm, tok, 0, unroll=8)
    pltpu.make_async_copy(ys_hbm.at[pl.ds(0, 2 * tm)], buf_ref.reshape(2 * tm, D_MODEL), sem).wait()
    mf = mf_ref[...]
    y = buf_ref[0] * mf[:, 0:1] + buf_ref[1] * mf[:, 1:2]
    o_ref[...] = _layer_norm_rows(ALPHA * x_ref[...] + y, g_ref[...], b_ref[...])


def moe_combine_ln(dest, ys, mf, x, g, b, tm=256):
    n = x.shape[0]
    row = pl.BlockSpec((tm, D_MODEL), lambda i, d: (i, 0))
    vec = pl.BlockSpec((1, D_MODEL), lambda i, d: (0, 0))
    return pl.pallas_call(
        _combine_kernel,
        grid_spec=pltpu.PrefetchScalarGridSpec(
            num_scalar_prefetch=1, grid=(n // tm,),
            in_specs=[pl.BlockSpec(memory_space=pl.ANY),
                      pl.BlockSpec((tm, LANES), lambda i, d: (i, 0)),
                      row, vec, vec],
            out_specs=row,
            scratch_shapes=[pltpu.VMEM((2, tm, D_MODEL), F32), pltpu.SemaphoreType.DMA(())]),
        out_shape=jax.ShapeDtypeStruct((n, D_MODEL), F32),
        compiler_params=_cparams(("arbitrary",)),
        name="moe_combine_ln",
    )(dest, ys, mf, x, g, b)


def moe_ffn_ln(x, w_router_pad, w_gu_bf, w_down_bf, layer_j, g, b):
    n = x.shape[0]
    tm = MOE_TILE
    mi, mf, cnt = moe_router(x, w_router_pad)
    counts = cnt[0, :N_EXPERTS]
    padded = (counts + tm - 1) // tm * tm
    ends = jnp.cumsum(padded)
    starts = ends - padded
    ids = mi[:, 0:2]
    ranks = mi[:, 2:4]
    dest = (starts[ids] + ranks).reshape(-1).astype(I32)
    n_tiles = (2 * n) // tm + N_EXPERTS
    tile_start = jnp.arange(n_tiles, dtype=I32) * tm
    tile_expert = jnp.minimum(jnp.sum(tile_start[:, None] >= ends[None, :], axis=1), N_EXPERTS - 1).astype(I32)
    n_used = (ends[-1] // tm).astype(I32).reshape(1)
    tile_expert = jnp.where(tile_start < ends[-1], tile_expert, tile_expert[jnp.maximum(n_used[0] - 1, 0)])
    xs = moe_dispatch(dest, x, n_tiles * tm)
    ys = moe_experts(tile_expert, n_used, xs, w_gu_bf, w_down_bf, layer_j)
    return moe_combine_ln(dest, ys, mf, x, g, b)


def kernel(x_prompt, x_sample, state_hgrn, state_lru_h, state_lru_conv, ln_mix_g, ln_mix_b, ln_ffn_g, ln_ffn_b, w_hgrn_in, hgrn_lb_logits, hgrn_norm_g, w_hgrn_out, w_lru_in, lru_conv_w, lru_conv_b, w_lru_rgate, b_lru_rgate, w_lru_igate, b_lru_igate, lru_lambda, w_lru_out, w_ffn_gu, w_ffn_down, w_router, w_moe_gu, w_moe_down):
    bp, tp, _ = x_prompt.shape
    bs, ts, _ = x_sample.shape
    n_p = bp * tp
    x = jnp.concatenate([x_prompt.reshape(n_p, D_MODEL), x_sample.reshape(bs * ts, D_MODEL)], axis=0)

    def vec(a, i):
        return a[i].reshape(1, D_MODEL)

    w_hgrn_in_bf = w_hgrn_in.astype(BF16)
    w_hgrn_out_bf = w_hgrn_out.astype(BF16)
    w_lru_in_bf = w_lru_in.astype(BF16)
    w_lru_out_bf = w_lru_out.astype(BF16)
    w_r_bf = w_lru_rgate.astype(BF16)
    w_i_bf = w_lru_igate.astype(BF16)
    w_ffn_gu_bf = w_ffn_gu.astype(BF16)
    w_ffn_down_bf = w_ffn_down.astype(BF16)
    w_moe_gu_bf = w_moe_gu.astype(BF16)
    w_moe_down_bf = w_moe_down.astype(BF16)
    w_router_pad = jnp.pad(w_router, ((0, 0), (0, 0), (0, LANES - N_EXPERTS)))

    hg_p = hg_s = None
    h_p, h_s, c_p, c_s = [], [], [], []
    for layer in range(DEPTH):
        j = layer // 2
        if layer % 2 == 0:
            q, k, lf, v, g = hgrn_in(x, w_hgrn_in_bf[j], hgrn_lb_logits, j)
            ng = vec(hgrn_norm_g, j)
            o, hg_p = hgrn_rec_prompt(q, k, lf, v, g, ng, hg_p, j, bp, tp)
            o, hg_s = hgrn_rec_sample(q, k, lf, v, g, ng, state_hgrn, o, hg_s, j, n_p, bs, ts)
            x = proj_ln(o, w_hgrn_out_bf[j], x, vec(ln_mix_g, layer), vec(ln_mix_b, layer))
            x = ffn_ln(x, w_ffn_gu_bf[j], w_ffn_down_bf[j], vec(ln_ffn_g, layer), vec(ln_ffn_b, layer))
        else:
            xb, gate = lru_in(x, w_lru_in_bf[j])
            wts = (lru_conv_w[j], vec(lru_conv_b, j), w_r_bf[j], vec(b_lru_rgate, j),
                   w_i_bf[j], vec(b_lru_igate, j), vec(lru_lambda, j))
            hgate, hp, cp = lru_rec_prompt(xb, gate, wts, bp, tp)
            cc_rows = jnp.pad(state_lru_conv[j], ((0, 0), (SUBLANES - (CONV_W - 1), 0), (0, 0)))
            h0_rows = jnp.pad(state_lru_h[j][:, None, :], ((0, 0), (0, ts - 1), (0, 0)))
            hgate, hs, cs = lru_rec_sample(xb, gate, cc_rows.reshape(bs * SUBLANES, D_MODEL),
                                           h0_rows.reshape(bs * ts, D_MODEL), wts, hgate, n_p, bs, ts)
            h_p.append(hp.reshape(bp, D_MODEL))
            h_s.append(hs.reshape(bs, D_MODEL))
            c_p.append(cp)
            c_s.append(cs)
            x = proj_ln(hgate, w_lru_out_bf[j], x, vec(ln_mix_g, layer), vec(ln_mix_b, layer))
            x = moe_ffn_ln(x, w_router_pad[j], w_moe_gu_bf, w_moe_down_bf, j,
                           vec(ln_ffn_g, layer), vec(ln_ffn_b, layer))

    y_prompt = x[:n_p].reshape(bp, tp, D_MODEL)
    y_sample = x[n_p:].reshape(bs, ts, D_MODEL)
    return (y_prompt, y_sample, hg_p, hg_s, jnp.stack(h_p), jnp.stack(h_s), jnp.stack(c_p), jnp.stack(c_s))
```

```python
import functools
import math

import jax
import jax.numpy as jnp
from jax import lax
from jax.experimental import pallas as pl
from jax.experimental.pallas import tpu as pltpu

F32 = jnp.float32
BF16 = jnp.bfloat16
I32 = jnp.int32

D_MODEL = 1024
DEPTH = 4
HG_DK = 128
HG_HEADS = 8
HG_DV = 128
CONV_W = 4
RG_C = 8.0
LRU_BLOCKS = 4
LRU_BW = 256
D_FF = 2816
N_EXPERTS = 8
D_FF_EXPERT = 3584
ALPHA = (2 * DEPTH) ** 0.25
LN_EPS = 1e-5
F_FLOOR = 1e-20

LANES = 128
SUBLANES = 8
VMEM_LIMIT = 56 * 1024 * 1024

HG_CHUNK_PROMPT = 16
MOE_TILE = 512


def _cparams(sem):
    return pltpu.CompilerParams(dimension_semantics=sem, vmem_limit_bytes=VMEM_LIMIT)


def _sigmoid(x):
    return 1.0 / (1.0 + jnp.exp(-x))


def _gelu_tanh(y):
    c = math.sqrt(2.0 / math.pi)
    return 0.5 * y * (1.0 + jnp.tanh(c * (y + 0.044715 * (y * y * y))))


def _layer_norm_rows(z, g, b):
    mu = jnp.mean(z, axis=-1, keepdims=True)
    zc = z - mu
    var = jnp.mean(zc * zc, axis=-1, keepdims=True)
    return zc * lax.rsqrt(var + LN_EPS) * g + b


def _hgrn_in_compute(layer_j, xb, w_ref, lbl_ref, q_ref, k_ref, lf_ref, v_ref, g_ref):
    l = lbl_ref[...]
    e = jnp.exp(l - jnp.max(l, axis=0, keepdims=True))
    p = e / jnp.sum(e, axis=0, keepdims=True)
    cs = p[0:1]
    for i in range(1, layer_j + 1):
        cs = cs + p[i:i + 1]
    lb = cs - p[0:1]

    def seg(i):
        return jnp.dot(xb, w_ref[:, i * D_MODEL:(i + 1) * D_MODEL], preferred_element_type=F32)

    q = seg(0)
    q_ref[...] = q * _sigmoid(q) * (HG_DK ** -0.5)
    fp = seg(1)
    f = lb + (1.0 - lb) * _sigmoid(fp)
    lf_ref[...] = jnp.log(jnp.maximum(f, F_FLOOR))
    k_ref[...] = (1.0 - lb) * _sigmoid(-fp)
    v_ref[...] = seg(2)
    g = seg(3)
    g_ref[...] = g * _sigmoid(g)


def _hgrn_in_kernel(layer_j, x_ref, w_ref, lbl_ref, *out_refs):
    _hgrn_in_compute(layer_j, x_ref[...].astype(BF16), w_ref, lbl_ref, *out_refs)


def hgrn_in(x, w_in_bf, lb_logits, layer_j, row0, nrows, tm=256):
    base = row0 // tm
    out = jax.ShapeDtypeStruct((nrows, D_MODEL), F32)
    return pl.pallas_call(
        functools.partial(_hgrn_in_kernel, layer_j),
        grid=(nrows // tm,),
        in_specs=[pl.BlockSpec((tm, D_MODEL), lambda i: (base + i, 0)),
                  pl.BlockSpec((D_MODEL, 4 * D_MODEL), lambda i: (0, 0)),
                  pl.BlockSpec(lb_logits.shape, lambda i: (0, 0))],
        out_specs=[pl.BlockSpec((tm, D_MODEL), lambda i: (i, 0))] * 5,
        out_shape=[out] * 5,
        compiler_params=_cparams(("arbitrary",)),
        name="hgrn_in",
    )(x, w_in_bf, lb_logits)


def _hgrn_chunk_head(C, q, k, lf, v, S):
    row = lax.broadcasted_iota(I32, (C, LANES), 0)
    G = lf
    d = 1
    while d < C:
        G = G + jnp.where(row >= d, pltpu.roll(G, d, 0), 0.0)
        d *= 2
    gend = G[C - 1:C, :]
    qt = q * jnp.exp(G)
    kt = k * jnp.exp(gend - G)
    o = jnp.dot(qt.astype(BF16), S.astype(BF16), preferred_element_type=F32)

    row8 = lax.broadcasted_iota(I32, (SUBLANES, LANES), 0)
    parts = []
    for gi in range(C // SUBLANES):
        lo = gi * SUBLANES
        Gg = G[lo:lo + SUBLANES]
        qg = q[lo:lo + SUBLANES]
        acc = jnp.zeros((SUBLANES, LANES), F32)
        for s in range(lo + SUBLANES):
            dec = jnp.exp(Gg - G[s:s + 1])
            if s >= lo:
                dec = jnp.where(row8 >= (s - lo), dec, 0.0)
            w = jnp.sum(qg * dec * k[s:s + 1], axis=-1, keepdims=True)
            acc = acc + w * v[s:s + 1]
        parts.append(acc)
    o = o + (parts[0] if len(parts) == 1 else jnp.concatenate(parts, axis=0))

    ecol = jnp.transpose(jnp.broadcast_to(jnp.exp(gend), (HG_DK, LANES)))
    upd = lax.dot_general(kt.astype(BF16), v.astype(BF16), (((0,), (0,)), ((), ())),
                          preferred_element_type=F32)
    return o, ecol * S + upd


def _hgrn_rec_kernel(C, cps, zero_init, *refs):
    if zero_init:
        q_ref, k_ref, lf_ref, v_ref, g_ref, ng_ref, o_ref, s_ref = refs

        @pl.when(pl.program_id(1) == 0)
        def _():
            s_ref[...] = jnp.zeros(s_ref.shape, F32)
    else:
        q_ref, k_ref, lf_ref, v_ref, g_ref, ng_ref, s0_ref, o_ref, s_ref = refs
        s_ref[...] = s0_ref[...]

    n_chunks = q_ref.shape[0] // C

    def body(i, carry):
        r0 = pl.multiple_of(i * C, C)
        si = i // cps
        rows = pl.ds(r0, C)
        for h in range(HG_HEADS):
            sl = slice(h * LANES, (h + 1) * LANES)
            o, s_new = _hgrn_chunk_head(C, q_ref[rows, sl], k_ref[rows, sl], lf_ref[rows, sl],
                                        v_ref[rows, sl], s_ref[si, h])
            ms = jnp.mean(o * o, axis=-1, keepdims=True)
            o_ref[rows, sl] = o * lax.rsqrt(ms + LN_EPS) * ng_ref[:, sl] * g_ref[rows, sl]
            s_ref[si, h] = s_new
        return carry

    lax.fori_loop(0, n_chunks, body, 0)


HG_CHUNK = 64
HG_SUB = 16
HG_SAFE_RANGE = 60.0


def _hgrn_chunk_masks(tt):
    import numpy as np
    t = np.arange(tt)[:, None]
    s = np.arange(tt)[None, :]
    same = (t // HG_CHUNK) == (s // HG_CHUNK)
    tri = (s <= t) & same
    mats = [tri, (s > t) & same]
    for blk in (HG_SUB, 2 * HG_SUB, 4 * HG_SUB):
        bnd = (t // blk) * blk + (blk // 2 if blk > HG_SUB else 0)
        mats.append(tri.astype(np.int32) - ((s < bnd) & same).astype(np.int32))
    return jnp.asarray(np.concatenate([np.asarray(m, np.float32) for m in mats], axis=0), BF16)


def _hgrn_prompt_kernel(layer_j, x_ref, win_ref, lbl_ref, ng_ref, m_ref, wout_ref, lng_ref, lnb_ref,
                        xo_ref, s_ref,
                        q_ref, k_ref, lf_ref, v_ref, g_ref, o_ref,
                        g_s, qi_s, ks_s, qd_s, kd_s, q1_s, k1_s, q2_s, k2_s, a_s, flag_s):
    tt = x_ref.shape[0]

    @pl.when(pl.program_id(1) == 0)
    def _():
        s_ref[...] = jnp.zeros(s_ref.shape, F32)

    _hgrn_in_compute(layer_j, x_ref[...].astype(BF16), win_ref, lbl_ref, q_ref, k_ref, lf_ref, v_ref, g_ref)

    lf = lf_ref[...]
    hi = lf.astype(BF16)
    lo = (lf - hi.astype(F32)).astype(BF16)

    def cum(m):
        mm = m_ref[m * tt:(m + 1) * tt, :]
        return (jnp.dot(mm, hi, preferred_element_type=F32)
                + jnp.dot(mm, lo, preferred_element_type=F32))

    q = q_ref[...]
    k = k_ref[...]
    G = cum(0)
    g_s[...] = G
    qi_s[...] = (q * jnp.exp(G)).astype(BF16)
    ks_s[...] = (k * jnp.exp(cum(1))).astype(BF16)
    e0 = cum(2)
    qd_s[...] = (q * jnp.exp(e0)).astype(BF16)
    kd_s[...] = (k * jnp.exp(-e0)).astype(BF16)
    flag_s[0] = (jnp.max(-e0) <= HG_SAFE_RANGE).astype(I32)
    e1 = cum(3)
    q1_s[...] = (q * jnp.exp(jnp.minimum(e1, 0.0))).astype(BF16)
    k1_s[...] = (k * jnp.exp(jnp.minimum(-e1, 0.0))).astype(BF16)
    e2 = cum(4)
    q2_s[...] = (q * jnp.exp(jnp.minimum(e2, 0.0))).astype(BF16)
    k2_s[...] = (k * jnp.exp(jnp.minimum(-e2, 0.0))).astype(BF16)

    def finish(rows, sl, o):
        ms = jnp.mean(o * o, axis=-1, keepdims=True)
        o_ref[rows, sl] = o * lax.rsqrt(ms + LN_EPS) * ng_ref[:, sl] * g_ref[rows, sl]

    C = HG_CHUNK
    ti = lax.broadcasted_iota(I32, (C, C), 0)
    si = lax.broadcasted_iota(I32, (C, C), 1)
    causal = si <= ti
    sub_bits = HG_SUB.bit_length() - 1
    m0 = causal & ((ti >> sub_bits) == (si >> sub_bits))
    m1 = causal & ((ti >> (sub_bits + 1)) == (si >> (sub_bits + 1))) & jnp.logical_not(m0)
    nt_dims = (((1,), (1,)), ((), ()))

    @pl.when(flag_s[0] == 1)
    def _():
        def body(c, carry):
            r0 = pl.multiple_of(c * C, C)
            rows = pl.ds(r0, C)
            heads = [slice(h * LANES, (h + 1) * LANES) for h in range(HG_HEADS)]
            for h, sl in enumerate(heads):
                a0 = lax.dot_general(qd_s[rows, sl], kd_s[rows, sl], nt_dims, preferred_element_type=F32)
                a1 = lax.dot_general(q1_s[rows, sl], k1_s[rows, sl], nt_dims, preferred_element_type=F32)
                a2 = lax.dot_general(q2_s[rows, sl], k2_s[rows, sl], nt_dims, preferred_element_type=F32)
                a = jnp.where(m0, a0, jnp.where(m1, a1, jnp.where(causal, a2, 0.0)))
                a_s[h] = a.astype(BF16)
            for h, sl in enumerate(heads):
                o = (jnp.dot(qi_s[rows, sl], s_ref[0, h].astype(BF16), preferred_element_type=F32)
                     + jnp.dot(a_s[h], v_ref[rows, sl].astype(BF16), preferred_element_type=F32))
                finish(rows, sl, o)
            for h, sl in enumerate(heads):
                gend = g_s[pl.ds(pl.multiple_of(r0 + C - SUBLANES, SUBLANES), SUBLANES), sl][SUBLANES - 1:]
                ecol = jnp.transpose(jnp.broadcast_to(jnp.exp(gend), (HG_DK, LANES)))
                upd = lax.dot_general(ks_s[rows, sl], v_ref[rows, sl].astype(BF16), (((0,), (0,)), ((), ())),
                                      preferred_element_type=F32)
                s_ref[0, h] = ecol * s_ref[0, h] + upd
            return carry

        lax.fori_loop(0, tt // C, body, 0)

    @pl.when(flag_s[0] == 0)
    def _():
        Cs = HG_CHUNK_PROMPT

        def body(i, carry):
            r0 = pl.multiple_of(i * Cs, Cs)
            rows = pl.ds(r0, Cs)
            for h in range(HG_HEADS):
                sl = slice(h * LANES, (h + 1) * LANES)
                o, s_new = _hgrn_chunk_head(Cs, q_ref[rows, sl], k_ref[rows, sl], lf_ref[rows, sl],
                                            v_ref[rows, sl], s_ref[0, h])
                finish(rows, sl, o)
                s_ref[0, h] = s_new
            return carry

        lax.fori_loop(0, tt // Cs, body, 0)

    y = jnp.dot(o_ref[...].astype(BF16), wout_ref[...], preferred_element_type=F32)
    xo_ref[...] = _layer_norm_rows(ALPHA * x_ref[...] + y, lng_ref[...], lnb_ref[...])


def hgrn_prompt_layer(x, w_in_bf, lb_logits, norm_g, w_out_bf, ln_g, ln_b, s_all, layer_j, batch, seq, tt=256):
    n = x.shape[0]
    nt = seq // tt
    row = pl.BlockSpec((tt, D_MODEL), lambda b, t: (b * nt + t, 0))
    vec = pl.BlockSpec((1, D_MODEL), lambda b, t: (0, 0))
    s_spec = pl.BlockSpec((None, 1, HG_HEADS, HG_DK, HG_DV), lambda b, t: (layer_j, b, 0, 0, 0))
    in_specs = [row,
                pl.BlockSpec((D_MODEL, 4 * D_MODEL), lambda b, t: (0, 0)),
                pl.BlockSpec(lb_logits.shape, lambda b, t: (0, 0)),
                vec,
                pl.BlockSpec((5 * tt, tt), lambda b, t: (0, 0)),
                pl.BlockSpec((D_MODEL, D_MODEL), lambda b, t: (0, 0)),
                vec, vec]
    args = [x, w_in_bf, lb_logits, norm_g, _hgrn_chunk_masks(tt), w_out_bf, ln_g, ln_b]
    aliases = {}
    kern = functools.partial(_hgrn_prompt_kernel, layer_j)
    if s_all is not None:
        def kern(*refs):
            _hgrn_prompt_kernel(layer_j, *refs[:8], *refs[9:])
        in_specs.append(pl.BlockSpec(memory_space=pl.ANY))
        args.append(s_all)
        aliases = {8: 1}
    f32_tile = pltpu.VMEM((tt, D_MODEL), F32)
    bf_tile = pltpu.VMEM((tt, D_MODEL), BF16)
    return pl.pallas_call(
        kern,
        grid=(batch, nt),
        in_specs=in_specs,
        out_specs=[row, s_spec],
        out_shape=[jax.ShapeDtypeStruct((n, D_MODEL), F32),
                   jax.ShapeDtypeStruct((2, batch, HG_HEADS, HG_DK, HG_DV), F32)],
        scratch_shapes=[f32_tile] * 7 + [bf_tile] * 8
                       + [pltpu.VMEM((HG_HEADS, HG_CHUNK, HG_CHUNK), BF16), pltpu.SMEM((1,), I32)],
        input_output_aliases=aliases,
        compiler_params=_cparams(("arbitrary", "arbitrary")),
        name="hgrn_prompt_layer",
    )(*args)


def hgrn_rec_sample(q, k, lf, v, g, norm_g, state, s_all, layer_j, nseq, seq, sg=8):
    n = q.shape[0]
    rt = sg * seq
    row = pl.BlockSpec((rt, D_MODEL), lambda i: (i, 0))
    s_in = pl.BlockSpec((None, sg, HG_HEADS, HG_DK, HG_DV), lambda i: (layer_j, i, 0, 0, 0))
    in_specs = [row] * 5 + [pl.BlockSpec((1, D_MODEL), lambda i: (0, 0)), s_in]
    args = [q, k, lf, v, g, norm_g, state]
    aliases = {}
    if s_all is not None:
        in_specs.append(pl.BlockSpec(memory_space=pl.ANY))
        args.append(s_all)
        aliases[7] = 1

    def kern(*refs):
        _hgrn_rec_kernel(seq, 1, False, *refs[:7], *refs[len(args):])

    return pl.pallas_call(
        kern,
        grid=(nseq // sg,),
        in_specs=in_specs,
        out_specs=[row, s_in],
        out_shape=[jax.ShapeDtypeStruct((n, D_MODEL), F32),
                   jax.ShapeDtypeStruct((2, nseq, HG_HEADS, HG_DK, HG_DV), F32)],
        input_output_aliases=aliases,
        compiler_params=_cparams(("arbitrary",)),
        name="hgrn_rec_sample",
    )(*args)


def _proj_ln_kernel(a_ref, w_ref, x_ref, g_ref, b_ref, o_ref):
    y = jnp.dot(a_ref[...].astype(BF16), w_ref[...], preferred_element_type=F32)
    o_ref[...] = _layer_norm_rows(ALPHA * x_ref[...] + y, g_ref[...], b_ref[...])


def proj_ln(a, w_bf, x, g, b, xo, row0, tm=512):
    n = x.shape[0]
    nrows = a.shape[0]
    base = row0 // tm
    row = pl.BlockSpec((tm, D_MODEL), lambda i: (base + i, 0))
    vec = pl.BlockSpec((1, D_MODEL), lambda i: (0, 0))

    def kern(a_ref, w_ref, x_ref, g_ref, b_ref, _, o_ref):
        _proj_ln_kernel(a_ref, w_ref, x_ref, g_ref, b_ref, o_ref)

    return pl.pallas_call(
        kern,
        grid=(nrows // tm,),
        in_specs=[pl.BlockSpec((tm, D_MODEL), lambda i: (i, 0)),
                  pl.BlockSpec((D_MODEL, D_MODEL), lambda i: (0, 0)), row, vec, vec,
                  pl.BlockSpec(memory_space=pl.ANY)],
        out_specs=row,
        out_shape=jax.ShapeDtypeStruct((n, D_MODEL), F32),
        input_output_aliases={5: 0},
        compiler_params=_cparams(("arbitrary",)),
        name="proj_ln",
    )(a, w_bf, x, g, b, xo)


def _ffn_kernel(x_ref, wg_ref, wu_ref, wd_ref, g_ref, b_ref, o_ref, acc_ref):
    f = pl.program_id(1)
    xb = x_ref[...].astype(BF16)
    gg = jnp.dot(xb, wg_ref[...], preferred_element_type=F32)
    uu = jnp.dot(xb, wu_ref[...], preferred_element_type=F32)
    h = (gg * _sigmoid(gg) * uu).astype(BF16)
    part = jnp.dot(h, wd_ref[...], preferred_element_type=F32)

    @pl.when(f == 0)
    def _():
        acc_ref[...] = part

    @pl.when(f > 0)
    def _():
        acc_ref[...] += part

    @pl.when(f == pl.num_programs(1) - 1)
    def _():
        o_ref[...] = _layer_norm_rows(ALPHA * x_ref[...] + acc_ref[...], g_ref[...], b_ref[...])


def ffn_ln(x, w_gu_bf, w_down_bf, g, b, tm=512, tf=1408):
    n = x.shape[0]
    nf = D_FF // tf
    row = pl.BlockSpec((tm, D_MODEL), lambda i, f: (i, 0))
    vec = pl.BlockSpec((1, D_MODEL), lambda i, f: (0, 0))
    return pl.pallas_call(
        _ffn_kernel,
        grid=(n // tm, nf),
        in_specs=[row,
                  pl.BlockSpec((D_MODEL, tf), lambda i, f: (0, f)),
                  pl.BlockSpec((D_MODEL, tf), lambda i, f: (0, nf + f)),
                  pl.BlockSpec((tf, D_MODEL), lambda i, f: (f, 0)),
                  vec, vec],
        out_specs=row,
        out_shape=jax.ShapeDtypeStruct((n, D_MODEL), F32),
        scratch_shapes=[pltpu.VMEM((tm, D_MODEL), F32)],
        compiler_params=_cparams(("arbitrary", "arbitrary")),
        name="ffn_ln",
    )(x, w_gu_bf, w_gu_bf, w_down_bf, g, b)


def _lru_in_kernel(x_ref, w_ref, xb_ref, gate_ref):
    xb = x_ref[...].astype(BF16)
    xb_ref[...] = jnp.dot(xb, w_ref[:, :D_MODEL], preferred_element_type=F32)
    y = jnp.dot(xb, w_ref[:, D_MODEL:], preferred_element_type=F32)
    gate_ref[...] = _gelu_tanh(y)


def lru_in(x, w_in_bf, row0, nrows, tm=512):
    base = row0 // tm
    row = pl.BlockSpec((tm, D_MODEL), lambda i: (i, 0))
    out = jax.ShapeDtypeStruct((nrows, D_MODEL), F32)
    return pl.pallas_call(
        _lru_in_kernel,
        grid=(nrows // tm,),
        in_specs=[pl.BlockSpec((tm, D_MODEL), lambda i: (base + i, 0)),
                  pl.BlockSpec((D_MODEL, 2 * D_MODEL), lambda i: (0, 0))],
        out_specs=[row, row],
        out_shape=[out, out],
        compiler_params=_cparams(("arbitrary",)),
        name="lru_in",
    )(x, w_in_bf)


def _lru_rec_kernel(L, carry, *refs):
    if carry:
        (x_ref, win_ref, cw_ref, cb_ref, wr_ref, br_ref, wi_ref, bi_ref, lam_ref, wout_ref, lng_ref, lnb_ref,
         xo_ref, hn_ref, cn_ref, xp_ref, hs_ref, hc_ref, gate_ref, hg_ref) = refs
        R = x_ref.shape[0]
    else:
        (xb_ref, gate_ref, cc_ref, h0_ref, cw_ref, cb_ref, wr_ref, br_ref, wi_ref, bi_ref, lam_ref,
         hg_ref, hn_ref, cn_ref, xp_ref, hs_ref) = refs
        R = xb_ref.shape[0]
    S = R // L
    PADR = SUBLANES
    t = pl.program_id(1) if carry else 0

    if carry:
        @pl.when(t == 0)
        def _():
            xp_ref[:, 0:PADR, :] = jnp.zeros((S, PADR, D_MODEL), F32)
            hc_ref[...] = jnp.zeros(hc_ref.shape, F32)

        @pl.when(t > 0)
        def _():
            xp_ref[:, 0:PADR, :] = xp_ref[:, L:L + PADR, :]

        xin = x_ref[...].astype(BF16)
        xp_ref[:, PADR:PADR + L, :] = jnp.dot(xin, win_ref[:, :D_MODEL],
                                              preferred_element_type=F32).reshape(S, L, D_MODEL)
        gate_ref[...] = _gelu_tanh(jnp.dot(xin, win_ref[:, D_MODEL:], preferred_element_type=F32))
    else:
        xp_ref[:, 0:PADR, :] = cc_ref[...].reshape(S, PADR, D_MODEL)
        xp_ref[:, PADR:PADR + L, :] = xb_ref[...].reshape(S, L, D_MODEL)

    xc = cb_ref[...].reshape(1, 1, D_MODEL) + xp_ref[:, PADR:PADR + L, :] * cw_ref[CONV_W - 1:CONV_W, :].reshape(1, 1, D_MODEL)
    for j in range(CONV_W - 1):
        off = PADR - (CONV_W - 1) + j
        xc = xc + xp_ref[:, off:off + L, :] * cw_ref[j:j + 1, :].reshape(1, 1, D_MODEL)
    xc = xc.reshape(R, D_MODEL)
    cn_ref[...] = xp_ref[:, L + PADR - (CONV_W - 1):L + PADR, :].reshape(cn_ref.shape)

    assert L & (L - 1) == 0
    pos = jnp.bitwise_and(lax.broadcasted_iota(I32, (R, LRU_BW), 0), L - 1)
    for nb in range(LRU_BLOCKS):
        sl = slice(nb * LRU_BW, (nb + 1) * LRU_BW)
        xcb = xc[:, sl]
        xh = xcb.astype(BF16)
        r = _sigmoid(jnp.dot(xh, wr_ref[nb], preferred_element_type=F32) + br_ref[:, sl])
        ig = _sigmoid(jnp.dot(xh, wi_ref[nb], preferred_element_type=F32) + bi_ref[:, sl])
        z = -lam_ref[:, sl]
        softplus = jnp.maximum(z, 0.0) + jnp.log1p(jnp.exp(-jnp.abs(z)))
        log_a = (-RG_C) * r * softplus
        a = jnp.exp(log_a)
        th = jnp.tanh(-log_a)
        u = jnp.sqrt(2.0 * th / (1.0 + th)) * (ig * xcb)
        if carry:
            h0rows = jnp.where(pos == 0, hc_ref[:, sl], 0.0)
        else:
            h0rows = h0_ref[:, sl]
        u = u + a * h0rows
        d = 1
        while d < L:
            m = pos >= d
            u = jnp.where(m, a * pltpu.roll(u, d, 0) + u, u)
            a = jnp.where(m, a * pltpu.roll(a, d, 0), a)
            d *= 2
        hg_ref[:, sl] = u * gate_ref[:, sl]
        hs_ref[:, :, sl] = u.reshape(S, L, LRU_BW)

    hlast = hs_ref[:, L - 1:L, :]
    hn_ref[...] = hlast.reshape(hn_ref.shape)
    if carry:
        hc_ref[...] = hlast.reshape(1, D_MODEL)
        y = jnp.dot(hg_ref[...].astype(BF16), wout_ref[...], preferred_element_type=F32)
        xo_ref[...] = _layer_norm_rows(ALPHA * x_ref[...] + y, lng_ref[...], lnb_ref[...])


def _lru_weight_specs(nidx):
    def c2(*_):
        return (0, 0)

    def c3(*_):
        return (0, 0, 0)
    return [pl.BlockSpec((CONV_W, D_MODEL), c2), pl.BlockSpec((1, D_MODEL), c2),
            pl.BlockSpec((LRU_BLOCKS, LRU_BW, LRU_BW), c3), pl.BlockSpec((1, D_MODEL), c2),
            pl.BlockSpec((LRU_BLOCKS, LRU_BW, LRU_BW), c3), pl.BlockSpec((1, D_MODEL), c2),
            pl.BlockSpec((1, D_MODEL), c2)]


def lru_prompt_layer(x, w_in_bf, wts, w_out_bf, ln_g, ln_b, batch, seq, tt=256):
    n = x.shape[0]
    nt = seq // tt
    row = pl.BlockSpec((tt, D_MODEL), lambda b, t: (b * nt + t, 0))
    vec = pl.BlockSpec((1, D_MODEL), lambda b, t: (0, 0))
    return pl.pallas_call(
        functools.partial(_lru_rec_kernel, tt, True),
        grid=(batch, nt),
        in_specs=[row, pl.BlockSpec((D_MODEL, 2 * D_MODEL), lambda b, t: (0, 0))] + _lru_weight_specs(2)
                 + [pl.BlockSpec((D_MODEL, D_MODEL), lambda b, t: (0, 0)), vec, vec],
        out_specs=[row,
                   pl.BlockSpec((1, 1, D_MODEL), lambda b, t: (b, 0, 0)),
                   pl.BlockSpec((1, CONV_W - 1, D_MODEL), lambda b, t: (b, 0, 0))],
        out_shape=[jax.ShapeDtypeStruct((n, D_MODEL), F32),
                   jax.ShapeDtypeStruct((batch, 1, D_MODEL), F32),
                   jax.ShapeDtypeStruct((batch, CONV_W - 1, D_MODEL), F32)],
        scratch_shapes=[pltpu.VMEM((1, tt + SUBLANES, D_MODEL), F32),
                        pltpu.VMEM((1, tt, D_MODEL), F32),
                        pltpu.VMEM((1, D_MODEL), F32),
                        pltpu.VMEM((tt, D_MODEL), F32),
                        pltpu.VMEM((tt, D_MODEL), F32)],
        compiler_params=_cparams(("arbitrary", "arbitrary")),
        name="lru_prompt_layer",
    )(x, w_in_bf, *wts, w_out_bf, ln_g, ln_b)


def lru_rec_sample(xb, gate, cc_rows, h0_rows, wts, nseq, seq, sg=32):
    n = xb.shape[0]
    rt = sg * seq
    row = pl.BlockSpec((rt, D_MODEL), lambda i: (i, 0))
    return pl.pallas_call(
        functools.partial(_lru_rec_kernel, seq, False),
        grid=(nseq // sg,),
        in_specs=[row, row, pl.BlockSpec((sg * SUBLANES, D_MODEL), lambda i: (i, 0)), row]
                 + _lru_weight_specs(1),
        out_specs=[row,
                   pl.BlockSpec((sg, 1, D_MODEL), lambda i: (i, 0, 0)),
                   pl.BlockSpec((sg, CONV_W - 1, D_MODEL), lambda i: (i, 0, 0))],
        out_shape=[jax.ShapeDtypeStruct((n, D_MODEL), F32),
                   jax.ShapeDtypeStruct((nseq, 1, D_MODEL), F32),
                   jax.ShapeDtypeStruct((nseq, CONV_W - 1, D_MODEL), F32)],
        scratch_shapes=[pltpu.VMEM((sg, seq + SUBLANES, D_MODEL), F32),
                        pltpu.VMEM((sg, seq, D_MODEL), F32)],
        compiler_params=_cparams(("arbitrary",)),
        name="lru_rec_sample",
    )(xb, gate, cc_rows, h0_rows, *wts)


def _router_kernel(x_ref, w_ref, mi_ref, mf_ref, cnt_ref, run_ref):
    i = pl.program_id(0)

    @pl.when(i == 0)
    def _():
        run_ref[...] = jnp.zeros(run_ref.shape, F32)

    tm = x_ref.shape[0]
    logits = jnp.dot(x_ref[...], w_ref[...], preferred_element_type=F32,
                     precision=lax.Precision.HIGHEST)
    lane_i = lax.broadcasted_iota(I32, (tm, LANES), 1)
    lane = lane_i.astype(F32)
    neg = jnp.float32(-jnp.inf)
    logits = jnp.where(lane_i < N_EXPERTS, logits, neg)
    m1 = jnp.max(logits, axis=-1, keepdims=True)
    i1 = jnp.min(jnp.where(logits == m1, lane, float(LANES)), axis=-1, keepdims=True)
    l2 = jnp.where(lane == i1, neg, logits)
    m2 = jnp.max(l2, axis=-1, keepdims=True)
    i2 = jnp.min(jnp.where(l2 == m2, lane, float(LANES)), axis=-1, keepdims=True)
    e2 = jnp.exp(m2 - m1)
    p1 = 1.0 / (1.0 + e2)
    p2 = e2 * p1

    hit1 = lane == i1
    hit2 = lane == i2
    onehot = jnp.where(hit1 | hit2, 1.0, 0.0)
    r_i = lax.broadcasted_iota(I32, (tm, tm), 0)
    c_i = lax.broadcasted_iota(I32, (tm, tm), 1)
    tri = jnp.where(c_i < r_i, 1.0, 0.0).astype(BF16)
    rank = jnp.dot(tri, onehot.astype(BF16), preferred_element_type=F32) + run_ref[0:1, :]
    r1 = jnp.sum(jnp.where(hit1, rank, 0.0), axis=-1, keepdims=True)
    r2 = jnp.sum(jnp.where(hit2, rank, 0.0), axis=-1, keepdims=True)
    total = run_ref[0:1, :] + jnp.sum(onehot, axis=0, keepdims=True)
    run_ref[...] = jnp.broadcast_to(total, run_ref.shape)
    cnt_ref[...] = jnp.broadcast_to(total, cnt_ref.shape).astype(I32)

    mi = jnp.where(lane_i == 0, i1, jnp.where(lane_i == 1, i2, 0.0))
    mi = jnp.where(lane_i == 2, r1, jnp.where(lane_i == 3, r2, mi))
    mi_ref[...] = mi.astype(I32)
    mf_ref[...] = jnp.where(lane_i == 0, p1, jnp.where(lane_i == 1, p2, 0.0))


def moe_router(x, w_router_pad, tm=512):
    n = x.shape[0]
    row = pl.BlockSpec((tm, LANES), lambda i: (i, 0))
    return pl.pallas_call(
        _router_kernel,
        grid=(n // tm,),
        in_specs=[pl.BlockSpec((tm, D_MODEL), lambda i: (i, 0)),
                  pl.BlockSpec((D_MODEL, LANES), lambda i: (0, 0))],
        out_specs=[row, row, pl.BlockSpec((SUBLANES, LANES), lambda i: (0, 0))],
        out_shape=[jax.ShapeDtypeStruct((n, LANES), I32),
                   jax.ShapeDtypeStruct((n, LANES), F32),
                   jax.ShapeDtypeStruct((SUBLANES, LANES), I32)],
        scratch_shapes=[pltpu.VMEM((SUBLANES, LANES), F32)],
        compiler_params=_cparams(("arbitrary",)),
        name="moe_router",
    )(x, w_router_pad)


def _dispatch_kernel(dest_ref, x_ref, xs_in, xs_hbm, sem):
    del xs_in
    i = pl.program_id(0)
    tm = x_ref.shape[0]

    def tok(j, c):
        t = i * tm + j
        src = x_ref.at[pl.ds(j, 1)]
        pltpu.make_async_copy(src, xs_hbm.at[pl.ds(dest_ref[2 * t], 1)], sem).start()
        pltpu.make_async_copy(src, xs_hbm.at[pl.ds(dest_ref[2 * t + 1], 1)], sem).start()
        return c

    lax.fori_loop(0, tm, tok, 0, unroll=8)
    for _ in range(2):
        pltpu.make_async_copy(x_ref, xs_hbm.at[pl.ds(0, tm)], sem).wait()


def moe_dispatch(dest, x, n_rows, tm=512):
    n = x.shape[0]
    xs0 = jnp.zeros((n_rows, D_MODEL), F32)
    anyspec = pl.BlockSpec(memory_space=pl.ANY)
    return pl.pallas_call(
        _dispatch_kernel,
        grid_spec=pltpu.PrefetchScalarGridSpec(
            num_scalar_prefetch=1, grid=(n // tm,),
            in_specs=[pl.BlockSpec((tm, D_MODEL), lambda i, d: (i, 0)), anyspec],
            out_specs=anyspec,
            scratch_shapes=[pltpu.SemaphoreType.DMA(())]),
        out_shape=jax.ShapeDtypeStruct((n_rows, D_MODEL), F32),
        input_output_aliases={2: 0},
        compiler_params=pltpu.CompilerParams(dimension_semantics=("arbitrary",),
                                             has_side_effects=True),
        name="moe_dispatch",
    )(dest, x, xs0)


def _expert_kernel(te_ref, nu_ref, xs_ref, wg_ref, wu_ref, wd_ref, ys_ref, xb_ref, acc_ref):
    i = pl.program_id(0)
    f = pl.program_id(1)
    used = i < nu_ref[0]

    @pl.when(used & (f == 0))
    def _():
        xb_ref[...] = xs_ref[...].astype(BF16)

    @pl.when(used)
    def _():
        xb = xb_ref[...]
        gg = jnp.dot(xb, wg_ref[...], preferred_element_type=F32)
        uu = jnp.dot(xb, wu_ref[...], preferred_element_type=F32)
        h = (gg * _sigmoid(gg) * uu).astype(BF16)
        part = jnp.dot(h, wd_ref[...], preferred_element_type=F32)

        @pl.when(f == 0)
        def _():
            acc_ref[...] = part

        @pl.when(f > 0)
        def _():
            acc_ref[...] += part

    @pl.when(f == pl.num_programs(1) - 1)
    def _():
        @pl.when(used)
        def _():
            ys_ref[...] = acc_ref[...]

        @pl.when(jnp.logical_not(used))
        def _():
            ys_ref[...] = jnp.zeros(ys_ref.shape, F32)


def moe_experts(tile_expert, n_used, xs, w_gu_bf, w_down_bf, layer_j, tf=1792):
    n_rows = xs.shape[0]
    tm = MOE_TILE
    nf = D_FF_EXPERT // tf

    def xrow(i, f, te, nu):
        return (jnp.minimum(i, nu[0] - 1), 0)

    def wg(i, f, te, nu):
        return (layer_j, te[i], 0, jnp.where(i < nu[0], f, nf - 1))

    def wu(i, f, te, nu):
        return (layer_j, te[i], 0, nf + jnp.where(i < nu[0], f, nf - 1))

    def wd(i, f, te, nu):
        return (layer_j, te[i], jnp.where(i < nu[0], f, nf - 1), 0)

    return pl.pallas_call(
        _expert_kernel,
        grid_spec=pltpu.PrefetchScalarGridSpec(
            num_scalar_prefetch=2, grid=(n_rows // tm, nf),
            in_specs=[pl.BlockSpec((tm, D_MODEL), xrow),
                      pl.BlockSpec((None, None, D_MODEL, tf), wg),
                      pl.BlockSpec((None, None, D_MODEL, tf), wu),
                      pl.BlockSpec((None, None, tf, D_MODEL), wd)],
            out_specs=pl.BlockSpec((tm, D_MODEL), lambda i, f, te, nu: (i, 0)),
            scratch_shapes=[pltpu.VMEM((tm, D_MODEL), BF16), pltpu.VMEM((tm, D_MODEL), F32)]),
        out_shape=jax.ShapeDtypeStruct((n_rows, D_MODEL), F32),
        compiler_params=_cparams(("arbitrary", "arbitrary")),
        name="moe_experts",
    )(tile_expert, n_used, xs, w_gu_bf, w_gu_bf, w_down_bf)


def _combine_kernel(dest_ref, ys_hbm, mf_ref, x_ref, g_ref, b_ref, o_ref, buf_ref, sem):
    i = pl.program_id(0)
    tm = x_ref.shape[0]

    def tok(j, c):
        t = i * tm + j
        pltpu.make_async_copy(ys_hbm.at[pl.ds(dest_ref[2 * t], 1)],
                              buf_ref.at[0, pl.ds(j, 1)], sem).start()
        pltpu.make_async_copy(ys_hbm.at[pl.ds(dest_ref[2 * t + 1], 1)],
                              buf_ref.at[1, pl.ds(j, 1)], sem).start()
        return c

    lax.fori_loop(0, tm, tok, 0, unroll=8)
    pltpu.make_async_copy(ys_hbm.at[pl.ds(0, 2 * tm)], buf_ref.reshape(2 * tm, D_MODEL), sem).wait()
    mf = mf_ref[...]
    y = buf_ref[0] * mf[:, 0:1] + buf_ref[1] * mf[:, 1:2]
    o_ref[...] = _layer_norm_rows(ALPHA * x_ref[...] + y, g_ref[...], b_ref[...])


def moe_combine_ln(dest, ys, mf, x, g, b, tm=256):
    n = x.shape[0]
    row = pl.BlockSpec((tm, D_MODEL), lambda i, d: (i, 0))
    vec = pl.BlockSpec((1, D_MODEL), lambda i, d: (0, 0))
    return pl.pallas_call(
        _combine_kernel,
        grid_spec=pltpu.PrefetchScalarGridSpec(
            num_scalar_prefetch=1, grid=(n // tm,),
            in_specs=[pl.BlockSpec(memory_space=pl.ANY),
                      pl.BlockSpec((tm, LANES), lambda i, d: (i, 0)),
                      row, vec, vec],
            out_specs=row,
            scratch_shapes=[pltpu.VMEM((2, tm, D_MODEL), F32), pltpu.SemaphoreType.DMA(())]),
        out_shape=jax.ShapeDtypeStruct((n, D_MODEL), F32),
        compiler_params=_cparams(("arbitrary",)),
        name="moe_combine_ln",
    )(dest, ys, mf, x, g, b)


def moe_ffn_ln(x, w_router_pad, w_gu_bf, w_down_bf, layer_j, g, b):
    n = x.shape[0]
    tm = MOE_TILE
    mi, mf, cnt = moe_router(x, w_router_pad)
    counts = cnt[0, :N_EXPERTS]
    padded = (counts + tm - 1) // tm * tm
    ends = jnp.cumsum(padded)
    starts = ends - padded
    ids = mi[:, 0:2]
    ranks = mi[:, 2:4]
    dest = (starts[ids] + ranks).reshape(-1).astype(I32)
    n_tiles = (2 * n) // tm + N_EXPERTS
    tile_start = jnp.arange(n_tiles, dtype=I32) * tm
    tile_expert = jnp.minimum(jnp.sum(tile_start[:, None] >= ends[None, :], axis=1), N_EXPERTS - 1).astype(I32)
    n_used = (ends[-1] // tm).astype(I32).reshape(1)
    tile_expert = jnp.where(tile_start < ends[-1], tile_expert, tile_expert[jnp.maximum(n_used[0] - 1, 0)])
    xs = moe_dispatch(dest, x, n_tiles * tm)
    ys = moe_experts(tile_expert, n_used, xs, w_gu_bf, w_down_bf, layer_j)
    return moe_combine_ln(dest, ys, mf, x, g, b)


def kernel(x_prompt, x_sample, state_hgrn, state_lru_h, state_lru_conv, ln_mix_g, ln_mix_b, ln_ffn_g, ln_ffn_b, w_hgrn_in, hgrn_lb_logits, hgrn_norm_g, w_hgrn_out, w_lru_in, lru_conv_w, lru_conv_b, w_lru_rgate, b_lru_rgate, w_lru_igate, b_lru_igate, lru_lambda, w_lru_out, w_ffn_gu, w_ffn_down, w_router, w_moe_gu, w_moe_down):
    bp, tp, _ = x_prompt.shape
    bs, ts, _ = x_sample.shape
    n_p = bp * tp
    x = jnp.concatenate([x_prompt.reshape(n_p, D_MODEL), x_sample.reshape(bs * ts, D_MODEL)], axis=0)

    def vec(a, i):
        return a[i].reshape(1, D_MODEL)

    w_hgrn_in_bf = w_hgrn_in.astype(BF16)
    w_hgrn_out_bf = w_hgrn_out.astype(BF16)
    w_lru_in_bf = w_lru_in.astype(BF16)
    w_lru_out_bf = w_lru_out.astype(BF16)
    w_r_bf = w_lru_rgate.astype(BF16)
    w_i_bf = w_lru_igate.astype(BF16)
    w_ffn_gu_bf = w_ffn_gu.astype(BF16)
    w_ffn_down_bf = w_ffn_down.astype(BF16)
    w_moe_gu_bf = w_moe_gu.astype(BF16)
    w_moe_down_bf = w_moe_down.astype(BF16)
    w_router_pad = jnp.pad(w_router, ((0, 0), (0, 0), (0, LANES - N_EXPERTS)))

    hg_p = hg_s = None
    h_p, h_s, c_p, c_s = [], [], [], []
    for layer in range(DEPTH):
        j = layer // 2
        lng, lnb = vec(ln_mix_g, layer), vec(ln_mix_b, layer)
        if layer % 2 == 0:
            ng = vec(hgrn_norm_g, j)
            xo, hg_p = hgrn_prompt_layer(x, w_hgrn_in_bf[j], hgrn_lb_logits, ng, w_hgrn_out_bf[j],
                                         lng, lnb, hg_p, j, bp, tp)
            q, k, lf, v, g = hgrn_in(x, w_hgrn_in_bf[j], hgrn_lb_logits, j, n_p, bs * ts)
            o, hg_s = hgrn_rec_sample(q, k, lf, v, g, ng, state_hgrn, hg_s, j, bs, ts)
            x = proj_ln(o, w_hgrn_out_bf[j], x, lng, lnb, xo, n_p)
            x = ffn_ln(x, w_ffn_gu_bf[j], w_ffn_down_bf[j], vec(ln_ffn_g, layer), vec(ln_ffn_b, layer))
        else:
            wts = (lru_conv_w[j], vec(lru_conv_b, j), w_r_bf[j], vec(b_lru_rgate, j),
                   w_i_bf[j], vec(b_lru_igate, j), vec(lru_lambda, j))
            xo, hp, cp = lru_prompt_layer(x, w_lru_in_bf[j], wts, w_lru_out_bf[j], lng, lnb, bp, tp)
            xb, gate = lru_in(x, w_lru_in_bf[j], n_p, bs * ts)
            cc_rows = jnp.pad(state_lru_conv[j], ((0, 0), (SUBLANES - (CONV_W - 1), 0), (0, 0)))
            h0_rows = jnp.pad(state_lru_h[j][:, None, :], ((0, 0), (0, ts - 1), (0, 0)))
            hgate, hs, cs = lru_rec_sample(xb, gate, cc_rows.reshape(bs * SUBLANES, D_MODEL),
                                           h0_rows.reshape(bs * ts, D_MODEL), wts, bs, ts)
            h_p.append(hp.reshape(bp, D_MODEL))
            h_s.append(hs.reshape(bs, D_MODEL))
            c_p.append(cp)
            c_s.append(cs)
            x = proj_ln(hgate, w_lru_out_bf[j], x, lng, lnb, xo, n_p)
            x = moe_ffn_ln(x, w_router_pad[j], w_moe_gu_bf, w_moe_down_bf, j,
                           vec(ln_ffn_g, layer), vec(ln_ffn_b, layer))

    y_prompt = x[:n_p].reshape(bp, tp, D_MODEL)
    y_sample = x[n_p:].reshape(bs, ts, D_MODEL)
    return (y_prompt, y_sample, hg_p, hg_s, jnp.stack(h_p), jnp.stack(h_s), jnp.stack(c_p), jnp.stack(c_s))
```

```python
import functools
import math

import jax
import jax.numpy as jnp
from jax import lax
from jax.experimental import pallas as pl
from jax.experimental.pallas import tpu as pltpu

F32 = jnp.float32
BF16 = jnp.bfloat16
I32 = jnp.int32

D_MODEL = 1024
DEPTH = 4
HG_DK = 128
HG_HEADS = 8
HG_DV = 128
CONV_W = 4
RG_C = 8.0
LRU_BLOCKS = 4
LRU_BW = 256
D_FF = 2816
N_EXPERTS = 8
D_FF_EXPERT = 3584
ALPHA = (2 * DEPTH) ** 0.25
LN_EPS = 1e-5
F_FLOOR = 1e-20

LANES = 128
SUBLANES = 8
VMEM_LIMIT = 56 * 1024 * 1024

HG_CHUNK_PROMPT = 16
MOE_TILE = 512


def _cparams(sem):
    return pltpu.CompilerParams(dimension_semantics=sem, vmem_limit_bytes=VMEM_LIMIT)


def _sigmoid(x):
    return 0.5 * jnp.tanh(0.5 * x) + 0.5


def _gelu_tanh(y):
    c = math.sqrt(2.0 / math.pi)
    return 0.5 * y * (1.0 + jnp.tanh(c * (y + 0.044715 * (y * y * y))))


def _layer_norm_rows(z, g, b):
    mu = jnp.mean(z, axis=-1, keepdims=True)
    zc = z - mu
    var = jnp.mean(zc * zc, axis=-1, keepdims=True)
    return zc * lax.rsqrt(var + LN_EPS) * g + b


def _hgrn_in_compute(layer_j, xb, w_ref, lbl_ref, q_ref, k_ref, lf_ref, v_ref, g_ref):
    l = lbl_ref[...]
    e = jnp.exp(l - jnp.max(l, axis=0, keepdims=True))
    p = e / jnp.sum(e, axis=0, keepdims=True)
    cs = p[0:1]
    for i in range(1, layer_j + 1):
        cs = cs + p[i:i + 1]
    lb = cs - p[0:1]

    def seg(i):
        return jnp.dot(xb, w_ref[:, i * D_MODEL:(i + 1) * D_MODEL], preferred_element_type=F32)

    q = seg(0)
    q_ref[...] = q * _sigmoid(q) * (HG_DK ** -0.5)
    fp = seg(1)
    sig = 1.0 / (1.0 + jnp.exp(-fp))
    f = lb + (1.0 - lb) * sig
    lf_ref[...] = jnp.log(jnp.maximum(f, F_FLOOR))
    k_ref[...] = (1.0 - lb) * (1.0 - sig)
    v_ref[...] = seg(2)
    g = seg(3)
    g_ref[...] = g * _sigmoid(g)


def _hgrn_in_kernel(layer_j, x_ref, w_ref, lbl_ref, *out_refs):
    _hgrn_in_compute(layer_j, x_ref[...].astype(BF16), w_ref, lbl_ref, *out_refs)


def hgrn_in(x, w_in_bf, lb_logits, layer_j, row0, nrows, tm=256):
    base = row0 // tm
    out = jax.ShapeDtypeStruct((nrows, D_MODEL), F32)
    return pl.pallas_call(
        functools.partial(_hgrn_in_kernel, layer_j),
        grid=(nrows // tm,),
        in_specs=[pl.BlockSpec((tm, D_MODEL), lambda i: (base + i, 0)),
                  pl.BlockSpec((D_MODEL, 4 * D_MODEL), lambda i: (0, 0)),
                  pl.BlockSpec(lb_logits.shape, lambda i: (0, 0))],
        out_specs=[pl.BlockSpec((tm, D_MODEL), lambda i: (i, 0))] * 5,
        out_shape=[out] * 5,
        compiler_params=_cparams(("arbitrary",)),
        name="hgrn_in",
    )(x, w_in_bf, lb_logits)


def _hgrn_chunk_head(C, q, k, lf, v, S):
    row = lax.broadcasted_iota(I32, (C, LANES), 0)
    G = lf
    d = 1
    while d < C:
        G = G + jnp.where(row >= d, pltpu.roll(G, d, 0), 0.0)
        d *= 2
    gend = G[C - 1:C, :]
    qt = q * jnp.exp(G)
    kt = k * jnp.exp(gend - G)
    o = jnp.dot(qt.astype(BF16), S.astype(BF16), preferred_element_type=F32)

    row8 = lax.broadcasted_iota(I32, (SUBLANES, LANES), 0)
    parts = []
    for gi in range(C // SUBLANES):
        lo = gi * SUBLANES
        Gg = G[lo:lo + SUBLANES]
        qg = q[lo:lo + SUBLANES]
        acc = jnp.zeros((SUBLANES, LANES), F32)
        for s in range(lo + SUBLANES):
            dec = jnp.exp(Gg - G[s:s + 1])
            if s >= lo:
                dec = jnp.where(row8 >= (s - lo), dec, 0.0)
            w = jnp.sum(qg * dec * k[s:s + 1], axis=-1, keepdims=True)
            acc = acc + w * v[s:s + 1]
        parts.append(acc)
    o = o + (parts[0] if len(parts) == 1 else jnp.concatenate(parts, axis=0))

    ecol = jnp.transpose(jnp.broadcast_to(jnp.exp(gend), (HG_DK, LANES)))
    upd = lax.dot_general(kt.astype(BF16), v.astype(BF16), (((0,), (0,)), ((), ())),
                          preferred_element_type=F32)
    return o, ecol * S + upd


def _hgrn_rec_kernel(C, cps, zero_init, *refs):
    if zero_init:
        q_ref, k_ref, lf_ref, v_ref, g_ref, ng_ref, o_ref, s_ref = refs

        @pl.when(pl.program_id(1) == 0)
        def _():
            s_ref[...] = jnp.zeros(s_ref.shape, F32)
    else:
        q_ref, k_ref, lf_ref, v_ref, g_ref, ng_ref, s0_ref, o_ref, s_ref = refs
        s_ref[...] = s0_ref[...]

    n_chunks = q_ref.shape[0] // C

    def body(i, carry):
        r0 = pl.multiple_of(i * C, C)
        si = i // cps
        rows = pl.ds(r0, C)
        for h in range(HG_HEADS):
            sl = slice(h * LANES, (h + 1) * LANES)
            o, s_new = _hgrn_chunk_head(C, q_ref[rows, sl], k_ref[rows, sl], lf_ref[rows, sl],
                                        v_ref[rows, sl], s_ref[si, h])
            ms = jnp.mean(o * o, axis=-1, keepdims=True)
            o_ref[rows, sl] = o * lax.rsqrt(ms + LN_EPS) * ng_ref[:, sl] * g_ref[rows, sl]
            s_ref[si, h] = s_new
        return carry

    lax.fori_loop(0, n_chunks, body, 0)


HG_CHUNK = 64
HG_SUB = 16
HG_SAFE_RANGE = 60.0


def _hgrn_chunk_masks(tt):
    import numpy as np
    t = np.arange(tt)[:, None]
    s = np.arange(tt)[None, :]
    tri = (s <= t) & ((t // HG_CHUNK) == (s // HG_CHUNK))
    return jnp.asarray(tri.astype(np.float32), BF16)


def _hgrn_prompt_kernel(layer_j, x_ref, win_ref, lbl_ref, ng_ref, m_ref, wout_ref, lng_ref, lnb_ref,
                        xo_ref, s_ref,
                        q_ref, k_ref, lf_ref, v_ref, g_ref, o_ref,
                        g_s, qi_s, ks_s, qd_s, kd_s, q1_s, k1_s, q2_s, k2_s, a_s, flag_s):
    tt = x_ref.shape[0]

    @pl.when(pl.program_id(1) == 0)
    def _():
        s_ref[...] = jnp.zeros(s_ref.shape, F32)

    _hgrn_in_compute(layer_j, x_ref[...].astype(BF16), win_ref, lbl_ref, q_ref, k_ref, lf_ref, v_ref, g_ref)

    lf = lf_ref[...]
    hi = lf.astype(BF16)
    lo = (lf - hi.astype(F32)).astype(BF16)
    g_s[...] = (jnp.dot(m_ref[...], hi, preferred_element_type=F32)
                + jnp.dot(m_ref[...], lo, preferred_element_type=F32))

    C = HG_CHUNK
    SB = HG_SUB
    worst = jnp.zeros((SB, D_MODEL), F32)
    for blk in range(tt // SB):
        r0 = blk * SB
        c0 = (r0 // C) * C
        rows = slice(r0, r0 + SB)

        def grow(r):
            return g_s[r - 1:r, :] if r > c0 else jnp.zeros((1, D_MODEL), F32)

        G = g_s[rows, :]
        q = q_ref[rows, :]
        k = k_ref[rows, :]
        qi_s[rows, :] = (q * jnp.exp(G)).astype(BF16)
        ks_s[rows, :] = (k * jnp.exp(g_s[c0 + C - 1:c0 + C, :] - G)).astype(BF16)
        e0 = G - grow(r0)
        qd_s[rows, :] = (q * jnp.exp(e0)).astype(BF16)
        kd_s[rows, :] = (k * jnp.exp(-e0)).astype(BF16)
        worst = jnp.maximum(worst, -e0)
        e1 = G - grow((r0 // (2 * SB)) * (2 * SB) + SB)
        q1_s[rows, :] = (q * jnp.exp(jnp.minimum(e1, 0.0))).astype(BF16)
        k1_s[rows, :] = (k * jnp.exp(jnp.minimum(-e1, 0.0))).astype(BF16)
        e2 = G - grow(c0 + C // 2)
        q2_s[rows, :] = (q * jnp.exp(jnp.minimum(e2, 0.0))).astype(BF16)
        k2_s[rows, :] = (k * jnp.exp(jnp.minimum(-e2, 0.0))).astype(BF16)
    flag_s[0] = (jnp.max(worst) <= HG_SAFE_RANGE).astype(I32)

    def finish(rows, sl, o):
        ms = jnp.mean(o * o, axis=-1, keepdims=True)
        o_ref[rows, sl] = o * lax.rsqrt(ms + LN_EPS) * ng_ref[:, sl] * g_ref[rows, sl]

    C = HG_CHUNK
    ti = lax.broadcasted_iota(I32, (C, C), 0)
    si = lax.broadcasted_iota(I32, (C, C), 1)
    causal = si <= ti
    sub_bits = HG_SUB.bit_length() - 1
    m0 = causal & ((ti >> sub_bits) == (si >> sub_bits))
    m1 = causal & ((ti >> (sub_bits + 1)) == (si >> (sub_bits + 1))) & jnp.logical_not(m0)
    nt_dims = (((1,), (1,)), ((), ()))

    @pl.when(flag_s[0] == 1)
    def _():
        def body(c, carry):
            r0 = pl.multiple_of(c * C, C)
            rows = pl.ds(r0, C)
            heads = [slice(h * LANES, (h + 1) * LANES) for h in range(HG_HEADS)]
            for h, sl in enumerate(heads):
                a0 = lax.dot_general(qd_s[rows, sl], kd_s[rows, sl], nt_dims, preferred_element_type=F32)
                a1 = lax.dot_general(q1_s[rows, sl], k1_s[rows, sl], nt_dims, preferred_element_type=F32)
                a2 = lax.dot_general(q2_s[rows, sl], k2_s[rows, sl], nt_dims, preferred_element_type=F32)
                a = jnp.where(m0, a0, jnp.where(m1, a1, jnp.where(causal, a2, 0.0)))
                a_s[h] = a.astype(BF16)
            for h, sl in enumerate(heads):
                o = (jnp.dot(qi_s[rows, sl], s_ref[0, h].astype(BF16), preferred_element_type=F32)
                     + jnp.dot(a_s[h], v_ref[rows, sl].astype(BF16), preferred_element_type=F32))
                finish(rows, sl, o)
            for h, sl in enumerate(heads):
                gend = g_s[pl.ds(pl.multiple_of(r0 + C - SUBLANES, SUBLANES), SUBLANES), sl][SUBLANES - 1:]
                ecol = jnp.transpose(jnp.broadcast_to(jnp.exp(gend), (HG_DK, LANES)))
                upd = lax.dot_general(ks_s[rows, sl], v_ref[rows, sl].astype(BF16), (((0,), (0,)), ((), ())),
                                      preferred_element_type=F32)
                s_ref[0, h] = ecol * s_ref[0, h] + upd
            return carry

        lax.fori_loop(0, tt // C, body, 0)

    @pl.when(flag_s[0] == 0)
    def _():
        Cs = HG_CHUNK_PROMPT

        def body(i, carry):
            r0 = pl.multiple_of(i * Cs, Cs)
            rows = pl.ds(r0, Cs)
            for h in range(HG_HEADS):
                sl = slice(h * LANES, (h + 1) * LANES)
                o, s_new = _hgrn_chunk_head(Cs, q_ref[rows, sl], k_ref[rows, sl], lf_ref[rows, sl],
                                            v_ref[rows, sl], s_ref[0, h])
                finish(rows, sl, o)
                s_ref[0, h] = s_new
            return carry

        lax.fori_loop(0, tt // Cs, body, 0)

    y = jnp.dot(o_ref[...].astype(BF16), wout_ref[...], preferred_element_type=F32)
    xo_ref[...] = _layer_norm_rows(ALPHA * x_ref[...] + y, lng_ref[...], lnb_ref[...])


def hgrn_prompt_layer(x, w_in_bf, lb_logits, norm_g, w_out_bf, ln_g, ln_b, s_all, layer_j, batch, seq, n,
                      tt=256):
    nt = seq // tt
    row = pl.BlockSpec((tt, D_MODEL), lambda b, t: (b * nt + t, 0))
    vec = pl.BlockSpec((1, D_MODEL), lambda b, t: (0, 0))
    s_spec = pl.BlockSpec((None, 1, HG_HEADS, HG_DK, HG_DV), lambda b, t: (layer_j, b, 0, 0, 0))
    in_specs = [row,
                pl.BlockSpec((D_MODEL, 4 * D_MODEL), lambda b, t: (0, 0)),
                pl.BlockSpec(lb_logits.shape, lambda b, t: (0, 0)),
                vec,
                pl.BlockSpec((tt, tt), lambda b, t: (0, 0)),
                pl.BlockSpec((D_MODEL, D_MODEL), lambda b, t: (0, 0)),
                vec, vec]
    args = [x, w_in_bf, lb_logits, norm_g, _hgrn_chunk_masks(tt), w_out_bf, ln_g, ln_b]
    aliases = {}
    kern = functools.partial(_hgrn_prompt_kernel, layer_j)
    if s_all is not None:
        def kern(*refs):
            _hgrn_prompt_kernel(layer_j, *refs[:8], *refs[9:])
        in_specs.append(pl.BlockSpec(memory_space=pl.ANY))
        args.append(s_all)
        aliases = {8: 1}
    f32_tile = pltpu.VMEM((tt, D_MODEL), F32)
    bf_tile = pltpu.VMEM((tt, D_MODEL), BF16)
    return pl.pallas_call(
        kern,
        grid=(batch, nt),
        in_specs=in_specs,
        out_specs=[row, s_spec],
        out_shape=[jax.ShapeDtypeStruct((n, D_MODEL), F32),
                   jax.ShapeDtypeStruct((2, batch, HG_HEADS, HG_DK, HG_DV), F32)],
        scratch_shapes=[f32_tile] * 7 + [bf_tile] * 8
                       + [pltpu.VMEM((HG_HEADS, HG_CHUNK, HG_CHUNK), BF16), pltpu.SMEM((1,), I32)],
        input_output_aliases=aliases,
        compiler_params=_cparams(("arbitrary", "arbitrary")),
        name="hgrn_prompt_layer",
    )(*args)


def hgrn_rec_sample(q, k, lf, v, g, norm_g, state, s_all, layer_j, nseq, seq, sg=8):
    n = q.shape[0]
    rt = sg * seq
    row = pl.BlockSpec((rt, D_MODEL), lambda i: (i, 0))
    s_in = pl.BlockSpec((None, sg, HG_HEADS, HG_DK, HG_DV), lambda i: (layer_j, i, 0, 0, 0))
    in_specs = [row] * 5 + [pl.BlockSpec((1, D_MODEL), lambda i: (0, 0)), s_in]
    args = [q, k, lf, v, g, norm_g, state]
    aliases = {}
    if s_all is not None:
        in_specs.append(pl.BlockSpec(memory_space=pl.ANY))
        args.append(s_all)
        aliases[7] = 1

    def kern(*refs):
        _hgrn_rec_kernel(seq, 1, False, *refs[:7], *refs[len(args):])

    return pl.pallas_call(
        kern,
        grid=(nseq // sg,),
        in_specs=in_specs,
        out_specs=[row, s_in],
        out_shape=[jax.ShapeDtypeStruct((n, D_MODEL), F32),
                   jax.ShapeDtypeStruct((2, nseq, HG_HEADS, HG_DK, HG_DV), F32)],
        input_output_aliases=aliases,
        compiler_params=_cparams(("arbitrary",)),
        name="hgrn_rec_sample",
    )(*args)


def _proj_ln_kernel(a_ref, w_ref, x_ref, g_ref, b_ref, o_ref):
    y = jnp.dot(a_ref[...].astype(BF16), w_ref[...], preferred_element_type=F32)
    o_ref[...] = _layer_norm_rows(ALPHA * x_ref[...] + y, g_ref[...], b_ref[...])


def proj_ln(a, w_bf, x, x_row0, g, b, xo, row0, tm=512):
    n = xo.shape[0]
    nrows = a.shape[0]
    base = row0 // tm
    xbase = x_row0 // tm
    row = pl.BlockSpec((tm, D_MODEL), lambda i: (base + i, 0))
    vec = pl.BlockSpec((1, D_MODEL), lambda i: (0, 0))

    def kern(a_ref, w_ref, x_ref, g_ref, b_ref, _, o_ref):
        _proj_ln_kernel(a_ref, w_ref, x_ref, g_ref, b_ref, o_ref)

    return pl.pallas_call(
        kern,
        grid=(nrows // tm,),
        in_specs=[pl.BlockSpec((tm, D_MODEL), lambda i: (i, 0)),
                  pl.BlockSpec((D_MODEL, D_MODEL), lambda i: (0, 0)),
                  pl.BlockSpec((tm, D_MODEL), lambda i: (xbase + i, 0)), vec, vec,
                  pl.BlockSpec(memory_space=pl.ANY)],
        out_specs=row,
        out_shape=jax.ShapeDtypeStruct((n, D_MODEL), F32),
        input_output_aliases={5: 0},
        compiler_params=_cparams(("arbitrary",)),
        name="proj_ln",
    )(a, w_bf, x, g, b, xo)


def _ffn_kernel(x_ref, wg_ref, wu_ref, wd_ref, g_ref, b_ref, o_ref, acc_ref):
    f = pl.program_id(1)
    xb = x_ref[...].astype(BF16)
    gg = jnp.dot(xb, wg_ref[...], preferred_element_type=F32)
    uu = jnp.dot(xb, wu_ref[...], preferred_element_type=F32)
    h = (gg * _sigmoid(gg) * uu).astype(BF16)
    part = jnp.dot(h, wd_ref[...], preferred_element_type=F32)

    @pl.when(f == 0)
    def _():
        acc_ref[...] = part

    @pl.when(f > 0)
    def _():
        acc_ref[...] += part

    @pl.when(f == pl.num_programs(1) - 1)
    def _():
        o_ref[...] = _layer_norm_rows(ALPHA * x_ref[...] + acc_ref[...], g_ref[...], b_ref[...])


def ffn_ln(x, w_gu_bf, w_down_bf, g, b, tm=512, tf=1408):
    n = x.shape[0]
    nf = D_FF // tf
    row = pl.BlockSpec((tm, D_MODEL), lambda i, f: (i, 0))
    vec = pl.BlockSpec((1, D_MODEL), lambda i, f: (0, 0))
    return pl.pallas_call(
        _ffn_kernel,
        grid=(n // tm, nf),
        in_specs=[row,
                  pl.BlockSpec((D_MODEL, tf), lambda i, f: (0, f)),
                  pl.BlockSpec((D_MODEL, tf), lambda i, f: (0, nf + f)),
                  pl.BlockSpec((tf, D_MODEL), lambda i, f: (f, 0)),
                  vec, vec],
        out_specs=row,
        out_shape=jax.ShapeDtypeStruct((n, D_MODEL), F32),
        scratch_shapes=[pltpu.VMEM((tm, D_MODEL), F32)],
        compiler_params=_cparams(("arbitrary", "arbitrary")),
        name="ffn_ln",
    )(x, w_gu_bf, w_gu_bf, w_down_bf, g, b)


def _lru_in_kernel(x_ref, w_ref, xb_ref, gate_ref):
    xb = x_ref[...].astype(BF16)
    xb_ref[...] = jnp.dot(xb, w_ref[:, :D_MODEL], preferred_element_type=F32)
    y = jnp.dot(xb, w_ref[:, D_MODEL:], preferred_element_type=F32)
    gate_ref[...] = _gelu_tanh(y)


def lru_in(x, w_in_bf, row0, nrows, tm=512):
    base = row0 // tm
    row = pl.BlockSpec((tm, D_MODEL), lambda i: (i, 0))
    out = jax.ShapeDtypeStruct((nrows, D_MODEL), F32)
    return pl.pallas_call(
        _lru_in_kernel,
        grid=(nrows // tm,),
        in_specs=[pl.BlockSpec((tm, D_MODEL), lambda i: (base + i, 0)),
                  pl.BlockSpec((D_MODEL, 2 * D_MODEL), lambda i: (0, 0))],
        out_specs=[row, row],
        out_shape=[out, out],
        compiler_params=_cparams(("arbitrary",)),
        name="lru_in",
    )(x, w_in_bf)


def _lru_rec_kernel(L, carry, *refs):
    if carry:
        (x_ref, win_ref, cw_ref, cb_ref, wr_ref, br_ref, wi_ref, bi_ref, lam_ref, wout_ref, lng_ref, lnb_ref,
         xo_ref, hn_ref, cn_ref, xp_ref, hs_ref, hc_ref, gate_ref, hg_ref) = refs
        R = x_ref.shape[0]
    else:
        (xb_ref, gate_ref, cc_ref, h0_ref, cw_ref, cb_ref, wr_ref, br_ref, wi_ref, bi_ref, lam_ref,
         hg_ref, hn_ref, cn_ref, xp_ref, hs_ref) = refs
        R = xb_ref.shape[0]
    S = R // L
    PADR = SUBLANES
    t = pl.program_id(1) if carry else 0

    if carry:
        @pl.when(t == 0)
        def _():
            xp_ref[:, 0:PADR, :] = jnp.zeros((S, PADR, D_MODEL), F32)
            hc_ref[...] = jnp.zeros(hc_ref.shape, F32)

        @pl.when(t > 0)
        def _():
            xp_ref[:, 0:PADR, :] = xp_ref[:, L:L + PADR, :]

        xin = x_ref[...].astype(BF16)
        xp_ref[:, PADR:PADR + L, :] = jnp.dot(xin, win_ref[:, :D_MODEL],
                                              preferred_element_type=F32).reshape(S, L, D_MODEL)
        gate_ref[...] = _gelu_tanh(jnp.dot(xin, win_ref[:, D_MODEL:], preferred_element_type=F32))
    else:
        xp_ref[:, 0:PADR, :] = cc_ref[...].reshape(S, PADR, D_MODEL)
        xp_ref[:, PADR:PADR + L, :] = xb_ref[...].reshape(S, L, D_MODEL)

    xc = cb_ref[...].reshape(1, 1, D_MODEL) + xp_ref[:, PADR:PADR + L, :] * cw_ref[CONV_W - 1:CONV_W, :].reshape(1, 1, D_MODEL)
    for j in range(CONV_W - 1):
        off = PADR - (CONV_W - 1) + j
        xc = xc + xp_ref[:, off:off + L, :] * cw_ref[j:j + 1, :].reshape(1, 1, D_MODEL)
    xc = xc.reshape(R, D_MODEL)
    cn_ref[...] = xp_ref[:, L + PADR - (CONV_W - 1):L + PADR, :].reshape(cn_ref.shape)

    assert L & (L - 1) == 0
    pos = jnp.bitwise_and(lax.broadcasted_iota(I32, (R, LRU_BW), 0), L - 1)
    for nb in range(LRU_BLOCKS):
        sl = slice(nb * LRU_BW, (nb + 1) * LRU_BW)
        xcb = xc[:, sl]
        xh = xcb.astype(BF16)
        r = _sigmoid(jnp.dot(xh, wr_ref[nb], preferred_element_type=F32) + br_ref[:, sl])
        ig = _sigmoid(jnp.dot(xh, wi_ref[nb], preferred_element_type=F32) + bi_ref[:, sl])
        z = -lam_ref[:, sl]
        softplus = jnp.maximum(z, 0.0) + jnp.log1p(jnp.exp(-jnp.abs(z)))
        log_a = (-RG_C) * r * softplus
        a = jnp.exp(log_a)
        u = jnp.sqrt(1.0 - a * a) * (ig * xcb)
        if carry:
            ng = R // SUBLANES
            u3 = u.reshape(ng, SUBLANES, LRU_BW)
            a3 = a.reshape(ng, SUBLANES, LRU_BW)
            pos8 = lax.broadcasted_iota(I32, (ng, SUBLANES, LRU_BW), 1)
            d = 1
            while d < SUBLANES:
                m = pos8 >= d
                u3 = jnp.where(m, a3 * pltpu.roll(u3, d, 1) + u3, u3)
                a3 = jnp.where(m, a3 * pltpu.roll(a3, d, 1), a3)
                d *= 2
            hprev = hc_ref[:, sl]
            for gi in range(ng):
                hgrp = u3[gi] + a3[gi] * hprev
                hs_ref[0, gi * SUBLANES:(gi + 1) * SUBLANES, sl] = hgrp
                hprev = hgrp[SUBLANES - 1:SUBLANES]
            hg_ref[:, sl] = hs_ref[0, :, sl] * gate_ref[:, sl]
        else:
            u = u + a * h0_ref[:, sl]
            d = 1
            while d < L:
                m = pos >= d
                u = jnp.where(m, a * pltpu.roll(u, d, 0) + u, u)
                a = jnp.where(m, a * pltpu.roll(a, d, 0), a)
                d *= 2
            hg_ref[:, sl] = u * gate_ref[:, sl]
            hs_ref[:, :, sl] = u.reshape(S, L, LRU_BW)

    hlast = hs_ref[:, L - 1:L, :]
    hn_ref[...] = hlast.reshape(hn_ref.shape)
    if carry:
        hc_ref[...] = hlast.reshape(1, D_MODEL)
        y = jnp.dot(hg_ref[...].astype(BF16), wout_ref[...], preferred_element_type=F32)
        xo_ref[...] = _layer_norm_rows(ALPHA * x_ref[...] + y, lng_ref[...], lnb_ref[...])


def _lru_weight_specs(nidx):
    def c2(*_):
        return (0, 0)

    def c3(*_):
        return (0, 0, 0)
    return [pl.BlockSpec((CONV_W, D_MODEL), c2), pl.BlockSpec((1, D_MODEL), c2),
            pl.BlockSpec((LRU_BLOCKS, LRU_BW, LRU_BW), c3), pl.BlockSpec((1, D_MODEL), c2),
            pl.BlockSpec((LRU_BLOCKS, LRU_BW, LRU_BW), c3), pl.BlockSpec((1, D_MODEL), c2),
            pl.BlockSpec((1, D_MODEL), c2)]


def lru_prompt_layer(x, w_in_bf, wts, w_out_bf, ln_g, ln_b, batch, seq, tt=256):
    n = x.shape[0]
    nt = seq // tt
    row = pl.BlockSpec((tt, D_MODEL), lambda b, t: (b * nt + t, 0))
    vec = pl.BlockSpec((1, D_MODEL), lambda b, t: (0, 0))
    return pl.pallas_call(
        functools.partial(_lru_rec_kernel, tt, True),
        grid=(batch, nt),
        in_specs=[row, pl.BlockSpec((D_MODEL, 2 * D_MODEL), lambda b, t: (0, 0))] + _lru_weight_specs(2)
                 + [pl.BlockSpec((D_MODEL, D_MODEL), lambda b, t: (0, 0)), vec, vec],
        out_specs=[row,
                   pl.BlockSpec((1, 1, D_MODEL), lambda b, t: (b, 0, 0)),
                   pl.BlockSpec((1, CONV_W - 1, D_MODEL), lambda b, t: (b, 0, 0))],
        out_shape=[jax.ShapeDtypeStruct((n, D_MODEL), F32),
                   jax.ShapeDtypeStruct((batch, 1, D_MODEL), F32),
                   jax.ShapeDtypeStruct((batch, CONV_W - 1, D_MODEL), F32)],
        scratch_shapes=[pltpu.VMEM((1, tt + SUBLANES, D_MODEL), F32),
                        pltpu.VMEM((1, tt, D_MODEL), F32),
                        pltpu.VMEM((1, D_MODEL), F32),
                        pltpu.VMEM((tt, D_MODEL), F32),
                        pltpu.VMEM((tt, D_MODEL), F32)],
        compiler_params=_cparams(("arbitrary", "arbitrary")),
        name="lru_prompt_layer",
    )(x, w_in_bf, *wts, w_out_bf, ln_g, ln_b)


def lru_rec_sample(xb, gate, cc_rows, h0_rows, wts, nseq, seq, sg=32):
    n = xb.shape[0]
    rt = sg * seq
    row = pl.BlockSpec((rt, D_MODEL), lambda i: (i, 0))
    return pl.pallas_call(
        functools.partial(_lru_rec_kernel, seq, False),
        grid=(nseq // sg,),
        in_specs=[row, row, pl.BlockSpec((sg * SUBLANES, D_MODEL), lambda i: (i, 0)), row]
                 + _lru_weight_specs(1),
        out_specs=[row,
                   pl.BlockSpec((sg, 1, D_MODEL), lambda i: (i, 0, 0)),
                   pl.BlockSpec((sg, CONV_W - 1, D_MODEL), lambda i: (i, 0, 0))],
        out_shape=[jax.ShapeDtypeStruct((n, D_MODEL), F32),
                   jax.ShapeDtypeStruct((nseq, 1, D_MODEL), F32),
                   jax.ShapeDtypeStruct((nseq, CONV_W - 1, D_MODEL), F32)],
        scratch_shapes=[pltpu.VMEM((sg, seq + SUBLANES, D_MODEL), F32),
                        pltpu.VMEM((sg, seq, D_MODEL), F32)],
        compiler_params=_cparams(("arbitrary",)),
        name="lru_rec_sample",
    )(xb, gate, cc_rows, h0_rows, *wts)


def _router_kernel(x_ref, w_ref, mi_ref, mf_ref, cnt_ref, run_ref):
    i = pl.program_id(0)

    @pl.when(i == 0)
    def _():
        run_ref[...] = jnp.zeros(run_ref.shape, F32)

    tm = x_ref.shape[0]
    logits = jnp.dot(x_ref[...], w_ref[...], preferred_element_type=F32,
                     precision=lax.Precision.HIGHEST)
    lane_i = lax.broadcasted_iota(I32, (tm, LANES), 1)
    lane = lane_i.astype(F32)
    neg = jnp.float32(-jnp.inf)
    logits = jnp.where(lane_i < N_EXPERTS, logits, neg)
    m1 = jnp.max(logits, axis=-1, keepdims=True)
    i1 = jnp.min(jnp.where(logits == m1, lane, float(LANES)), axis=-1, keepdims=True)
    l2 = jnp.where(lane == i1, neg, logits)
    m2 = jnp.max(l2, axis=-1, keepdims=True)
    i2 = jnp.min(jnp.where(l2 == m2, lane, float(LANES)), axis=-1, keepdims=True)
    e2 = jnp.exp(m2 - m1)
    p1 = 1.0 / (1.0 + e2)
    p2 = e2 * p1

    hit1 = lane == i1
    hit2 = lane == i2
    onehot = jnp.where(hit1 | hit2, 1.0, 0.0)
    r_i = lax.broadcasted_iota(I32, (tm, tm), 0)
    c_i = lax.broadcasted_iota(I32, (tm, tm), 1)
    tri = jnp.where(c_i < r_i, 1.0, 0.0).astype(BF16)
    rank = jnp.dot(tri, onehot.astype(BF16), preferred_element_type=F32) + run_ref[0:1, :]
    r1 = jnp.sum(jnp.where(hit1, rank, 0.0), axis=-1, keepdims=True)
    r2 = jnp.sum(jnp.where(hit2, rank, 0.0), axis=-1, keepdims=True)
    total = run_ref[0:1, :] + jnp.sum(onehot, axis=0, keepdims=True)
    run_ref[...] = jnp.broadcast_to(total, run_ref.shape)
    cnt_ref[...] = jnp.broadcast_to(total, cnt_ref.shape).astype(I32)

    mi = jnp.where(lane_i == 0, i1, jnp.where(lane_i == 1, i2, 0.0))
    mi = jnp.where(lane_i == 2, r1, jnp.where(lane_i == 3, r2, mi))
    mi_ref[...] = mi.astype(I32)
    mf_ref[...] = jnp.where(lane_i == 0, p1, jnp.where(lane_i == 1, p2, 0.0))


def moe_router(x, w_router_pad, tm=512):
    n = x.shape[0]
    row = pl.BlockSpec((tm, LANES), lambda i: (i, 0))
    return pl.pallas_call(
        _router_kernel,
        grid=(n // tm,),
        in_specs=[pl.BlockSpec((tm, D_MODEL), lambda i: (i, 0)),
                  pl.BlockSpec((D_MODEL, LANES), lambda i: (0, 0))],
        out_specs=[row, row, pl.BlockSpec((SUBLANES, LANES), lambda i: (0, 0))],
        out_shape=[jax.ShapeDtypeStruct((n, LANES), I32),
                   jax.ShapeDtypeStruct((n, LANES), F32),
                   jax.ShapeDtypeStruct((SUBLANES, LANES), I32)],
        scratch_shapes=[pltpu.VMEM((SUBLANES, LANES), F32)],
        compiler_params=_cparams(("arbitrary",)),
        name="moe_router",
    )(x, w_router_pad)


ZERO_ROWS = 256


def _dispatch_kernel(dest_ref, ztile_ref, x_ref, xs_hbm, zero_ref, sem):
    i = pl.program_id(0)
    tm = x_ref.shape[0]

    @pl.when(i == 0)
    def _():
        zero_ref[...] = jnp.zeros(zero_ref.shape, F32)
        copies = [pltpu.make_async_copy(
                      zero_ref,
                      xs_hbm.at[pl.ds(pl.multiple_of(ztile_ref[e] + c * ZERO_ROWS, ZERO_ROWS), ZERO_ROWS)], sem)
                  for e in range(N_EXPERTS) for c in range(MOE_TILE // ZERO_ROWS)]
        for cp in copies:
            cp.start()
        for cp in copies:
            cp.wait()

    def tok(j, c):
        t = i * tm + j
        src = x_ref.at[pl.ds(j, 1)]
        pltpu.make_async_copy(src, xs_hbm.at[pl.ds(dest_ref[2 * t], 1)], sem).start()
        pltpu.make_async_copy(src, xs_hbm.at[pl.ds(dest_ref[2 * t + 1], 1)], sem).start()
        return c

    lax.fori_loop(0, tm, tok, 0, unroll=8)
    for _ in range(2):
        pltpu.make_async_copy(x_ref, xs_hbm.at[pl.ds(0, tm)], sem).wait()


def moe_dispatch(dest, ztile, x, n_rows, tm=512):
    n = x.shape[0]
    return pl.pallas_call(
        _dispatch_kernel,
        grid_spec=pltpu.PrefetchScalarGridSpec(
            num_scalar_prefetch=2, grid=(n // tm,),
            in_specs=[pl.BlockSpec((tm, D_MODEL), lambda i, d, z: (i, 0))],
            out_specs=pl.BlockSpec(memory_space=pl.ANY),
            scratch_shapes=[pltpu.VMEM((ZERO_ROWS, D_MODEL), F32), pltpu.SemaphoreType.DMA(())]),
        out_shape=jax.ShapeDtypeStruct((n_rows, D_MODEL), F32),
        compiler_params=pltpu.CompilerParams(dimension_semantics=("arbitrary",),
                                             has_side_effects=True),
        name="moe_dispatch",
    )(dest, ztile, x)


def _expert_kernel(te_ref, nu_ref, xs_ref, wg_ref, wu_ref, wd_ref, ys_ref, xb_ref, acc_ref):
    i = pl.program_id(0)
    f = pl.program_id(1)
    used = i < nu_ref[0]

    @pl.when(used & (f == 0))
    def _():
        xb_ref[...] = xs_ref[...].astype(BF16)

    @pl.when(used)
    def _():
        xb = xb_ref[...]
        gg = jnp.dot(xb, wg_ref[...], preferred_element_type=F32)
        uu = jnp.dot(xb, wu_ref[...], preferred_element_type=F32)
        h = (gg * _sigmoid(gg) * uu).astype(BF16)
        part = jnp.dot(h, wd_ref[...], preferred_element_type=F32)

        @pl.when(f == 0)
        def _():
            acc_ref[...] = part

        @pl.when(f > 0)
        def _():
            acc_ref[...] += part

    @pl.when(f == pl.num_programs(1) - 1)
    def _():
        @pl.when(used)
        def _():
            ys_ref[...] = acc_ref[...]

        @pl.when(jnp.logical_not(used))
        def _():
            ys_ref[...] = jnp.zeros(ys_ref.shape, F32)


def moe_experts(tile_expert, n_used, xs, w_gu_bf, w_down_bf, layer_j, tf=1792):
    n_rows = xs.shape[0]
    tm = MOE_TILE
    nf = D_FF_EXPERT // tf

    def xrow(i, f, te, nu):
        return (jnp.minimum(i, nu[0] - 1), 0)

    def wg(i, f, te, nu):
        return (layer_j, te[i], 0, jnp.where(i < nu[0], f, nf - 1))

    def wu(i, f, te, nu):
        return (layer_j, te[i], 0, nf + jnp.where(i < nu[0], f, nf - 1))

    def wd(i, f, te, nu):
        return (layer_j, te[i], jnp.where(i < nu[0], f, nf - 1), 0)

    return pl.pallas_call(
        _expert_kernel,
        grid_spec=pltpu.PrefetchScalarGridSpec(
            num_scalar_prefetch=2, grid=(n_rows // tm, nf),
            in_specs=[pl.BlockSpec((tm, D_MODEL), xrow),
                      pl.BlockSpec((None, None, D_MODEL, tf), wg),
                      pl.BlockSpec((None, None, D_MODEL, tf), wu),
                      pl.BlockSpec((None, None, tf, D_MODEL), wd)],
            out_specs=pl.BlockSpec((tm, D_MODEL), lambda i, f, te, nu: (i, 0)),
            scratch_shapes=[pltpu.VMEM((tm, D_MODEL), BF16), pltpu.VMEM((tm, D_MODEL), F32)]),
        out_shape=jax.ShapeDtypeStruct((n_rows, D_MODEL), F32),
        compiler_params=_cparams(("arbitrary", "arbitrary")),
        name="moe_experts",
    )(tile_expert, n_used, xs, w_gu_bf, w_gu_bf, w_down_bf)


def _combine_kernel(nsplit, dest_ref, ys_hbm, mf_ref, x_ref, g_ref, b_ref, *rest):
    if nsplit is None:
        o_ref, buf_ref, sem = rest
    else:
        oa_ref, ob_ref, buf_ref, sem = rest
    i = pl.program_id(0)
    tm = x_ref.shape[0]

    def tok(j, c):
        t = i * tm + j
        pltpu.make_async_copy(ys_hbm.at[pl.ds(dest_ref[2 * t], 1)],
                              buf_ref.at[0, pl.ds(j, 1)], sem).start()
        pltpu.make_async_copy(ys_hbm.at[pl.ds(dest_ref[2 * t + 1], 1)],
                              buf_ref.at[1, pl.ds(j, 1)], sem).start()
        return c

    lax.fori_loop(0, tm, tok, 0, unroll=8)
    pltpu.make_async_copy(ys_hbm.at[pl.ds(0, 2 * tm)], buf_ref.reshape(2 * tm, D_MODEL), sem).wait()
    mf = mf_ref[...]
    y = buf_ref[0] * mf[:, 0:1] + buf_ref[1] * mf[:, 1:2]
    res = _layer_norm_rows(ALPHA * x_ref[...] + y, g_ref[...], b_ref[...])
    if nsplit is None:
        o_ref[...] = res
    else:
        @pl.when(i < nsplit)
        def _():
            oa_ref[...] = res

        @pl.when(i >= nsplit)
        def _():
            ob_ref[...] = res


def moe_combine_ln(dest, ys, mf, x, g, b, split_rows=None, tm=256):
    n = x.shape[0]
    row = pl.BlockSpec((tm, D_MODEL), lambda i, d: (i, 0))
    vec = pl.BlockSpec((1, D_MODEL), lambda i, d: (0, 0))
    if split_rows is None:
        nsplit = None
        out_specs = row
        out_shape = jax.ShapeDtypeStruct((n, D_MODEL), F32)
    else:
        nsplit = split_rows // tm
        out_specs = [pl.BlockSpec((tm, D_MODEL), lambda i, d: (jnp.minimum(i, nsplit - 1), 0)),
                     pl.BlockSpec((tm, D_MODEL), lambda i, d: (jnp.maximum(i - nsplit, 0), 0))]
        out_shape = [jax.ShapeDtypeStruct((split_rows, D_MODEL), F32),
                     jax.ShapeDtypeStruct((n - split_rows, D_MODEL), F32)]
    return pl.pallas_call(
        functools.partial(_combine_kernel, nsplit),
        grid_spec=pltpu.PrefetchScalarGridSpec(
            num_scalar_prefetch=1, grid=(n // tm,),
            in_specs=[pl.BlockSpec(memory_space=pl.ANY),
                      pl.BlockSpec((tm, LANES), lambda i, d: (i, 0)),
                      row, vec, vec],
            out_specs=out_specs,
            scratch_shapes=[pltpu.VMEM((2, tm, D_MODEL), F32), pltpu.SemaphoreType.DMA(())]),
        out_shape=out_shape,
        compiler_params=_cparams(("arbitrary",)),
        name="moe_combine_ln",
    )(dest, ys, mf, x, g, b)


def moe_ffn_ln(x, w_router_pad, w_gu, w_down, layer_j, g, b, split_rows=None):
    n = x.shape[0]
    tm = MOE_TILE
    mi, mf, cnt = moe_router(x, w_router_pad)
    counts = cnt[0, :N_EXPERTS]
    padded = (counts + tm - 1) // tm * tm
    ends = jnp.cumsum(padded)
    starts = ends - padded
    ids = mi[:, 0:2]
    ranks = mi[:, 2:4]
    dest = (starts[ids] + ranks).reshape(-1).astype(I32)
    n_tiles = (2 * n) // tm + N_EXPERTS
    tile_start = jnp.arange(n_tiles, dtype=I32) * tm
    tile_expert = jnp.minimum(jnp.sum(tile_start[:, None] >= ends[None, :], axis=1), N_EXPERTS - 1).astype(I32)
    n_used = (ends[-1] // tm).astype(I32).reshape(1)
    tile_expert = jnp.where(tile_start < ends[-1], tile_expert, tile_expert[jnp.maximum(n_used[0] - 1, 0)])
    ztile = jnp.maximum(ends - tm, 0).astype(I32)
    xs = moe_dispatch(dest, ztile, x, n_tiles * tm)
    ys = moe_experts(tile_expert, n_used, xs, w_gu, w_down, layer_j)
    return moe_combine_ln(dest, ys, mf, x, g, b, split_rows)


def kernel(x_prompt, x_sample, state_hgrn, state_lru_h, state_lru_conv, ln_mix_g, ln_mix_b, ln_ffn_g, ln_ffn_b, w_hgrn_in, hgrn_lb_logits, hgrn_norm_g, w_hgrn_out, w_lru_in, lru_conv_w, lru_conv_b, w_lru_rgate, b_lru_rgate, w_lru_igate, b_lru_igate, lru_lambda, w_lru_out, w_ffn_gu, w_ffn_down, w_router, w_moe_gu, w_moe_down):
    bp, tp, _ = x_prompt.shape
    bs, ts, _ = x_sample.shape
    n_p = bp * tp
    n = n_p + bs * ts
    x = x_prompt.reshape(n_p, D_MODEL)
    x_smp, smp_row0 = x_sample.reshape(bs * ts, D_MODEL), 0

    def vec(a, i):
        return a[i].reshape(1, D_MODEL)

    w_hgrn_in_bf = w_hgrn_in.astype(BF16)
    w_hgrn_out_bf = w_hgrn_out.astype(BF16)
    w_lru_in_bf = w_lru_in.astype(BF16)
    w_lru_out_bf = w_lru_out.astype(BF16)
    w_r_bf = w_lru_rgate.astype(BF16)
    w_i_bf = w_lru_igate.astype(BF16)
    w_ffn_gu_bf = w_ffn_gu.astype(BF16)
    w_ffn_down_bf = w_ffn_down.astype(BF16)
    w_moe_gu_bf = w_moe_gu.astype(BF16)
    w_moe_down_bf = w_moe_down.astype(BF16)
    w_router_pad = jnp.pad(w_router, ((0, 0), (0, 0), (0, LANES - N_EXPERTS)))

    hg_p = hg_s = None
    h_p, h_s, c_p, c_s = [], [], [], []
    for layer in range(DEPTH):
        j = layer // 2
        lng, lnb = vec(ln_mix_g, layer), vec(ln_mix_b, layer)
        if layer % 2 == 0:
            ng = vec(hgrn_norm_g, j)
            xo, hg_p = hgrn_prompt_layer(x, w_hgrn_in_bf[j], hgrn_lb_logits, ng, w_hgrn_out_bf[j],
                                         lng, lnb, hg_p, j, bp, tp, n)
            q, k, lf, v, g = hgrn_in(x_smp, w_hgrn_in_bf[j], hgrn_lb_logits, j, smp_row0, bs * ts)
            o, hg_s = hgrn_rec_sample(q, k, lf, v, g, ng, state_hgrn, hg_s, j, bs, ts)
            x = proj_ln(o, w_hgrn_out_bf[j], x_smp, smp_row0, lng, lnb, xo, n_p)
            x = ffn_ln(x, w_ffn_gu_bf[j], w_ffn_down_bf[j], vec(ln_ffn_g, layer), vec(ln_ffn_b, layer))
            x_smp, smp_row0 = x, n_p
        else:
            wts = (lru_conv_w[j], vec(lru_conv_b, j), w_r_bf[j], vec(b_lru_rgate, j),
                   w_i_bf[j], vec(b_lru_igate, j), vec(lru_lambda, j))
            xo, hp, cp = lru_prompt_layer(x, w_lru_in_bf[j], wts, w_lru_out_bf[j], lng, lnb, bp, tp)
            xb, gate = lru_in(x_smp, w_lru_in_bf[j], smp_row0, bs * ts)
            cc_rows = jnp.pad(state_lru_conv[j], ((0, 0), (SUBLANES - (CONV_W - 1), 0), (0, 0)))
            h0_rows = jnp.pad(state_lru_h[j][:, None, :], ((0, 0), (0, ts - 1), (0, 0)))
            hgate, hs, cs = lru_rec_sample(xb, gate, cc_rows.reshape(bs * SUBLANES, D_MODEL),
                                           h0_rows.reshape(bs * ts, D_MODEL), wts, bs, ts)
            h_p.append(hp.reshape(bp, D_MODEL))
            h_s.append(hs.reshape(bs, D_MODEL))
            c_p.append(cp)
            c_s.append(cs)
            x = proj_ln(hgate, w_lru_out_bf[j], x_smp, smp_row0, lng, lnb, xo, n_p)
            x = moe_ffn_ln(x, w_router_pad[j], w_moe_gu_bf, w_moe_down_bf, j,
                           vec(ln_ffn_g, layer), vec(ln_ffn_b, layer),
                           split_rows=n_p if layer == DEPTH - 1 else None)
            x_smp, smp_row0 = x, n_p

    y_prompt = x[0].reshape(bp, tp, D_MODEL)
    y_sample = x[1].reshape(bs, ts, D_MODEL)
    return (y_prompt, y_sample, hg_p, hg_s, jnp.stack(h_p), jnp.stack(h_s), jnp.stack(c_p), jnp.stack(c_s))
```

```python
import functools
import math

import jax
import jax.numpy as jnp
from jax import lax
from jax.experimental import pallas as pl
from jax.experimental.pallas import tpu as pltpu

F32 = jnp.float32
BF16 = jnp.bfloat16
I32 = jnp.int32

D_MODEL = 1024
DEPTH = 4
HG_DK = 128
HG_HEADS = 8
HG_DV = 128
CONV_W = 4
RG_C = 8.0
LRU_BLOCKS = 4
LRU_BW = 256
D_FF = 2816
N_EXPERTS = 8
D_FF_EXPERT = 3584
ALPHA = (2 * DEPTH) ** 0.25
LN_EPS = 1e-5
F_FLOOR = 1e-20

LANES = 128
SUBLANES = 8
VMEM_LIMIT = 56 * 1024 * 1024

HG_CHUNK_PROMPT = 16
MOE_TILE = 512


def _cparams(sem):
    return pltpu.CompilerParams(dimension_semantics=sem, vmem_limit_bytes=VMEM_LIMIT)


def _sigmoid(x):
    return 0.5 * jnp.tanh(0.5 * x) + 0.5


def _gelu_tanh(y):
    c = math.sqrt(2.0 / math.pi)
    return 0.5 * y * (1.0 + jnp.tanh(c * (y + 0.044715 * (y * y * y))))


def _layer_norm_rows(z, g, b):
    mu = jnp.mean(z, axis=-1, keepdims=True)
    zc = z - mu
    var = jnp.mean(zc * zc, axis=-1, keepdims=True)
    return zc * lax.rsqrt(var + LN_EPS) * g + b


def _hgrn_in_compute(layer_j, xb, w_ref, lbl_ref, q_ref, k_ref, lf_ref, v_ref, g_ref):
    l = lbl_ref[...]
    e = jnp.exp(l - jnp.max(l, axis=0, keepdims=True))
    p = e / jnp.sum(e, axis=0, keepdims=True)
    cs = p[0:1]
    for i in range(1, layer_j + 1):
        cs = cs + p[i:i + 1]
    lb = cs - p[0:1]

    def seg(i):
        return jnp.dot(xb, w_ref[:, i * D_MODEL:(i + 1) * D_MODEL], preferred_element_type=F32)

    q = seg(0)
    q_ref[...] = q * _sigmoid(q) * (HG_DK ** -0.5)
    fp = seg(1)
    sig = 1.0 / (1.0 + jnp.exp(-fp))
    f = lb + (1.0 - lb) * sig
    lf_ref[...] = jnp.log(jnp.maximum(f, F_FLOOR))
    k_ref[...] = (1.0 - lb) * (1.0 - sig)
    v_ref[...] = seg(2)
    g = seg(3)
    g_ref[...] = g * _sigmoid(g)


def _hgrn_in_kernel(layer_j, x_ref, w_ref, lbl_ref, *out_refs):
    _hgrn_in_compute(layer_j, x_ref[...].astype(BF16), w_ref, lbl_ref, *out_refs)


def hgrn_in(x, w_in_bf, lb_logits, layer_j, row0, nrows, tm=256):
    base = row0 // tm
    out = jax.ShapeDtypeStruct((nrows, D_MODEL), F32)
    return pl.pallas_call(
        functools.partial(_hgrn_in_kernel, layer_j),
        grid=(nrows // tm,),
        in_specs=[pl.BlockSpec((tm, D_MODEL), lambda i: (base + i, 0)),
                  pl.BlockSpec((D_MODEL, 4 * D_MODEL), lambda i: (0, 0)),
                  pl.BlockSpec(lb_logits.shape, lambda i: (0, 0))],
        out_specs=[pl.BlockSpec((tm, D_MODEL), lambda i: (i, 0))] * 5,
        out_shape=[out] * 5,
        compiler_params=_cparams(("arbitrary",)),
        name="hgrn_in",
    )(x, w_in_bf, lb_logits)


def _hgrn_prefix(C, lf):
    row = lax.broadcasted_iota(I32, (C, LANES), 0)
    G = lf
    d = 1
    while d < C:
        G = G + jnp.where(row >= d, pltpu.roll(G, d, 0), 0.0)
        d *= 2
    return G


def _hgrn_diag(C, G, q, k, v):
    row8 = lax.broadcasted_iota(I32, (SUBLANES, LANES), 0)
    parts = []
    for gi in range(C // SUBLANES):
        lo = gi * SUBLANES
        Gg = G[lo:lo + SUBLANES]
        qg = q[lo:lo + SUBLANES]
        acc = jnp.zeros((SUBLANES, LANES), F32)
        for s in range(lo + SUBLANES):
            dec = jnp.exp(Gg - G[s:s + 1])
            if s >= lo:
                dec = jnp.where(row8 >= (s - lo), dec, 0.0)
            w = jnp.sum(qg * dec * k[s:s + 1], axis=-1, keepdims=True)
            acc = acc + w * v[s:s + 1]
        parts.append(acc)
    return parts[0] if len(parts) == 1 else jnp.concatenate(parts, axis=0)


def _hgrn_state_update(C, G, k, v, S):
    gend = G[C - 1:C, :]
    kt = k * jnp.exp(gend - G)
    ecol = jnp.transpose(jnp.broadcast_to(jnp.exp(gend), (HG_DK, LANES)))
    upd = lax.dot_general(kt.astype(BF16), v.astype(BF16), (((0,), (0,)), ((), ())),
                          preferred_element_type=F32)
    return ecol * S + upd


def _hgrn_chunk_head(C, q, k, lf, v, S):
    G = _hgrn_prefix(C, lf)
    o = jnp.dot((q * jnp.exp(G)).astype(BF16), S.astype(BF16), preferred_element_type=F32)
    return o + _hgrn_diag(C, G, q, k, v), _hgrn_state_update(C, G, k, v, S)


def _hgrn_rec_kernel(C, q_ref, k_ref, lf_ref, v_ref, g_ref, ng_ref, s0_ref, o_ref, s_ref):
    heads = [slice(h * LANES, (h + 1) * LANES) for h in range(HG_HEADS)]

    def body(i, carry):
        rows = pl.ds(pl.multiple_of(i * C, C), C)
        gs, os_ = [], []
        for h, sl in enumerate(heads):
            G = _hgrn_prefix(C, lf_ref[rows, sl])
            gs.append(G)
            os_.append(jnp.dot((q_ref[rows, sl] * jnp.exp(G)).astype(BF16), s0_ref[i, h].astype(BF16),
                               preferred_element_type=F32))
        for h, sl in enumerate(heads):
            o = os_[h] + _hgrn_diag(C, gs[h], q_ref[rows, sl], k_ref[rows, sl], v_ref[rows, sl])
            ms = jnp.mean(o * o, axis=-1, keepdims=True)
            o_ref[rows, sl] = o * lax.rsqrt(ms + LN_EPS) * ng_ref[:, sl] * g_ref[rows, sl]
        for h, sl in enumerate(heads):
            s_ref[i, h] = _hgrn_state_update(C, gs[h], k_ref[rows, sl], v_ref[rows, sl], s0_ref[i, h])
        return carry

    lax.fori_loop(0, q_ref.shape[0] // C, body, 0)


HG_CHUNK = 64
HG_SUB = 16
HG_SAFE_RANGE = 60.0


def _hgrn_chunk_masks(tt):
    import numpy as np
    t = np.arange(tt)[:, None]
    s = np.arange(tt)[None, :]
    tri = (s <= t) & ((t // HG_CHUNK) == (s // HG_CHUNK))
    return jnp.asarray(tri.astype(np.float32), BF16)


def _hgrn_prompt_kernel(layer_j, x_ref, win_ref, lbl_ref, ng_ref, m_ref, wout_ref, lng_ref, lnb_ref,
                        xo_ref, s_ref,
                        q_ref, k_ref, lf_ref, v_ref, g_ref, o_ref,
                        g_s, qi_s, ks_s, qd_s, kd_s, q1_s, k1_s, q2_s, k2_s, a_s, flag_s):
    tt = x_ref.shape[0]

    @pl.when(pl.program_id(1) == 0)
    def _():
        s_ref[...] = jnp.zeros(s_ref.shape, F32)

    _hgrn_in_compute(layer_j, x_ref[...].astype(BF16), win_ref, lbl_ref, q_ref, k_ref, lf_ref, v_ref, g_ref)

    lf = lf_ref[...]
    hi = lf.astype(BF16)
    lo = (lf - hi.astype(F32)).astype(BF16)
    g_s[...] = (jnp.dot(m_ref[...], hi, preferred_element_type=F32)
                + jnp.dot(m_ref[...], lo, preferred_element_type=F32))

    C = HG_CHUNK
    SB = HG_SUB
    worst = jnp.zeros((SB, D_MODEL), F32)
    for blk in range(tt // SB):
        r0 = blk * SB
        c0 = (r0 // C) * C
        rows = slice(r0, r0 + SB)

        def grow(r):
            return g_s[r - 1:r, :] if r > c0 else jnp.zeros((1, D_MODEL), F32)

        G = g_s[rows, :]
        q = q_ref[rows, :]
        k = k_ref[rows, :]
        qi_s[rows, :] = (q * jnp.exp(G)).astype(BF16)
        ks_s[rows, :] = (k * jnp.exp(g_s[c0 + C - 1:c0 + C, :] - G)).astype(BF16)
        e0 = G - grow(r0)
        qd_s[rows, :] = (q * jnp.exp(e0)).astype(BF16)
        kd_s[rows, :] = (k * jnp.exp(-e0)).astype(BF16)
        worst = jnp.maximum(worst, -e0)
        e1 = G - grow((r0 // (2 * SB)) * (2 * SB) + SB)
        q1_s[rows, :] = (q * jnp.exp(jnp.minimum(e1, 0.0))).astype(BF16)
        k1_s[rows, :] = (k * jnp.exp(jnp.minimum(-e1, 0.0))).astype(BF16)
        e2 = G - grow(c0 + C // 2)
        q2_s[rows, :] = (q * jnp.exp(jnp.minimum(e2, 0.0))).astype(BF16)
        k2_s[rows, :] = (k * jnp.exp(jnp.minimum(-e2, 0.0))).astype(BF16)
    flag_s[0] = (jnp.max(worst) <= HG_SAFE_RANGE).astype(I32)

    def finish(rows, sl, o):
        ms = jnp.mean(o * o, axis=-1, keepdims=True)
        o_ref[rows, sl] = o * lax.rsqrt(ms + LN_EPS) * ng_ref[:, sl] * g_ref[rows, sl]

    C = HG_CHUNK
    ti = lax.broadcasted_iota(I32, (C, C), 0)
    si = lax.broadcasted_iota(I32, (C, C), 1)
    causal = si <= ti
    sub_bits = HG_SUB.bit_length() - 1
    m0 = causal & ((ti >> sub_bits) == (si >> sub_bits))
    m1 = causal & ((ti >> (sub_bits + 1)) == (si >> (sub_bits + 1))) & jnp.logical_not(m0)
    nt_dims = (((1,), (1,)), ((), ()))

    @pl.when(flag_s[0] == 1)
    def _():
        def body(c, carry):
            r0 = pl.multiple_of(c * C, C)
            rows = pl.ds(r0, C)
            heads = [slice(h * LANES, (h + 1) * LANES) for h in range(HG_HEADS)]
            for h, sl in enumerate(heads):
                a0 = lax.dot_general(qd_s[rows, sl], kd_s[rows, sl], nt_dims, preferred_element_type=F32)
                a1 = lax.dot_general(q1_s[rows, sl], k1_s[rows, sl], nt_dims, preferred_element_type=F32)
                a2 = lax.dot_general(q2_s[rows, sl], k2_s[rows, sl], nt_dims, preferred_element_type=F32)
                a = jnp.where(m0, a0, jnp.where(m1, a1, jnp.where(causal, a2, 0.0)))
                a_s[h] = a.astype(BF16)
            for h, sl in enumerate(heads):
                o = (jnp.dot(qi_s[rows, sl], s_ref[0, h].astype(BF16), preferred_element_type=F32)
                     + jnp.dot(a_s[h], v_ref[rows, sl].astype(BF16), preferred_element_type=F32))
                finish(rows, sl, o)
            for h, sl in enumerate(heads):
                gend = g_s[pl.ds(pl.multiple_of(r0 + C - SUBLANES, SUBLANES), SUBLANES), sl][SUBLANES - 1:]
                ecol = jnp.transpose(jnp.broadcast_to(jnp.exp(gend), (HG_DK, LANES)))
                upd = lax.dot_general(ks_s[rows, sl], v_ref[rows, sl].astype(BF16), (((0,), (0,)), ((), ())),
                                      preferred_element_type=F32)
                s_ref[0, h] = ecol * s_ref[0, h] + upd
            return carry

        lax.fori_loop(0, tt // C, body, 0)

    @pl.when(flag_s[0] == 0)
    def _():
        Cs = HG_CHUNK_PROMPT

        def body(i, carry):
            r0 = pl.multiple_of(i * Cs, Cs)
            rows = pl.ds(r0, Cs)
            for h in range(HG_HEADS):
                sl = slice(h * LANES, (h + 1) * LANES)
                o, s_new = _hgrn_chunk_head(Cs, q_ref[rows, sl], k_ref[rows, sl], lf_ref[rows, sl],
                                            v_ref[rows, sl], s_ref[0, h])
                finish(rows, sl, o)
                s_ref[0, h] = s_new
            return carry

        lax.fori_loop(0, tt // Cs, body, 0)

    y = jnp.dot(o_ref[...].astype(BF16), wout_ref[...], preferred_element_type=F32)
    xo_ref[...] = _layer_norm_rows(ALPHA * x_ref[...] + y, lng_ref[...], lnb_ref[...])


def hgrn_prompt_layer(x, w_in_bf, lb_logits, norm_g, w_out_bf, ln_g, ln_b, s_all, layer_j, batch, seq, n,
                      tt=256):
    nt = seq // tt
    row = pl.BlockSpec((tt, D_MODEL), lambda b, t: (b * nt + t, 0))
    vec = pl.BlockSpec((1, D_MODEL), lambda b, t: (0, 0))
    s_spec = pl.BlockSpec((None, 1, HG_HEADS, HG_DK, HG_DV), lambda b, t: (layer_j, b, 0, 0, 0))
    in_specs = [row,
                pl.BlockSpec((D_MODEL, 4 * D_MODEL), lambda b, t: (0, 0)),
                pl.BlockSpec(lb_logits.shape, lambda b, t: (0, 0)),
                vec,
                pl.BlockSpec((tt, tt), lambda b, t: (0, 0)),
                pl.BlockSpec((D_MODEL, D_MODEL), lambda b, t: (0, 0)),
                vec, vec]
    args = [x, w_in_bf, lb_logits, norm_g, _hgrn_chunk_masks(tt), w_out_bf, ln_g, ln_b]
    aliases = {}
    kern = functools.partial(_hgrn_prompt_kernel, layer_j)
    if s_all is not None:
        def kern(*refs):
            _hgrn_prompt_kernel(layer_j, *refs[:8], *refs[9:])
        in_specs.append(pl.BlockSpec(memory_space=pl.ANY))
        args.append(s_all)
        aliases = {8: 1}
    f32_tile = pltpu.VMEM((tt, D_MODEL), F32)
    bf_tile = pltpu.VMEM((tt, D_MODEL), BF16)
    return pl.pallas_call(
        kern,
        grid=(batch, nt),
        in_specs=in_specs,
        out_specs=[row, s_spec],
        out_shape=[jax.ShapeDtypeStruct((n, D_MODEL), F32),
                   jax.ShapeDtypeStruct((2, batch, HG_HEADS, HG_DK, HG_DV), F32)],
        scratch_shapes=[f32_tile] * 7 + [bf_tile] * 8
                       + [pltpu.VMEM((HG_HEADS, HG_CHUNK, HG_CHUNK), BF16), pltpu.SMEM((1,), I32)],
        input_output_aliases=aliases,
        compiler_params=_cparams(("arbitrary", "arbitrary")),
        name="hgrn_prompt_layer",
    )(*args)


def hgrn_rec_sample(q, k, lf, v, g, norm_g, state, s_all, layer_j, nseq, seq, sg=8):
    n = q.shape[0]
    rt = sg * seq
    row = pl.BlockSpec((rt, D_MODEL), lambda i: (i, 0))
    s_in = pl.BlockSpec((None, sg, HG_HEADS, HG_DK, HG_DV), lambda i: (layer_j, i, 0, 0, 0))
    in_specs = [row] * 5 + [pl.BlockSpec((1, D_MODEL), lambda i: (0, 0)), s_in]
    args = [q, k, lf, v, g, norm_g, state]
    aliases = {}
    if s_all is not None:
        in_specs.append(pl.BlockSpec(memory_space=pl.ANY))
        args.append(s_all)
        aliases[7] = 1

    def kern(*refs):
        _hgrn_rec_kernel(seq, *refs[:7], *refs[len(args):])

    return pl.pallas_call(
        kern,
        grid=(nseq // sg,),
        in_specs=in_specs,
        out_specs=[row, s_in],
        out_shape=[jax.ShapeDtypeStruct((n, D_MODEL), F32),
                   jax.ShapeDtypeStruct((2, nseq, HG_HEADS, HG_DK, HG_DV), F32)],
        input_output_aliases=aliases,
        compiler_params=_cparams(("arbitrary",)),
        name="hgrn_rec_sample",
    )(*args)


def _proj_ln_kernel(a_ref, w_ref, x_ref, g_ref, b_ref, o_ref):
    y = jnp.dot(a_ref[...].astype(BF16), w_ref[...], preferred_element_type=F32)
    o_ref[...] = _layer_norm_rows(ALPHA * x_ref[...] + y, g_ref[...], b_ref[...])


def proj_ln(a, w_bf, x, x_row0, g, b, xo, row0, tm=512):
    n = xo.shape[0]
    nrows = a.shape[0]
    base = row0 // tm
    xbase = x_row0 // tm
    row = pl.BlockSpec((tm, D_MODEL), lambda i: (base + i, 0))
    vec = pl.BlockSpec((1, D_MODEL), lambda i: (0, 0))

    def kern(a_ref, w_ref, x_ref, g_ref, b_ref, _, o_ref):
        _proj_ln_kernel(a_ref, w_ref, x_ref, g_ref, b_ref, o_ref)

    return pl.pallas_call(
        kern,
        grid=(nrows // tm,),
        in_specs=[pl.BlockSpec((tm, D_MODEL), lambda i: (i, 0)),
                  pl.BlockSpec((D_MODEL, D_MODEL), lambda i: (0, 0)),
                  pl.BlockSpec((tm, D_MODEL), lambda i: (xbase + i, 0)), vec, vec,
                  pl.BlockSpec(memory_space=pl.ANY)],
        out_specs=row,
        out_shape=jax.ShapeDtypeStruct((n, D_MODEL), F32),
        input_output_aliases={5: 0},
        compiler_params=_cparams(("arbitrary",)),
        name="proj_ln",
    )(a, w_bf, x, g, b, xo)


CAST_STEPS = 64


def _ffn_kernel(x_ref, wg_ref, wu_ref, wd_ref, g_ref, b_ref, cgu_ref, cdn_ref, o_ref, cgu_o, cdn_o, acc_ref):
    f = pl.program_id(1)
    cgu_o[...] = cgu_ref[...].astype(BF16)
    cdn_o[...] = cdn_ref[...].astype(BF16)
    xb = x_ref[...].astype(BF16)
    gg = jnp.dot(xb, wg_ref[...], preferred_element_type=F32)
    uu = jnp.dot(xb, wu_ref[...], preferred_element_type=F32)
    h = (gg * _sigmoid(gg) * uu).astype(BF16)
    part = jnp.dot(h, wd_ref[...], preferred_element_type=F32)

    @pl.when(f == 0)
    def _():
        acc_ref[...] = part

    @pl.when(f > 0)
    def _():
        acc_ref[...] += part

    @pl.when(f == pl.num_programs(1) - 1)
    def _():
        o_ref[...] = _layer_norm_rows(ALPHA * x_ref[...] + acc_ref[...], g_ref[...], b_ref[...])


def ffn_ln(x, w_gu_bf, w_down_bf, g, b, moe_gu, moe_down, layer_j, tm=512, tf=1408):
    n = x.shape[0]
    nf = D_FF // tf
    assert (n // tm) * nf >= CAST_STEPS
    gu2 = moe_gu.reshape(moe_gu.shape[0], N_EXPERTS * D_MODEL, 2 * D_FF_EXPERT)
    dn2 = moe_down.reshape(moe_down.shape[0], N_EXPERTS * D_FF_EXPERT, D_MODEL)
    gu_rows = gu2.shape[1] // CAST_STEPS
    dn_rows = dn2.shape[1] // CAST_STEPS
    row = pl.BlockSpec((tm, D_MODEL), lambda i, f: (i, 0))
    vec = pl.BlockSpec((1, D_MODEL), lambda i, f: (0, 0))

    def slab(i, f):
        return jnp.minimum(i * nf + f, CAST_STEPS - 1)

    xo, gu_bf, dn_bf = pl.pallas_call(
        _ffn_kernel,
        grid=(n // tm, nf),
        in_specs=[row,
                  pl.BlockSpec((D_MODEL, tf), lambda i, f: (0, f)),
                  pl.BlockSpec((D_MODEL, tf), lambda i, f: (0, nf + f)),
                  pl.BlockSpec((tf, D_MODEL), lambda i, f: (f, 0)),
                  vec, vec,
                  pl.BlockSpec((None, gu_rows, 2 * D_FF_EXPERT), lambda i, f: (layer_j, slab(i, f), 0)),
                  pl.BlockSpec((None, dn_rows, D_MODEL), lambda i, f: (layer_j, slab(i, f), 0))],
        out_specs=[row,
                   pl.BlockSpec((gu_rows, 2 * D_FF_EXPERT), lambda i, f: (slab(i, f), 0)),
                   pl.BlockSpec((dn_rows, D_MODEL), lambda i, f: (slab(i, f), 0))],
        out_shape=[jax.ShapeDtypeStruct((n, D_MODEL), F32),
                   jax.ShapeDtypeStruct(gu2.shape[1:], BF16),
                   jax.ShapeDtypeStruct(dn2.shape[1:], BF16)],
        scratch_shapes=[pltpu.VMEM((tm, D_MODEL), F32)],
        compiler_params=_cparams(("arbitrary", "arbitrary")),
        name="ffn_ln",
    )(x, w_gu_bf, w_gu_bf, w_down_bf, g, b, gu2, dn2)
    return (xo, gu_bf.reshape(N_EXPERTS, D_MODEL, 2 * D_FF_EXPERT),
            dn_bf.reshape(N_EXPERTS, D_FF_EXPERT, D_MODEL))


def _lru_in_kernel(x_ref, w_ref, xb_ref, gate_ref):
    xb = x_ref[...].astype(BF16)
    xb_ref[...] = jnp.dot(xb, w_ref[:, :D_MODEL], preferred_element_type=F32)
    y = jnp.dot(xb, w_ref[:, D_MODEL:], preferred_element_type=F32)
    gate_ref[...] = _gelu_tanh(y)


def lru_in(x, w_in_bf, row0, nrows, tm=512):
    base = row0 // tm
    row = pl.BlockSpec((tm, D_MODEL), lambda i: (i, 0))
    out = jax.ShapeDtypeStruct((nrows, D_MODEL), F32)
    return pl.pallas_call(
        _lru_in_kernel,
        grid=(nrows // tm,),
        in_specs=[pl.BlockSpec((tm, D_MODEL), lambda i: (base + i, 0)),
                  pl.BlockSpec((D_MODEL, 2 * D_MODEL), lambda i: (0, 0))],
        out_specs=[row, row],
        out_shape=[out, out],
        compiler_params=_cparams(("arbitrary",)),
        name="lru_in",
    )(x, w_in_bf)


def _lru_rec_kernel(L, carry, *refs):
    if carry:
        (x_ref, win_ref, cw_ref, cb_ref, wr_ref, br_ref, wi_ref, bi_ref, lam_ref, wout_ref, lng_ref, lnb_ref,
         xo_ref, hn_ref, cn_ref, xp_ref, hs_ref, hc_ref, gate_ref, hg_ref) = refs
        R = x_ref.shape[0]
    else:
        (xb_ref, gate_ref, cc_ref, h0_ref, cw_ref, cb_ref, wr_ref, br_ref, wi_ref, bi_ref, lam_ref,
         hg_ref, hn_ref, cn_ref, xp_ref, hs_ref) = refs
        R = xb_ref.shape[0]
    S = R // L
    PADR = SUBLANES
    t = pl.program_id(1) if carry else 0

    if carry:
        @pl.when(t == 0)
        def _():
            xp_ref[:, 0:PADR, :] = jnp.zeros((S, PADR, D_MODEL), F32)
            hc_ref[...] = jnp.zeros(hc_ref.shape, F32)

        @pl.when(t > 0)
        def _():
            xp_ref[:, 0:PADR, :] = xp_ref[:, L:L + PADR, :]

        xin = x_ref[...].astype(BF16)
        xp_ref[:, PADR:PADR + L, :] = jnp.dot(xin, win_ref[:, :D_MODEL],
                                              preferred_element_type=F32).reshape(S, L, D_MODEL)
        gate_ref[...] = _gelu_tanh(jnp.dot(xin, win_ref[:, D_MODEL:], preferred_element_type=F32))
    else:
        xp_ref[:, 0:PADR, :] = cc_ref[...].reshape(S, PADR, D_MODEL)
        xp_ref[:, PADR:PADR + L, :] = xb_ref[...].reshape(S, L, D_MODEL)

    xc = cb_ref[...].reshape(1, 1, D_MODEL) + xp_ref[:, PADR:PADR + L, :] * cw_ref[CONV_W - 1:CONV_W, :].reshape(1, 1, D_MODEL)
    for j in range(CONV_W - 1):
        off = PADR - (CONV_W - 1) + j
        xc = xc + xp_ref[:, off:off + L, :] * cw_ref[j:j + 1, :].reshape(1, 1, D_MODEL)
    xc = xc.reshape(R, D_MODEL)
    cn_ref[...] = xp_ref[:, L + PADR - (CONV_W - 1):L + PADR, :].reshape(cn_ref.shape)

    assert L & (L - 1) == 0
    pos = jnp.bitwise_and(lax.broadcasted_iota(I32, (R, LRU_BW), 0), L - 1)
    for nb in range(LRU_BLOCKS):
        sl = slice(nb * LRU_BW, (nb + 1) * LRU_BW)
        xcb = xc[:, sl]
        xh = xcb.astype(BF16)
        r = _sigmoid(jnp.dot(xh, wr_ref[nb], preferred_element_type=F32) + br_ref[:, sl])
        ig = _sigmoid(jnp.dot(xh, wi_ref[nb], preferred_element_type=F32) + bi_ref[:, sl])
        z = -lam_ref[:, sl]
        softplus = jnp.maximum(z, 0.0) + jnp.log1p(jnp.exp(-jnp.abs(z)))
        log_a = (-RG_C) * r * softplus
        a = jnp.exp(log_a)
        u = jnp.sqrt(1.0 - a * a) * (ig * xcb)
        if carry:
            ng = R // SUBLANES
            u3 = u.reshape(ng, SUBLANES, LRU_BW)
            a3 = a.reshape(ng, SUBLANES, LRU_BW)
            pos8 = lax.broadcasted_iota(I32, (ng, SUBLANES, LRU_BW), 1)
            d = 1
            while d < SUBLANES:
                m = pos8 >= d
                u3 = jnp.where(m, a3 * pltpu.roll(u3, d, 1) + u3, u3)
                a3 = jnp.where(m, a3 * pltpu.roll(a3, d, 1), a3)
                d *= 2
            hprev = hc_ref[:, sl]
            for gi in range(ng):
                hgrp = u3[gi] + a3[gi] * hprev
                hs_ref[0, gi * SUBLANES:(gi + 1) * SUBLANES, sl] = hgrp
                hprev = hgrp[SUBLANES - 1:SUBLANES]
            hg_ref[:, sl] = hs_ref[0, :, sl] * gate_ref[:, sl]
        else:
            u = u + a * h0_ref[:, sl]
            d = 1
            while d < L:
                m = pos >= d
                u = jnp.where(m, a * pltpu.roll(u, d, 0) + u, u)
                a = jnp.where(m, a * pltpu.roll(a, d, 0), a)
                d *= 2
            hg_ref[:, sl] = u * gate_ref[:, sl]
            hs_ref[:, :, sl] = u.reshape(S, L, LRU_BW)

    hlast = hs_ref[:, L - 1:L, :]
    hn_ref[...] = hlast.reshape(hn_ref.shape)
    if carry:
        hc_ref[...] = hlast.reshape(1, D_MODEL)
        y = jnp.dot(hg_ref[...].astype(BF16), wout_ref[...], preferred_element_type=F32)
        xo_ref[...] = _layer_norm_rows(ALPHA * x_ref[...] + y, lng_ref[...], lnb_ref[...])


def _lru_weight_specs(nidx):
    def c2(*_):
        return (0, 0)

    def c3(*_):
        return (0, 0, 0)
    return [pl.BlockSpec((CONV_W, D_MODEL), c2), pl.BlockSpec((1, D_MODEL), c2),
            pl.BlockSpec((LRU_BLOCKS, LRU_BW, LRU_BW), c3), pl.BlockSpec((1, D_MODEL), c2),
            pl.BlockSpec((LRU_BLOCKS, LRU_BW, LRU_BW), c3), pl.BlockSpec((1, D_MODEL), c2),
            pl.BlockSpec((1, D_MODEL), c2)]


def lru_prompt_layer(x, w_in_bf, wts, w_out_bf, ln_g, ln_b, batch, seq, tt=256):
    n = x.shape[0]
    nt = seq // tt
    row = pl.BlockSpec((tt, D_MODEL), lambda b, t: (b * nt + t, 0))
    vec = pl.BlockSpec((1, D_MODEL), lambda b, t: (0, 0))
    return pl.pallas_call(
        functools.partial(_lru_rec_kernel, tt, True),
        grid=(batch, nt),
        in_specs=[row, pl.BlockSpec((D_MODEL, 2 * D_MODEL), lambda b, t: (0, 0))] + _lru_weight_specs(2)
                 + [pl.BlockSpec((D_MODEL, D_MODEL), lambda b, t: (0, 0)), vec, vec],
        out_specs=[row,
                   pl.BlockSpec((1, 1, D_MODEL), lambda b, t: (b, 0, 0)),
                   pl.BlockSpec((1, CONV_W - 1, D_MODEL), lambda b, t: (b, 0, 0))],
        out_shape=[jax.ShapeDtypeStruct((n, D_MODEL), F32),
                   jax.ShapeDtypeStruct((batch, 1, D_MODEL), F32),
                   jax.ShapeDtypeStruct((batch, CONV_W - 1, D_MODEL), F32)],
        scratch_shapes=[pltpu.VMEM((1, tt + SUBLANES, D_MODEL), F32),
                        pltpu.VMEM((1, tt, D_MODEL), F32),
                        pltpu.VMEM((1, D_MODEL), F32),
                        pltpu.VMEM((tt, D_MODEL), F32),
                        pltpu.VMEM((tt, D_MODEL), F32)],
        compiler_params=_cparams(("arbitrary", "arbitrary")),
        name="lru_prompt_layer",
    )(x, w_in_bf, *wts, w_out_bf, ln_g, ln_b)


def lru_rec_sample(xb, gate, cc_rows, h0_rows, wts, nseq, seq, sg=32):
    n = xb.shape[0]
    rt = sg * seq
    row = pl.BlockSpec((rt, D_MODEL), lambda i: (i, 0))
    return pl.pallas_call(
        functools.partial(_lru_rec_kernel, seq, False),
        grid=(nseq // sg,),
        in_specs=[row, row, pl.BlockSpec((sg * SUBLANES, D_MODEL), lambda i: (i, 0)), row]
                 + _lru_weight_specs(1),
        out_specs=[row,
                   pl.BlockSpec((sg, 1, D_MODEL), lambda i: (i, 0, 0)),
                   pl.BlockSpec((sg, CONV_W - 1, D_MODEL), lambda i: (i, 0, 0))],
        out_shape=[jax.ShapeDtypeStruct((n, D_MODEL), F32),
                   jax.ShapeDtypeStruct((nseq, 1, D_MODEL), F32),
                   jax.ShapeDtypeStruct((nseq, CONV_W - 1, D_MODEL), F32)],
        scratch_shapes=[pltpu.VMEM((sg, seq + SUBLANES, D_MODEL), F32),
                        pltpu.VMEM((sg, seq, D_MODEL), F32)],
        compiler_params=_cparams(("arbitrary",)),
        name="lru_rec_sample",
    )(xb, gate, cc_rows, h0_rows, *wts)


def _router_kernel(x_ref, w_ref, mi_ref, mf_ref, cnt_ref, run_ref):
    i = pl.program_id(0)

    @pl.when(i == 0)
    def _():
        run_ref[...] = jnp.zeros(run_ref.shape, F32)

    tm = x_ref.shape[0]
    logits = jnp.dot(x_ref[...], w_ref[...], preferred_element_type=F32,
                     precision=lax.Precision.HIGHEST)
    lane_i = lax.broadcasted_iota(I32, (tm, LANES), 1)
    lane = lane_i.astype(F32)
    neg = jnp.float32(-jnp.inf)
    logits = jnp.where(lane_i < N_EXPERTS, logits, neg)
    m1 = jnp.max(logits, axis=-1, keepdims=True)
    i1 = jnp.min(jnp.where(logits == m1, lane, float(LANES)), axis=-1, keepdims=True)
    l2 = jnp.where(lane == i1, neg, logits)
    m2 = jnp.max(l2, axis=-1, keepdims=True)
    i2 = jnp.min(jnp.where(l2 == m2, lane, float(LANES)), axis=-1, keepdims=True)
    e2 = jnp.exp(m2 - m1)
    p1 = 1.0 / (1.0 + e2)
    p2 = e2 * p1

    hit1 = lane == i1
    hit2 = lane == i2
    onehot = jnp.where(hit1 | hit2, 1.0, 0.0)
    r_i = lax.broadcasted_iota(I32, (tm, tm), 0)
    c_i = lax.broadcasted_iota(I32, (tm, tm), 1)
    tri = jnp.where(c_i < r_i, 1.0, 0.0).astype(BF16)
    rank = jnp.dot(tri, onehot.astype(BF16), preferred_element_type=F32) + run_ref[0:1, :]
    r1 = jnp.sum(jnp.where(hit1, rank, 0.0), axis=-1, keepdims=True)
    r2 = jnp.sum(jnp.where(hit2, rank, 0.0), axis=-1, keepdims=True)
    total = run_ref[0:1, :] + jnp.sum(onehot, axis=0, keepdims=True)
    run_ref[...] = jnp.broadcast_to(total, run_ref.shape)
    cnt_ref[...] = jnp.broadcast_to(total, cnt_ref.shape).astype(I32)

    mi = jnp.where(lane_i == 0, i1, jnp.where(lane_i == 1, i2, 0.0))
    mi = jnp.where(lane_i == 2, r1, jnp.where(lane_i == 3, r2, mi))
    mi_ref[...] = mi.astype(I32)
    mf_ref[...] = jnp.where(lane_i == 0, p1, jnp.where(lane_i == 1, p2, 0.0))


def moe_router(x, w_router_pad, tm=512):
    n = x.shape[0]
    row = pl.BlockSpec((tm, LANES), lambda i: (i, 0))
    return pl.pallas_call(
        _router_kernel,
        grid=(n // tm,),
        in_specs=[pl.BlockSpec((tm, D_MODEL), lambda i: (i, 0)),
                  pl.BlockSpec((D_MODEL, LANES), lambda i: (0, 0))],
        out_specs=[row, row, pl.BlockSpec((SUBLANES, LANES), lambda i: (0, 0))],
        out_shape=[jax.ShapeDtypeStruct((n, LANES), I32),
                   jax.ShapeDtypeStruct((n, LANES), F32),
                   jax.ShapeDtypeStruct((SUBLANES, LANES), I32)],
        scratch_shapes=[pltpu.VMEM((SUBLANES, LANES), F32)],
        compiler_params=_cparams(("arbitrary",)),
        name="moe_router",
    )(x, w_router_pad)


ZERO_ROWS = 256


def _dispatch_kernel(dest_ref, ztile_ref, x_ref, xs_hbm, zero_ref, sem):
    i = pl.program_id(0)
    tm = x_ref.shape[0]

    @pl.when(i == 0)
    def _():
        zero_ref[...] = jnp.zeros(zero_ref.shape, F32)
        copies = [pltpu.make_async_copy(
                      zero_ref,
                      xs_hbm.at[pl.ds(pl.multiple_of(ztile_ref[e] + c * ZERO_ROWS, ZERO_ROWS), ZERO_ROWS)], sem)
                  for e in range(N_EXPERTS) for c in range(MOE_TILE // ZERO_ROWS)]
        for cp in copies:
            cp.start()
        for cp in copies:
            cp.wait()

    def tok(j, c):
        t = i * tm + j
        src = x_ref.at[pl.ds(j, 1)]
        pltpu.make_async_copy(src, xs_hbm.at[pl.ds(dest_ref[2 * t], 1)], sem).start()
        pltpu.make_async_copy(src, xs_hbm.at[pl.ds(dest_ref[2 * t + 1], 1)], sem).start()
        return c

    lax.fori_loop(0, tm, tok, 0, unroll=8)
    for _ in range(2):
        pltpu.make_async_copy(x_ref, xs_hbm.at[pl.ds(0, tm)], sem).wait()


def moe_dispatch(dest, ztile, x, n_rows, tm=512):
    n = x.shape[0]
    return pl.pallas_call(
        _dispatch_kernel,
        grid_spec=pltpu.PrefetchScalarGridSpec(
            num_scalar_prefetch=2, grid=(n // tm,),
            in_specs=[pl.BlockSpec((tm, D_MODEL), lambda i, d, z: (i, 0))],
            out_specs=pl.BlockSpec(memory_space=pl.ANY),
            scratch_shapes=[pltpu.VMEM((ZERO_ROWS, D_MODEL), F32), pltpu.SemaphoreType.DMA(())]),
        out_shape=jax.ShapeDtypeStruct((n_rows, D_MODEL), F32),
        compiler_params=pltpu.CompilerParams(dimension_semantics=("arbitrary",),
                                             has_side_effects=True),
        name="moe_dispatch",
    )(dest, ztile, x)


def _expert_kernel(te_ref, nu_ref, xs_ref, wg_ref, wu_ref, wd_ref, ys_ref, xb_ref, acc_ref):
    i = pl.program_id(0)
    f = pl.program_id(1)
    used = i < nu_ref[0]

    @pl.when(used & (f == 0))
    def _():
        xb_ref[...] = xs_ref[...].astype(BF16)

    @pl.when(used)
    def _():
        xb = xb_ref[...]
        gg = jnp.dot(xb, wg_ref[...], preferred_element_type=F32)
        uu = jnp.dot(xb, wu_ref[...], preferred_element_type=F32)
        h = (gg * _sigmoid(gg) * uu).astype(BF16)
        part = jnp.dot(h, wd_ref[...], preferred_element_type=F32)

        @pl.when(f == 0)
        def _():
            acc_ref[...] = part

        @pl.when(f > 0)
        def _():
            acc_ref[...] += part

    @pl.when(f == pl.num_programs(1) - 1)
    def _():
        @pl.when(used)
        def _():
            ys_ref[...] = acc_ref[...]

        @pl.when(jnp.logical_not(used))
        def _():
            ys_ref[...] = jnp.zeros(ys_ref.shape, F32)


def moe_experts(tile_expert, n_used, xs, w_gu_bf, w_down_bf, tf=1792):
    n_rows = xs.shape[0]
    tm = MOE_TILE
    nf = D_FF_EXPERT // tf

    def xrow(i, f, te, nu):
        return (jnp.minimum(i, nu[0] - 1), 0)

    def wg(i, f, te, nu):
        return (te[i], 0, jnp.where(i < nu[0], f, nf - 1))

    def wu(i, f, te, nu):
        return (te[i], 0, nf + jnp.where(i < nu[0], f, nf - 1))

    def wd(i, f, te, nu):
        return (te[i], jnp.where(i < nu[0], f, nf - 1), 0)

    return pl.pallas_call(
        _expert_kernel,
        grid_spec=pltpu.PrefetchScalarGridSpec(
            num_scalar_prefetch=2, grid=(n_rows // tm, nf),
            in_specs=[pl.BlockSpec((tm, D_MODEL), xrow),
                      pl.BlockSpec((None, D_MODEL, tf), wg),
                      pl.BlockSpec((None, D_MODEL, tf), wu),
                      pl.BlockSpec((None, tf, D_MODEL), wd)],
            out_specs=pl.BlockSpec((tm, D_MODEL), lambda i, f, te, nu: (i, 0)),
            scratch_shapes=[pltpu.VMEM((tm, D_MODEL), BF16), pltpu.VMEM((tm, D_MODEL), F32)]),
        out_shape=jax.ShapeDtypeStruct((n_rows, D_MODEL), F32),
        compiler_params=_cparams(("arbitrary", "arbitrary")),
        name="moe_experts",
    )(tile_expert, n_used, xs, w_gu_bf, w_gu_bf, w_down_bf)


def _combine_kernel(nsplit, dest_ref, ys_hbm, mf_ref, x_ref, g_ref, b_ref, *rest):
    if nsplit is None:
        o_ref, buf_ref, sem = rest
    else:
        oa_ref, ob_ref, buf_ref, sem = rest
    i = pl.program_id(0)
    tm = x_ref.shape[0]

    def tok(j, c):
        t = i * tm + j
        pltpu.make_async_copy(ys_hbm.at[pl.ds(dest_ref[2 * t], 1)],
                              buf_ref.at[0, pl.ds(j, 1)], sem).start()
        pltpu.make_async_copy(ys_hbm.at[pl.ds(dest_ref[2 * t + 1], 1)],
                              buf_ref.at[1, pl.ds(j, 1)], sem).start()
        return c

    lax.fori_loop(0, tm, tok, 0, unroll=8)
    pltpu.make_async_copy(ys_hbm.at[pl.ds(0, 2 * tm)], buf_ref.reshape(2 * tm, D_MODEL), sem).wait()
    mf = mf_ref[...]
    y = buf_ref[0] * mf[:, 0:1] + buf_ref[1] * mf[:, 1:2]
    res = _layer_norm_rows(ALPHA * x_ref[...] + y, g_ref[...], b_ref[...])
    if nsplit is None:
        o_ref[...] = res
    else:
        @pl.when(i < nsplit)
        def _():
            oa_ref[...] = res

        @pl.when(i >= nsplit)
        def _():
            ob_ref[...] = res


def moe_combine_ln(dest, ys, mf, x, g, b, split_rows=None, tm=256):
    n = x.shape[0]
    row = pl.BlockSpec((tm, D_MODEL), lambda i, d: (i, 0))
    vec = pl.BlockSpec((1, D_MODEL), lambda i, d: (0, 0))
    if split_rows is None:
        nsplit = None
        out_specs = row
        out_shape = jax.ShapeDtypeStruct((n, D_MODEL), F32)
    else:
        nsplit = split_rows // tm
        out_specs = [pl.BlockSpec((tm, D_MODEL), lambda i, d: (jnp.minimum(i, nsplit - 1), 0)),
                     pl.BlockSpec((tm, D_MODEL), lambda i, d: (jnp.maximum(i - nsplit, 0), 0))]
        out_shape = [jax.ShapeDtypeStruct((split_rows, D_MODEL), F32),
                     jax.ShapeDtypeStruct((n - split_rows, D_MODEL), F32)]
    return pl.pallas_call(
        functools.partial(_combine_kernel, nsplit),
        grid_spec=pltpu.PrefetchScalarGridSpec(
            num_scalar_prefetch=1, grid=(n // tm,),
            in_specs=[pl.BlockSpec(memory_space=pl.ANY),
                      pl.BlockSpec((tm, LANES), lambda i, d: (i, 0)),
                      row, vec, vec],
            out_specs=out_specs,
            scratch_shapes=[pltpu.VMEM((2, tm, D_MODEL), F32), pltpu.SemaphoreType.DMA(())]),
        out_shape=out_shape,
        compiler_params=_cparams(("arbitrary",)),
        name="moe_combine_ln",
    )(dest, ys, mf, x, g, b)


def moe_ffn_ln(x, w_router_pad, w_gu, w_down, g, b, split_rows=None):
    n = x.shape[0]
    tm = MOE_TILE
    mi, mf, cnt = moe_router(x, w_router_pad)
    counts = cnt[0, :N_EXPERTS]
    padded = (counts + tm - 1) // tm * tm
    ends = jnp.cumsum(padded)
    starts = ends - padded
    ids = mi[:, 0:2]
    ranks = mi[:, 2:4]
    dest = (starts[ids] + ranks).reshape(-1).astype(I32)
    n_tiles = (2 * n) // tm + N_EXPERTS
    tile_start = jnp.arange(n_tiles, dtype=I32) * tm
    tile_expert = jnp.minimum(jnp.sum(tile_start[:, None] >= ends[None, :], axis=1), N_EXPERTS - 1).astype(I32)
    n_used = (ends[-1] // tm).astype(I32).reshape(1)
    tile_expert = jnp.where(tile_start < ends[-1], tile_expert, tile_expert[jnp.maximum(n_used[0] - 1, 0)])
    ztile = jnp.maximum(ends - tm, 0).astype(I32)
    xs = moe_dispatch(dest, ztile, x, n_tiles * tm)
    ys = moe_experts(tile_expert, n_used, xs, w_gu, w_down)
    return moe_combine_ln(dest, ys, mf, x, g, b, split_rows)


def kernel(x_prompt, x_sample, state_hgrn, state_lru_h, state_lru_conv, ln_mix_g, ln_mix_b, ln_ffn_g, ln_ffn_b, w_hgrn_in, hgrn_lb_logits, hgrn_norm_g, w_hgrn_out, w_lru_in, lru_conv_w, lru_conv_b, w_lru_rgate, b_lru_rgate, w_lru_igate, b_lru_igate, lru_lambda, w_lru_out, w_ffn_gu, w_ffn_down, w_router, w_moe_gu, w_moe_down):
    bp, tp, _ = x_prompt.shape
    bs, ts, _ = x_sample.shape
    n_p = bp * tp
    n = n_p + bs * ts
    x = x_prompt.reshape(n_p, D_MODEL)
    x_smp, smp_row0 = x_sample.reshape(bs * ts, D_MODEL), 0

    def vec(a, i):
        return a[i].reshape(1, D_MODEL)

    w_hgrn_in_bf = w_hgrn_in.astype(BF16)
    w_hgrn_out_bf = w_hgrn_out.astype(BF16)
    w_lru_in_bf = w_lru_in.astype(BF16)
    w_lru_out_bf = w_lru_out.astype(BF16)
    w_r_bf = w_lru_rgate.astype(BF16)
    w_i_bf = w_lru_igate.astype(BF16)
    w_ffn_gu_bf = w_ffn_gu.astype(BF16)
    w_ffn_down_bf = w_ffn_down.astype(BF16)
    w_router_pad = jnp.pad(w_router, ((0, 0), (0, 0), (0, LANES - N_EXPERTS)))

    hg_p = hg_s = None
    h_p, h_s, c_p, c_s = [], [], [], []
    for layer in range(DEPTH):
        j = layer // 2
        lng, lnb = vec(ln_mix_g, layer), vec(ln_mix_b, layer)
        if layer % 2 == 0:
            ng = vec(hgrn_norm_g, j)
            xo, hg_p = hgrn_prompt_layer(x, w_hgrn_in_bf[j], hgrn_lb_logits, ng, w_hgrn_out_bf[j],
                                         lng, lnb, hg_p, j, bp, tp, n)
            q, k, lf, v, g = hgrn_in(x_smp, w_hgrn_in_bf[j], hgrn_lb_logits, j, smp_row0, bs * ts)
            o, hg_s = hgrn_rec_sample(q, k, lf, v, g, ng, state_hgrn, hg_s, j, bs, ts)
            x = proj_ln(o, w_hgrn_out_bf[j], x_smp, smp_row0, lng, lnb, xo, n_p)
            x, moe_gu_bf, moe_down_bf = ffn_ln(x, w_ffn_gu_bf[j], w_ffn_down_bf[j], vec(ln_ffn_g, layer),
                                               vec(ln_ffn_b, layer), w_moe_gu, w_moe_down, j)
            x_smp, smp_row0 = x, n_p
        else:
            wts = (lru_conv_w[j], vec(lru_conv_b, j), w_r_bf[j], vec(b_lru_rgate, j),
                   w_i_bf[j], vec(b_lru_igate, j), vec(lru_lambda, j))
            xo, hp, cp = lru_prompt_layer(x, w_lru_in_bf[j], wts, w_lru_out_bf[j], lng, lnb, bp, tp)
            xb, gate = lru_in(x_smp, w_lru_in_bf[j], smp_row0, bs * ts)
            cc_rows = jnp.pad(state_lru_conv[j], ((0, 0), (SUBLANES - (CONV_W - 1), 0), (0, 0)))
            h0_rows = jnp.pad(state_lru_h[j][:, None, :], ((0, 0), (0, ts - 1), (0, 0)))
            hgate, hs, cs = lru_rec_sample(xb, gate, cc_rows.reshape(bs * SUBLANES, D_MODEL),
                                           h0_rows.reshape(bs * ts, D_MODEL), wts, bs, ts)
            h_p.append(hp.reshape(bp, D_MODEL))
            h_s.append(hs.reshape(bs, D_MODEL))
            c_p.append(cp)
            c_s.append(cs)
            x = proj_ln(hgate, w_lru_out_bf[j], x_smp, smp_row0, lng, lnb, xo, n_p)
            x = moe_ffn_ln(x, w_router_pad[j], moe_gu_bf, moe_down_bf,
                           vec(ln_ffn_g, layer), vec(ln_ffn_b, layer),
                           split_rows=n_p if layer == DEPTH - 1 else None)
            x_smp, smp_row0 = x, n_p

    y_prompt = x[0].reshape(bp, tp, D_MODEL)
    y_sample = x[1].reshape(bs, ts, D_MODEL)
    return (y_prompt, y_sample, hg_p, hg_s, jnp.stack(h_p), jnp.stack(h_s), jnp.stack(c_p), jnp.stack(c_s))
```

```python
import functools
import math

import jax
import jax.numpy as jnp
from jax import lax
from jax.experimental import pallas as pl
from jax.experimental.pallas import tpu as pltpu

F32 = jnp.float32
BF16 = jnp.bfloat16
I32 = jnp.int32

D_MODEL = 1024
DEPTH = 4
HG_DK = 128
HG_HEADS = 8
HG_DV = 128
CONV_W = 4
RG_C = 8.0
LRU_BLOCKS = 4
LRU_BW = 256
D_FF = 2816
N_EXPERTS = 8
D_FF_EXPERT = 3584
ALPHA = (2 * DEPTH) ** 0.25
LN_EPS = 1e-5
F_FLOOR = 1e-20
LOG2E = math.log2(math.e)

LANES = 128
SUBLANES = 8
VMEM_LIMIT = 56 * 1024 * 1024

HG_CHUNK_PROMPT = 16
MOE_TILE = 512


def _cparams(sem):
    return pltpu.CompilerParams(dimension_semantics=sem, vmem_limit_bytes=VMEM_LIMIT)


def _sigmoid(x):
    return 0.5 * jnp.tanh(0.5 * x) + 0.5


def _gelu_tanh(y):
    c = math.sqrt(2.0 / math.pi)
    return 0.5 * y * (1.0 + jnp.tanh(c * (y + 0.044715 * (y * y * y))))


def _layer_norm_rows(z, g, b):
    mu = jnp.mean(z, axis=-1, keepdims=True)
    zc = z - mu
    var = jnp.mean(zc * zc, axis=-1, keepdims=True)
    return zc * lax.rsqrt(var + LN_EPS) * g + b


def _hgrn_in_compute(layer_j, xb, w_ref, lbl_ref, q_ref, k_ref, lf_ref, v_ref, g_ref):
    l = lbl_ref[...]
    e = jnp.exp(l - jnp.max(l, axis=0, keepdims=True))
    p = e / jnp.sum(e, axis=0, keepdims=True)
    cs = p[0:1]
    for i in range(1, layer_j + 1):
        cs = cs + p[i:i + 1]
    lb = cs - p[0:1]

    def seg(i):
        return jnp.dot(xb, w_ref[:, i * D_MODEL:(i + 1) * D_MODEL], preferred_element_type=F32)

    q = seg(0)
    q_ref[...] = q * _sigmoid(q) * (HG_DK ** -0.5)
    fp = seg(1)
    sig = 1.0 / (1.0 + jnp.exp(-fp))
    f = lb + (1.0 - lb) * sig
    lf_ref[...] = jnp.log(jnp.maximum(f, F_FLOOR))
    k_ref[...] = (1.0 - lb) * (1.0 - sig)
    v_ref[...] = seg(2)
    g = seg(3)
    g_ref[...] = g * _sigmoid(g)


def _hgrn_in_kernel(layer_j, x_ref, w_ref, lbl_ref, *out_refs):
    _hgrn_in_compute(layer_j, x_ref[...].astype(BF16), w_ref, lbl_ref, *out_refs)


def hgrn_in(x, w_in_bf, lb_logits, layer_j, row0, nrows, tm=256):
    base = row0 // tm
    out = jax.ShapeDtypeStruct((nrows, D_MODEL), F32)
    return pl.pallas_call(
        functools.partial(_hgrn_in_kernel, layer_j),
        grid=(nrows // tm,),
        in_specs=[pl.BlockSpec((tm, D_MODEL), lambda i: (base + i, 0)),
                  pl.BlockSpec((D_MODEL, 4 * D_MODEL), lambda i: (0, 0)),
                  pl.BlockSpec(lb_logits.shape, lambda i: (0, 0))],
        out_specs=[pl.BlockSpec((tm, D_MODEL), lambda i: (i, 0))] * 5,
        out_shape=[out] * 5,
        compiler_params=_cparams(("arbitrary",)),
        name="hgrn_in",
    )(x, w_in_bf, lb_logits)


def _hgrn_prefix(C, lf):
    row = lax.broadcasted_iota(I32, (C, LANES), 0)
    G = lf
    d = 1
    while d < C:
        G = G + jnp.where(row >= d, pltpu.roll(G, d, 0), 0.0)
        d *= 2
    return G


def _hgrn_diag(C, G, q, k, v):
    row8 = lax.broadcasted_iota(I32, (SUBLANES, LANES), 0)
    parts = []
    for gi in range(C // SUBLANES):
        lo = gi * SUBLANES
        Gg = G[lo:lo + SUBLANES]
        qg = q[lo:lo + SUBLANES]
        acc = jnp.zeros((SUBLANES, LANES), F32)
        for s in range(lo + SUBLANES):
            dec = jnp.exp(Gg - G[s:s + 1])
            if s >= lo:
                dec = jnp.where(row8 >= (s - lo), dec, 0.0)
            w = jnp.sum(qg * dec * k[s:s + 1], axis=-1, keepdims=True)
            acc = acc + w * v[s:s + 1]
        parts.append(acc)
    return parts[0] if len(parts) == 1 else jnp.concatenate(parts, axis=0)


def _hgrn_state_update(C, G, k, v, S):
    gend = G[C - 1:C, :]
    kt = k * jnp.exp(gend - G)
    ecol = jnp.transpose(jnp.broadcast_to(jnp.exp(gend), (HG_DK, LANES)))
    upd = lax.dot_general(kt.astype(BF16), v.astype(BF16), (((0,), (0,)), ((), ())),
                          preferred_element_type=F32)
    return ecol * S + upd


def _hgrn_chunk_head(C, q, k, lf, v, S):
    G = _hgrn_prefix(C, lf)
    o = jnp.dot((q * jnp.exp(G)).astype(BF16), S.astype(BF16), preferred_element_type=F32)
    return o + _hgrn_diag(C, G, q, k, v), _hgrn_state_update(C, G, k, v, S)


def _hgrn_rec_kernel(C, q_ref, k_ref, lf_ref, v_ref, g_ref, ng_ref, s0_ref, o_ref, s_ref):
    heads = [slice(h * LANES, (h + 1) * LANES) for h in range(HG_HEADS)]

    def body(i, carry):
        rows = pl.ds(pl.multiple_of(i * C, C), C)
        gs, os_ = [], []
        for h, sl in enumerate(heads):
            G = _hgrn_prefix(C, lf_ref[rows, sl])
            gs.append(G)
            os_.append(jnp.dot((q_ref[rows, sl] * jnp.exp(G)).astype(BF16), s0_ref[i, h].astype(BF16),
                               preferred_element_type=F32))
        for h, sl in enumerate(heads):
            o = os_[h] + _hgrn_diag(C, gs[h], q_ref[rows, sl], k_ref[rows, sl], v_ref[rows, sl])
            ms = jnp.mean(o * o, axis=-1, keepdims=True)
            o_ref[rows, sl] = o * lax.rsqrt(ms + LN_EPS) * ng_ref[:, sl] * g_ref[rows, sl]
        for h, sl in enumerate(heads):
            s_ref[i, h] = _hgrn_state_update(C, gs[h], k_ref[rows, sl], v_ref[rows, sl], s0_ref[i, h])
        return carry

    lax.fori_loop(0, q_ref.shape[0] // C, body, 0)


HG_CHUNK = 64
HG_SUB = 16
HG_SAFE_RANGE = 60.0


def _hgrn_chunk_masks(tt):
    import numpy as np
    t = np.arange(tt)[:, None]
    s = np.arange(tt)[None, :]
    tri = (s <= t) & ((t // HG_CHUNK) == (s // HG_CHUNK))
    return jnp.asarray(tri.astype(np.float32), BF16)


def _hgrn_prompt_kernel(layer_j, x_ref, win_ref, lbl_ref, ng_ref, m_ref, wout_ref, lng_ref, lnb_ref,
                        xo_ref, s_ref,
                        q_ref, k_ref, lf_ref, v_ref, g_ref, o_ref,
                        g_s, qi_s, ks_s, qd_s, kd_s, q1_s, k1_s, q2_s, k2_s, a_s, flag_s):
    tt = x_ref.shape[0]

    @pl.when(pl.program_id(1) == 0)
    def _():
        s_ref[...] = jnp.zeros(s_ref.shape, F32)

    _hgrn_in_compute(layer_j, x_ref[...].astype(BF16), win_ref, lbl_ref, q_ref, k_ref, lf_ref, v_ref, g_ref)

    lf = lf_ref[...] * LOG2E
    hi = lf.astype(BF16)
    lo = (lf - hi.astype(F32)).astype(BF16)
    g_s[...] = (jnp.dot(m_ref[...], hi, preferred_element_type=F32)
                + jnp.dot(m_ref[...], lo, preferred_element_type=F32))

    C = HG_CHUNK
    SB = HG_SUB
    worst = jnp.zeros((SB, D_MODEL), F32)
    for blk in range(tt // SB):
        r0 = blk * SB
        c0 = (r0 // C) * C
        rows = slice(r0, r0 + SB)

        def grow(r):
            return g_s[r - 1:r, :] if r > c0 else jnp.zeros((1, D_MODEL), F32)

        G = g_s[rows, :]
        q = q_ref[rows, :]
        k = k_ref[rows, :]
        qi_s[rows, :] = (q * jnp.exp2(G)).astype(BF16)
        ks_s[rows, :] = (k * jnp.exp2(g_s[c0 + C - 1:c0 + C, :] - G)).astype(BF16)
        e0 = G - grow(r0)
        qd_s[rows, :] = (q * jnp.exp2(e0)).astype(BF16)
        kd_s[rows, :] = (k * jnp.exp2(-e0)).astype(BF16)
        worst = jnp.maximum(worst, -e0)
        e1 = G - grow((r0 // (2 * SB)) * (2 * SB) + SB)
        q1_s[rows, :] = (q * jnp.exp2(jnp.minimum(e1, 0.0))).astype(BF16)
        k1_s[rows, :] = (k * jnp.exp2(jnp.minimum(-e1, 0.0))).astype(BF16)
        e2 = G - grow(c0 + C // 2)
        q2_s[rows, :] = (q * jnp.exp2(jnp.minimum(e2, 0.0))).astype(BF16)
        k2_s[rows, :] = (k * jnp.exp2(jnp.minimum(-e2, 0.0))).astype(BF16)
    flag_s[0] = (jnp.max(worst) <= HG_SAFE_RANGE * LOG2E).astype(I32)

    def finish(rows, sl, o):
        ms = jnp.mean(o * o, axis=-1, keepdims=True)
        o_ref[rows, sl] = o * lax.rsqrt(ms + LN_EPS) * ng_ref[:, sl] * g_ref[rows, sl]

    C = HG_CHUNK
    ti = lax.broadcasted_iota(I32, (C, C), 0)
    si = lax.broadcasted_iota(I32, (C, C), 1)
    causal = si <= ti
    sub_bits = HG_SUB.bit_length() - 1
    m0 = causal & ((ti >> sub_bits) == (si >> sub_bits))
    m1 = causal & ((ti >> (sub_bits + 1)) == (si >> (sub_bits + 1))) & jnp.logical_not(m0)
    nt_dims = (((1,), (1,)), ((), ()))

    @pl.when(flag_s[0] == 1)
    def _():
        def body(c, carry):
            r0 = pl.multiple_of(c * C, C)
            rows = pl.ds(r0, C)
            heads = [slice(h * LANES, (h + 1) * LANES) for h in range(HG_HEADS)]
            for h, sl in enumerate(heads):
                a0 = lax.dot_general(qd_s[rows, sl], kd_s[rows, sl], nt_dims, preferred_element_type=F32)
                a1 = lax.dot_general(q1_s[rows, sl], k1_s[rows, sl], nt_dims, preferred_element_type=F32)
                a2 = lax.dot_general(q2_s[rows, sl], k2_s[rows, sl], nt_dims, preferred_element_type=F32)
                a = jnp.where(m0, a0, jnp.where(m1, a1, jnp.where(causal, a2, 0.0)))
                a_s[h] = a.astype(BF16)
            for h, sl in enumerate(heads):
                o = (jnp.dot(qi_s[rows, sl], s_ref[0, h].astype(BF16), preferred_element_type=F32)
                     + jnp.dot(a_s[h], v_ref[rows, sl].astype(BF16), preferred_element_type=F32))
                finish(rows, sl, o)
            for h, sl in enumerate(heads):
                gend = g_s[pl.ds(pl.multiple_of(r0 + C - SUBLANES, SUBLANES), SUBLANES), sl][SUBLANES - 1:]
                ecol = jnp.transpose(jnp.broadcast_to(jnp.exp2(gend), (HG_DK, LANES)))
                upd = lax.dot_general(ks_s[rows, sl], v_ref[rows, sl].astype(BF16), (((0,), (0,)), ((), ())),
                                      preferred_element_type=F32)
                s_ref[0, h] = ecol * s_ref[0, h] + upd
            return carry

        lax.fori_loop(0, tt // C, body, 0)

    @pl.when(flag_s[0] == 0)
    def _():
        Cs = HG_CHUNK_PROMPT

        def body(i, carry):
            r0 = pl.multiple_of(i * Cs, Cs)
            rows = pl.ds(r0, Cs)
            for h in range(HG_HEADS):
                sl = slice(h * LANES, (h + 1) * LANES)
                o, s_new = _hgrn_chunk_head(Cs, q_ref[rows, sl], k_ref[rows, sl], lf_ref[rows, sl],
                                            v_ref[rows, sl], s_ref[0, h])
                finish(rows, sl, o)
                s_ref[0, h] = s_new
            return carry

        lax.fori_loop(0, tt // Cs, body, 0)

    y = jnp.dot(o_ref[...].astype(BF16), wout_ref[...], preferred_element_type=F32)
    xo_ref[...] = _layer_norm_rows(ALPHA * x_ref[...] + y, lng_ref[...], lnb_ref[...])


def hgrn_prompt_layer(x, w_in_bf, lb_logits, norm_g, w_out_bf, ln_g, ln_b, s_all, layer_j, batch, seq, n,
                      tt=256):
    nt = seq // tt
    row = pl.BlockSpec((tt, D_MODEL), lambda b, t: (b * nt + t, 0))
    vec = pl.BlockSpec((1, D_MODEL), lambda b, t: (0, 0))
    s_spec = pl.BlockSpec((None, 1, HG_HEADS, HG_DK, HG_DV), lambda b, t: (layer_j, b, 0, 0, 0))
    in_specs = [row,
                pl.BlockSpec((D_MODEL, 4 * D_MODEL), lambda b, t: (0, 0)),
                pl.BlockSpec(lb_logits.shape, lambda b, t: (0, 0)),
                vec,
                pl.BlockSpec((tt, tt), lambda b, t: (0, 0)),
                pl.BlockSpec((D_MODEL, D_MODEL), lambda b, t: (0, 0)),
                vec, vec]
    args = [x, w_in_bf, lb_logits, norm_g, _hgrn_chunk_masks(tt), w_out_bf, ln_g, ln_b]
    aliases = {}
    kern = functools.partial(_hgrn_prompt_kernel, layer_j)
    if s_all is not None:
        def kern(*refs):
            _hgrn_prompt_kernel(layer_j, *refs[:8], *refs[9:])
        in_specs.append(pl.BlockSpec(memory_space=pl.ANY))
        args.append(s_all)
        aliases = {8: 1}
    f32_tile = pltpu.VMEM((tt, D_MODEL), F32)
    bf_tile = pltpu.VMEM((tt, D_MODEL), BF16)
    return pl.pallas_call(
        kern,
        grid=(batch, nt),
        in_specs=in_specs,
        out_specs=[row, s_spec],
        out_shape=[jax.ShapeDtypeStruct((n, D_MODEL), F32),
                   jax.ShapeDtypeStruct((2, batch, HG_HEADS, HG_DK, HG_DV), F32)],
        scratch_shapes=[f32_tile] * 7 + [bf_tile] * 8
                       + [pltpu.VMEM((HG_HEADS, HG_CHUNK, HG_CHUNK), BF16), pltpu.SMEM((1,), I32)],
        input_output_aliases=aliases,
        compiler_params=_cparams(("arbitrary", "arbitrary")),
        name="hgrn_prompt_layer",
    )(*args)


def hgrn_rec_sample(q, k, lf, v, g, norm_g, state, s_all, layer_j, nseq, seq, sg=8):
    n = q.shape[0]
    rt = sg * seq
    row = pl.BlockSpec((rt, D_MODEL), lambda i: (i, 0))
    s_in = pl.BlockSpec((None, sg, HG_HEADS, HG_DK, HG_DV), lambda i: (layer_j, i, 0, 0, 0))
    in_specs = [row] * 5 + [pl.BlockSpec((1, D_MODEL), lambda i: (0, 0)), s_in]
    args = [q, k, lf, v, g, norm_g, state]
    aliases = {}
    if s_all is not None:
        in_specs.append(pl.BlockSpec(memory_space=pl.ANY))
        args.append(s_all)
        aliases[7] = 1

    def kern(*refs):
        _hgrn_rec_kernel(seq, *refs[:7], *refs[len(args):])

    return pl.pallas_call(
        kern,
        grid=(nseq // sg,),
        in_specs=in_specs,
        out_specs=[row, s_in],
        out_shape=[jax.ShapeDtypeStruct((n, D_MODEL), F32),
                   jax.ShapeDtypeStruct((2, nseq, HG_HEADS, HG_DK, HG_DV), F32)],
        input_output_aliases=aliases,
        compiler_params=_cparams(("arbitrary",)),
        name="hgrn_rec_sample",
    )(*args)


def _proj_ln_kernel(a_ref, w_ref, x_ref, g_ref, b_ref, o_ref):
    y = jnp.dot(a_ref[...].astype(BF16), w_ref[...], preferred_element_type=F32)
    o_ref[...] = _layer_norm_rows(ALPHA * x_ref[...] + y, g_ref[...], b_ref[...])


def proj_ln(a, w_bf, x, x_row0, g, b, xo, row0, tm=512):
    n = xo.shape[0]
    nrows = a.shape[0]
    base = row0 // tm
    xbase = x_row0 // tm
    row = pl.BlockSpec((tm, D_MODEL), lambda i: (base + i, 0))
    vec = pl.BlockSpec((1, D_MODEL), lambda i: (0, 0))

    def kern(a_ref, w_ref, x_ref, g_ref, b_ref, _, o_ref):
        _proj_ln_kernel(a_ref, w_ref, x_ref, g_ref, b_ref, o_ref)

    return pl.pallas_call(
        kern,
        grid=(nrows // tm,),
        in_specs=[pl.BlockSpec((tm, D_MODEL), lambda i: (i, 0)),
                  pl.BlockSpec((D_MODEL, D_MODEL), lambda i: (0, 0)),
                  pl.BlockSpec((tm, D_MODEL), lambda i: (xbase + i, 0)), vec, vec,
                  pl.BlockSpec(memory_space=pl.ANY)],
        out_specs=row,
        out_shape=jax.ShapeDtypeStruct((n, D_MODEL), F32),
        input_output_aliases={5: 0},
        compiler_params=_cparams(("arbitrary",)),
        name="proj_ln",
    )(a, w_bf, x, g, b, xo)


CAST_STEPS = 64


def _ffn_kernel(x_ref, wg_ref, wu_ref, wd_ref, g_ref, b_ref, cgu_ref, cdn_ref, o_ref, cgu_o, cdn_o, acc_ref):
    f = pl.program_id(1)
    cgu_o[...] = cgu_ref[...].astype(BF16)
    cdn_o[...] = cdn_ref[...].astype(BF16)
    xb = x_ref[...].astype(BF16)
    gg = jnp.dot(xb, wg_ref[...], preferred_element_type=F32)
    uu = jnp.dot(xb, wu_ref[...], preferred_element_type=F32)
    h = (gg * _sigmoid(gg) * uu).astype(BF16)
    part = jnp.dot(h, wd_ref[...], preferred_element_type=F32)

    @pl.when(f == 0)
    def _():
        acc_ref[...] = part

    @pl.when(f > 0)
    def _():
        acc_ref[...] += part

    @pl.when(f == pl.num_programs(1) - 1)
    def _():
        o_ref[...] = _layer_norm_rows(ALPHA * x_ref[...] + acc_ref[...], g_ref[...], b_ref[...])


def ffn_ln(x, w_gu_bf, w_down_bf, g, b, moe_gu, moe_down, layer_j, tm=512, tf=1408):
    n = x.shape[0]
    nf = D_FF // tf
    assert (n // tm) * nf >= CAST_STEPS
    gu2 = moe_gu.reshape(moe_gu.shape[0], N_EXPERTS * D_MODEL, 2 * D_FF_EXPERT)
    dn2 = moe_down.reshape(moe_down.shape[0], N_EXPERTS * D_FF_EXPERT, D_MODEL)
    gu_rows = gu2.shape[1] // CAST_STEPS
    dn_rows = dn2.shape[1] // CAST_STEPS
    row = pl.BlockSpec((tm, D_MODEL), lambda i, f: (i, 0))
    vec = pl.BlockSpec((1, D_MODEL), lambda i, f: (0, 0))

    def slab(i, f):
        return jnp.minimum(i * nf + f, CAST_STEPS - 1)

    xo, gu_bf, dn_bf = pl.pallas_call(
        _ffn_kernel,
        grid=(n // tm, nf),
        in_specs=[row,
                  pl.BlockSpec((D_MODEL, tf), lambda i, f: (0, f)),
                  pl.BlockSpec((D_MODEL, tf), lambda i, f: (0, nf + f)),
                  pl.BlockSpec((tf, D_MODEL), lambda i, f: (f, 0)),
                  vec, vec,
                  pl.BlockSpec((None, gu_rows, 2 * D_FF_EXPERT), lambda i, f: (layer_j, slab(i, f), 0)),
                  pl.BlockSpec((None, dn_rows, D_MODEL), lambda i, f: (layer_j, slab(i, f), 0))],
        out_specs=[row,
                   pl.BlockSpec((gu_rows, 2 * D_FF_EXPERT), lambda i, f: (slab(i, f), 0)),
                   pl.BlockSpec((dn_rows, D_MODEL), lambda i, f: (slab(i, f), 0))],
        out_shape=[jax.ShapeDtypeStruct((n, D_MODEL), F32),
                   jax.ShapeDtypeStruct(gu2.shape[1:], BF16),
                   jax.ShapeDtypeStruct(dn2.shape[1:], BF16)],
        scratch_shapes=[pltpu.VMEM((tm, D_MODEL), F32)],
        compiler_params=_cparams(("arbitrary", "arbitrary")),
        name="ffn_ln",
    )(x, w_gu_bf, w_gu_bf, w_down_bf, g, b, gu2, dn2)
    return (xo, gu_bf.reshape(N_EXPERTS, D_MODEL, 2 * D_FF_EXPERT),
            dn_bf.reshape(N_EXPERTS, D_FF_EXPERT, D_MODEL))


def _lru_in_kernel(x_ref, w_ref, xb_ref, gate_ref):
    xb = x_ref[...].astype(BF16)
    xb_ref[...] = jnp.dot(xb, w_ref[:, :D_MODEL], preferred_element_type=F32)
    y = jnp.dot(xb, w_ref[:, D_MODEL:], preferred_element_type=F32)
    gate_ref[...] = _gelu_tanh(y)


def lru_in(x, w_in_bf, row0, nrows, tm=512):
    base = row0 // tm
    row = pl.BlockSpec((tm, D_MODEL), lambda i: (i, 0))
    out = jax.ShapeDtypeStruct((nrows, D_MODEL), F32)
    return pl.pallas_call(
        _lru_in_kernel,
        grid=(nrows // tm,),
        in_specs=[pl.BlockSpec((tm, D_MODEL), lambda i: (base + i, 0)),
                  pl.BlockSpec((D_MODEL, 2 * D_MODEL), lambda i: (0, 0))],
        out_specs=[row, row],
        out_shape=[out, out],
        compiler_params=_cparams(("arbitrary",)),
        name="lru_in",
    )(x, w_in_bf)


def _lru_rec_kernel(L, carry, *refs):
    if carry:
        (x_ref, win_ref, cw_ref, cb_ref, wr_ref, br_ref, wi_ref, bi_ref, lam_ref, wout_ref, lng_ref, lnb_ref,
         xo_ref, hn_ref, cn_ref, xp_ref, hs_ref, hc_ref, gate_ref, hg_ref) = refs
        R = x_ref.shape[0]
    else:
        (xb_ref, gate_ref, cc_ref, h0_ref, cw_ref, cb_ref, wr_ref, br_ref, wi_ref, bi_ref, lam_ref,
         hg_ref, hn_ref, cn_ref, xp_ref, hs_ref) = refs
        R = xb_ref.shape[0]
    S = R // L
    PADR = SUBLANES
    t = pl.program_id(1) if carry else 0

    if carry:
        @pl.when(t == 0)
        def _():
            xp_ref[:, 0:PADR, :] = jnp.zeros((S, PADR, D_MODEL), F32)
            hc_ref[...] = jnp.zeros(hc_ref.shape, F32)

        @pl.when(t > 0)
        def _():
            xp_ref[:, 0:PADR, :] = xp_ref[:, L:L + PADR, :]

        xin = x_ref[...].astype(BF16)
        xp_ref[:, PADR:PADR + L, :] = jnp.dot(xin, win_ref[:, :D_MODEL],
                                              preferred_element_type=F32).reshape(S, L, D_MODEL)
        gate_ref[...] = _gelu_tanh(jnp.dot(xin, win_ref[:, D_MODEL:], preferred_element_type=F32))
    else:
        xp_ref[:, 0:PADR, :] = cc_ref[...].reshape(S, PADR, D_MODEL)
        xp_ref[:, PADR:PADR + L, :] = xb_ref[...].reshape(S, L, D_MODEL)

    xc = cb_ref[...].reshape(1, 1, D_MODEL) + xp_ref[:, PADR:PADR + L, :] * cw_ref[CONV_W - 1:CONV_W, :].reshape(1, 1, D_MODEL)
    for j in range(CONV_W - 1):
        off = PADR - (CONV_W - 1) + j
        xc = xc + xp_ref[:, off:off + L, :] * cw_ref[j:j + 1, :].reshape(1, 1, D_MODEL)
    xc = xc.reshape(R, D_MODEL)
    cn_ref[...] = xp_ref[:, L + PADR - (CONV_W - 1):L + PADR, :].reshape(cn_ref.shape)

    assert L & (L - 1) == 0
    pos = jnp.bitwise_and(lax.broadcasted_iota(I32, (R, LRU_BW), 0), L - 1)
    for nb in range(LRU_BLOCKS):
        sl = slice(nb * LRU_BW, (nb + 1) * LRU_BW)
        xcb = xc[:, sl]
        xh = xcb.astype(BF16)
        r = _sigmoid(jnp.dot(xh, wr_ref[nb], preferred_element_type=F32) + br_ref[:, sl])
        ig = _sigmoid(jnp.dot(xh, wi_ref[nb], preferred_element_type=F32) + bi_ref[:, sl])
        z = -lam_ref[:, sl]
        softplus = jnp.maximum(z, 0.0) + jnp.log1p(jnp.exp(-jnp.abs(z)))
        log_a = (-RG_C) * r * softplus
        a = jnp.exp(log_a)
        u = jnp.sqrt(1.0 - a * a) * (ig * xcb)
        if carry:
            ng = R // SUBLANES
            u3 = u.reshape(ng, SUBLANES, LRU_BW)
            a3 = a.reshape(ng, SUBLANES, LRU_BW)
            pos8 = lax.broadcasted_iota(I32, (ng, SUBLANES, LRU_BW), 1)
            d = 1
            while d < SUBLANES:
                m = pos8 >= d
                u3 = jnp.where(m, a3 * pltpu.roll(u3, d, 1) + u3, u3)
                a3 = jnp.where(m, a3 * pltpu.roll(a3, d, 1), a3)
                d *= 2
            hprev = hc_ref[:, sl]
            for gi in range(ng):
                hgrp = u3[gi] + a3[gi] * hprev
                hs_ref[0, gi * SUBLANES:(gi + 1) * SUBLANES, sl] = hgrp
                hprev = hgrp[SUBLANES - 1:SUBLANES]
            hg_ref[:, sl] = hs_ref[0, :, sl] * gate_ref[:, sl]
        else:
            u = u + a * h0_ref[:, sl]
            d = 1
            while d < L:
                m = pos >= d
                u = jnp.where(m, a * pltpu.roll(u, d, 0) + u, u)
                a = jnp.where(m, a * pltpu.roll(a, d, 0), a)
                d *= 2
            hg_ref[:, sl] = u * gate_ref[:, sl]
            hs_ref[:, :, sl] = u.reshape(S, L, LRU_BW)

    hlast = hs_ref[:, L - 1:L, :]
    hn_ref[...] = hlast.reshape(hn_ref.shape)
    if carry:
        hc_ref[...] = hlast.reshape(1, D_MODEL)
        y = jnp.dot(hg_ref[...].astype(BF16), wout_ref[...], preferred_element_type=F32)
        xo_ref[...] = _layer_norm_rows(ALPHA * x_ref[...] + y, lng_ref[...], lnb_ref[...])


def _lru_weight_specs(nidx):
    def c2(*_):
        return (0, 0)

    def c3(*_):
        return (0, 0, 0)
    return [pl.BlockSpec((CONV_W, D_MODEL), c2), pl.BlockSpec((1, D_MODEL), c2),
            pl.BlockSpec((LRU_BLOCKS, LRU_BW, LRU_BW), c3), pl.BlockSpec((1, D_MODEL), c2),
            pl.BlockSpec((LRU_BLOCKS, LRU_BW, LRU_BW), c3), pl.BlockSpec((1, D_MODEL), c2),
            pl.BlockSpec((1, D_MODEL), c2)]


def lru_prompt_layer(x, w_in_bf, wts, w_out_bf, ln_g, ln_b, batch, seq, tt=256):
    n = x.shape[0]
    nt = seq // tt
    row = pl.BlockSpec((tt, D_MODEL), lambda b, t: (b * nt + t, 0))
    vec = pl.BlockSpec((1, D_MODEL), lambda b, t: (0, 0))
    return pl.pallas_call(
        functools.partial(_lru_rec_kernel, tt, True),
        grid=(batch, nt),
        in_specs=[row, pl.BlockSpec((D_MODEL, 2 * D_MODEL), lambda b, t: (0, 0))] + _lru_weight_specs(2)
                 + [pl.BlockSpec((D_MODEL, D_MODEL), lambda b, t: (0, 0)), vec, vec],
        out_specs=[row,
                   pl.BlockSpec((1, 1, D_MODEL), lambda b, t: (b, 0, 0)),
                   pl.BlockSpec((1, CONV_W - 1, D_MODEL), lambda b, t: (b, 0, 0))],
        out_shape=[jax.ShapeDtypeStruct((n, D_MODEL), F32),
                   jax.ShapeDtypeStruct((batch, 1, D_MODEL), F32),
                   jax.ShapeDtypeStruct((batch, CONV_W - 1, D_MODEL), F32)],
        scratch_shapes=[pltpu.VMEM((1, tt + SUBLANES, D_MODEL), F32),
                        pltpu.VMEM((1, tt, D_MODEL), F32),
                        pltpu.VMEM((1, D_MODEL), F32),
                        pltpu.VMEM((tt, D_MODEL), F32),
                        pltpu.VMEM((tt, D_MODEL), F32)],
        compiler_params=_cparams(("arbitrary", "arbitrary")),
        name="lru_prompt_layer",
    )(x, w_in_bf, *wts, w_out_bf, ln_g, ln_b)


def lru_rec_sample(xb, gate, cc_rows, h0_rows, wts, nseq, seq, sg=32):
    n = xb.shape[0]
    rt = sg * seq
    row = pl.BlockSpec((rt, D_MODEL), lambda i: (i, 0))
    return pl.pallas_call(
        functools.partial(_lru_rec_kernel, seq, False),
        grid=(nseq // sg,),
        in_specs=[row, row, pl.BlockSpec((sg * SUBLANES, D_MODEL), lambda i: (i, 0)), row]
                 + _lru_weight_specs(1),
        out_specs=[row,
                   pl.BlockSpec((sg, 1, D_MODEL), lambda i: (i, 0, 0)),
                   pl.BlockSpec((sg, CONV_W - 1, D_MODEL), lambda i: (i, 0, 0))],
        out_shape=[jax.ShapeDtypeStruct((n, D_MODEL), F32),
                   jax.ShapeDtypeStruct((nseq, 1, D_MODEL), F32),
                   jax.ShapeDtypeStruct((nseq, CONV_W - 1, D_MODEL), F32)],
        scratch_shapes=[pltpu.VMEM((sg, seq + SUBLANES, D_MODEL), F32),
                        pltpu.VMEM((sg, seq, D_MODEL), F32)],
        compiler_params=_cparams(("arbitrary",)),
        name="lru_rec_sample",
    )(xb, gate, cc_rows, h0_rows, *wts)


def _router_kernel(x_ref, w_ref, mi_ref, mf_ref, cnt_ref, run_ref):
    i = pl.program_id(0)

    @pl.when(i == 0)
    def _():
        run_ref[...] = jnp.zeros(run_ref.shape, F32)

    tm = x_ref.shape[0]
    logits = jnp.dot(x_ref[...], w_ref[...], preferred_element_type=F32,
                     precision=lax.Precision.HIGHEST)
    lane_i = lax.broadcasted_iota(I32, (tm, LANES), 1)
    lane = lane_i.astype(F32)
    neg = jnp.float32(-jnp.inf)
    logits = jnp.where(lane_i < N_EXPERTS, logits, neg)
    m1 = jnp.max(logits, axis=-1, keepdims=True)
    i1 = jnp.min(jnp.where(logits == m1, lane, float(LANES)), axis=-1, keepdims=True)
    l2 = jnp.where(lane == i1, neg, logits)
    m2 = jnp.max(l2, axis=-1, keepdims=True)
    i2 = jnp.min(jnp.where(l2 == m2, lane, float(LANES)), axis=-1, keepdims=True)
    e2 = jnp.exp(m2 - m1)
    p1 = 1.0 / (1.0 + e2)
    p2 = e2 * p1

    hit1 = lane == i1
    hit2 = lane == i2
    onehot = jnp.where(hit1 | hit2, 1.0, 0.0)
    r_i = lax.broadcasted_iota(I32, (tm, tm), 0)
    c_i = lax.broadcasted_iota(I32, (tm, tm), 1)
    tri = jnp.where(c_i < r_i, 1.0, 0.0).astype(BF16)
    rank = jnp.dot(tri, onehot.astype(BF16), preferred_element_type=F32) + run_ref[0:1, :]
    r1 = jnp.sum(jnp.where(hit1, rank, 0.0), axis=-1, keepdims=True)
    r2 = jnp.sum(jnp.where(hit2, rank, 0.0), axis=-1, keepdims=True)
    total = run_ref[0:1, :] + jnp.sum(onehot, axis=0, keepdims=True)
    run_ref[...] = jnp.broadcast_to(total, run_ref.shape)
    cnt_ref[...] = jnp.broadcast_to(total, cnt_ref.shape).astype(I32)

    mi = jnp.where(lane_i == 0, i1, jnp.where(lane_i == 1, i2, 0.0))
    mi = jnp.where(lane_i == 2, r1, jnp.where(lane_i == 3, r2, mi))
    mi_ref[...] = mi.astype(I32)
    mf_ref[...] = jnp.where(lane_i == 0, p1, jnp.where(lane_i == 1, p2, 0.0))


def moe_router(x, w_router_pad, tm=512):
    n = x.shape[0]
    row = pl.BlockSpec((tm, LANES), lambda i: (i, 0))
    return pl.pallas_call(
        _router_kernel,
        grid=(n // tm,),
        in_specs=[pl.BlockSpec((tm, D_MODEL), lambda i: (i, 0)),
                  pl.BlockSpec((D_MODEL, LANES), lambda i: (0, 0))],
        out_specs=[row, row, pl.BlockSpec((SUBLANES, LANES), lambda i: (0, 0))],
        out_shape=[jax.ShapeDtypeStruct((n, LANES), I32),
                   jax.ShapeDtypeStruct((n, LANES), F32),
                   jax.ShapeDtypeStruct((SUBLANES, LANES), I32)],
        scratch_shapes=[pltpu.VMEM((SUBLANES, LANES), F32)],
        compiler_params=_cparams(("arbitrary",)),
        name="moe_router",
    )(x, w_router_pad)


ZERO_ROWS = 256


def _dispatch_kernel(dest_ref, ztile_ref, x_ref, xs_hbm, zero_ref, sem):
    i = pl.program_id(0)
    tm = x_ref.shape[0]

    @pl.when(i == 0)
    def _():
        zero_ref[...] = jnp.zeros(zero_ref.shape, F32)
        copies = [pltpu.make_async_copy(
                      zero_ref,
                      xs_hbm.at[pl.ds(pl.multiple_of(ztile_ref[e] + c * ZERO_ROWS, ZERO_ROWS), ZERO_ROWS)], sem)
                  for e in range(N_EXPERTS) for c in range(MOE_TILE // ZERO_ROWS)]
        for cp in copies:
            cp.start()
        for cp in copies:
            cp.wait()

    def group(jj, c):
        r8 = pl.multiple_of(jj * SUBLANES, SUBLANES)
        grp = x_ref.at[pl.ds(r8, SUBLANES)]
        t0 = 2 * (i * tm + r8)
        for u in range(SUBLANES):
            src = grp.at[pl.ds(u, 1)]
            pltpu.make_async_copy(src, xs_hbm.at[pl.ds(dest_ref[t0 + 2 * u], 1)], sem).start()
            pltpu.make_async_copy(src, xs_hbm.at[pl.ds(dest_ref[t0 + 2 * u + 1], 1)], sem).start()
        return c

    lax.fori_loop(0, tm // SUBLANES, group, 0)
    for _ in range(2):
        pltpu.make_async_copy(x_ref, xs_hbm.at[pl.ds(0, tm)], sem).wait()


def moe_dispatch(dest, ztile, x, n_rows, tm=512):
    n = x.shape[0]
    return pl.pallas_call(
        _dispatch_kernel,
        grid_spec=pltpu.PrefetchScalarGridSpec(
            num_scalar_prefetch=2, grid=(n // tm,),
            in_specs=[pl.BlockSpec((tm, D_MODEL), lambda i, d, z: (i, 0))],
            out_specs=pl.BlockSpec(memory_space=pl.ANY),
            scratch_shapes=[pltpu.VMEM((ZERO_ROWS, D_MODEL), F32), pltpu.SemaphoreType.DMA(())]),
        out_shape=jax.ShapeDtypeStruct((n_rows, D_MODEL), F32),
        compiler_params=pltpu.CompilerParams(dimension_semantics=("arbitrary",),
                                             has_side_effects=True),
        name="moe_dispatch",
    )(dest, ztile, x)


def _expert_kernel(te_ref, nu_ref, xs_ref, wg_ref, wu_ref, wd_ref, ys_ref, xb_ref, acc_ref):
    i = pl.program_id(0)
    f = pl.program_id(1)
    used = i < nu_ref[0]

    @pl.when(used & (f == 0))
    def _():
        xb_ref[...] = xs_ref[...].astype(BF16)

    @pl.when(used)
    def _():
        xb = xb_ref[...]
        gg = jnp.dot(xb, wg_ref[...], preferred_element_type=F32)
        uu = jnp.dot(xb, wu_ref[...], preferred_element_type=F32)
        h = (gg * _sigmoid(gg) * uu).astype(BF16)
        part = jnp.dot(h, wd_ref[...], preferred_element_type=F32)

        @pl.when(f == 0)
        def _():
            acc_ref[...] = part

        @pl.when(f > 0)
        def _():
            acc_ref[...] += part

    @pl.when(f == pl.num_programs(1) - 1)
    def _():
        @pl.when(used)
        def _():
            ys_ref[...] = acc_ref[...]

        @pl.when(jnp.logical_not(used))
        def _():
            ys_ref[...] = jnp.zeros(ys_ref.shape, F32)


def moe_experts(tile_expert, n_used, xs, w_gu_bf, w_down_bf, tf=1792):
    n_rows = xs.shape[0]
    tm = MOE_TILE
    nf = D_FF_EXPERT // tf

    def xrow(i, f, te, nu):
        return (jnp.minimum(i, nu[0] - 1), 0)

    def wg(i, f, te, nu):
        return (te[i], 0, jnp.where(i < nu[0], f, nf - 1))

    def wu(i, f, te, nu):
        return (te[i], 0, nf + jnp.where(i < nu[0], f, nf - 1))

    def wd(i, f, te, nu):
        return (te[i], jnp.where(i < nu[0], f, nf - 1), 0)

    return pl.pallas_call(
        _expert_kernel,
        grid_spec=pltpu.PrefetchScalarGridSpec(
            num_scalar_prefetch=2, grid=(n_rows // tm, nf),
            in_specs=[pl.BlockSpec((tm, D_MODEL), xrow),
                      pl.BlockSpec((None, D_MODEL, tf), wg),
                      pl.BlockSpec((None, D_MODEL, tf), wu),
                      pl.BlockSpec((None, tf, D_MODEL), wd)],
            out_specs=pl.BlockSpec((tm, D_MODEL), lambda i, f, te, nu: (i, 0)),
            scratch_shapes=[pltpu.VMEM((tm, D_MODEL), BF16), pltpu.VMEM((tm, D_MODEL), F32)]),
        out_shape=jax.ShapeDtypeStruct((n_rows, D_MODEL), F32),
        compiler_params=_cparams(("arbitrary", "arbitrary")),
        name="moe_experts",
    )(tile_expert, n_used, xs, w_gu_bf, w_gu_bf, w_down_bf)


def _combine_kernel(nsplit, dest_ref, ys_hbm, mf_ref, x_ref, g_ref, b_ref, *rest):
    if nsplit is None:
        o_ref, buf_ref, sem = rest
    else:
        oa_ref, ob_ref, buf_ref, sem = rest
    i = pl.program_id(0)
    n_steps = pl.num_programs(0)
    tm = x_ref.shape[0]

    def gather(step, slot):
        def group(jj, c):
            r8 = pl.multiple_of(jj * SUBLANES, SUBLANES)
            g0 = buf_ref.at[slot, 0, pl.ds(r8, SUBLANES)]
            g1 = buf_ref.at[slot, 1, pl.ds(r8, SUBLANES)]
            t0 = 2 * (step * tm + r8)
            for u in range(SUBLANES):
                pltpu.make_async_copy(ys_hbm.at[pl.ds(dest_ref[t0 + 2 * u], 1)], g0.at[pl.ds(u, 1)],
                                      sem.at[slot]).start()
                pltpu.make_async_copy(ys_hbm.at[pl.ds(dest_ref[t0 + 2 * u + 1], 1)], g1.at[pl.ds(u, 1)],
                                      sem.at[slot]).start()
            return c

        lax.fori_loop(0, tm // SUBLANES, group, 0)

    slot = lax.rem(i, 2)

    @pl.when(i == 0)
    def _():
        gather(i, slot)

    @pl.when(i + 1 < n_steps)
    def _():
        gather(i + 1, 1 - slot)

    for s in range(2):
        pltpu.make_async_copy(ys_hbm.at[pl.ds(0, tm)], buf_ref.at[slot, s], sem.at[slot]).wait()
    mf = mf_ref[...]
    y = buf_ref[slot, 0] * mf[:, 0:1] + buf_ref[slot, 1] * mf[:, 1:2]
    res = _layer_norm_rows(ALPHA * x_ref[...] + y, g_ref[...], b_ref[...])
    if nsplit is None:
        o_ref[...] = res
    else:
        @pl.when(i < nsplit)
        def _():
            oa_ref[...] = res

        @pl.when(i >= nsplit)
        def _():
            ob_ref[...] = res


def moe_combine_ln(dest, ys, mf, x, g, b, split_rows=None, tm=256):
    n = x.shape[0]
    row = pl.BlockSpec((tm, D_MODEL), lambda i, d: (i, 0))
    vec = pl.BlockSpec((1, D_MODEL), lambda i, d: (0, 0))
    if split_rows is None:
        nsplit = None
        out_specs = row
        out_shape = jax.ShapeDtypeStruct((n, D_MODEL), F32)
    else:
        nsplit = split_rows // tm
        out_specs = [pl.BlockSpec((tm, D_MODEL), lambda i, d: (jnp.minimum(i, nsplit - 1), 0)),
                     pl.BlockSpec((tm, D_MODEL), lambda i, d: (jnp.maximum(i - nsplit, 0), 0))]
        out_shape = [jax.ShapeDtypeStruct((split_rows, D_MODEL), F32),
                     jax.ShapeDtypeStruct((n - split_rows, D_MODEL), F32)]
    return pl.pallas_call(
        functools.partial(_combine_kernel, nsplit),
        grid_spec=pltpu.PrefetchScalarGridSpec(
            num_scalar_prefetch=1, grid=(n // tm,),
            in_specs=[pl.BlockSpec(memory_space=pl.ANY),
                      pl.BlockSpec((tm, LANES), lambda i, d: (i, 0)),
                      row, vec, vec],
            out_specs=out_specs,
            scratch_shapes=[pltpu.VMEM((2, 2, tm, D_MODEL), F32), pltpu.SemaphoreType.DMA((2,))]),
        out_shape=out_shape,
        compiler_params=_cparams(("arbitrary",)),
        name="moe_combine_ln",
    )(dest, ys, mf, x, g, b)


def moe_ffn_ln(x, w_router_pad, w_gu, w_down, g, b, split_rows=None):
    n = x.shape[0]
    tm = MOE_TILE
    mi, mf, cnt = moe_router(x, w_router_pad)
    counts = cnt[0, :N_EXPERTS]
    padded = (counts + tm - 1) // tm * tm
    ends = jnp.cumsum(padded)
    starts = ends - padded
    ids = mi[:, 0:2]
    ranks = mi[:, 2:4]
    dest = (starts[ids] + ranks).reshape(-1).astype(I32)
    n_tiles = (2 * n) // tm + N_EXPERTS
    tile_start = jnp.arange(n_tiles, dtype=I32) * tm
    tile_expert = jnp.minimum(jnp.sum(tile_start[:, None] >= ends[None, :], axis=1), N_EXPERTS - 1).astype(I32)
    n_used = (ends[-1] // tm).astype(I32).reshape(1)
    tile_expert = jnp.where(tile_start < ends[-1], tile_expert, tile_expert[jnp.maximum(n_used[0] - 1, 0)])
    ztile = jnp.maximum(ends - tm, 0).astype(I32)
    xs = moe_dispatch(dest, ztile, x, n_tiles * tm)
    ys = moe_experts(tile_expert, n_used, xs, w_gu, w_down)
    return moe_combine_ln(dest, ys, mf, x, g, b, split_rows)


def kernel(x_prompt, x_sample, state_hgrn, state_lru_h, state_lru_conv, ln_mix_g, ln_mix_b, ln_ffn_g, ln_ffn_b, w_hgrn_in, hgrn_lb_logits, hgrn_norm_g, w_hgrn_out, w_lru_in, lru_conv_w, lru_conv_b, w_lru_rgate, b_lru_rgate, w_lru_igate, b_lru_igate, lru_lambda, w_lru_out, w_ffn_gu, w_ffn_down, w_router, w_moe_gu, w_moe_down):
    bp, tp, _ = x_prompt.shape
    bs, ts, _ = x_sample.shape
    n_p = bp * tp
    n = n_p + bs * ts
    x = x_prompt.reshape(n_p, D_MODEL)
    x_smp, smp_row0 = x_sample.reshape(bs * ts, D_MODEL), 0

    def vec(a, i):
        return a[i].reshape(1, D_MODEL)

    w_hgrn_in_bf = w_hgrn_in.astype(BF16)
    w_hgrn_out_bf = w_hgrn_out.astype(BF16)
    w_lru_in_bf = w_lru_in.astype(BF16)
    w_lru_out_bf = w_lru_out.astype(BF16)
    w_r_bf = w_lru_rgate.astype(BF16)
    w_i_bf = w_lru_igate.astype(BF16)
    w_ffn_gu_bf = w_ffn_gu.astype(BF16)
    w_ffn_down_bf = w_ffn_down.astype(BF16)
    w_router_pad = jnp.pad(w_router, ((0, 0), (0, 0), (0, LANES - N_EXPERTS)))

    hg_p = hg_s = None
    h_p, h_s, c_p, c_s = [], [], [], []
    for layer in range(DEPTH):
        j = layer // 2
        lng, lnb = vec(ln_mix_g, layer), vec(ln_mix_b, layer)
        if layer % 2 == 0:
            ng = vec(hgrn_norm_g, j)
            xo, hg_p = hgrn_prompt_layer(x, w_hgrn_in_bf[j], hgrn_lb_logits, ng, w_hgrn_out_bf[j],
                                         lng, lnb, hg_p, j, bp, tp, n)
            q, k, lf, v, g = hgrn_in(x_smp, w_hgrn_in_bf[j], hgrn_lb_logits, j, smp_row0, bs * ts)
            o, hg_s = hgrn_rec_sample(q, k, lf, v, g, ng, state_hgrn, hg_s, j, bs, ts)
            x = proj_ln(o, w_hgrn_out_bf[j], x_smp, smp_row0, lng, lnb, xo, n_p)
            x, moe_gu_bf, moe_down_bf = ffn_ln(x, w_ffn_gu_bf[j], w_ffn_down_bf[j], vec(ln_ffn_g, layer),
                                               vec(ln_ffn_b, layer), w_moe_gu, w_moe_down, j)
            x_smp, smp_row0 = x, n_p
        else:
            wts = (lru_conv_w[j], vec(lru_conv_b, j), w_r_bf[j], vec(b_lru_rgate, j),
                   w_i_bf[j], vec(b_lru_igate, j), vec(lru_lambda, j))
            xo, hp, cp = lru_prompt_layer(x, w_lru_in_bf[j], wts, w_lru_out_bf[j], lng, lnb, bp, tp)
            xb, gate = lru_in(x_smp, w_lru_in_bf[j], smp_row0, bs * ts)
            cc_rows = jnp.pad(state_lru_conv[j], ((0, 0), (SUBLANES - (CONV_W - 1), 0), (0, 0)))
            h0_rows = jnp.pad(state_lru_h[j][:, None, :], ((0, 0), (0, ts - 1), (0, 0)))
            hgate, hs, cs = lru_rec_sample(xb, gate, cc_rows.reshape(bs * SUBLANES, D_MODEL),
                                           h0_rows.reshape(bs * ts, D_MODEL), wts, bs, ts)
            h_p.append(hp.reshape(bp, D_MODEL))
            h_s.append(hs.reshape(bs, D_MODEL))
            c_p.append(cp)
            c_s.append(cs)
            x = proj_ln(hgate, w_lru_out_bf[j], x_smp, smp_row0, lng, lnb, xo, n_p)
            x = moe_ffn_ln(x, w_router_pad[j], moe_gu_bf, moe_down_bf,
                           vec(ln_ffn_g, layer), vec(ln_ffn_b, layer),
                           split_rows=n_p if layer == DEPTH - 1 else None)
            x_smp, smp_row0 = x, n_p

    y_prompt = x[0].reshape(bp, tp, D_MODEL)
    y_sample = x[1].reshape(bs, ts, D_MODEL)
    return (y_prompt, y_sample, hg_p, hg_s, jnp.stack(h_p), jnp.stack(h_s), jnp.stack(c_p), jnp.stack(c_s))
```

```python
import functools
import math

import jax
import jax.numpy as jnp
from jax import lax
from jax.experimental import pallas as pl
from jax.experimental.pallas import tpu as pltpu

F32 = jnp.float32
BF16 = jnp.bfloat16
I32 = jnp.int32

D_MODEL = 1024
DEPTH = 4
HG_DK = 128
HG_HEADS = 8
HG_DV = 128
CONV_W = 4
RG_C = 8.0
LRU_BLOCKS = 4
LRU_BW = 256
D_FF = 2816
N_EXPERTS = 8
D_FF_EXPERT = 3584
ALPHA = (2 * DEPTH) ** 0.25
LN_EPS = 1e-5
F_FLOOR = 1e-20
LOG2E = math.log2(math.e)

LANES = 128
SUBLANES = 8
VMEM_LIMIT = 56 * 1024 * 1024

HG_CHUNK_PROMPT = 16
MOE_TILE = 512


def _cparams(sem):
    return pltpu.CompilerParams(dimension_semantics=sem, vmem_limit_bytes=VMEM_LIMIT)


def _sigmoid(x):
    return 0.5 * jnp.tanh(0.5 * x) + 0.5


def _gelu_tanh(y):
    c = math.sqrt(2.0 / math.pi)
    return 0.5 * y * (1.0 + jnp.tanh(c * (y + 0.044715 * (y * y * y))))


def _layer_norm_rows(z, g, b):
    mu = jnp.mean(z, axis=-1, keepdims=True)
    zc = z - mu
    var = jnp.mean(zc * zc, axis=-1, keepdims=True)
    return zc * lax.rsqrt(var + LN_EPS) * g + b


def _hgrn_in_compute(layer_j, xb, w_ref, lbl_ref, q_ref, k_ref, lf_ref, v_ref, g_ref):
    l = lbl_ref[...]
    e = jnp.exp(l - jnp.max(l, axis=0, keepdims=True))
    p = e / jnp.sum(e, axis=0, keepdims=True)
    cs = p[0:1]
    for i in range(1, layer_j + 1):
        cs = cs + p[i:i + 1]
    lb = cs - p[0:1]

    def seg(i):
        return jnp.dot(xb, w_ref[:, i * D_MODEL:(i + 1) * D_MODEL], preferred_element_type=F32)

    q = seg(0)
    q_ref[...] = q * _sigmoid(q) * (HG_DK ** -0.5)
    fp = seg(1)
    sig = 1.0 / (1.0 + jnp.exp(-fp))
    f = lb + (1.0 - lb) * sig
    lf_ref[...] = jnp.log(jnp.maximum(f, F_FLOOR))
    k_ref[...] = (1.0 - lb) * (1.0 - sig)
    v_ref[...] = seg(2)
    g = seg(3)
    g_ref[...] = g * _sigmoid(g)


def _hgrn_in_kernel(layer_j, x_ref, w_ref, lbl_ref, *out_refs):
    _hgrn_in_compute(layer_j, x_ref[...].astype(BF16), w_ref, lbl_ref, *out_refs)


def hgrn_in(x, w_in_bf, lb_logits, layer_j, row0, nrows, tm=256):
    base = row0 // tm
    out = jax.ShapeDtypeStruct((nrows, D_MODEL), F32)
    return pl.pallas_call(
        functools.partial(_hgrn_in_kernel, layer_j),
        grid=(nrows // tm,),
        in_specs=[pl.BlockSpec((tm, D_MODEL), lambda i: (base + i, 0)),
                  pl.BlockSpec((D_MODEL, 4 * D_MODEL), lambda i: (0, 0)),
                  pl.BlockSpec(lb_logits.shape, lambda i: (0, 0))],
        out_specs=[pl.BlockSpec((tm, D_MODEL), lambda i: (i, 0))] * 5,
        out_shape=[out] * 5,
        compiler_params=_cparams(("arbitrary",)),
        name="hgrn_in",
    )(x, w_in_bf, lb_logits)


def _hgrn_prefix(C, lf):
    row = lax.broadcasted_iota(I32, (C, LANES), 0)
    G = lf
    d = 1
    while d < C:
        G = G + jnp.where(row >= d, pltpu.roll(G, d, 0), 0.0)
        d *= 2
    return G


def _hgrn_diag(C, G, q, k, v):
    row8 = lax.broadcasted_iota(I32, (SUBLANES, LANES), 0)
    parts = []
    for gi in range(C // SUBLANES):
        lo = gi * SUBLANES
        Gg = G[lo:lo + SUBLANES]
        qg = q[lo:lo + SUBLANES]
        acc = jnp.zeros((SUBLANES, LANES), F32)
        for s in range(lo + SUBLANES):
            dec = jnp.exp(Gg - G[s:s + 1])
            if s >= lo:
                dec = jnp.where(row8 >= (s - lo), dec, 0.0)
            w = jnp.sum(qg * dec * k[s:s + 1], axis=-1, keepdims=True)
            acc = acc + w * v[s:s + 1]
        parts.append(acc)
    return parts[0] if len(parts) == 1 else jnp.concatenate(parts, axis=0)


def _hgrn_state_update(C, G, k, v, S):
    gend = G[C - 1:C, :]
    kt = k * jnp.exp(gend - G)
    ecol = jnp.transpose(jnp.broadcast_to(jnp.exp(gend), (HG_DK, LANES)))
    upd = lax.dot_general(kt.astype(BF16), v.astype(BF16), (((0,), (0,)), ((), ())),
                          preferred_element_type=F32)
    return ecol * S + upd


def _hgrn_chunk_head(C, q, k, lf, v, S):
    G = _hgrn_prefix(C, lf)
    o = jnp.dot((q * jnp.exp(G)).astype(BF16), S.astype(BF16), preferred_element_type=F32)
    return o + _hgrn_diag(C, G, q, k, v), _hgrn_state_update(C, G, k, v, S)


def _hgrn_rec_kernel(C, q_ref, k_ref, lf_ref, v_ref, g_ref, ng_ref, s0_ref, o_ref, s_ref):
    heads = [slice(h * LANES, (h + 1) * LANES) for h in range(HG_HEADS)]

    def body(i, carry):
        rows = pl.ds(pl.multiple_of(i * C, C), C)
        gs, os_ = [], []
        for h, sl in enumerate(heads):
            G = _hgrn_prefix(C, lf_ref[rows, sl])
            gs.append(G)
            os_.append(jnp.dot((q_ref[rows, sl] * jnp.exp(G)).astype(BF16), s0_ref[i, h].astype(BF16),
                               preferred_element_type=F32))
        for h, sl in enumerate(heads):
            o = os_[h] + _hgrn_diag(C, gs[h], q_ref[rows, sl], k_ref[rows, sl], v_ref[rows, sl])
            ms = jnp.mean(o * o, axis=-1, keepdims=True)
            o_ref[rows, sl] = o * lax.rsqrt(ms + LN_EPS) * ng_ref[:, sl] * g_ref[rows, sl]
        for h, sl in enumerate(heads):
            s_ref[i, h] = _hgrn_state_update(C, gs[h], k_ref[rows, sl], v_ref[rows, sl], s0_ref[i, h])
        return carry

    lax.fori_loop(0, q_ref.shape[0] // C, body, 0)


HG_CHUNK = 64
HG_SUB = 16
HG_SAFE_RANGE = 60.0


def _hgrn_chunk_masks(tt):
    import numpy as np
    t = np.arange(tt)[:, None]
    s = np.arange(tt)[None, :]
    tri = (s <= t) & ((t // HG_CHUNK) == (s // HG_CHUNK))
    return jnp.asarray(tri.astype(np.float32), BF16)


def _hgrn_prompt_kernel(layer_j, x_ref, win_ref, lbl_ref, ng_ref, m_ref, wout_ref, lng_ref, lnb_ref,
                        xo_ref, s_ref,
                        q_ref, k_ref, lf_ref, v_ref, g_ref, o_ref,
                        g_s, qi_s, ks_s, qd_s, kd_s, q1_s, k1_s, q2_s, k2_s, a_s, flag_s):
    tt = x_ref.shape[0]

    @pl.when(pl.program_id(1) == 0)
    def _():
        s_ref[...] = jnp.zeros(s_ref.shape, F32)

    _hgrn_in_compute(layer_j, x_ref[...].astype(BF16), win_ref, lbl_ref, q_ref, k_ref, lf_ref, v_ref, g_ref)

    lf = lf_ref[...] * LOG2E
    hi = lf.astype(BF16)
    lo = (lf - hi.astype(F32)).astype(BF16)
    g_s[...] = (jnp.dot(m_ref[...], hi, preferred_element_type=F32)
                + jnp.dot(m_ref[...], lo, preferred_element_type=F32))

    C = HG_CHUNK
    SB = HG_SUB
    worst = jnp.zeros((SB, D_MODEL), F32)
    for blk in range(tt // SB):
        r0 = blk * SB
        c0 = (r0 // C) * C
        rows = slice(r0, r0 + SB)

        def grow(r):
            return g_s[r - 1:r, :] if r > c0 else jnp.zeros((1, D_MODEL), F32)

        G = g_s[rows, :]
        q = q_ref[rows, :]
        k = k_ref[rows, :]
        qi_s[rows, :] = (q * jnp.exp2(G)).astype(BF16)
        ks_s[rows, :] = (k * jnp.exp2(g_s[c0 + C - 1:c0 + C, :] - G)).astype(BF16)
        e0 = G - grow(r0)
        qd_s[rows, :] = (q * jnp.exp2(e0)).astype(BF16)
        kd_s[rows, :] = (k * jnp.exp2(-e0)).astype(BF16)
        worst = jnp.maximum(worst, -e0)
        e1 = G - grow((r0 // (2 * SB)) * (2 * SB) + SB)
        q1_s[rows, :] = (q * jnp.exp2(jnp.minimum(e1, 0.0))).astype(BF16)
        k1_s[rows, :] = (k * jnp.exp2(jnp.minimum(-e1, 0.0))).astype(BF16)
        e2 = G - grow(c0 + C // 2)
        q2_s[rows, :] = (q * jnp.exp2(jnp.minimum(e2, 0.0))).astype(BF16)
        k2_s[rows, :] = (k * jnp.exp2(jnp.minimum(-e2, 0.0))).astype(BF16)
    flag_s[0] = (jnp.max(worst) <= HG_SAFE_RANGE * LOG2E).astype(I32)

    def finish(rows, sl, o):
        ms = jnp.mean(o * o, axis=-1, keepdims=True)
        o_ref[rows, sl] = o * lax.rsqrt(ms + LN_EPS) * ng_ref[:, sl] * g_ref[rows, sl]

    C = HG_CHUNK
    ti = lax.broadcasted_iota(I32, (C, C), 0)
    si = lax.broadcasted_iota(I32, (C, C), 1)
    causal = si <= ti
    sub_bits = HG_SUB.bit_length() - 1
    m0 = causal & ((ti >> sub_bits) == (si >> sub_bits))
    m1 = causal & ((ti >> (sub_bits + 1)) == (si >> (sub_bits + 1))) & jnp.logical_not(m0)
    nt_dims = (((1,), (1,)), ((), ()))

    @pl.when(flag_s[0] == 1)
    def _():
        def body(c, carry):
            r0 = pl.multiple_of(c * C, C)
            rows = pl.ds(r0, C)
            heads = [slice(h * LANES, (h + 1) * LANES) for h in range(HG_HEADS)]
            for h, sl in enumerate(heads):
                a0 = lax.dot_general(qd_s[rows, sl], kd_s[rows, sl], nt_dims, preferred_element_type=F32)
                a1 = lax.dot_general(q1_s[rows, sl], k1_s[rows, sl], nt_dims, preferred_element_type=F32)
                a2 = lax.dot_general(q2_s[rows, sl], k2_s[rows, sl], nt_dims, preferred_element_type=F32)
                a = jnp.where(m0, a0, jnp.where(m1, a1, jnp.where(causal, a2, 0.0)))
                a_s[h] = a.astype(BF16)
            for h, sl in enumerate(heads):
                o = (jnp.dot(qi_s[rows, sl], s_ref[0, h].astype(BF16), preferred_element_type=F32)
                     + jnp.dot(a_s[h], v_ref[rows, sl].astype(BF16), preferred_element_type=F32))
                finish(rows, sl, o)
            for h, sl in enumerate(heads):
                gend = g_s[pl.ds(pl.multiple_of(r0 + C - SUBLANES, SUBLANES), SUBLANES), sl][SUBLANES - 1:]
                ecol = jnp.transpose(jnp.broadcast_to(jnp.exp2(gend), (HG_DK, LANES)))
                upd = lax.dot_general(ks_s[rows, sl], v_ref[rows, sl].astype(BF16), (((0,), (0,)), ((), ())),
                                      preferred_element_type=F32)
                s_ref[0, h] = ecol * s_ref[0, h] + upd
            return carry

        lax.fori_loop(0, tt // C, body, 0)

    @pl.when(flag_s[0] == 0)
    def _():
        Cs = HG_CHUNK_PROMPT

        def body(i, carry):
            r0 = pl.multiple_of(i * Cs, Cs)
            rows = pl.ds(r0, Cs)
            for h in range(HG_HEADS):
                sl = slice(h * LANES, (h + 1) * LANES)
                o, s_new = _hgrn_chunk_head(Cs, q_ref[rows, sl], k_ref[rows, sl], lf_ref[rows, sl],
                                            v_ref[rows, sl], s_ref[0, h])
                finish(rows, sl, o)
                s_ref[0, h] = s_new
            return carry

        lax.fori_loop(0, tt // Cs, body, 0)

    y = jnp.dot(o_ref[...].astype(BF16), wout_ref[...], preferred_element_type=F32)
    xo_ref[...] = _layer_norm_rows(ALPHA * x_ref[...] + y, lng_ref[...], lnb_ref[...])


def hgrn_prompt_layer(x, w_in_bf, lb_logits, norm_g, w_out_bf, ln_g, ln_b, s_all, layer_j, batch, seq, n,
                      tt=256):
    nt = seq // tt
    row = pl.BlockSpec((tt, D_MODEL), lambda b, t: (b * nt + t, 0))
    vec = pl.BlockSpec((1, D_MODEL), lambda b, t: (0, 0))
    s_spec = pl.BlockSpec((None, 1, HG_HEADS, HG_DK, HG_DV), lambda b, t: (layer_j, b, 0, 0, 0))
    in_specs = [row,
                pl.BlockSpec((D_MODEL, 4 * D_MODEL), lambda b, t: (0, 0)),
                pl.BlockSpec(lb_logits.shape, lambda b, t: (0, 0)),
                vec,
                pl.BlockSpec((tt, tt), lambda b, t: (0, 0)),
                pl.BlockSpec((D_MODEL, D_MODEL), lambda b, t: (0, 0)),
                vec, vec]
    args = [x, w_in_bf, lb_logits, norm_g, _hgrn_chunk_masks(tt), w_out_bf, ln_g, ln_b]
    aliases = {}
    kern = functools.partial(_hgrn_prompt_kernel, layer_j)
    if s_all is not None:
        def kern(*refs):
            _hgrn_prompt_kernel(layer_j, *refs[:8], *refs[9:])
        in_specs.append(pl.BlockSpec(memory_space=pl.ANY))
        args.append(s_all)
        aliases = {8: 1}
    f32_tile = pltpu.VMEM((tt, D_MODEL), F32)
    bf_tile = pltpu.VMEM((tt, D_MODEL), BF16)
    return pl.pallas_call(
        kern,
        grid=(batch, nt),
        in_specs=in_specs,
        out_specs=[row, s_spec],
        out_shape=[jax.ShapeDtypeStruct((n, D_MODEL), F32),
                   jax.ShapeDtypeStruct((2, batch, HG_HEADS, HG_DK, HG_DV), F32)],
        scratch_shapes=[f32_tile] * 7 + [bf_tile] * 8
                       + [pltpu.VMEM((HG_HEADS, HG_CHUNK, HG_CHUNK), BF16), pltpu.SMEM((1,), I32)],
        input_output_aliases=aliases,
        compiler_params=_cparams(("arbitrary", "arbitrary")),
        name="hgrn_prompt_layer",
    )(*args)


def hgrn_rec_sample(q, k, lf, v, g, norm_g, state, s_all, layer_j, nseq, seq, sg=8):
    n = q.shape[0]
    rt = sg * seq
    row = pl.BlockSpec((rt, D_MODEL), lambda i: (i, 0))
    s_in = pl.BlockSpec((None, sg, HG_HEADS, HG_DK, HG_DV), lambda i: (layer_j, i, 0, 0, 0))
    in_specs = [row] * 5 + [pl.BlockSpec((1, D_MODEL), lambda i: (0, 0)), s_in]
    args = [q, k, lf, v, g, norm_g, state]
    aliases = {}
    if s_all is not None:
        in_specs.append(pl.BlockSpec(memory_space=pl.ANY))
        args.append(s_all)
        aliases[7] = 1

    def kern(*refs):
        _hgrn_rec_kernel(seq, *refs[:7], *refs[len(args):])

    return pl.pallas_call(
        kern,
        grid=(nseq // sg,),
        in_specs=in_specs,
        out_specs=[row, s_in],
        out_shape=[jax.ShapeDtypeStruct((n, D_MODEL), F32),
                   jax.ShapeDtypeStruct((2, nseq, HG_HEADS, HG_DK, HG_DV), F32)],
        input_output_aliases=aliases,
        compiler_params=_cparams(("arbitrary",)),
        name="hgrn_rec_sample",
    )(*args)


def _proj_ln_kernel(a_ref, w_ref, x_ref, g_ref, b_ref, o_ref):
    y = jnp.dot(a_ref[...].astype(BF16), w_ref[...], preferred_element_type=F32)
    o_ref[...] = _layer_norm_rows(ALPHA * x_ref[...] + y, g_ref[...], b_ref[...])


def proj_ln(a, w_bf, x, x_row0, g, b, xo, row0, tm=512):
    n = xo.shape[0]
    nrows = a.shape[0]
    base = row0 // tm
    xbase = x_row0 // tm
    row = pl.BlockSpec((tm, D_MODEL), lambda i: (base + i, 0))
    vec = pl.BlockSpec((1, D_MODEL), lambda i: (0, 0))

    def kern(a_ref, w_ref, x_ref, g_ref, b_ref, _, o_ref):
        _proj_ln_kernel(a_ref, w_ref, x_ref, g_ref, b_ref, o_ref)

    return pl.pallas_call(
        kern,
        grid=(nrows // tm,),
        in_specs=[pl.BlockSpec((tm, D_MODEL), lambda i: (i, 0)),
                  pl.BlockSpec((D_MODEL, D_MODEL), lambda i: (0, 0)),
                  pl.BlockSpec((tm, D_MODEL), lambda i: (xbase + i, 0)), vec, vec,
                  pl.BlockSpec(memory_space=pl.ANY)],
        out_specs=row,
        out_shape=jax.ShapeDtypeStruct((n, D_MODEL), F32),
        input_output_aliases={5: 0},
        compiler_params=_cparams(("arbitrary",)),
        name="proj_ln",
    )(a, w_bf, x, g, b, xo)


def _ffn_kernel(x_ref, wg_ref, wu_ref, wd_ref, g_ref, b_ref, o_ref, acc_ref):
    f = pl.program_id(1)
    xb = x_ref[...].astype(BF16)
    gg = jnp.dot(xb, wg_ref[...], preferred_element_type=F32)
    uu = jnp.dot(xb, wu_ref[...], preferred_element_type=F32)
    h = (gg * _sigmoid(gg) * uu).astype(BF16)
    part = jnp.dot(h, wd_ref[...], preferred_element_type=F32)

    @pl.when(f == 0)
    def _():
        acc_ref[...] = part

    @pl.when(f > 0)
    def _():
        acc_ref[...] += part

    @pl.when(f == pl.num_programs(1) - 1)
    def _():
        o_ref[...] = _layer_norm_rows(ALPHA * x_ref[...] + acc_ref[...], g_ref[...], b_ref[...])


def ffn_ln(x, w_gu_bf, w_down_bf, g, b, tm=512, tf=1408):
    n = x.shape[0]
    nf = D_FF // tf
    row = pl.BlockSpec((tm, D_MODEL), lambda i, f: (i, 0))
    vec = pl.BlockSpec((1, D_MODEL), lambda i, f: (0, 0))
    return pl.pallas_call(
        _ffn_kernel,
        grid=(n // tm, nf),
        in_specs=[row,
                  pl.BlockSpec((D_MODEL, tf), lambda i, f: (0, f)),
                  pl.BlockSpec((D_MODEL, tf), lambda i, f: (0, nf + f)),
                  pl.BlockSpec((tf, D_MODEL), lambda i, f: (f, 0)),
                  vec, vec],
        out_specs=row,
        out_shape=jax.ShapeDtypeStruct((n, D_MODEL), F32),
        scratch_shapes=[pltpu.VMEM((tm, D_MODEL), F32)],
        compiler_params=_cparams(("arbitrary", "arbitrary")),
        name="ffn_ln",
    )(x, w_gu_bf, w_gu_bf, w_down_bf, g, b)


def _lru_in_kernel(x_ref, w_ref, xb_ref, gate_ref):
    xb = x_ref[...].astype(BF16)
    xb_ref[...] = jnp.dot(xb, w_ref[:, :D_MODEL], preferred_element_type=F32)
    y = jnp.dot(xb, w_ref[:, D_MODEL:], preferred_element_type=F32)
    gate_ref[...] = _gelu_tanh(y)


def lru_in(x, w_in_bf, row0, nrows, tm=512):
    base = row0 // tm
    row = pl.BlockSpec((tm, D_MODEL), lambda i: (i, 0))
    out = jax.ShapeDtypeStruct((nrows, D_MODEL), F32)
    return pl.pallas_call(
        _lru_in_kernel,
        grid=(nrows // tm,),
        in_specs=[pl.BlockSpec((tm, D_MODEL), lambda i: (base + i, 0)),
                  pl.BlockSpec((D_MODEL, 2 * D_MODEL), lambda i: (0, 0))],
        out_specs=[row, row],
        out_shape=[out, out],
        compiler_params=_cparams(("arbitrary",)),
        name="lru_in",
    )(x, w_in_bf)


def _lru_rec_kernel(L, carry, *refs):
    if carry:
        (x_ref, win_ref, cw_ref, cb_ref, wr_ref, br_ref, wi_ref, bi_ref, lam_ref, wout_ref, lng_ref, lnb_ref,
         cgu_ref, cdn_ref, xo_ref, hn_ref, cn_ref, cgu_o, cdn_o, xp_ref, hs_ref, hc_ref, gate_ref, hg_ref) = refs
        R = x_ref.shape[0]
        cgu_o[...] = cgu_ref[...].astype(BF16)
        cdn_o[...] = cdn_ref[...].astype(BF16)
    else:
        (xb_ref, gate_ref, cc_ref, h0_ref, cw_ref, cb_ref, wr_ref, br_ref, wi_ref, bi_ref, lam_ref,
         hg_ref, hn_ref, cn_ref, xp_ref, hs_ref) = refs
        R = xb_ref.shape[0]
    S = R // L
    PADR = SUBLANES
    t = pl.program_id(1) if carry else 0

    if carry:
        @pl.when(t == 0)
        def _():
            xp_ref[:, 0:PADR, :] = jnp.zeros((S, PADR, D_MODEL), F32)
            hc_ref[...] = jnp.zeros(hc_ref.shape, F32)

        @pl.when(t > 0)
        def _():
            xp_ref[:, 0:PADR, :] = xp_ref[:, L:L + PADR, :]

        xin = x_ref[...].astype(BF16)
        xp_ref[:, PADR:PADR + L, :] = jnp.dot(xin, win_ref[:, :D_MODEL],
                                              preferred_element_type=F32).reshape(S, L, D_MODEL)
        gate_ref[...] = _gelu_tanh(jnp.dot(xin, win_ref[:, D_MODEL:], preferred_element_type=F32))
    else:
        xp_ref[:, 0:PADR, :] = cc_ref[...].reshape(S, PADR, D_MODEL)
        xp_ref[:, PADR:PADR + L, :] = xb_ref[...].reshape(S, L, D_MODEL)

    xc = cb_ref[...].reshape(1, 1, D_MODEL) + xp_ref[:, PADR:PADR + L, :] * cw_ref[CONV_W - 1:CONV_W, :].reshape(1, 1, D_MODEL)
    for j in range(CONV_W - 1):
        off = PADR - (CONV_W - 1) + j
        xc = xc + xp_ref[:, off:off + L, :] * cw_ref[j:j + 1, :].reshape(1, 1, D_MODEL)
    xc = xc.reshape(R, D_MODEL)
    cn_ref[...] = xp_ref[:, L + PADR - (CONV_W - 1):L + PADR, :].reshape(cn_ref.shape)

    assert L & (L - 1) == 0
    pos = jnp.bitwise_and(lax.broadcasted_iota(I32, (R, LRU_BW), 0), L - 1)
    for nb in range(LRU_BLOCKS):
        sl = slice(nb * LRU_BW, (nb + 1) * LRU_BW)
        xcb = xc[:, sl]
        xh = xcb.astype(BF16)
        r = _sigmoid(jnp.dot(xh, wr_ref[nb], preferred_element_type=F32) + br_ref[:, sl])
        ig = _sigmoid(jnp.dot(xh, wi_ref[nb], preferred_element_type=F32) + bi_ref[:, sl])
        z = -lam_ref[:, sl]
        softplus = jnp.maximum(z, 0.0) + jnp.log1p(jnp.exp(-jnp.abs(z)))
        log_a = (-RG_C) * r * softplus
        a = jnp.exp(log_a)
        u = jnp.sqrt(1.0 - a * a) * (ig * xcb)
        if carry:
            ng = R // SUBLANES
            u3 = u.reshape(ng, SUBLANES, LRU_BW)
            a3 = a.reshape(ng, SUBLANES, LRU_BW)
            pos8 = lax.broadcasted_iota(I32, (ng, SUBLANES, LRU_BW), 1)
            d = 1
            while d < SUBLANES:
                m = pos8 >= d
                u3 = jnp.where(m, a3 * pltpu.roll(u3, d, 1) + u3, u3)
                a3 = jnp.where(m, a3 * pltpu.roll(a3, d, 1), a3)
                d *= 2
            hprev = hc_ref[:, sl]
            for gi in range(ng):
                hgrp = u3[gi] + a3[gi] * hprev
                hs_ref[0, gi * SUBLANES:(gi + 1) * SUBLANES, sl] = hgrp
                hprev = hgrp[SUBLANES - 1:SUBLANES]
            hg_ref[:, sl] = hs_ref[0, :, sl] * gate_ref[:, sl]
        else:
            u = u + a * h0_ref[:, sl]
            d = 1
            while d < L:
                m = pos >= d
                u = jnp.where(m, a * pltpu.roll(u, d, 0) + u, u)
                a = jnp.where(m, a * pltpu.roll(a, d, 0), a)
                d *= 2
            hg_ref[:, sl] = u * gate_ref[:, sl]
            hs_ref[:, :, sl] = u.reshape(S, L, LRU_BW)

    hlast = hs_ref[:, L - 1:L, :]
    hn_ref[...] = hlast.reshape(hn_ref.shape)
    if carry:
        hc_ref[...] = hlast.reshape(1, D_MODEL)
        y = jnp.dot(hg_ref[...].astype(BF16), wout_ref[...], preferred_element_type=F32)
        xo_ref[...] = _layer_norm_rows(ALPHA * x_ref[...] + y, lng_ref[...], lnb_ref[...])


def _lru_weight_specs(nidx):
    def c2(*_):
        return (0, 0)

    def c3(*_):
        return (0, 0, 0)
    return [pl.BlockSpec((CONV_W, D_MODEL), c2), pl.BlockSpec((1, D_MODEL), c2),
            pl.BlockSpec((LRU_BLOCKS, LRU_BW, LRU_BW), c3), pl.BlockSpec((1, D_MODEL), c2),
            pl.BlockSpec((LRU_BLOCKS, LRU_BW, LRU_BW), c3), pl.BlockSpec((1, D_MODEL), c2),
            pl.BlockSpec((1, D_MODEL), c2)]


def lru_prompt_layer(x, w_in_bf, wts, w_out_bf, ln_g, ln_b, moe_gu, moe_down, layer_j, batch, seq, tt=256):
    n = x.shape[0]
    nt = seq // tt
    steps = batch * nt
    gu2 = moe_gu.reshape(moe_gu.shape[0], N_EXPERTS * D_MODEL, 2 * D_FF_EXPERT)
    dn2 = moe_down.reshape(moe_down.shape[0], N_EXPERTS * D_FF_EXPERT, D_MODEL)
    gu_rows = gu2.shape[1] // steps
    dn_rows = dn2.shape[1] // steps
    assert gu_rows * steps == gu2.shape[1] and dn_rows * steps == dn2.shape[1]
    row = pl.BlockSpec((tt, D_MODEL), lambda b, t: (b * nt + t, 0))
    vec = pl.BlockSpec((1, D_MODEL), lambda b, t: (0, 0))
    xo, hn, cn, gu_bf, dn_bf = pl.pallas_call(
        functools.partial(_lru_rec_kernel, tt, True),
        grid=(batch, nt),
        in_specs=[row, pl.BlockSpec((D_MODEL, 2 * D_MODEL), lambda b, t: (0, 0))] + _lru_weight_specs(2)
                 + [pl.BlockSpec((D_MODEL, D_MODEL), lambda b, t: (0, 0)), vec, vec,
                    pl.BlockSpec((None, gu_rows, 2 * D_FF_EXPERT), lambda b, t: (layer_j, b * nt + t, 0)),
                    pl.BlockSpec((None, dn_rows, D_MODEL), lambda b, t: (layer_j, b * nt + t, 0))],
        out_specs=[row,
                   pl.BlockSpec((1, 1, D_MODEL), lambda b, t: (b, 0, 0)),
                   pl.BlockSpec((1, CONV_W - 1, D_MODEL), lambda b, t: (b, 0, 0)),
                   pl.BlockSpec((gu_rows, 2 * D_FF_EXPERT), lambda b, t: (b * nt + t, 0)),
                   pl.BlockSpec((dn_rows, D_MODEL), lambda b, t: (b * nt + t, 0))],
        out_shape=[jax.ShapeDtypeStruct((n, D_MODEL), F32),
                   jax.ShapeDtypeStruct((batch, 1, D_MODEL), F32),
                   jax.ShapeDtypeStruct((batch, CONV_W - 1, D_MODEL), F32),
                   jax.ShapeDtypeStruct(gu2.shape[1:], BF16),
                   jax.ShapeDtypeStruct(dn2.shape[1:], BF16)],
        scratch_shapes=[pltpu.VMEM((1, tt + SUBLANES, D_MODEL), F32),
                        pltpu.VMEM((1, tt, D_MODEL), F32),
                        pltpu.VMEM((1, D_MODEL), F32),
                        pltpu.VMEM((tt, D_MODEL), F32),
                        pltpu.VMEM((tt, D_MODEL), F32)],
        compiler_params=_cparams(("arbitrary", "arbitrary")),
        name="lru_prompt_layer",
    )(x, w_in_bf, *wts, w_out_bf, ln_g, ln_b, gu2, dn2)
    return (xo, hn, cn, gu_bf.reshape(N_EXPERTS, D_MODEL, 2 * D_FF_EXPERT),
            dn_bf.reshape(N_EXPERTS, D_FF_EXPERT, D_MODEL))


def lru_rec_sample(xb, gate, cc_rows, h0_rows, wts, nseq, seq, sg=32):
    n = xb.shape[0]
    rt = sg * seq
    row = pl.BlockSpec((rt, D_MODEL), lambda i: (i, 0))
    return pl.pallas_call(
        functools.partial(_lru_rec_kernel, seq, False),
        grid=(nseq // sg,),
        in_specs=[row, row, pl.BlockSpec((sg * SUBLANES, D_MODEL), lambda i: (i, 0)), row]
                 + _lru_weight_specs(1),
        out_specs=[row,
                   pl.BlockSpec((sg, 1, D_MODEL), lambda i: (i, 0, 0)),
                   pl.BlockSpec((sg, CONV_W - 1, D_MODEL), lambda i: (i, 0, 0))],
        out_shape=[jax.ShapeDtypeStruct((n, D_MODEL), F32),
                   jax.ShapeDtypeStruct((nseq, 1, D_MODEL), F32),
                   jax.ShapeDtypeStruct((nseq, CONV_W - 1, D_MODEL), F32)],
        scratch_shapes=[pltpu.VMEM((sg, seq + SUBLANES, D_MODEL), F32),
                        pltpu.VMEM((sg, seq, D_MODEL), F32)],
        compiler_params=_cparams(("arbitrary",)),
        name="lru_rec_sample",
    )(xb, gate, cc_rows, h0_rows, *wts)


def _router_kernel(x_ref, w_ref, mi_ref, mf_ref, cnt_ref, run_ref):
    i = pl.program_id(0)

    @pl.when(i == 0)
    def _():
        run_ref[...] = jnp.zeros(run_ref.shape, F32)

    tm = x_ref.shape[0]
    x = x_ref[...]
    w = w_ref[...]
    xh = x.astype(BF16)
    xl = (x - xh.astype(F32)).astype(BF16)
    wh = w.astype(BF16)
    wl = (w - wh.astype(F32)).astype(BF16)
    logits = (jnp.dot(xh, wh, preferred_element_type=F32) + jnp.dot(xh, wl, preferred_element_type=F32)
              + jnp.dot(xl, wh, preferred_element_type=F32))
    lane_i = lax.broadcasted_iota(I32, (tm, LANES), 1)
    lane = lane_i.astype(F32)
    neg = jnp.float32(-jnp.inf)
    logits = jnp.where(lane_i < N_EXPERTS, logits, neg)
    m1 = jnp.max(logits, axis=-1, keepdims=True)
    i1 = jnp.min(jnp.where(logits == m1, lane, float(LANES)), axis=-1, keepdims=True)
    l2 = jnp.where(lane == i1, neg, logits)
    m2 = jnp.max(l2, axis=-1, keepdims=True)
    i2 = jnp.min(jnp.where(l2 == m2, lane, float(LANES)), axis=-1, keepdims=True)
    e2 = jnp.exp(m2 - m1)
    p1 = 1.0 / (1.0 + e2)
    p2 = e2 * p1

    hit1 = lane == i1
    hit2 = lane == i2
    onehot = jnp.where(hit1 | hit2, 1.0, 0.0)
    r_i = lax.broadcasted_iota(I32, (tm, tm), 0)
    c_i = lax.broadcasted_iota(I32, (tm, tm), 1)
    tri = jnp.where(c_i < r_i, 1.0, 0.0).astype(BF16)
    rank = jnp.dot(tri, onehot.astype(BF16), preferred_element_type=F32) + run_ref[0:1, :]
    r1 = jnp.sum(jnp.where(hit1, rank, 0.0), axis=-1, keepdims=True)
    r2 = jnp.sum(jnp.where(hit2, rank, 0.0), axis=-1, keepdims=True)
    total = run_ref[0:1, :] + jnp.sum(onehot, axis=0, keepdims=True)
    run_ref[...] = jnp.broadcast_to(total, run_ref.shape)
    cnt_ref[...] = jnp.broadcast_to(total, cnt_ref.shape).astype(I32)

    mi = jnp.where(lane_i == 0, i1, jnp.where(lane_i == 1, i2, 0.0))
    mi = jnp.where(lane_i == 2, r1, jnp.where(lane_i == 3, r2, mi))
    mi_ref[...] = mi.astype(I32)
    mf_ref[...] = jnp.where(lane_i == 0, p1, jnp.where(lane_i == 1, p2, 0.0))


def moe_router(x, w_router_pad, tm=512):
    n = x.shape[0]
    row = pl.BlockSpec((tm, LANES), lambda i: (i, 0))
    return pl.pallas_call(
        _router_kernel,
        grid=(n // tm,),
        in_specs=[pl.BlockSpec((tm, D_MODEL), lambda i: (i, 0)),
                  pl.BlockSpec((D_MODEL, LANES), lambda i: (0, 0))],
        out_specs=[row, row, pl.BlockSpec((SUBLANES, LANES), lambda i: (0, 0))],
        out_shape=[jax.ShapeDtypeStruct((n, LANES), I32),
                   jax.ShapeDtypeStruct((n, LANES), F32),
                   jax.ShapeDtypeStruct((SUBLANES, LANES), I32)],
        scratch_shapes=[pltpu.VMEM((SUBLANES, LANES), F32)],
        compiler_params=_cparams(("arbitrary",)),
        name="moe_router",
    )(x, w_router_pad)


ZERO_ROWS = 256


def _dispatch_kernel(dest_ref, ztile_ref, x_ref, xs_hbm, zero_ref, sem):
    i = pl.program_id(0)
    tm = x_ref.shape[0]

    @pl.when(i == 0)
    def _():
        zero_ref[...] = jnp.zeros(zero_ref.shape, F32)
        copies = [pltpu.make_async_copy(
                      zero_ref,
                      xs_hbm.at[pl.ds(pl.multiple_of(ztile_ref[e] + c * ZERO_ROWS, ZERO_ROWS), ZERO_ROWS)], sem)
                  for e in range(N_EXPERTS) for c in range(MOE_TILE // ZERO_ROWS)]
        for cp in copies:
            cp.start()
        for cp in copies:
            cp.wait()

    def group(jj, c):
        r8 = pl.multiple_of(jj * SUBLANES, SUBLANES)
        grp = x_ref.at[pl.ds(r8, SUBLANES)]
        t0 = 2 * (i * tm + r8)
        for u in range(SUBLANES):
            src = grp.at[pl.ds(u, 1)]
            pltpu.make_async_copy(src, xs_hbm.at[pl.ds(dest_ref[t0 + 2 * u], 1)], sem).start()
            pltpu.make_async_copy(src, xs_hbm.at[pl.ds(dest_ref[t0 + 2 * u + 1], 1)], sem).start()
        return c

    lax.fori_loop(0, tm // SUBLANES, group, 0)
    for _ in range(2):
        pltpu.make_async_copy(x_ref, xs_hbm.at[pl.ds(0, tm)], sem).wait()


def moe_dispatch(dest, ztile, x, n_rows, tm=512):
    n = x.shape[0]
    return pl.pallas_call(
        _dispatch_kernel,
        grid_spec=pltpu.PrefetchScalarGridSpec(
            num_scalar_prefetch=2, grid=(n // tm,),
            in_specs=[pl.BlockSpec((tm, D_MODEL), lambda i, d, z: (i, 0))],
            out_specs=pl.BlockSpec(memory_space=pl.ANY),
            scratch_shapes=[pltpu.VMEM((ZERO_ROWS, D_MODEL), F32), pltpu.SemaphoreType.DMA(())]),
        out_shape=jax.ShapeDtypeStruct((n_rows, D_MODEL), F32),
        compiler_params=pltpu.CompilerParams(dimension_semantics=("arbitrary",),
                                             has_side_effects=True),
        name="moe_dispatch",
    )(dest, ztile, x)


def _expert_kernel(te_ref, nu_ref, xs_ref, wg_ref, wu_ref, wd_ref, ys_ref, xb_ref, acc_ref):
    i = pl.program_id(0)
    f = pl.program_id(1)
    used = i < nu_ref[0]

    @pl.when(used & (f == 0))
    def _():
        xb_ref[...] = xs_ref[...].astype(BF16)

    @pl.when(used)
    def _():
        xb = xb_ref[...]
        gg = jnp.dot(xb, wg_ref[...], preferred_element_type=F32)
        uu = jnp.dot(xb, wu_ref[...], preferred_element_type=F32)
        h = (gg * _sigmoid(gg) * uu).astype(BF16)
        part = jnp.dot(h, wd_ref[...], preferred_element_type=F32)

        @pl.when(f == 0)
        def _():
            acc_ref[...] = part

        @pl.when(f > 0)
        def _():
            acc_ref[...] += part

    @pl.when(f == pl.num_programs(1) - 1)
    def _():
        @pl.when(used)
        def _():
            ys_ref[...] = acc_ref[...]

        @pl.when(jnp.logical_not(used))
        def _():
            ys_ref[...] = jnp.zeros(ys_ref.shape, F32)


def moe_experts(tile_expert, n_used, xs, w_gu_bf, w_down_bf, tf=1792):
    n_rows = xs.shape[0]
    tm = MOE_TILE
    nf = D_FF_EXPERT // tf

    def xrow(i, f, te, nu):
        return (jnp.minimum(i, nu[0] - 1), 0)

    def wg(i, f, te, nu):
        return (te[i], 0, jnp.where(i < nu[0], f, nf - 1))

    def wu(i, f, te, nu):
        return (te[i], 0, nf + jnp.where(i < nu[0], f, nf - 1))

    def wd(i, f, te, nu):
        return (te[i], jnp.where(i < nu[0], f, nf - 1), 0)

    return pl.pallas_call(
        _expert_kernel,
        grid_spec=pltpu.PrefetchScalarGridSpec(
            num_scalar_prefetch=2, grid=(n_rows // tm, nf),
            in_specs=[pl.BlockSpec((tm, D_MODEL), xrow),
                      pl.BlockSpec((None, D_MODEL, tf), wg),
                      pl.BlockSpec((None, D_MODEL, tf), wu),
                      pl.BlockSpec((None, tf, D_MODEL), wd)],
            out_specs=pl.BlockSpec((tm, D_MODEL), lambda i, f, te, nu: (i, 0)),
            scratch_shapes=[pltpu.VMEM((tm, D_MODEL), BF16), pltpu.VMEM((tm, D_MODEL), F32)]),
        out_shape=jax.ShapeDtypeStruct((n_rows, D_MODEL), F32),
        compiler_params=_cparams(("arbitrary", "arbitrary")),
        name="moe_experts",
    )(tile_expert, n_used, xs, w_gu_bf, w_gu_bf, w_down_bf)


def _combine_kernel(nsplit, dest_ref, ys_hbm, mf_ref, x_ref, g_ref, b_ref, *rest):
    if nsplit is None:
        o_ref, buf_ref, sem = rest
    else:
        oa_ref, ob_ref, buf_ref, sem = rest
    i = pl.program_id(0)
    n_steps = pl.num_programs(0)
    tm = x_ref.shape[0]

    def gather(step, slot):
        def group(jj, c):
            r8 = pl.multiple_of(jj * SUBLANES, SUBLANES)
            g0 = buf_ref.at[slot, 0, pl.ds(r8, SUBLANES)]
            g1 = buf_ref.at[slot, 1, pl.ds(r8, SUBLANES)]
            t0 = 2 * (step * tm + r8)
            for u in range(SUBLANES):
                pltpu.make_async_copy(ys_hbm.at[pl.ds(dest_ref[t0 + 2 * u], 1)], g0.at[pl.ds(u, 1)],
                                      sem.at[slot]).start()
                pltpu.make_async_copy(ys_hbm.at[pl.ds(dest_ref[t0 + 2 * u + 1], 1)], g1.at[pl.ds(u, 1)],
                                      sem.at[slot]).start()
            return c

        lax.fori_loop(0, tm // SUBLANES, group, 0)

    slot = lax.rem(i, 2)

    @pl.when(i == 0)
    def _():
        gather(i, slot)

    @pl.when(i + 1 < n_steps)
    def _():
        gather(i + 1, 1 - slot)

    for s in range(2):
        pltpu.make_async_copy(ys_hbm.at[pl.ds(0, tm)], buf_ref.at[slot, s], sem.at[slot]).wait()
    mf = mf_ref[...]
    y = buf_ref[slot, 0] * mf[:, 0:1] + buf_ref[slot, 1] * mf[:, 1:2]
    res = _layer_norm_rows(ALPHA * x_ref[...] + y, g_ref[...], b_ref[...])
    if nsplit is None:
        o_ref[...] = res
    else:
        @pl.when(i < nsplit)
        def _():
            oa_ref[...] = res

        @pl.when(i >= nsplit)
        def _():
            ob_ref[...] = res


def moe_combine_ln(dest, ys, mf, x, g, b, split_rows=None, tm=256):
    n = x.shape[0]
    row = pl.BlockSpec((tm, D_MODEL), lambda i, d: (i, 0))
    vec = pl.BlockSpec((1, D_MODEL), lambda i, d: (0, 0))
    if split_rows is None:
        nsplit = None
        out_specs = row
        out_shape = jax.ShapeDtypeStruct((n, D_MODEL), F32)
    else:
        nsplit = split_rows // tm
        out_specs = [pl.BlockSpec((tm, D_MODEL), lambda i, d: (jnp.minimum(i, nsplit - 1), 0)),
                     pl.BlockSpec((tm, D_MODEL), lambda i, d: (jnp.maximum(i - nsplit, 0), 0))]
        out_shape = [jax.ShapeDtypeStruct((split_rows, D_MODEL), F32),
                     jax.ShapeDtypeStruct((n - split_rows, D_MODEL), F32)]
    return pl.pallas_call(
        functools.partial(_combine_kernel, nsplit),
        grid_spec=pltpu.PrefetchScalarGridSpec(
            num_scalar_prefetch=1, grid=(n // tm,),
            in_specs=[pl.BlockSpec(memory_space=pl.ANY),
                      pl.BlockSpec((tm, LANES), lambda i, d: (i, 0)),
                      row, vec, vec],
            out_specs=out_specs,
            scratch_shapes=[pltpu.VMEM((2, 2, tm, D_MODEL), F32), pltpu.SemaphoreType.DMA((2,))]),
        out_shape=out_shape,
        compiler_params=_cparams(("arbitrary",)),
        name="moe_combine_ln",
    )(dest, ys, mf, x, g, b)


def moe_ffn_ln(x, w_router_pad, w_gu, w_down, g, b, split_rows=None):
    n = x.shape[0]
    tm = MOE_TILE
    mi, mf, cnt = moe_router(x, w_router_pad)
    counts = cnt[0, :N_EXPERTS]
    padded = (counts + tm - 1) // tm * tm
    ends = jnp.cumsum(padded)
    starts = ends - padded
    ids = mi[:, 0:2]
    ranks = mi[:, 2:4]
    dest = (starts[ids] + ranks).reshape(-1).astype(I32)
    n_tiles = (2 * n) // tm + N_EXPERTS
    tile_start = jnp.arange(n_tiles, dtype=I32) * tm
    tile_expert = jnp.minimum(jnp.sum(tile_start[:, None] >= ends[None, :], axis=1), N_EXPERTS - 1).astype(I32)
    n_used = (ends[-1] // tm).astype(I32).reshape(1)
    tile_expert = jnp.where(tile_start < ends[-1], tile_expert, tile_expert[jnp.maximum(n_used[0] - 1, 0)])
    ztile = jnp.maximum(ends - tm, 0).astype(I32)
    xs = moe_dispatch(dest, ztile, x, n_tiles * tm)
    ys = moe_experts(tile_expert, n_used, xs, w_gu, w_down)
    return moe_combine_ln(dest, ys, mf, x, g, b, split_rows)


def kernel(x_prompt, x_sample, state_hgrn, state_lru_h, state_lru_conv, ln_mix_g, ln_mix_b, ln_ffn_g, ln_ffn_b, w_hgrn_in, hgrn_lb_logits, hgrn_norm_g, w_hgrn_out, w_lru_in, lru_conv_w, lru_conv_b, w_lru_rgate, b_lru_rgate, w_lru_igate, b_lru_igate, lru_lambda, w_lru_out, w_ffn_gu, w_ffn_down, w_router, w_moe_gu, w_moe_down):
    bp, tp, _ = x_prompt.shape
    bs, ts, _ = x_sample.shape
    n_p = bp * tp
    n = n_p + bs * ts
    x = x_prompt.reshape(n_p, D_MODEL)
    x_smp, smp_row0 = x_sample.reshape(bs * ts, D_MODEL), 0

    def vec(a, i):
        return a[i].reshape(1, D_MODEL)

    w_hgrn_in_bf = w_hgrn_in.astype(BF16)
    w_hgrn_out_bf = w_hgrn_out.astype(BF16)
    w_lru_in_bf = w_lru_in.astype(BF16)
    w_lru_out_bf = w_lru_out.astype(BF16)
    w_r_bf = w_lru_rgate.astype(BF16)
    w_i_bf = w_lru_igate.astype(BF16)
    w_ffn_gu_bf = w_ffn_gu.astype(BF16)
    w_ffn_down_bf = w_ffn_down.astype(BF16)
    w_router_pad = jnp.pad(w_router, ((0, 0), (0, 0), (0, LANES - N_EXPERTS)))

    hg_p = hg_s = None
    h_p, h_s, c_p, c_s = [], [], [], []
    for layer in range(DEPTH):
        j = layer // 2
        lng, lnb = vec(ln_mix_g, layer), vec(ln_mix_b, layer)
        if layer % 2 == 0:
            ng = vec(hgrn_norm_g, j)
            xo, hg_p = hgrn_prompt_layer(x, w_hgrn_in_bf[j], hgrn_lb_logits, ng, w_hgrn_out_bf[j],
                                         lng, lnb, hg_p, j, bp, tp, n)
            q, k, lf, v, g = hgrn_in(x_smp, w_hgrn_in_bf[j], hgrn_lb_logits, j, smp_row0, bs * ts)
            o, hg_s = hgrn_rec_sample(q, k, lf, v, g, ng, state_hgrn, hg_s, j, bs, ts)
            x = proj_ln(o, w_hgrn_out_bf[j], x_smp, smp_row0, lng, lnb, xo, n_p)
            x = ffn_ln(x, w_ffn_gu_bf[j], w_ffn_down_bf[j], vec(ln_ffn_g, layer), vec(ln_ffn_b, layer))
            x_smp, smp_row0 = x, n_p
        else:
            wts = (lru_conv_w[j], vec(lru_conv_b, j), w_r_bf[j], vec(b_lru_rgate, j),
                   w_i_bf[j], vec(b_lru_igate, j), vec(lru_lambda, j))
            xo, hp, cp, moe_gu_bf, moe_down_bf = lru_prompt_layer(
                x, w_lru_in_bf[j], wts, w_lru_out_bf[j], lng, lnb, w_moe_gu, w_moe_down, j, bp, tp)
            xb, gate = lru_in(x_smp, w_lru_in_bf[j], smp_row0, bs * ts)
            cc_rows = jnp.pad(state_lru_conv[j], ((0, 0), (SUBLANES - (CONV_W - 1), 0), (0, 0)))
            h0_rows = jnp.pad(state_lru_h[j][:, None, :], ((0, 0), (0, ts - 1), (0, 0)))
            hgate, hs, cs = lru_rec_sample(xb, gate, cc_rows.reshape(bs * SUBLANES, D_MODEL),
                                           h0_rows.reshape(bs * ts, D_MODEL), wts, bs, ts)
            h_p.append(hp.reshape(bp, D_MODEL))
            h_s.append(hs.reshape(bs, D_MODEL))
            c_p.append(cp)
            c_s.append(cs)
            x = proj_ln(hgate, w_lru_out_bf[j], x_smp, smp_row0, lng, lnb, xo, n_p)
            x = moe_ffn_ln(x, w_router_pad[j], moe_gu_bf, moe_down_bf,
                           vec(ln_ffn_g, layer), vec(ln_ffn_b, layer),
                           split_rows=n_p if layer == DEPTH - 1 else None)
            x_smp, smp_row0 = x, n_p

    y_prompt = x[0].reshape(bp, tp, D_MODEL)
    y_sample = x[1].reshape(bs, ts, D_MODEL)
    return (y_prompt, y_sample, hg_p, hg_s, jnp.stack(h_p), jnp.stack(h_s), jnp.stack(c_p), jnp.stack(c_s))
```

```python
import functools
import math

import jax
import jax.numpy as jnp
from jax import lax
from jax.experimental import pallas as pl
from jax.experimental.pallas import tpu as pltpu

F32 = jnp.float32
BF16 = jnp.bfloat16
I32 = jnp.int32

D_MODEL = 1024
DEPTH = 4
HG_DK = 128
HG_HEADS = 8
HG_DV = 128
CONV_W = 4
RG_C = 8.0
LRU_BLOCKS = 4
LRU_BW = 256
D_FF = 2816
N_EXPERTS = 8
D_FF_EXPERT = 3584
ALPHA = (2 * DEPTH) ** 0.25
LN_EPS = 1e-5
F_FLOOR = 1e-20
LOG2E = math.log2(math.e)

LANES = 128
SUBLANES = 8
VMEM_LIMIT = 56 * 1024 * 1024

HG_CHUNK_PROMPT = 16
MOE_TILE = 512


def _cparams(sem):
    return pltpu.CompilerParams(dimension_semantics=sem, vmem_limit_bytes=VMEM_LIMIT)


def _sigmoid(x):
    return 0.5 * jnp.tanh(0.5 * x) + 0.5


def _gelu_tanh(y):
    c = math.sqrt(2.0 / math.pi)
    return 0.5 * y * (1.0 + jnp.tanh(c * (y + 0.044715 * (y * y * y))))


def _layer_norm_rows(z, g, b):
    mu = jnp.mean(z, axis=-1, keepdims=True)
    zc = z - mu
    var = jnp.mean(zc * zc, axis=-1, keepdims=True)
    return zc * lax.rsqrt(var + LN_EPS) * g + b


def _hgrn_in_compute(layer_j, xb, w_ref, lbl_ref, q_ref, k_ref, lf_ref, v_ref, g_ref):
    l = lbl_ref[...]
    e = jnp.exp(l - jnp.max(l, axis=0, keepdims=True))
    p = e / jnp.sum(e, axis=0, keepdims=True)
    cs = p[0:1]
    for i in range(1, layer_j + 1):
        cs = cs + p[i:i + 1]
    lb = cs - p[0:1]

    def seg(i):
        return jnp.dot(xb, w_ref[:, i * D_MODEL:(i + 1) * D_MODEL], preferred_element_type=F32)

    q = seg(0)
    q_ref[...] = q * _sigmoid(q) * (HG_DK ** -0.5)
    fp = seg(1)
    sig = 1.0 / (1.0 + jnp.exp(-fp))
    f = lb + (1.0 - lb) * sig
    lf_ref[...] = jnp.log(jnp.maximum(f, F_FLOOR))
    k_ref[...] = (1.0 - lb) * (1.0 - sig)
    v_ref[...] = seg(2)
    g = seg(3)
    g_ref[...] = g * _sigmoid(g)


def _hgrn_in_kernel(layer_j, x_ref, w_ref, lbl_ref, *out_refs):
    _hgrn_in_compute(layer_j, x_ref[...].astype(BF16), w_ref, lbl_ref, *out_refs)


def hgrn_in(x, w_in_bf, lb_logits, layer_j, row0, nrows, tm=256):
    base = row0 // tm
    out = jax.ShapeDtypeStruct((nrows, D_MODEL), F32)
    return pl.pallas_call(
        functools.partial(_hgrn_in_kernel, layer_j),
        grid=(nrows // tm,),
        in_specs=[pl.BlockSpec((tm, D_MODEL), lambda i: (base + i, 0)),
                  pl.BlockSpec((D_MODEL, 4 * D_MODEL), lambda i: (0, 0)),
                  pl.BlockSpec(lb_logits.shape, lambda i: (0, 0))],
        out_specs=[pl.BlockSpec((tm, D_MODEL), lambda i: (i, 0))] * 5,
        out_shape=[out] * 5,
        compiler_params=_cparams(("arbitrary",)),
        name="hgrn_in",
    )(x, w_in_bf, lb_logits)


def _hgrn_prefix(C, lf):
    row = lax.broadcasted_iota(I32, (C, LANES), 0)
    G = lf
    d = 1
    while d < C:
        G = G + jnp.where(row >= d, pltpu.roll(G, d, 0), 0.0)
        d *= 2
    return G


def _hgrn_diag(C, G, q, k, v):
    row8 = lax.broadcasted_iota(I32, (SUBLANES, LANES), 0)
    parts = []
    for gi in range(C // SUBLANES):
        lo = gi * SUBLANES
        Gg = G[lo:lo + SUBLANES]
        qg = q[lo:lo + SUBLANES]
        acc = jnp.zeros((SUBLANES, LANES), F32)
        for s in range(lo + SUBLANES):
            dec = jnp.exp(Gg - G[s:s + 1])
            if s >= lo:
                dec = jnp.where(row8 >= (s - lo), dec, 0.0)
            w = jnp.sum(qg * dec * k[s:s + 1], axis=-1, keepdims=True)
            acc = acc + w * v[s:s + 1]
        parts.append(acc)
    return parts[0] if len(parts) == 1 else jnp.concatenate(parts, axis=0)


def _hgrn_state_update(C, G, k, v, S):
    gend = G[C - 1:C, :]
    kt = k * jnp.exp(gend - G)
    ecol = jnp.transpose(jnp.broadcast_to(jnp.exp(gend), (HG_DK, LANES)))
    upd = lax.dot_general(kt.astype(BF16), v.astype(BF16), (((0,), (0,)), ((), ())),
                          preferred_element_type=F32)
    return ecol * S + upd


def _hgrn_chunk_head(C, q, k, lf, v, S):
    G = _hgrn_prefix(C, lf)
    o = jnp.dot((q * jnp.exp(G)).astype(BF16), S.astype(BF16), preferred_element_type=F32)
    return o + _hgrn_diag(C, G, q, k, v), _hgrn_state_update(C, G, k, v, S)


def _hgrn_rec_kernel(C, q_ref, k_ref, lf_ref, v_ref, g_ref, ng_ref, s0_ref, o_ref, s_ref):
    heads = [slice(h * LANES, (h + 1) * LANES) for h in range(HG_HEADS)]

    def body(i, carry):
        rows = pl.ds(pl.multiple_of(i * C, C), C)
        gs, os_ = [], []
        for h, sl in enumerate(heads):
            G = _hgrn_prefix(C, lf_ref[rows, sl])
            gs.append(G)
            os_.append(jnp.dot((q_ref[rows, sl] * jnp.exp(G)).astype(BF16), s0_ref[i, h].astype(BF16),
                               preferred_element_type=F32))
        for h, sl in enumerate(heads):
            o = os_[h] + _hgrn_diag(C, gs[h], q_ref[rows, sl], k_ref[rows, sl], v_ref[rows, sl])
            ms = jnp.mean(o * o, axis=-1, keepdims=True)
            o_ref[rows, sl] = o * lax.rsqrt(ms + LN_EPS) * ng_ref[:, sl] * g_ref[rows, sl]
        for h, sl in enumerate(heads):
            s_ref[i, h] = _hgrn_state_update(C, gs[h], k_ref[rows, sl], v_ref[rows, sl], s0_ref[i, h])
        return carry

    lax.fori_loop(0, q_ref.shape[0] // C, body, 0)


HG_CHUNK = 64
HG_SUB = 16
HG_SAFE_RANGE = 60.0


def _hgrn_chunk_masks(tt):
    import numpy as np
    t = np.arange(tt)[:, None]
    s = np.arange(tt)[None, :]
    tri = (s <= t) & ((t // HG_CHUNK) == (s // HG_CHUNK))
    return jnp.asarray(tri.astype(np.float32), BF16)


def _hgrn_prompt_kernel(layer_j, x_ref, win_ref, lbl_ref, ng_ref, m_ref, wout_ref, lng_ref, lnb_ref,
                        xo_ref, s_ref,
                        q_ref, k_ref, lf_ref, v_ref, g_ref, o_ref,
                        g_s, qi_s, ks_s, qd_s, kd_s, q1_s, k1_s, q2_s, k2_s, a_s, flag_s):
    tt = x_ref.shape[0]

    @pl.when(pl.program_id(1) == 0)
    def _():
        s_ref[...] = jnp.zeros(s_ref.shape, F32)

    _hgrn_in_compute(layer_j, x_ref[...].astype(BF16), win_ref, lbl_ref, q_ref, k_ref, lf_ref, v_ref, g_ref)

    lf = lf_ref[...] * LOG2E
    hi = lf.astype(BF16)
    lo = (lf - hi.astype(F32)).astype(BF16)
    g_s[...] = (jnp.dot(m_ref[...], hi, preferred_element_type=F32)
                + jnp.dot(m_ref[...], lo, preferred_element_type=F32))

    C = HG_CHUNK
    SB = HG_SUB
    worst = jnp.zeros((SB, D_MODEL), F32)
    for blk in range(tt // SB):
        r0 = blk * SB
        c0 = (r0 // C) * C
        rows = slice(r0, r0 + SB)

        def grow(r):
            return g_s[r - 1:r, :] if r > c0 else jnp.zeros((1, D_MODEL), F32)

        G = g_s[rows, :]
        q = q_ref[rows, :]
        k = k_ref[rows, :]
        qi_s[rows, :] = (q * jnp.exp2(G)).astype(BF16)
        ks_s[rows, :] = (k * jnp.exp2(g_s[c0 + C - 1:c0 + C, :] - G)).astype(BF16)
        e0 = G - grow(r0)
        qd_s[rows, :] = (q * jnp.exp2(e0)).astype(BF16)
        kd_s[rows, :] = (k * jnp.exp2(-e0)).astype(BF16)
        worst = jnp.maximum(worst, -e0)
        e1 = G - grow((r0 // (2 * SB)) * (2 * SB) + SB)
        q1_s[rows, :] = (q * jnp.exp2(jnp.minimum(e1, 0.0))).astype(BF16)
        k1_s[rows, :] = (k * jnp.exp2(jnp.minimum(-e1, 0.0))).astype(BF16)
        e2 = G - grow(c0 + C // 2)
        q2_s[rows, :] = (q * jnp.exp2(jnp.minimum(e2, 0.0))).astype(BF16)
        k2_s[rows, :] = (k * jnp.exp2(jnp.minimum(-e2, 0.0))).astype(BF16)
    flag_s[0] = (jnp.max(worst) <= HG_SAFE_RANGE * LOG2E).astype(I32)

    def finish(rows, sl, o):
        ms = jnp.mean(o * o, axis=-1, keepdims=True)
        o_ref[rows, sl] = o * lax.rsqrt(ms + LN_EPS) * ng_ref[:, sl] * g_ref[rows, sl]

    C = HG_CHUNK
    ti = lax.broadcasted_iota(I32, (C, C), 0)
    si = lax.broadcasted_iota(I32, (C, C), 1)
    causal = si <= ti
    sub_bits = HG_SUB.bit_length() - 1
    m0 = causal & ((ti >> sub_bits) == (si >> sub_bits))
    m1 = causal & ((ti >> (sub_bits + 1)) == (si >> (sub_bits + 1))) & jnp.logical_not(m0)
    nt_dims = (((1,), (1,)), ((), ()))

    @pl.when(flag_s[0] == 1)
    def _():
        def body(c, carry):
            r0 = pl.multiple_of(c * C, C)
            rows = pl.ds(r0, C)
            heads = [slice(h * LANES, (h + 1) * LANES) for h in range(HG_HEADS)]
            for h, sl in enumerate(heads):
                a0 = lax.dot_general(qd_s[rows, sl], kd_s[rows, sl], nt_dims, preferred_element_type=F32)
                a1 = lax.dot_general(q1_s[rows, sl], k1_s[rows, sl], nt_dims, preferred_element_type=F32)
                a2 = lax.dot_general(q2_s[rows, sl], k2_s[rows, sl], nt_dims, preferred_element_type=F32)
                a = jnp.where(m0, a0, jnp.where(m1, a1, jnp.where(causal, a2, 0.0)))
                a_s[h] = a.astype(BF16)
            for h, sl in enumerate(heads):
                o = (jnp.dot(qi_s[rows, sl], s_ref[0, h].astype(BF16), preferred_element_type=F32)
                     + jnp.dot(a_s[h], v_ref[rows, sl].astype(BF16), preferred_element_type=F32))
                finish(rows, sl, o)
            for h, sl in enumerate(heads):
                gend = g_s[pl.ds(pl.multiple_of(r0 + C - SUBLANES, SUBLANES), SUBLANES), sl][SUBLANES - 1:]
                ecol = jnp.transpose(jnp.broadcast_to(jnp.exp2(gend), (HG_DK, LANES)))
                upd = lax.dot_general(ks_s[rows, sl], v_ref[rows, sl].astype(BF16), (((0,), (0,)), ((), ())),
                                      preferred_element_type=F32)
                s_ref[0, h] = ecol * s_ref[0, h] + upd
            return carry

        lax.fori_loop(0, tt // C, body, 0)

    @pl.when(flag_s[0] == 0)
    def _():
        Cs = HG_CHUNK_PROMPT

        def body(i, carry):
            r0 = pl.multiple_of(i * Cs, Cs)
            rows = pl.ds(r0, Cs)
            for h in range(HG_HEADS):
                sl = slice(h * LANES, (h + 1) * LANES)
                o, s_new = _hgrn_chunk_head(Cs, q_ref[rows, sl], k_ref[rows, sl], lf_ref[rows, sl],
                                            v_ref[rows, sl], s_ref[0, h])
                finish(rows, sl, o)
                s_ref[0, h] = s_new
            return carry

        lax.fori_loop(0, tt // Cs, body, 0)

    y = jnp.dot(o_ref[...].astype(BF16), wout_ref[...], preferred_element_type=F32)
    xo_ref[...] = _layer_norm_rows(ALPHA * x_ref[...] + y, lng_ref[...], lnb_ref[...])


def hgrn_prompt_layer(x, w_in_bf, lb_logits, norm_g, w_out_bf, ln_g, ln_b, s_all, layer_j, batch, seq, n,
                      tt=256):
    nt = seq // tt
    row = pl.BlockSpec((tt, D_MODEL), lambda b, t: (b * nt + t, 0))
    vec = pl.BlockSpec((1, D_MODEL), lambda b, t: (0, 0))
    s_spec = pl.BlockSpec((None, 1, HG_HEADS, HG_DK, HG_DV), lambda b, t: (layer_j, b, 0, 0, 0))
    in_specs = [row,
                pl.BlockSpec((D_MODEL, 4 * D_MODEL), lambda b, t: (0, 0)),
                pl.BlockSpec(lb_logits.shape, lambda b, t: (0, 0)),
                vec,
                pl.BlockSpec((tt, tt), lambda b, t: (0, 0)),
                pl.BlockSpec((D_MODEL, D_MODEL), lambda b, t: (0, 0)),
                vec, vec]
    args = [x, w_in_bf, lb_logits, norm_g, _hgrn_chunk_masks(tt), w_out_bf, ln_g, ln_b]
    aliases = {}
    kern = functools.partial(_hgrn_prompt_kernel, layer_j)
    if s_all is not None:
        def kern(*refs):
            _hgrn_prompt_kernel(layer_j, *refs[:8], *refs[9:])
        in_specs.append(pl.BlockSpec(memory_space=pl.ANY))
        args.append(s_all)
        aliases = {8: 1}
    f32_tile = pltpu.VMEM((tt, D_MODEL), F32)
    bf_tile = pltpu.VMEM((tt, D_MODEL), BF16)
    return pl.pallas_call(
        kern,
        grid=(batch, nt),
        in_specs=in_specs,
        out_specs=[row, s_spec],
        out_shape=[jax.ShapeDtypeStruct((n, D_MODEL), F32),
                   jax.ShapeDtypeStruct((2, batch, HG_HEADS, HG_DK, HG_DV), F32)],
        scratch_shapes=[f32_tile] * 7 + [bf_tile] * 8
                       + [pltpu.VMEM((HG_HEADS, HG_CHUNK, HG_CHUNK), BF16), pltpu.SMEM((1,), I32)],
        input_output_aliases=aliases,
        compiler_params=_cparams(("arbitrary", "arbitrary")),
        name="hgrn_prompt_layer",
    )(*args)


def hgrn_rec_sample(q, k, lf, v, g, norm_g, state, s_all, layer_j, nseq, seq, sg=8):
    n = q.shape[0]
    rt = sg * seq
    row = pl.BlockSpec((rt, D_MODEL), lambda i: (i, 0))
    s_in = pl.BlockSpec((None, sg, HG_HEADS, HG_DK, HG_DV), lambda i: (layer_j, i, 0, 0, 0))
    in_specs = [row] * 5 + [pl.BlockSpec((1, D_MODEL), lambda i: (0, 0)), s_in]
    args = [q, k, lf, v, g, norm_g, state]
    aliases = {}
    if s_all is not None:
        in_specs.append(pl.BlockSpec(memory_space=pl.ANY))
        args.append(s_all)
        aliases[7] = 1

    def kern(*refs):
        _hgrn_rec_kernel(seq, *refs[:7], *refs[len(args):])

    return pl.pallas_call(
        kern,
        grid=(nseq // sg,),
        in_specs=in_specs,
        out_specs=[row, s_in],
        out_shape=[jax.ShapeDtypeStruct((n, D_MODEL), F32),
                   jax.ShapeDtypeStruct((2, nseq, HG_HEADS, HG_DK, HG_DV), F32)],
        input_output_aliases=aliases,
        compiler_params=_cparams(("arbitrary",)),
        name="hgrn_rec_sample",
    )(*args)


def _proj_ln_kernel(a_ref, w_ref, x_ref, g_ref, b_ref, o_ref):
    y = jnp.dot(a_ref[...].astype(BF16), w_ref[...], preferred_element_type=F32)
    o_ref[...] = _layer_norm_rows(ALPHA * x_ref[...] + y, g_ref[...], b_ref[...])


def proj_ln(a, w_bf, x, x_row0, g, b, xo, row0, tm=512):
    n = xo.shape[0]
    nrows = a.shape[0]
    base = row0 // tm
    xbase = x_row0 // tm
    row = pl.BlockSpec((tm, D_MODEL), lambda i: (base + i, 0))
    vec = pl.BlockSpec((1, D_MODEL), lambda i: (0, 0))

    def kern(a_ref, w_ref, x_ref, g_ref, b_ref, _, o_ref):
        _proj_ln_kernel(a_ref, w_ref, x_ref, g_ref, b_ref, o_ref)

    return pl.pallas_call(
        kern,
        grid=(nrows // tm,),
        in_specs=[pl.BlockSpec((tm, D_MODEL), lambda i: (i, 0)),
                  pl.BlockSpec((D_MODEL, D_MODEL), lambda i: (0, 0)),
                  pl.BlockSpec((tm, D_MODEL), lambda i: (xbase + i, 0)), vec, vec,
                  pl.BlockSpec(memory_space=pl.ANY)],
        out_specs=row,
        out_shape=jax.ShapeDtypeStruct((n, D_MODEL), F32),
        input_output_aliases={5: 0},
        compiler_params=_cparams(("arbitrary",)),
        name="proj_ln",
    )(a, w_bf, x, g, b, xo)


def _ffn_kernel(x_ref, wg_ref, wu_ref, wd_ref, g_ref, b_ref, o_ref):
    xb = x_ref[...].astype(BF16)
    gg = jnp.dot(xb, wg_ref[...], preferred_element_type=F32)
    uu = jnp.dot(xb, wu_ref[...], preferred_element_type=F32)
    h = (gg * _sigmoid(gg) * uu).astype(BF16)
    y = jnp.dot(h, wd_ref[...], preferred_element_type=F32)
    o_ref[...] = _layer_norm_rows(ALPHA * x_ref[...] + y, g_ref[...], b_ref[...])


def ffn_ln(x, w_gu_bf, w_down_bf, g, b, tm=512):
    n = x.shape[0]
    row = pl.BlockSpec((tm, D_MODEL), lambda i: (i, 0))
    vec = pl.BlockSpec((1, D_MODEL), lambda i: (0, 0))
    once = pl.Buffered(1)
    return pl.pallas_call(
        _ffn_kernel,
        grid=(n // tm,),
        in_specs=[row,
                  pl.BlockSpec((D_MODEL, D_FF), lambda i: (0, 0), pipeline_mode=once),
                  pl.BlockSpec((D_MODEL, D_FF), lambda i: (0, 1), pipeline_mode=once),
                  pl.BlockSpec((D_FF, D_MODEL), lambda i: (0, 0), pipeline_mode=once),
                  vec, vec],
        out_specs=row,
        out_shape=jax.ShapeDtypeStruct((n, D_MODEL), F32),
        compiler_params=_cparams(("arbitrary",)),
        name="ffn_ln",
    )(x, w_gu_bf, w_gu_bf, w_down_bf, g, b)


def _lru_in_kernel(x_ref, w_ref, xb_ref, gate_ref):
    xb = x_ref[...].astype(BF16)
    xb_ref[...] = jnp.dot(xb, w_ref[:, :D_MODEL], preferred_element_type=F32)
    y = jnp.dot(xb, w_ref[:, D_MODEL:], preferred_element_type=F32)
    gate_ref[...] = _gelu_tanh(y)


def lru_in(x, w_in_bf, row0, nrows, tm=512):
    base = row0 // tm
    row = pl.BlockSpec((tm, D_MODEL), lambda i: (i, 0))
    out = jax.ShapeDtypeStruct((nrows, D_MODEL), F32)
    return pl.pallas_call(
        _lru_in_kernel,
        grid=(nrows // tm,),
        in_specs=[pl.BlockSpec((tm, D_MODEL), lambda i: (base + i, 0)),
                  pl.BlockSpec((D_MODEL, 2 * D_MODEL), lambda i: (0, 0))],
        out_specs=[row, row],
        out_shape=[out, out],
        compiler_params=_cparams(("arbitrary",)),
        name="lru_in",
    )(x, w_in_bf)


def _lru_rec_kernel(L, carry, *refs):
    if carry:
        (x_ref, win_ref, cw_ref, cb_ref, wr_ref, br_ref, wi_ref, bi_ref, lam_ref, wout_ref, lng_ref, lnb_ref,
         cgu_ref, cdn_ref, xo_ref, hn_ref, cn_ref, cgu_o, cdn_o, xp_ref, hs_ref, hc_ref, gate_ref, hg_ref) = refs
        R = x_ref.shape[0]
        cgu_o[...] = cgu_ref[...].astype(BF16)
        cdn_o[...] = cdn_ref[...].astype(BF16)
    else:
        (xb_ref, gate_ref, cc_ref, h0_ref, cw_ref, cb_ref, wr_ref, br_ref, wi_ref, bi_ref, lam_ref,
         hg_ref, hn_ref, cn_ref, xp_ref, hs_ref) = refs
        R = xb_ref.shape[0]
    S = R // L
    PADR = SUBLANES
    t = pl.program_id(1) if carry else 0

    if carry:
        @pl.when(t == 0)
        def _():
            xp_ref[:, 0:PADR, :] = jnp.zeros((S, PADR, D_MODEL), F32)
            hc_ref[...] = jnp.zeros(hc_ref.shape, F32)

        @pl.when(t > 0)
        def _():
            xp_ref[:, 0:PADR, :] = xp_ref[:, L:L + PADR, :]

        xin = x_ref[...].astype(BF16)
        xp_ref[:, PADR:PADR + L, :] = jnp.dot(xin, win_ref[:, :D_MODEL],
                                              preferred_element_type=F32).reshape(S, L, D_MODEL)
        gate_ref[...] = _gelu_tanh(jnp.dot(xin, win_ref[:, D_MODEL:], preferred_element_type=F32))
    else:
        xp_ref[:, 0:PADR, :] = cc_ref[...].reshape(S, PADR, D_MODEL)
        xp_ref[:, PADR:PADR + L, :] = xb_ref[...].reshape(S, L, D_MODEL)

    xc = cb_ref[...].reshape(1, 1, D_MODEL) + xp_ref[:, PADR:PADR + L, :] * cw_ref[CONV_W - 1:CONV_W, :].reshape(1, 1, D_MODEL)
    for j in range(CONV_W - 1):
        off = PADR - (CONV_W - 1) + j
        xc = xc + xp_ref[:, off:off + L, :] * cw_ref[j:j + 1, :].reshape(1, 1, D_MODEL)
    xc = xc.reshape(R, D_MODEL)
    cn_ref[...] = xp_ref[:, L + PADR - (CONV_W - 1):L + PADR, :].reshape(cn_ref.shape)

    assert L & (L - 1) == 0
    pos = jnp.bitwise_and(lax.broadcasted_iota(I32, (R, LRU_BW), 0), L - 1)
    for nb in range(LRU_BLOCKS):
        sl = slice(nb * LRU_BW, (nb + 1) * LRU_BW)
        xcb = xc[:, sl]
        xh = xcb.astype(BF16)
        r = _sigmoid(jnp.dot(xh, wr_ref[nb], preferred_element_type=F32) + br_ref[:, sl])
        ig = _sigmoid(jnp.dot(xh, wi_ref[nb], preferred_element_type=F32) + bi_ref[:, sl])
        z = -lam_ref[:, sl]
        softplus = jnp.maximum(z, 0.0) + jnp.log1p(jnp.exp(-jnp.abs(z)))
        log_a = (-RG_C) * r * softplus
        a = jnp.exp(log_a)
        u = jnp.sqrt(1.0 - a * a) * (ig * xcb)
        if carry:
            ng = R // SUBLANES
            u3 = u.reshape(ng, SUBLANES, LRU_BW)
            a3 = a.reshape(ng, SUBLANES, LRU_BW)
            pos8 = lax.broadcasted_iota(I32, (ng, SUBLANES, LRU_BW), 1)
            d = 1
            while d < SUBLANES:
                m = pos8 >= d
                u3 = jnp.where(m, a3 * pltpu.roll(u3, d, 1) + u3, u3)
                a3 = jnp.where(m, a3 * pltpu.roll(a3, d, 1), a3)
                d *= 2
            hprev = hc_ref[:, sl]
            for gi in range(ng):
                hgrp = u3[gi] + a3[gi] * hprev
                hs_ref[0, gi * SUBLANES:(gi + 1) * SUBLANES, sl] = hgrp
                hprev = hgrp[SUBLANES - 1:SUBLANES]
            hg_ref[:, sl] = hs_ref[0, :, sl] * gate_ref[:, sl]
        else:
            u = u + a * h0_ref[:, sl]
            d = 1
            while d < L:
                m = pos >= d
                u = jnp.where(m, a * pltpu.roll(u, d, 0) + u, u)
                a = jnp.where(m, a * pltpu.roll(a, d, 0), a)
                d *= 2
            hg_ref[:, sl] = u * gate_ref[:, sl]
            hs_ref[:, :, sl] = u.reshape(S, L, LRU_BW)

    hlast = hs_ref[:, L - 1:L, :]
    hn_ref[...] = hlast.reshape(hn_ref.shape)
    if carry:
        hc_ref[...] = hlast.reshape(1, D_MODEL)
        y = jnp.dot(hg_ref[...].astype(BF16), wout_ref[...], preferred_element_type=F32)
        xo_ref[...] = _layer_norm_rows(ALPHA * x_ref[...] + y, lng_ref[...], lnb_ref[...])


def _lru_weight_specs(nidx):
    def c2(*_):
        return (0, 0)

    def c3(*_):
        return (0, 0, 0)
    return [pl.BlockSpec((CONV_W, D_MODEL), c2), pl.BlockSpec((1, D_MODEL), c2),
            pl.BlockSpec((LRU_BLOCKS, LRU_BW, LRU_BW), c3), pl.BlockSpec((1, D_MODEL), c2),
            pl.BlockSpec((LRU_BLOCKS, LRU_BW, LRU_BW), c3), pl.BlockSpec((1, D_MODEL), c2),
            pl.BlockSpec((1, D_MODEL), c2)]


def lru_prompt_layer(x, w_in_bf, wts, w_out_bf, ln_g, ln_b, moe_gu, moe_down, layer_j, batch, seq, tt=256):
    n = x.shape[0]
    nt = seq // tt
    steps = batch * nt
    gu2 = moe_gu.reshape(moe_gu.shape[0], N_EXPERTS * D_MODEL, 2 * D_FF_EXPERT)
    dn2 = moe_down.reshape(moe_down.shape[0], N_EXPERTS * D_FF_EXPERT, D_MODEL)
    gu_rows = gu2.shape[1] // steps
    dn_rows = dn2.shape[1] // steps
    assert gu_rows * steps == gu2.shape[1] and dn_rows * steps == dn2.shape[1]
    row = pl.BlockSpec((tt, D_MODEL), lambda b, t: (b * nt + t, 0))
    vec = pl.BlockSpec((1, D_MODEL), lambda b, t: (0, 0))
    xo, hn, cn, gu_bf, dn_bf = pl.pallas_call(
        functools.partial(_lru_rec_kernel, tt, True),
        grid=(batch, nt),
        in_specs=[row, pl.BlockSpec((D_MODEL, 2 * D_MODEL), lambda b, t: (0, 0))] + _lru_weight_specs(2)
                 + [pl.BlockSpec((D_MODEL, D_MODEL), lambda b, t: (0, 0)), vec, vec,
                    pl.BlockSpec((None, gu_rows, 2 * D_FF_EXPERT), lambda b, t: (layer_j, b * nt + t, 0)),
                    pl.BlockSpec((None, dn_rows, D_MODEL), lambda b, t: (layer_j, b * nt + t, 0))],
        out_specs=[row,
                   pl.BlockSpec((1, 1, D_MODEL), lambda b, t: (b, 0, 0)),
                   pl.BlockSpec((1, CONV_W - 1, D_MODEL), lambda b, t: (b, 0, 0)),
                   pl.BlockSpec((gu_rows, 2 * D_FF_EXPERT), lambda b, t: (b * nt + t, 0)),
                   pl.BlockSpec((dn_rows, D_MODEL), lambda b, t: (b * nt + t, 0))],
        out_shape=[jax.ShapeDtypeStruct((n, D_MODEL), F32),
                   jax.ShapeDtypeStruct((batch, 1, D_MODEL), F32),
                   jax.ShapeDtypeStruct((batch, CONV_W - 1, D_MODEL), F32),
                   jax.ShapeDtypeStruct(gu2.shape[1:], BF16),
                   jax.ShapeDtypeStruct(dn2.shape[1:], BF16)],
        scratch_shapes=[pltpu.VMEM((1, tt + SUBLANES, D_MODEL), F32),
                        pltpu.VMEM((1, tt, D_MODEL), F32),
                        pltpu.VMEM((1, D_MODEL), F32),
                        pltpu.VMEM((tt, D_MODEL), F32),
                        pltpu.VMEM((tt, D_MODEL), F32)],
        compiler_params=_cparams(("arbitrary", "arbitrary")),
        name="lru_prompt_layer",
    )(x, w_in_bf, *wts, w_out_bf, ln_g, ln_b, gu2, dn2)
    return (xo, hn, cn, gu_bf.reshape(N_EXPERTS, D_MODEL, 2 * D_FF_EXPERT),
            dn_bf.reshape(N_EXPERTS, D_FF_EXPERT, D_MODEL))


def lru_rec_sample(xb, gate, cc_rows, h0_rows, wts, nseq, seq, sg=32):
    n = xb.shape[0]
    rt = sg * seq
    row = pl.BlockSpec((rt, D_MODEL), lambda i: (i, 0))
    return pl.pallas_call(
        functools.partial(_lru_rec_kernel, seq, False),
        grid=(nseq // sg,),
        in_specs=[row, row, pl.BlockSpec((sg * SUBLANES, D_MODEL), lambda i: (i, 0)), row]
                 + _lru_weight_specs(1),
        out_specs=[row,
                   pl.BlockSpec((sg, 1, D_MODEL), lambda i: (i, 0, 0)),
                   pl.BlockSpec((sg, CONV_W - 1, D_MODEL), lambda i: (i, 0, 0))],
        out_shape=[jax.ShapeDtypeStruct((n, D_MODEL), F32),
                   jax.ShapeDtypeStruct((nseq, 1, D_MODEL), F32),
                   jax.ShapeDtypeStruct((nseq, CONV_W - 1, D_MODEL), F32)],
        scratch_shapes=[pltpu.VMEM((sg, seq + SUBLANES, D_MODEL), F32),
                        pltpu.VMEM((sg, seq, D_MODEL), F32)],
        compiler_params=_cparams(("arbitrary",)),
        name="lru_rec_sample",
    )(xb, gate, cc_rows, h0_rows, *wts)


def _router_kernel(x_ref, w_ref, mi_ref, mf_ref, cnt_ref, run_ref):
    i = pl.program_id(0)

    @pl.when(i == 0)
    def _():
        run_ref[...] = jnp.zeros(run_ref.shape, F32)

    tm = x_ref.shape[0]
    x = x_ref[...]
    w = w_ref[...]
    xh = x.astype(BF16)
    xl = (x - xh.astype(F32)).astype(BF16)
    wh = w.astype(BF16)
    wl = (w - wh.astype(F32)).astype(BF16)
    logits = (jnp.dot(xh, wh, preferred_element_type=F32) + jnp.dot(xh, wl, preferred_element_type=F32)
              + jnp.dot(xl, wh, preferred_element_type=F32))
    lane_i = lax.broadcasted_iota(I32, (tm, LANES), 1)
    lane = lane_i.astype(F32)
    neg = jnp.float32(-jnp.inf)
    logits = jnp.where(lane_i < N_EXPERTS, logits, neg)
    m1 = jnp.max(logits, axis=-1, keepdims=True)
    i1 = jnp.min(jnp.where(logits == m1, lane, float(LANES)), axis=-1, keepdims=True)
    l2 = jnp.where(lane == i1, neg, logits)
    m2 = jnp.max(l2, axis=-1, keepdims=True)
    i2 = jnp.min(jnp.where(l2 == m2, lane, float(LANES)), axis=-1, keepdims=True)
    e2 = jnp.exp(m2 - m1)
    p1 = 1.0 / (1.0 + e2)
    p2 = e2 * p1

    hit1 = lane == i1
    hit2 = lane == i2
    onehot = jnp.where(hit1 | hit2, 1.0, 0.0)
    r_i = lax.broadcasted_iota(I32, (tm, tm), 0)
    c_i = lax.broadcasted_iota(I32, (tm, tm), 1)
    tri = jnp.where(c_i < r_i, 1.0, 0.0).astype(BF16)
    rank = jnp.dot(tri, onehot.astype(BF16), preferred_element_type=F32) + run_ref[0:1, :]
    r1 = jnp.sum(jnp.where(hit1, rank, 0.0), axis=-1, keepdims=True)
    r2 = jnp.sum(jnp.where(hit2, rank, 0.0), axis=-1, keepdims=True)
    total = run_ref[0:1, :] + jnp.sum(onehot, axis=0, keepdims=True)
    run_ref[...] = jnp.broadcast_to(total, run_ref.shape)
    cnt_ref[...] = jnp.broadcast_to(total, cnt_ref.shape).astype(I32)

    mi = jnp.where(lane_i == 0, i1, jnp.where(lane_i == 1, i2, 0.0))
    mi = jnp.where(lane_i == 2, r1, jnp.where(lane_i == 3, r2, mi))
    mi_ref[...] = mi.astype(I32)
    mf_ref[...] = jnp.where(lane_i == 0, p1, jnp.where(lane_i == 1, p2, 0.0))


def moe_router(x, w_router_pad, tm=512):
    n = x.shape[0]
    row = pl.BlockSpec((tm, LANES), lambda i: (i, 0))
    return pl.pallas_call(
        _router_kernel,
        grid=(n // tm,),
        in_specs=[pl.BlockSpec((tm, D_MODEL), lambda i: (i, 0)),
                  pl.BlockSpec((D_MODEL, LANES), lambda i: (0, 0))],
        out_specs=[row, row, pl.BlockSpec((SUBLANES, LANES), lambda i: (0, 0))],
        out_shape=[jax.ShapeDtypeStruct((n, LANES), I32),
                   jax.ShapeDtypeStruct((n, LANES), F32),
                   jax.ShapeDtypeStruct((SUBLANES, LANES), I32)],
        scratch_shapes=[pltpu.VMEM((SUBLANES, LANES), F32)],
        compiler_params=_cparams(("arbitrary",)),
        name="moe_router",
    )(x, w_router_pad)


ZERO_ROWS = 256


def _dispatch_kernel(dest_ref, ztile_ref, x_ref, xs_hbm, zero_ref, sem):
    i = pl.program_id(0)
    tm = x_ref.shape[0]

    @pl.when(i == 0)
    def _():
        zero_ref[...] = jnp.zeros(zero_ref.shape, F32)
        copies = [pltpu.make_async_copy(
                      zero_ref,
                      xs_hbm.at[pl.ds(pl.multiple_of(ztile_ref[e] + c * ZERO_ROWS, ZERO_ROWS), ZERO_ROWS)], sem)
                  for e in range(N_EXPERTS) for c in range(MOE_TILE // ZERO_ROWS)]
        for cp in copies:
            cp.start()
        for cp in copies:
            cp.wait()

    def group(jj, c):
        r8 = pl.multiple_of(jj * SUBLANES, SUBLANES)
        grp = x_ref.at[pl.ds(r8, SUBLANES)]
        t0 = 2 * (i * tm + r8)
        for u in range(SUBLANES):
            src = grp.at[pl.ds(u, 1)]
            pltpu.make_async_copy(src, xs_hbm.at[pl.ds(dest_ref[t0 + 2 * u], 1)], sem).start()
            pltpu.make_async_copy(src, xs_hbm.at[pl.ds(dest_ref[t0 + 2 * u + 1], 1)], sem).start()
        return c

    lax.fori_loop(0, tm // SUBLANES, group, 0)
    for _ in range(2):
        pltpu.make_async_copy(x_ref, xs_hbm.at[pl.ds(0, tm)], sem).wait()


def moe_dispatch(dest, ztile, x, n_rows, tm=512):
    n = x.shape[0]
    return pl.pallas_call(
        _dispatch_kernel,
        grid_spec=pltpu.PrefetchScalarGridSpec(
            num_scalar_prefetch=2, grid=(n // tm,),
            in_specs=[pl.BlockSpec((tm, D_MODEL), lambda i, d, z: (i, 0))],
            out_specs=pl.BlockSpec(memory_space=pl.ANY),
            scratch_shapes=[pltpu.VMEM((ZERO_ROWS, D_MODEL), F32), pltpu.SemaphoreType.DMA(())]),
        out_shape=jax.ShapeDtypeStruct((n_rows, D_MODEL), F32),
        compiler_params=pltpu.CompilerParams(dimension_semantics=("arbitrary",),
                                             has_side_effects=True),
        name="moe_dispatch",
    )(dest, ztile, x)


def _expert_kernel(te_ref, nu_ref, xs_ref, wg_ref, wu_ref, wd_ref, ys_ref, xb_ref, acc_ref):
    i = pl.program_id(0)
    f = pl.program_id(1)
    used = i < nu_ref[0]

    @pl.when(used & (f == 0))
    def _():
        xb_ref[...] = xs_ref[...].astype(BF16)

    @pl.when(used)
    def _():
        xb = xb_ref[...]
        gg = jnp.dot(xb, wg_ref[...], preferred_element_type=F32)
        uu = jnp.dot(xb, wu_ref[...], preferred_element_type=F32)
        h = (gg * _sigmoid(gg) * uu).astype(BF16)
        part = jnp.dot(h, wd_ref[...], preferred_element_type=F32)

        @pl.when(f == 0)
        def _():
            acc_ref[...] = part

        @pl.when(f > 0)
        def _():
            acc_ref[...] += part

    @pl.when(f == pl.num_programs(1) - 1)
    def _():
        @pl.when(used)
        def _():
            ys_ref[...] = acc_ref[...]

        @pl.when(jnp.logical_not(used))
        def _():
            ys_ref[...] = jnp.zeros(ys_ref.shape, F32)


def moe_experts(tile_expert, n_used, xs, w_gu_bf, w_down_bf, tf=1792):
    n_rows = xs.shape[0]
    tm = MOE_TILE
    nf = D_FF_EXPERT // tf

    def xrow(i, f, te, nu):
        return (jnp.minimum(i, nu[0] - 1), 0)

    def wg(i, f, te, nu):
        return (te[i], 0, jnp.where(i < nu[0], f, nf - 1))

    def wu(i, f, te, nu):
        return (te[i], 0, nf + jnp.where(i < nu[0], f, nf - 1))

    def wd(i, f, te, nu):
        return (te[i], jnp.where(i < nu[0], f, nf - 1), 0)

    return pl.pallas_call(
        _expert_kernel,
        grid_spec=pltpu.PrefetchScalarGridSpec(
            num_scalar_prefetch=2, grid=(n_rows // tm, nf),
            in_specs=[pl.BlockSpec((tm, D_MODEL), xrow),
                      pl.BlockSpec((None, D_MODEL, tf), wg),
                      pl.BlockSpec((None, D_MODEL, tf), wu),
                      pl.BlockSpec((None, tf, D_MODEL), wd)],
            out_specs=pl.BlockSpec((tm, D_MODEL), lambda i, f, te, nu: (i, 0)),
            scratch_shapes=[pltpu.VMEM((tm, D_MODEL), BF16), pltpu.VMEM((tm, D_MODEL), F32)]),
        out_shape=jax.ShapeDtypeStruct((n_rows, D_MODEL), F32),
        compiler_params=_cparams(("arbitrary", "arbitrary")),
        name="moe_experts",
    )(tile_expert, n_used, xs, w_gu_bf, w_gu_bf, w_down_bf)


def _combine_kernel(nsplit, dest_ref, ys_hbm, mf_ref, x_ref, g_ref, b_ref, *rest):
    if nsplit is None:
        o_ref, buf_ref, sem = rest
    else:
        oa_ref, ob_ref, buf_ref, sem = rest
    i = pl.program_id(0)
    n_steps = pl.num_programs(0)
    tm = x_ref.shape[0]

    def gather(step, slot):
        def group(jj, c):
            r8 = pl.multiple_of(jj * SUBLANES, SUBLANES)
            g0 = buf_ref.at[slot, 0, pl.ds(r8, SUBLANES)]
            g1 = buf_ref.at[slot, 1, pl.ds(r8, SUBLANES)]
            t0 = 2 * (step * tm + r8)
            for u in range(SUBLANES):
                pltpu.make_async_copy(ys_hbm.at[pl.ds(dest_ref[t0 + 2 * u], 1)], g0.at[pl.ds(u, 1)],
                                      sem.at[slot]).start()
                pltpu.make_async_copy(ys_hbm.at[pl.ds(dest_ref[t0 + 2 * u + 1], 1)], g1.at[pl.ds(u, 1)],
                                      sem.at[slot]).start()
            return c

        lax.fori_loop(0, tm // SUBLANES, group, 0)

    slot = lax.rem(i, 2)

    @pl.when(i == 0)
    def _():
        gather(i, slot)

    @pl.when(i + 1 < n_steps)
    def _():
        gather(i + 1, 1 - slot)

    for s in range(2):
        pltpu.make_async_copy(ys_hbm.at[pl.ds(0, tm)], buf_ref.at[slot, s], sem.at[slot]).wait()
    mf = mf_ref[...]
    y = buf_ref[slot, 0] * mf[:, 0:1] + buf_ref[slot, 1] * mf[:, 1:2]
    res = _layer_norm_rows(ALPHA * x_ref[...] + y, g_ref[...], b_ref[...])
    if nsplit is None:
        o_ref[...] = res
    else:
        @pl.when(i < nsplit)
        def _():
            oa_ref[...] = res

        @pl.when(i >= nsplit)
        def _():
            ob_ref[...] = res


def moe_combine_ln(dest, ys, mf, x, g, b, split_rows=None, tm=256):
    n = x.shape[0]
    row = pl.BlockSpec((tm, D_MODEL), lambda i, d: (i, 0))
    vec = pl.BlockSpec((1, D_MODEL), lambda i, d: (0, 0))
    if split_rows is None:
        nsplit = None
        out_specs = row
        out_shape = jax.ShapeDtypeStruct((n, D_MODEL), F32)
    else:
        nsplit = split_rows // tm
        out_specs = [pl.BlockSpec((tm, D_MODEL), lambda i, d: (jnp.minimum(i, nsplit - 1), 0)),
                     pl.BlockSpec((tm, D_MODEL), lambda i, d: (jnp.maximum(i - nsplit, 0), 0))]
        out_shape = [jax.ShapeDtypeStruct((split_rows, D_MODEL), F32),
                     jax.ShapeDtypeStruct((n - split_rows, D_MODEL), F32)]
    return pl.pallas_call(
        functools.partial(_combine_kernel, nsplit),
        grid_spec=pltpu.PrefetchScalarGridSpec(
            num_scalar_prefetch=1, grid=(n // tm,),
            in_specs=[pl.BlockSpec(memory_space=pl.ANY),
                      pl.BlockSpec((tm, LANES), lambda i, d: (i, 0)),
                      row, vec, vec],
            out_specs=out_specs,
            scratch_shapes=[pltpu.VMEM((2, 2, tm, D_MODEL), F32), pltpu.SemaphoreType.DMA((2,))]),
        out_shape=out_shape,
        compiler_params=_cparams(("arbitrary",)),
        name="moe_combine_ln",
    )(dest, ys, mf, x, g, b)


def moe_ffn_ln(x, w_router_pad, w_gu, w_down, g, b, split_rows=None):
    n = x.shape[0]
    tm = MOE_TILE
    mi, mf, cnt = moe_router(x, w_router_pad)
    counts = cnt[0, :N_EXPERTS]
    padded = (counts + tm - 1) // tm * tm
    ends = jnp.cumsum(padded)
    starts = ends - padded
    ids = mi[:, 0:2]
    ranks = mi[:, 2:4]
    dest = (starts[ids] + ranks).reshape(-1).astype(I32)
    n_tiles = (2 * n) // tm + N_EXPERTS
    tile_start = jnp.arange(n_tiles, dtype=I32) * tm
    tile_expert = jnp.minimum(jnp.sum(tile_start[:, None] >= ends[None, :], axis=1), N_EXPERTS - 1).astype(I32)
    n_used = (ends[-1] // tm).astype(I32).reshape(1)
    tile_expert = jnp.where(tile_start < ends[-1], tile_expert, tile_expert[jnp.maximum(n_used[0] - 1, 0)])
    ztile = jnp.maximum(ends - tm, 0).astype(I32)
    xs = moe_dispatch(dest, ztile, x, n_tiles * tm)
    ys = moe_experts(tile_expert, n_used, xs, w_gu, w_down)
    return moe_combine_ln(dest, ys, mf, x, g, b, split_rows)


def kernel(x_prompt, x_sample, state_hgrn, state_lru_h, state_lru_conv, ln_mix_g, ln_mix_b, ln_ffn_g, ln_ffn_b, w_hgrn_in, hgrn_lb_logits, hgrn_norm_g, w_hgrn_out, w_lru_in, lru_conv_w, lru_conv_b, w_lru_rgate, b_lru_rgate, w_lru_igate, b_lru_igate, lru_lambda, w_lru_out, w_ffn_gu, w_ffn_down, w_router, w_moe_gu, w_moe_down):
    bp, tp, _ = x_prompt.shape
    bs, ts, _ = x_sample.shape
    n_p = bp * tp
    n = n_p + bs * ts
    x = x_prompt.reshape(n_p, D_MODEL)
    x_smp, smp_row0 = x_sample.reshape(bs * ts, D_MODEL), 0

    def vec(a, i):
        return a[i].reshape(1, D_MODEL)

    w_hgrn_in_bf = w_hgrn_in.astype(BF16)
    w_hgrn_out_bf = w_hgrn_out.astype(BF16)
    w_lru_in_bf = w_lru_in.astype(BF16)
    w_lru_out_bf = w_lru_out.astype(BF16)
    w_r_bf = w_lru_rgate.astype(BF16)
    w_i_bf = w_lru_igate.astype(BF16)
    w_ffn_gu_bf = w_ffn_gu.astype(BF16)
    w_ffn_down_bf = w_ffn_down.astype(BF16)
    w_router_pad = jnp.pad(w_router, ((0, 0), (0, 0), (0, LANES - N_EXPERTS)))

    hg_p = hg_s = None
    h_p, h_s, c_p, c_s = [], [], [], []
    for layer in range(DEPTH):
        j = layer // 2
        lng, lnb = vec(ln_mix_g, layer), vec(ln_mix_b, layer)
        if layer % 2 == 0:
            ng = vec(hgrn_norm_g, j)
            xo, hg_p = hgrn_prompt_layer(x, w_hgrn_in_bf[j], hgrn_lb_logits, ng, w_hgrn_out_bf[j],
                                         lng, lnb, hg_p, j, bp, tp, n)
            q, k, lf, v, g = hgrn_in(x_smp, w_hgrn_in_bf[j], hgrn_lb_logits, j, smp_row0, bs * ts)
            o, hg_s = hgrn_rec_sample(q, k, lf, v, g, ng, state_hgrn, hg_s, j, bs, ts)
            x = proj_ln(o, w_hgrn_out_bf[j], x_smp, smp_row0, lng, lnb, xo, n_p)
            x = ffn_ln(x, w_ffn_gu_bf[j], w_ffn_down_bf[j], vec(ln_ffn_g, layer), vec(ln_ffn_b, layer))
            x_smp, smp_row0 = x, n_p
        else:
            wts = (lru_conv_w[j], vec(lru_conv_b, j), w_r_bf[j], vec(b_lru_rgate, j),
                   w_i_bf[j], vec(b_lru_igate, j), vec(lru_lambda, j))
            xo, hp, cp, moe_gu_bf, moe_down_bf = lru_prompt_layer(
                x, w_lru_in_bf[j], wts, w_lru_out_bf[j], lng, lnb, w_moe_gu, w_moe_down, j, bp, tp)
            xb, gate = lru_in(x_smp, w_lru_in_bf[j], smp_row0, bs * ts)
            cc_rows = jnp.pad(state_lru_conv[j], ((0, 0), (SUBLANES - (CONV_W - 1), 0), (0, 0)))
            h0_rows = jnp.pad(state_lru_h[j][:, None, :], ((0, 0), (0, ts - 1), (0, 0)))
            hgate, hs, cs = lru_rec_sample(xb, gate, cc_rows.reshape(bs * SUBLANES, D_MODEL),
                                           h0_rows.reshape(bs * ts, D_MODEL), wts, bs, ts)
            h_p.append(hp.reshape(bp, D_MODEL))
            h_s.append(hs.reshape(bs, D_MODEL))
            c_p.append(cp)
            c_s.append(cs)
            x = proj_ln(hgate, w_lru_out_bf[j], x_smp, smp_row0, lng, lnb, xo, n_p)
            x = moe_ffn_ln(x, w_router_pad[j], moe_gu_bf, moe_down_bf,
                           vec(ln_ffn_g, layer), vec(ln_ffn_b, layer),
                           split_rows=n_p if layer == DEPTH - 1 else None)
            x_smp, smp_row0 = x, n_p

    y_prompt = x[0].reshape(bp, tp, D_MODEL)
    y_sample = x[1].reshape(bs, ts, D_MODEL)
    return (y_prompt, y_sample, hg_p, hg_s, jnp.stack(h_p), jnp.stack(h_s), jnp.stack(c_p), jnp.stack(c_s))
```

```python
import functools
import math

import jax
import jax.numpy as jnp
from jax import lax
from jax.experimental import pallas as pl
from jax.experimental.pallas import tpu as pltpu

F32 = jnp.float32
BF16 = jnp.bfloat16
I32 = jnp.int32

D_MODEL = 1024
DEPTH = 4
HG_DK = 128
HG_HEADS = 8
HG_DV = 128
CONV_W = 4
RG_C = 8.0
LRU_BLOCKS = 4
LRU_BW = 256
D_FF = 2816
N_EXPERTS = 8
D_FF_EXPERT = 3584
ALPHA = (2 * DEPTH) ** 0.25
LN_EPS = 1e-5
F_FLOOR = 1e-20
LOG2E = math.log2(math.e)

LANES = 128
SUBLANES = 8
VMEM_LIMIT = 56 * 1024 * 1024

HG_CHUNK_PROMPT = 16
MOE_TILE = 512


def _cparams(sem):
    return pltpu.CompilerParams(dimension_semantics=sem, vmem_limit_bytes=VMEM_LIMIT)


def _sigmoid(x):
    return 0.5 * jnp.tanh(0.5 * x) + 0.5


def _gelu_tanh(y):
    c = math.sqrt(2.0 / math.pi)
    return 0.5 * y * (1.0 + jnp.tanh(c * (y + 0.044715 * (y * y * y))))


def _layer_norm_rows(z, g, b):
    mu = jnp.mean(z, axis=-1, keepdims=True)
    zc = z - mu
    var = jnp.mean(zc * zc, axis=-1, keepdims=True)
    return zc * lax.rsqrt(var + LN_EPS) * g + b


def _hgrn_in_compute(layer_j, xb, w_ref, lbl_ref, q_ref, k_ref, lf_ref, v_ref, g_ref):
    l = lbl_ref[...]
    e = jnp.exp(l - jnp.max(l, axis=0, keepdims=True))
    p = e / jnp.sum(e, axis=0, keepdims=True)
    cs = p[0:1]
    for i in range(1, layer_j + 1):
        cs = cs + p[i:i + 1]
    lb = cs - p[0:1]

    def seg(i):
        return jnp.dot(xb, w_ref[:, i * D_MODEL:(i + 1) * D_MODEL], preferred_element_type=F32)

    q = seg(0)
    q_ref[...] = q * _sigmoid(q) * (HG_DK ** -0.5)
    fp = seg(1)
    sig = 1.0 / (1.0 + jnp.exp(-fp))
    f = lb + (1.0 - lb) * sig
    lf_ref[...] = jnp.log(jnp.maximum(f, F_FLOOR))
    k_ref[...] = (1.0 - lb) * (1.0 - sig)
    v_ref[...] = seg(2)
    g = seg(3)
    g_ref[...] = g * _sigmoid(g)


def _hgrn_in_kernel(layer_j, x_ref, w_ref, lbl_ref, *out_refs):
    _hgrn_in_compute(layer_j, x_ref[...].astype(BF16), w_ref, lbl_ref, *out_refs)


def hgrn_in(x, w_in_bf, lb_logits, layer_j, row0, nrows, tm=256):
    base = row0 // tm
    out = jax.ShapeDtypeStruct((nrows, D_MODEL), F32)
    return pl.pallas_call(
        functools.partial(_hgrn_in_kernel, layer_j),
        grid=(nrows // tm,),
        in_specs=[pl.BlockSpec((tm, D_MODEL), lambda i: (base + i, 0)),
                  pl.BlockSpec((D_MODEL, 4 * D_MODEL), lambda i: (0, 0)),
                  pl.BlockSpec(lb_logits.shape, lambda i: (0, 0))],
        out_specs=[pl.BlockSpec((tm, D_MODEL), lambda i: (i, 0))] * 5,
        out_shape=[out] * 5,
        compiler_params=_cparams(("arbitrary",)),
        name="hgrn_in",
    )(x, w_in_bf, lb_logits)


def _hgrn_prefix(C, lf):
    row = lax.broadcasted_iota(I32, (C, LANES), 0)
    G = lf
    d = 1
    while d < C:
        G = G + jnp.where(row >= d, pltpu.roll(G, d, 0), 0.0)
        d *= 2
    return G


def _hgrn_diag(C, G, q, k, v):
    row8 = lax.broadcasted_iota(I32, (SUBLANES, LANES), 0)
    parts = []
    for gi in range(C // SUBLANES):
        lo = gi * SUBLANES
        Gg = G[lo:lo + SUBLANES]
        qg = q[lo:lo + SUBLANES]
        acc = jnp.zeros((SUBLANES, LANES), F32)
        for s in range(lo + SUBLANES):
            dec = jnp.exp(Gg - G[s:s + 1])
            if s >= lo:
                dec = jnp.where(row8 >= (s - lo), dec, 0.0)
            w = jnp.sum(qg * dec * k[s:s + 1], axis=-1, keepdims=True)
            acc = acc + w * v[s:s + 1]
        parts.append(acc)
    return parts[0] if len(parts) == 1 else jnp.concatenate(parts, axis=0)


def _hgrn_state_update(C, G, k, v, S):
    gend = G[C - 1:C, :]
    kt = k * jnp.exp(gend - G)
    ecol = jnp.transpose(jnp.broadcast_to(jnp.exp(gend), (HG_DK, LANES)))
    upd = lax.dot_general(kt.astype(BF16), v.astype(BF16), (((0,), (0,)), ((), ())),
                          preferred_element_type=F32)
    return ecol * S + upd


def _hgrn_chunk_head(C, q, k, lf, v, S):
    G = _hgrn_prefix(C, lf)
    o = jnp.dot((q * jnp.exp(G)).astype(BF16), S.astype(BF16), preferred_element_type=F32)
    return o + _hgrn_diag(C, G, q, k, v), _hgrn_state_update(C, G, k, v, S)


def _hgrn_rec_kernel(C, q_ref, k_ref, lf_ref, v_ref, g_ref, ng_ref, s0_ref, o_ref, s_ref):
    heads = [slice(h * LANES, (h + 1) * LANES) for h in range(HG_HEADS)]

    def body(i, carry):
        rows = pl.ds(pl.multiple_of(i * C, C), C)
        gs, os_ = [], []
        for h, sl in enumerate(heads):
            G = _hgrn_prefix(C, lf_ref[rows, sl])
            gs.append(G)
            os_.append(jnp.dot((q_ref[rows, sl] * jnp.exp(G)).astype(BF16), s0_ref[i, h].astype(BF16),
                               preferred_element_type=F32))
        for h, sl in enumerate(heads):
            o = os_[h] + _hgrn_diag(C, gs[h], q_ref[rows, sl], k_ref[rows, sl], v_ref[rows, sl])
            ms = jnp.mean(o * o, axis=-1, keepdims=True)
            o_ref[rows, sl] = o * lax.rsqrt(ms + LN_EPS) * ng_ref[:, sl] * g_ref[rows, sl]
        for h, sl in enumerate(heads):
            s_ref[i, h] = _hgrn_state_update(C, gs[h], k_ref[rows, sl], v_ref[rows, sl], s0_ref[i, h])
        return carry

    lax.fori_loop(0, q_ref.shape[0] // C, body, 0)


HG_CHUNK = 64
HG_SUB = 16
HG_SAFE_RANGE = 60.0


def _hgrn_chunk_masks(tt):
    import numpy as np
    t = np.arange(tt)[:, None]
    s = np.arange(tt)[None, :]
    tri = (s <= t) & ((t // HG_CHUNK) == (s // HG_CHUNK))
    return jnp.asarray(tri.astype(np.float32), BF16)


def _hgrn_prompt_kernel(layer_j, x_ref, win_ref, lbl_ref, ng_ref, m_ref, wout_ref, lng_ref, lnb_ref,
                        xo_ref, s_ref,
                        q_ref, k_ref, lf_ref, v_ref, g_ref, o_ref,
                        g_s, qi_s, ks_s, qd_s, kd_s, q1_s, k1_s, q2_s, k2_s, a_s, flag_s):
    tt = x_ref.shape[0]

    @pl.when(pl.program_id(1) == 0)
    def _():
        s_ref[...] = jnp.zeros(s_ref.shape, F32)

    _hgrn_in_compute(layer_j, x_ref[...].astype(BF16), win_ref, lbl_ref, q_ref, k_ref, lf_ref, v_ref, g_ref)

    lf = lf_ref[...] * LOG2E
    hi = lf.astype(BF16)
    lo = (lf - hi.astype(F32)).astype(BF16)
    g_s[...] = (jnp.dot(m_ref[...], hi, preferred_element_type=F32)
                + jnp.dot(m_ref[...], lo, preferred_element_type=F32))

    C = HG_CHUNK
    SB = HG_SUB
    worst = jnp.zeros((SB, D_MODEL), F32)
    for blk in range(tt // SB):
        r0 = blk * SB
        c0 = (r0 // C) * C
        rows = slice(r0, r0 + SB)

        def grow(r):
            return g_s[r - 1:r, :] if r > c0 else jnp.zeros((1, D_MODEL), F32)

        G = g_s[rows, :]
        q = q_ref[rows, :]
        k = k_ref[rows, :]
        qi_s[rows, :] = (q * jnp.exp2(G)).astype(BF16)
        ks_s[rows, :] = (k * jnp.exp2(g_s[c0 + C - 1:c0 + C, :] - G)).astype(BF16)
        e0 = G - grow(r0)
        qd_s[rows, :] = (q * jnp.exp2(e0)).astype(BF16)
        kd_s[rows, :] = (k * jnp.exp2(-e0)).astype(BF16)
        worst = jnp.maximum(worst, -e0)
        e1 = G - grow((r0 // (2 * SB)) * (2 * SB) + SB)
        q1_s[rows, :] = (q * jnp.exp2(jnp.minimum(e1, 0.0))).astype(BF16)
        k1_s[rows, :] = (k * jnp.exp2(jnp.minimum(-e1, 0.0))).astype(BF16)
        e2 = G - grow(c0 + C // 2)
        q2_s[rows, :] = (q * jnp.exp2(jnp.minimum(e2, 0.0))).astype(BF16)
        k2_s[rows, :] = (k * jnp.exp2(jnp.minimum(-e2, 0.0))).astype(BF16)
    flag_s[0] = (jnp.max(worst) <= HG_SAFE_RANGE * LOG2E).astype(I32)

    def finish(rows, sl, o):
        ms = jnp.mean(o * o, axis=-1, keepdims=True)
        o_ref[rows, sl] = o * lax.rsqrt(ms + LN_EPS) * ng_ref[:, sl] * g_ref[rows, sl]

    C = HG_CHUNK
    ti = lax.broadcasted_iota(I32, (C, C), 0)
    si = lax.broadcasted_iota(I32, (C, C), 1)
    causal = si <= ti
    sub_bits = HG_SUB.bit_length() - 1
    m0 = causal & ((ti >> sub_bits) == (si >> sub_bits))
    m1 = causal & ((ti >> (sub_bits + 1)) == (si >> (sub_bits + 1))) & jnp.logical_not(m0)
    nt_dims = (((1,), (1,)), ((), ()))

    @pl.when(flag_s[0] == 1)
    def _():
        def body(c, carry):
            r0 = pl.multiple_of(c * C, C)
            rows = pl.ds(r0, C)
            heads = [slice(h * LANES, (h + 1) * LANES) for h in range(HG_HEADS)]
            for h, sl in enumerate(heads):
                a0 = lax.dot_general(qd_s[rows, sl], kd_s[rows, sl], nt_dims, preferred_element_type=F32)
                a1 = lax.dot_general(q1_s[rows, sl], k1_s[rows, sl], nt_dims, preferred_element_type=F32)
                a2 = lax.dot_general(q2_s[rows, sl], k2_s[rows, sl], nt_dims, preferred_element_type=F32)
                a = jnp.where(m0, a0, jnp.where(m1, a1, jnp.where(causal, a2, 0.0)))
                a_s[h] = a.astype(BF16)
            for h, sl in enumerate(heads):
                o = (jnp.dot(qi_s[rows, sl], s_ref[0, h].astype(BF16), preferred_element_type=F32)
                     + jnp.dot(a_s[h], v_ref[rows, sl].astype(BF16), preferred_element_type=F32))
                finish(rows, sl, o)
            for h, sl in enumerate(heads):
                gend = g_s[pl.ds(pl.multiple_of(r0 + C - SUBLANES, SUBLANES), SUBLANES), sl][SUBLANES - 1:]
                ecol = jnp.transpose(jnp.broadcast_to(jnp.exp2(gend), (HG_DK, LANES)))
                upd = lax.dot_general(ks_s[rows, sl], v_ref[rows, sl].astype(BF16), (((0,), (0,)), ((), ())),
                                      preferred_element_type=F32)
                s_ref[0, h] = ecol * s_ref[0, h] + upd
            return carry

        lax.fori_loop(0, tt // C, body, 0)

    @pl.when(flag_s[0] == 0)
    def _():
        Cs = HG_CHUNK_PROMPT

        def body(i, carry):
            r0 = pl.multiple_of(i * Cs, Cs)
            rows = pl.ds(r0, Cs)
            for h in range(HG_HEADS):
                sl = slice(h * LANES, (h + 1) * LANES)
                o, s_new = _hgrn_chunk_head(Cs, q_ref[rows, sl], k_ref[rows, sl], lf_ref[rows, sl],
                                            v_ref[rows, sl], s_ref[0, h])
                finish(rows, sl, o)
                s_ref[0, h] = s_new
            return carry

        lax.fori_loop(0, tt // Cs, body, 0)

    y = jnp.dot(o_ref[...].astype(BF16), wout_ref[...], preferred_element_type=F32)
    xo_ref[...] = _layer_norm_rows(ALPHA * x_ref[...] + y, lng_ref[...], lnb_ref[...])


def hgrn_prompt_layer(x, w_in_bf, lb_logits, norm_g, w_out_bf, ln_g, ln_b, s_all, layer_j, batch, seq, n,
                      tt=256):
    nt = seq // tt
    row = pl.BlockSpec((tt, D_MODEL), lambda b, t: (b * nt + t, 0))
    vec = pl.BlockSpec((1, D_MODEL), lambda b, t: (0, 0))
    s_spec = pl.BlockSpec((None, 1, HG_HEADS, HG_DK, HG_DV), lambda b, t: (layer_j, b, 0, 0, 0))
    in_specs = [row,
                pl.BlockSpec((D_MODEL, 4 * D_MODEL), lambda b, t: (0, 0)),
                pl.BlockSpec(lb_logits.shape, lambda b, t: (0, 0)),
                vec,
                pl.BlockSpec((tt, tt), lambda b, t: (0, 0)),
                pl.BlockSpec((D_MODEL, D_MODEL), lambda b, t: (0, 0)),
                vec, vec]
    args = [x, w_in_bf, lb_logits, norm_g, _hgrn_chunk_masks(tt), w_out_bf, ln_g, ln_b]
    aliases = {}
    kern = functools.partial(_hgrn_prompt_kernel, layer_j)
    if s_all is not None:
        def kern(*refs):
            _hgrn_prompt_kernel(layer_j, *refs[:8], *refs[9:])
        in_specs.append(pl.BlockSpec(memory_space=pl.ANY))
        args.append(s_all)
        aliases = {8: 1}
    f32_tile = pltpu.VMEM((tt, D_MODEL), F32)
    bf_tile = pltpu.VMEM((tt, D_MODEL), BF16)
    return pl.pallas_call(
        kern,
        grid=(batch, nt),
        in_specs=in_specs,
        out_specs=[row, s_spec],
        out_shape=[jax.ShapeDtypeStruct((n, D_MODEL), F32),
                   jax.ShapeDtypeStruct((2, batch, HG_HEADS, HG_DK, HG_DV), F32)],
        scratch_shapes=[f32_tile] * 7 + [bf_tile] * 8
                       + [pltpu.VMEM((HG_HEADS, HG_CHUNK, HG_CHUNK), BF16), pltpu.SMEM((1,), I32)],
        input_output_aliases=aliases,
        compiler_params=_cparams(("arbitrary", "arbitrary")),
        name="hgrn_prompt_layer",
    )(*args)


def hgrn_rec_sample(q, k, lf, v, g, norm_g, state, s_all, layer_j, nseq, seq, sg=8):
    n = q.shape[0]
    rt = sg * seq
    row = pl.BlockSpec((rt, D_MODEL), lambda i: (i, 0))
    s_in = pl.BlockSpec((None, sg, HG_HEADS, HG_DK, HG_DV), lambda i: (layer_j, i, 0, 0, 0))
    in_specs = [row] * 5 + [pl.BlockSpec((1, D_MODEL), lambda i: (0, 0)), s_in]
    args = [q, k, lf, v, g, norm_g, state]
    aliases = {}
    if s_all is not None:
        in_specs.append(pl.BlockSpec(memory_space=pl.ANY))
        args.append(s_all)
        aliases[7] = 1

    def kern(*refs):
        _hgrn_rec_kernel(seq, *refs[:7], *refs[len(args):])

    return pl.pallas_call(
        kern,
        grid=(nseq // sg,),
        in_specs=in_specs,
        out_specs=[row, s_in],
        out_shape=[jax.ShapeDtypeStruct((n, D_MODEL), F32),
                   jax.ShapeDtypeStruct((2, nseq, HG_HEADS, HG_DK, HG_DV), F32)],
        input_output_aliases=aliases,
        compiler_params=_cparams(("arbitrary",)),
        name="hgrn_rec_sample",
    )(*args)


def _proj_ln_kernel(a_ref, w_ref, x_ref, g_ref, b_ref, o_ref):
    y = jnp.dot(a_ref[...].astype(BF16), w_ref[...], preferred_element_type=F32)
    o_ref[...] = _layer_norm_rows(ALPHA * x_ref[...] + y, g_ref[...], b_ref[...])


def proj_ln(a, w_bf, x, x_row0, g, b, xo, row0, tm=512):
    n = xo.shape[0]
    nrows = a.shape[0]
    base = row0 // tm
    xbase = x_row0 // tm
    row = pl.BlockSpec((tm, D_MODEL), lambda i: (base + i, 0))
    vec = pl.BlockSpec((1, D_MODEL), lambda i: (0, 0))

    def kern(a_ref, w_ref, x_ref, g_ref, b_ref, _, o_ref):
        _proj_ln_kernel(a_ref, w_ref, x_ref, g_ref, b_ref, o_ref)

    return pl.pallas_call(
        kern,
        grid=(nrows // tm,),
        in_specs=[pl.BlockSpec((tm, D_MODEL), lambda i: (i, 0)),
                  pl.BlockSpec((D_MODEL, D_MODEL), lambda i: (0, 0)),
                  pl.BlockSpec((tm, D_MODEL), lambda i: (xbase + i, 0)), vec, vec,
                  pl.BlockSpec(memory_space=pl.ANY)],
        out_specs=row,
        out_shape=jax.ShapeDtypeStruct((n, D_MODEL), F32),
        input_output_aliases={5: 0},
        compiler_params=_cparams(("arbitrary",)),
        name="proj_ln",
    )(a, w_bf, x, g, b, xo)


def _ffn_kernel(x_ref, wg_ref, wu_ref, wd_ref, g_ref, b_ref, o_ref):
    xb = x_ref[...].astype(BF16)
    gg = jnp.dot(xb, wg_ref[...], preferred_element_type=F32)
    uu = jnp.dot(xb, wu_ref[...], preferred_element_type=F32)
    h = (gg * _sigmoid(gg) * uu).astype(BF16)
    y = jnp.dot(h, wd_ref[...], preferred_element_type=F32)
    o_ref[...] = _layer_norm_rows(ALPHA * x_ref[...] + y, g_ref[...], b_ref[...])


def ffn_ln(x, w_gu_bf, w_down_bf, g, b, tm=512):
    n = x.shape[0]
    row = pl.BlockSpec((tm, D_MODEL), lambda i: (i, 0))
    vec = pl.BlockSpec((1, D_MODEL), lambda i: (0, 0))
    once = pl.Buffered(1)
    return pl.pallas_call(
        _ffn_kernel,
        grid=(n // tm,),
        in_specs=[row,
                  pl.BlockSpec((D_MODEL, D_FF), lambda i: (0, 0), pipeline_mode=once),
                  pl.BlockSpec((D_MODEL, D_FF), lambda i: (0, 1), pipeline_mode=once),
                  pl.BlockSpec((D_FF, D_MODEL), lambda i: (0, 0), pipeline_mode=once),
                  vec, vec],
        out_specs=row,
        out_shape=jax.ShapeDtypeStruct((n, D_MODEL), F32),
        compiler_params=_cparams(("arbitrary",)),
        name="ffn_ln",
    )(x, w_gu_bf, w_gu_bf, w_down_bf, g, b)


def _lru_in_kernel(x_ref, w_ref, xb_ref, gate_ref):
    xb = x_ref[...].astype(BF16)
    xb_ref[...] = jnp.dot(xb, w_ref[:, :D_MODEL], preferred_element_type=F32)
    y = jnp.dot(xb, w_ref[:, D_MODEL:], preferred_element_type=F32)
    gate_ref[...] = _gelu_tanh(y)


def lru_in(x, w_in_bf, row0, nrows, tm=512):
    base = row0 // tm
    row = pl.BlockSpec((tm, D_MODEL), lambda i: (i, 0))
    out = jax.ShapeDtypeStruct((nrows, D_MODEL), F32)
    return pl.pallas_call(
        _lru_in_kernel,
        grid=(nrows // tm,),
        in_specs=[pl.BlockSpec((tm, D_MODEL), lambda i: (base + i, 0)),
                  pl.BlockSpec((D_MODEL, 2 * D_MODEL), lambda i: (0, 0))],
        out_specs=[row, row],
        out_shape=[out, out],
        compiler_params=_cparams(("arbitrary",)),
        name="lru_in",
    )(x, w_in_bf)


def _lru_rec_kernel(L, carry, *refs):
    if carry:
        (x_ref, win_ref, cw_ref, cb_ref, wr_ref, br_ref, wi_ref, bi_ref, lam_ref, wout_ref, lng_ref, lnb_ref,
         cgu_ref, cdn_ref, xo_ref, hn_ref, cn_ref, cgu_o, cdn_o, xp_ref, hs_ref, hc_ref, gate_ref, hg_ref) = refs
        R = x_ref.shape[0]
        cgu_o[...] = cgu_ref[...].astype(BF16)
        cdn_o[...] = cdn_ref[...].astype(BF16)
    else:
        (xb_ref, gate_ref, cc_ref, h0_ref, cw_ref, cb_ref, wr_ref, br_ref, wi_ref, bi_ref, lam_ref,
         hg_ref, hn_ref, cn_ref, xp_ref, hs_ref) = refs
        R = xb_ref.shape[0]
    S = R // L
    PADR = SUBLANES
    t = pl.program_id(1) if carry else 0

    if carry:
        @pl.when(t == 0)
        def _():
            xp_ref[:, 0:PADR, :] = jnp.zeros((S, PADR, D_MODEL), F32)
            hc_ref[...] = jnp.zeros(hc_ref.shape, F32)

        @pl.when(t > 0)
        def _():
            xp_ref[:, 0:PADR, :] = xp_ref[:, L:L + PADR, :]

        xin = x_ref[...].astype(BF16)
        xp_ref[:, PADR:PADR + L, :] = jnp.dot(xin, win_ref[:, :D_MODEL],
                                              preferred_element_type=F32).reshape(S, L, D_MODEL)
        gate_ref[...] = _gelu_tanh(jnp.dot(xin, win_ref[:, D_MODEL:], preferred_element_type=F32))
    else:
        xp_ref[:, 0:PADR, :] = cc_ref[...].reshape(S, PADR, D_MODEL)
        xp_ref[:, PADR:PADR + L, :] = xb_ref[...].reshape(S, L, D_MODEL)

    xc = cb_ref[...].reshape(1, 1, D_MODEL) + xp_ref[:, PADR:PADR + L, :] * cw_ref[CONV_W - 1:CONV_W, :].reshape(1, 1, D_MODEL)
    for j in range(CONV_W - 1):
        off = PADR - (CONV_W - 1) + j
        xc = xc + xp_ref[:, off:off + L, :] * cw_ref[j:j + 1, :].reshape(1, 1, D_MODEL)
    xc = xc.reshape(R, D_MODEL)
    cn_ref[...] = xp_ref[:, L + PADR - (CONV_W - 1):L + PADR, :].reshape(cn_ref.shape)

    assert L & (L - 1) == 0
    pos = jnp.bitwise_and(lax.broadcasted_iota(I32, (R, LRU_BW), 0), L - 1)
    for nb in range(LRU_BLOCKS):
        sl = slice(nb * LRU_BW, (nb + 1) * LRU_BW)
        xcb = xc[:, sl]
        xh = xcb.astype(BF16)
        r = _sigmoid(jnp.dot(xh, wr_ref[nb], preferred_element_type=F32) + br_ref[:, sl])
        ig = _sigmoid(jnp.dot(xh, wi_ref[nb], preferred_element_type=F32) + bi_ref[:, sl])
        z = -lam_ref[:, sl]
        softplus = jnp.maximum(z, 0.0) + jnp.log1p(jnp.exp(-jnp.abs(z)))
        log_a = (-RG_C) * r * softplus
        a = jnp.exp(log_a)
        u = jnp.sqrt(1.0 - a * a) * (ig * xcb)
        if carry:
            ng = R // SUBLANES
            u3 = u.reshape(ng, SUBLANES, LRU_BW)
            a3 = a.reshape(ng, SUBLANES, LRU_BW)
            pos8 = lax.broadcasted_iota(I32, (ng, SUBLANES, LRU_BW), 1)
            d = 1
            while d < SUBLANES:
                m = pos8 >= d
                u3 = jnp.where(m, a3 * pltpu.roll(u3, d, 1) + u3, u3)
                a3 = jnp.where(m, a3 * pltpu.roll(a3, d, 1), a3)
                d *= 2
            hprev = hc_ref[:, sl]
            for gi in range(ng):
                hgrp = u3[gi] + a3[gi] * hprev
                hs_ref[0, gi * SUBLANES:(gi + 1) * SUBLANES, sl] = hgrp
                hprev = hgrp[SUBLANES - 1:SUBLANES]
            hg_ref[:, sl] = hs_ref[0, :, sl] * gate_ref[:, sl]
        else:
            u = u + a * h0_ref[:, sl]
            d = 1
            while d < L:
                m = pos >= d
                u = jnp.where(m, a * pltpu.roll(u, d, 0) + u, u)
                a = jnp.where(m, a * pltpu.roll(a, d, 0), a)
                d *= 2
            hg_ref[:, sl] = u * gate_ref[:, sl]
            hs_ref[:, :, sl] = u.reshape(S, L, LRU_BW)

    hlast = hs_ref[:, L - 1:L, :]
    hn_ref[...] = hlast.reshape(hn_ref.shape)
    if carry:
        hc_ref[...] = hlast.reshape(1, D_MODEL)
        y = jnp.dot(hg_ref[...].astype(BF16), wout_ref[...], preferred_element_type=F32)
        xo_ref[...] = _layer_norm_rows(ALPHA * x_ref[...] + y, lng_ref[...], lnb_ref[...])


def _lru_weight_specs(nidx):
    def c2(*_):
        return (0, 0)

    def c3(*_):
        return (0, 0, 0)
    return [pl.BlockSpec((CONV_W, D_MODEL), c2), pl.BlockSpec((1, D_MODEL), c2),
            pl.BlockSpec((LRU_BLOCKS, LRU_BW, LRU_BW), c3), pl.BlockSpec((1, D_MODEL), c2),
            pl.BlockSpec((LRU_BLOCKS, LRU_BW, LRU_BW), c3), pl.BlockSpec((1, D_MODEL), c2),
            pl.BlockSpec((1, D_MODEL), c2)]


def lru_prompt_layer(x, w_in_bf, wts, w_out_bf, ln_g, ln_b, moe_gu, moe_down, layer_j, batch, seq, tt=256):
    n = x.shape[0]
    nt = seq // tt
    steps = batch * nt
    gu2 = moe_gu.reshape(moe_gu.shape[0], N_EXPERTS * D_MODEL, 2 * D_FF_EXPERT)
    dn2 = moe_down.reshape(moe_down.shape[0], N_EXPERTS * D_FF_EXPERT, D_MODEL)
    gu_rows = gu2.shape[1] // steps
    dn_rows = dn2.shape[1] // steps
    assert gu_rows * steps == gu2.shape[1] and dn_rows * steps == dn2.shape[1]
    row = pl.BlockSpec((tt, D_MODEL), lambda b, t: (b * nt + t, 0))
    vec = pl.BlockSpec((1, D_MODEL), lambda b, t: (0, 0))
    xo, hn, cn, gu_bf, dn_bf = pl.pallas_call(
        functools.partial(_lru_rec_kernel, tt, True),
        grid=(batch, nt),
        in_specs=[row, pl.BlockSpec((D_MODEL, 2 * D_MODEL), lambda b, t: (0, 0))] + _lru_weight_specs(2)
                 + [pl.BlockSpec((D_MODEL, D_MODEL), lambda b, t: (0, 0)), vec, vec,
                    pl.BlockSpec((None, gu_rows, 2 * D_FF_EXPERT), lambda b, t: (layer_j, b * nt + t, 0)),
                    pl.BlockSpec((None, dn_rows, D_MODEL), lambda b, t: (layer_j, b * nt + t, 0))],
        out_specs=[row,
                   pl.BlockSpec((1, 1, D_MODEL), lambda b, t: (b, 0, 0)),
                   pl.BlockSpec((1, CONV_W - 1, D_MODEL), lambda b, t: (b, 0, 0)),
                   pl.BlockSpec((gu_rows, 2 * D_FF_EXPERT), lambda b, t: (b * nt + t, 0)),
                   pl.BlockSpec((dn_rows, D_MODEL), lambda b, t: (b * nt + t, 0))],
        out_shape=[jax.ShapeDtypeStruct((n, D_MODEL), F32),
                   jax.ShapeDtypeStruct((batch, 1, D_MODEL), F32),
                   jax.ShapeDtypeStruct((batch, CONV_W - 1, D_MODEL), F32),
                   jax.ShapeDtypeStruct(gu2.shape[1:], BF16),
                   jax.ShapeDtypeStruct(dn2.shape[1:], BF16)],
        scratch_shapes=[pltpu.VMEM((1, tt + SUBLANES, D_MODEL), F32),
                        pltpu.VMEM((1, tt, D_MODEL), F32),
                        pltpu.VMEM((1, D_MODEL), F32),
                        pltpu.VMEM((tt, D_MODEL), F32),
                        pltpu.VMEM((tt, D_MODEL), F32)],
        compiler_params=_cparams(("arbitrary", "arbitrary")),
        name="lru_prompt_layer",
    )(x, w_in_bf, *wts, w_out_bf, ln_g, ln_b, gu2, dn2)
    return (xo, hn, cn, gu_bf.reshape(N_EXPERTS, D_MODEL, 2 * D_FF_EXPERT),
            dn_bf.reshape(N_EXPERTS, D_FF_EXPERT, D_MODEL))


def lru_rec_sample(xb, gate, cc_rows, h0_rows, wts, nseq, seq, sg=32):
    n = xb.shape[0]
    rt = sg * seq
    row = pl.BlockSpec((rt, D_MODEL), lambda i: (i, 0))
    return pl.pallas_call(
        functools.partial(_lru_rec_kernel, seq, False),
        grid=(nseq // sg,),
        in_specs=[row, row, pl.BlockSpec((sg * SUBLANES, D_MODEL), lambda i: (i, 0)), row]
                 + _lru_weight_specs(1),
        out_specs=[row,
                   pl.BlockSpec((sg, 1, D_MODEL), lambda i: (i, 0, 0)),
                   pl.BlockSpec((sg, CONV_W - 1, D_MODEL), lambda i: (i, 0, 0))],
        out_shape=[jax.ShapeDtypeStruct((n, D_MODEL), F32),
                   jax.ShapeDtypeStruct((nseq, 1, D_MODEL), F32),
                   jax.ShapeDtypeStruct((nseq, CONV_W - 1, D_MODEL), F32)],
        scratch_shapes=[pltpu.VMEM((sg, seq + SUBLANES, D_MODEL), F32),
                        pltpu.VMEM((sg, seq, D_MODEL), F32)],
        compiler_params=_cparams(("arbitrary",)),
        name="lru_rec_sample",
    )(xb, gate, cc_rows, h0_rows, *wts)


def _router_kernel(x_ref, w_ref, mi_ref, mf_ref, cnt_ref, run_ref):
    i = pl.program_id(0)

    @pl.when(i == 0)
    def _():
        run_ref[...] = jnp.zeros(run_ref.shape, F32)

    tm = x_ref.shape[0]
    x = x_ref[...]
    w = w_ref[...]
    xh = x.astype(BF16)
    xl = (x - xh.astype(F32)).astype(BF16)
    wh = w.astype(BF16)
    wl = (w - wh.astype(F32)).astype(BF16)
    hh_hl = jnp.dot(xh, jnp.concatenate([wh, wl], axis=1), preferred_element_type=F32)
    logits = hh_hl[:, :LANES] + hh_hl[:, LANES:] + jnp.dot(xl, wh, preferred_element_type=F32)
    lane_i = lax.broadcasted_iota(I32, (tm, LANES), 1)
    lane = lane_i.astype(F32)
    neg = jnp.float32(-jnp.inf)
    logits = jnp.where(lane_i < N_EXPERTS, logits, neg)
    m1 = jnp.max(logits, axis=-1, keepdims=True)
    i1 = jnp.min(jnp.where(logits == m1, lane, float(LANES)), axis=-1, keepdims=True)
    l2 = jnp.where(lane == i1, neg, logits)
    m2 = jnp.max(l2, axis=-1, keepdims=True)
    i2 = jnp.min(jnp.where(l2 == m2, lane, float(LANES)), axis=-1, keepdims=True)
    e2 = jnp.exp(m2 - m1)
    p1 = 1.0 / (1.0 + e2)
    p2 = e2 * p1

    hit1 = lane == i1
    hit2 = lane == i2
    onehot = jnp.where(hit1 | hit2, 1.0, 0.0)
    r_i = lax.broadcasted_iota(I32, (tm, tm), 0)
    c_i = lax.broadcasted_iota(I32, (tm, tm), 1)
    tri = jnp.where(c_i < r_i, 1.0, 0.0).astype(BF16)
    rank = jnp.dot(tri, onehot.astype(BF16), preferred_element_type=F32) + run_ref[0:1, :]
    r1 = jnp.sum(jnp.where(hit1, rank, 0.0), axis=-1, keepdims=True)
    r2 = jnp.sum(jnp.where(hit2, rank, 0.0), axis=-1, keepdims=True)
    total = run_ref[0:1, :] + jnp.sum(onehot, axis=0, keepdims=True)
    run_ref[...] = jnp.broadcast_to(total, run_ref.shape)
    cnt_ref[...] = jnp.broadcast_to(total, cnt_ref.shape).astype(I32)

    mi = jnp.where(lane_i == 0, i1, jnp.where(lane_i == 1, i2, 0.0))
    mi = jnp.where(lane_i == 2, r1, jnp.where(lane_i == 3, r2, mi))
    mi_ref[...] = mi.astype(I32)
    mf_ref[...] = jnp.where(lane_i == 0, p1, jnp.where(lane_i == 1, p2, 0.0))


def moe_router(x, w_router_pad, tm=512):
    n = x.shape[0]
    row = pl.BlockSpec((tm, LANES), lambda i: (i, 0))
    return pl.pallas_call(
        _router_kernel,
        grid=(n // tm,),
        in_specs=[pl.BlockSpec((tm, D_MODEL), lambda i: (i, 0)),
                  pl.BlockSpec((D_MODEL, LANES), lambda i: (0, 0))],
        out_specs=[row, row, pl.BlockSpec((SUBLANES, LANES), lambda i: (0, 0))],
        out_shape=[jax.ShapeDtypeStruct((n, LANES), I32),
                   jax.ShapeDtypeStruct((n, LANES), F32),
                   jax.ShapeDtypeStruct((SUBLANES, LANES), I32)],
        scratch_shapes=[pltpu.VMEM((SUBLANES, LANES), F32)],
        compiler_params=_cparams(("arbitrary",)),
        name="moe_router",
    )(x, w_router_pad)


ZERO_ROWS = 256


def _dispatch_kernel(dest_ref, ztile_ref, x_ref, xs_hbm, zero_ref, sem):
    i = pl.program_id(0)
    tm = x_ref.shape[0]

    @pl.when(i == 0)
    def _():
        zero_ref[...] = jnp.zeros(zero_ref.shape, F32)
        copies = [pltpu.make_async_copy(
                      zero_ref,
                      xs_hbm.at[pl.ds(pl.multiple_of(ztile_ref[e] + c * ZERO_ROWS, ZERO_ROWS), ZERO_ROWS)], sem)
                  for e in range(N_EXPERTS) for c in range(MOE_TILE // ZERO_ROWS)]
        for cp in copies:
            cp.start()
        for cp in copies:
            cp.wait()

    def group(jj, c):
        r8 = pl.multiple_of(jj * SUBLANES, SUBLANES)
        grp = x_ref.at[pl.ds(r8, SUBLANES)]
        t0 = 2 * (i * tm + r8)
        for u in range(SUBLANES):
            src = grp.at[pl.ds(u, 1)]
            pltpu.make_async_copy(src, xs_hbm.at[pl.ds(dest_ref[t0 + 2 * u], 1)], sem).start()
            pltpu.make_async_copy(src, xs_hbm.at[pl.ds(dest_ref[t0 + 2 * u + 1], 1)], sem).start()
        return c

    lax.fori_loop(0, tm // SUBLANES, group, 0)
    for _ in range(2):
        pltpu.make_async_copy(x_ref, xs_hbm.at[pl.ds(0, tm)], sem).wait()


def moe_dispatch(dest, ztile, x, n_rows, tm=512):
    n = x.shape[0]
    return pl.pallas_call(
        _dispatch_kernel,
        grid_spec=pltpu.PrefetchScalarGridSpec(
            num_scalar_prefetch=2, grid=(n // tm,),
            in_specs=[pl.BlockSpec((tm, D_MODEL), lambda i, d, z: (i, 0))],
            out_specs=pl.BlockSpec(memory_space=pl.ANY),
            scratch_shapes=[pltpu.VMEM((ZERO_ROWS, D_MODEL), F32), pltpu.SemaphoreType.DMA(())]),
        out_shape=jax.ShapeDtypeStruct((n_rows, D_MODEL), F32),
        compiler_params=pltpu.CompilerParams(dimension_semantics=("arbitrary",),
                                             has_side_effects=True),
        name="moe_dispatch",
    )(dest, ztile, x)


def _expert_kernel(te_ref, nu_ref, xs_ref, wg_ref, wu_ref, wd_ref, ys_ref, acc_ref):
    i = pl.program_id(0)
    f = pl.program_id(1)
    used = i < nu_ref[0]

    @pl.when(used)
    def _():
        xb = xs_ref[...].astype(BF16)
        gg = jnp.dot(xb, wg_ref[...], preferred_element_type=F32)
        uu = jnp.dot(xb, wu_ref[...], preferred_element_type=F32)
        h = (gg * _sigmoid(gg) * uu).astype(BF16)
        part = jnp.dot(h, wd_ref[...], preferred_element_type=F32)

        @pl.when(f == 0)
        def _():
            acc_ref[...] = part

        @pl.when(f > 0)
        def _():
            acc_ref[...] += part

    @pl.when(f == pl.num_programs(1) - 1)
    def _():
        @pl.when(used)
        def _():
            ys_ref[...] = acc_ref[...]

        @pl.when(jnp.logical_not(used))
        def _():
            ys_ref[...] = jnp.zeros(ys_ref.shape, F32)


def moe_experts(tile_expert, n_used, xs, w_gu_bf, w_down_bf, tf=1792):
    n_rows = xs.shape[0]
    tm = MOE_TILE
    nf = D_FF_EXPERT // tf

    def xrow(i, f, te, nu):
        return (jnp.minimum(i, nu[0] - 1), 0)

    def wg(i, f, te, nu):
        return (te[i], 0, jnp.where(i < nu[0], f, nf - 1))

    def wu(i, f, te, nu):
        return (te[i], 0, nf + jnp.where(i < nu[0], f, nf - 1))

    def wd(i, f, te, nu):
        return (te[i], jnp.where(i < nu[0], f, nf - 1), 0)

    return pl.pallas_call(
        _expert_kernel,
        grid_spec=pltpu.PrefetchScalarGridSpec(
            num_scalar_prefetch=2, grid=(n_rows // tm, nf),
            in_specs=[pl.BlockSpec((tm, D_MODEL), xrow),
                      pl.BlockSpec((None, D_MODEL, tf), wg),
                      pl.BlockSpec((None, D_MODEL, tf), wu),
                      pl.BlockSpec((None, tf, D_MODEL), wd)],
            out_specs=pl.BlockSpec((tm, D_MODEL), lambda i, f, te, nu: (i, 0)),
            scratch_shapes=[pltpu.VMEM((tm, D_MODEL), F32)]),
        out_shape=jax.ShapeDtypeStruct((n_rows, D_MODEL), F32),
        compiler_params=_cparams(("arbitrary", "arbitrary")),
        name="moe_experts",
    )(tile_expert, n_used, xs, w_gu_bf, w_gu_bf, w_down_bf)


def _combine_kernel(nsplit, dest_ref, ys_hbm, mf_ref, x_ref, g_ref, b_ref, *rest):
    if nsplit is None:
        o_ref, buf_ref, sem = rest
    else:
        oa_ref, ob_ref, buf_ref, sem = rest
    i = pl.program_id(0)
    n_steps = pl.num_programs(0)
    tm = x_ref.shape[0]

    def gather(step, slot):
        def group(jj, c):
            r8 = pl.multiple_of(jj * SUBLANES, SUBLANES)
            g0 = buf_ref.at[slot, 0, pl.ds(r8, SUBLANES)]
            g1 = buf_ref.at[slot, 1, pl.ds(r8, SUBLANES)]
            t0 = 2 * (step * tm + r8)
            for u in range(SUBLANES):
                pltpu.make_async_copy(ys_hbm.at[pl.ds(dest_ref[t0 + 2 * u], 1)], g0.at[pl.ds(u, 1)],
                                      sem.at[slot]).start()
                pltpu.make_async_copy(ys_hbm.at[pl.ds(dest_ref[t0 + 2 * u + 1], 1)], g1.at[pl.ds(u, 1)],
                                      sem.at[slot]).start()
            return c

        lax.fori_loop(0, tm // SUBLANES, group, 0)

    slot = lax.rem(i, 2)

    @pl.when(i == 0)
    def _():
        gather(i, slot)

    @pl.when(i + 1 < n_steps)
    def _():
        gather(i + 1, 1 - slot)

    for s in range(2):
        pltpu.make_async_copy(ys_hbm.at[pl.ds(0, tm)], buf_ref.at[slot, s], sem.at[slot]).wait()
    mf = mf_ref[...]
    y = buf_ref[slot, 0] * mf[:, 0:1] + buf_ref[slot, 1] * mf[:, 1:2]
    res = _layer_norm_rows(ALPHA * x_ref[...] + y, g_ref[...], b_ref[...])
    if nsplit is None:
        o_ref[...] = res
    else:
        @pl.when(i < nsplit)
        def _():
            oa_ref[...] = res

        @pl.when(i >= nsplit)
        def _():
            ob_ref[...] = res


def moe_combine_ln(dest, ys, mf, x, g, b, split_rows=None, tm=256):
    n = x.shape[0]
    row = pl.BlockSpec((tm, D_MODEL), lambda i, d: (i, 0))
    vec = pl.BlockSpec((1, D_MODEL), lambda i, d: (0, 0))
    if split_rows is None:
        nsplit = None
        out_specs = row
        out_shape = jax.ShapeDtypeStruct((n, D_MODEL), F32)
    else:
        nsplit = split_rows // tm
        out_specs = [pl.BlockSpec((tm, D_MODEL), lambda i, d: (jnp.minimum(i, nsplit - 1), 0)),
                     pl.BlockSpec((tm, D_MODEL), lambda i, d: (jnp.maximum(i - nsplit, 0), 0))]
        out_shape = [jax.ShapeDtypeStruct((split_rows, D_MODEL), F32),
                     jax.ShapeDtypeStruct((n - split_rows, D_MODEL), F32)]
    return pl.pallas_call(
        functools.partial(_combine_kernel, nsplit),
        grid_spec=pltpu.PrefetchScalarGridSpec(
            num_scalar_prefetch=1, grid=(n // tm,),
            in_specs=[pl.BlockSpec(memory_space=pl.ANY),
                      pl.BlockSpec((tm, LANES), lambda i, d: (i, 0)),
                      row, vec, vec],
            out_specs=out_specs,
            scratch_shapes=[pltpu.VMEM((2, 2, tm, D_MODEL), F32), pltpu.SemaphoreType.DMA((2,))]),
        out_shape=out_shape,
        compiler_params=_cparams(("arbitrary",)),
        name="moe_combine_ln",
    )(dest, ys, mf, x, g, b)


def moe_ffn_ln(x, w_router_pad, w_gu, w_down, g, b, split_rows=None):
    n = x.shape[0]
    tm = MOE_TILE
    mi, mf, cnt = moe_router(x, w_router_pad)
    counts = cnt[0, :N_EXPERTS]
    padded = (counts + tm - 1) // tm * tm
    ends = jnp.cumsum(padded)
    starts = ends - padded
    ids = mi[:, 0:2]
    ranks = mi[:, 2:4]
    dest = (starts[ids] + ranks).reshape(-1).astype(I32)
    n_tiles = (2 * n) // tm + N_EXPERTS
    tile_start = jnp.arange(n_tiles, dtype=I32) * tm
    tile_expert = jnp.minimum(jnp.sum(tile_start[:, None] >= ends[None, :], axis=1), N_EXPERTS - 1).astype(I32)
    n_used = (ends[-1] // tm).astype(I32).reshape(1)
    tile_expert = jnp.where(tile_start < ends[-1], tile_expert, tile_expert[jnp.maximum(n_used[0] - 1, 0)])
    ztile = jnp.maximum(ends - tm, 0).astype(I32)
    xs = moe_dispatch(dest, ztile, x, n_tiles * tm)
    ys = moe_experts(tile_expert, n_used, xs, w_gu, w_down)
    return moe_combine_ln(dest, ys, mf, x, g, b, split_rows)


def kernel(x_prompt, x_sample, state_hgrn, state_lru_h, state_lru_conv, ln_mix_g, ln_mix_b, ln_ffn_g, ln_ffn_b, w_hgrn_in, hgrn_lb_logits, hgrn_norm_g, w_hgrn_out, w_lru_in, lru_conv_w, lru_conv_b, w_lru_rgate, b_lru_rgate, w_lru_igate, b_lru_igate, lru_lambda, w_lru_out, w_ffn_gu, w_ffn_down, w_router, w_moe_gu, w_moe_down):
    bp, tp, _ = x_prompt.shape
    bs, ts, _ = x_sample.shape
    n_p = bp * tp
    n = n_p + bs * ts
    x = x_prompt.reshape(n_p, D_MODEL)
    x_smp, smp_row0 = x_sample.reshape(bs * ts, D_MODEL), 0

    def vec(a, i):
        return a[i].reshape(1, D_MODEL)

    n_pairs = DEPTH // 2
    w_hgrn_in_bf = [w_hgrn_in[i].astype(BF16) for i in range(n_pairs)]
    w_hgrn_out_bf = [w_hgrn_out[i].astype(BF16) for i in range(n_pairs)]
    w_lru_in_bf = [w_lru_in[i].astype(BF16) for i in range(n_pairs)]
    w_lru_out_bf = [w_lru_out[i].astype(BF16) for i in range(n_pairs)]
    w_r_bf = [w_lru_rgate[i].astype(BF16) for i in range(n_pairs)]
    w_i_bf = [w_lru_igate[i].astype(BF16) for i in range(n_pairs)]
    w_ffn_gu_bf = [w_ffn_gu[i].astype(BF16) for i in range(n_pairs)]
    w_ffn_down_bf = [w_ffn_down[i].astype(BF16) for i in range(n_pairs)]
    w_router_pad = jnp.pad(w_router, ((0, 0), (0, 0), (0, LANES - N_EXPERTS)))

    hg_p = hg_s = None
    h_p, h_s, c_p, c_s = [], [], [], []
    for layer in range(DEPTH):
        j = layer // 2
        lng, lnb = vec(ln_mix_g, layer), vec(ln_mix_b, layer)
        if layer % 2 == 0:
            ng = vec(hgrn_norm_g, j)
            xo, hg_p = hgrn_prompt_layer(x, w_hgrn_in_bf[j], hgrn_lb_logits, ng, w_hgrn_out_bf[j],
                                         lng, lnb, hg_p, j, bp, tp, n)
            q, k, lf, v, g = hgrn_in(x_smp, w_hgrn_in_bf[j], hgrn_lb_logits, j, smp_row0, bs * ts)
            o, hg_s = hgrn_rec_sample(q, k, lf, v, g, ng, state_hgrn, hg_s, j, bs, ts)
            x = proj_ln(o, w_hgrn_out_bf[j], x_smp, smp_row0, lng, lnb, xo, n_p)
            x = ffn_ln(x, w_ffn_gu_bf[j], w_ffn_down_bf[j], vec(ln_ffn_g, layer), vec(ln_ffn_b, layer))
            x_smp, smp_row0 = x, n_p
        else:
            wts = (lru_conv_w[j], vec(lru_conv_b, j), w_r_bf[j], vec(b_lru_rgate, j),
                   w_i_bf[j], vec(b_lru_igate, j), vec(lru_lambda, j))
            xo, hp, cp, moe_gu_bf, moe_down_bf = lru_prompt_layer(
                x, w_lru_in_bf[j], wts, w_lru_out_bf[j], lng, lnb, w_moe_gu, w_moe_down, j, bp, tp)
            xb, gate = lru_in(x_smp, w_lru_in_bf[j], smp_row0, bs * ts)
            cc_rows = jnp.pad(state_lru_conv[j], ((0, 0), (SUBLANES - (CONV_W - 1), 0), (0, 0)))
            h0_rows = jnp.pad(state_lru_h[j][:, None, :], ((0, 0), (0, ts - 1), (0, 0)))
            hgate, hs, cs = lru_rec_sample(xb, gate, cc_rows.reshape(bs * SUBLANES, D_MODEL),
                                           h0_rows.reshape(bs * ts, D_MODEL), wts, bs, ts)
            h_p.append(hp.reshape(bp, D_MODEL))
            h_s.append(hs.reshape(bs, D_MODEL))
            c_p.append(cp)
            c_s.append(cs)
            x = proj_ln(hgate, w_lru_out_bf[j], x_smp, smp_row0, lng, lnb, xo, n_p)
            x = moe_ffn_ln(x, w_router_pad[j], moe_gu_bf, moe_down_bf,
                           vec(ln_ffn_g, layer), vec(ln_ffn_b, layer),
                           split_rows=n_p if layer == DEPTH - 1 else None)
            x_smp, smp_row0 = x, n_p

    y_prompt = x[0].reshape(bp, tp, D_MODEL)
    y_sample = x[1].reshape(bs, ts, D_MODEL)
    return (y_prompt, y_sample, hg_p, hg_s, jnp.stack(h_p), jnp.stack(h_s), jnp.stack(c_p), jnp.stack(c_s))
```

```python
import functools
import math

import jax
import jax.numpy as jnp
from jax import lax
from jax.experimental import pallas as pl
from jax.experimental.pallas import tpu as pltpu

F32 = jnp.float32
BF16 = jnp.bfloat16
I32 = jnp.int32

D_MODEL = 1024
DEPTH = 4
HG_DK = 128
HG_HEADS = 8
HG_DV = 128
CONV_W = 4
RG_C = 8.0
LRU_BLOCKS = 4
LRU_BW = 256
D_FF = 2816
N_EXPERTS = 8
D_FF_EXPERT = 3584
ALPHA = (2 * DEPTH) ** 0.25
LN_EPS = 1e-5
F_FLOOR = 1e-20
LOG2E = math.log2(math.e)

LANES = 128
SUBLANES = 8
VMEM_LIMIT = 56 * 1024 * 1024

HG_CHUNK_PROMPT = 16
MOE_TILE = 512


def _cparams(sem):
    return pltpu.CompilerParams(dimension_semantics=sem, vmem_limit_bytes=VMEM_LIMIT)


def _layer_block(w, layer, block_shape, index_map, **kw):
    if w.ndim == len(block_shape):
        return pl.BlockSpec(block_shape, index_map, **kw)
    return pl.BlockSpec((None,) + tuple(block_shape), lambda *a: (layer,) + tuple(index_map(*a)), **kw)


def _sigmoid(x):
    return 0.5 * jnp.tanh(0.5 * x) + 0.5


def _gelu_tanh(y):
    c = math.sqrt(2.0 / math.pi)
    return 0.5 * y * (1.0 + jnp.tanh(c * (y + 0.044715 * (y * y * y))))


def _layer_norm_rows(z, g, b):
    mu = jnp.mean(z, axis=-1, keepdims=True)
    zc = z - mu
    var = jnp.mean(zc * zc, axis=-1, keepdims=True)
    return zc * lax.rsqrt(var + LN_EPS) * g + b


def _hgrn_in_compute(layer_j, xb, w_ref, lbl_ref, q_ref, k_ref, lf_ref, v_ref, g_ref):
    l = lbl_ref[...]
    e = jnp.exp(l - jnp.max(l, axis=0, keepdims=True))
    p = e / jnp.sum(e, axis=0, keepdims=True)
    cs = p[0:1]
    for i in range(1, layer_j + 1):
        cs = cs + p[i:i + 1]
    lb = cs - p[0:1]

    def seg(i):
        return jnp.dot(xb, w_ref[:, i * D_MODEL:(i + 1) * D_MODEL], preferred_element_type=F32)

    q = seg(0)
    q_ref[...] = q * _sigmoid(q) * (HG_DK ** -0.5)
    fp = seg(1)
    sig = 1.0 / (1.0 + jnp.exp(-fp))
    f = lb + (1.0 - lb) * sig
    lf_ref[...] = jnp.log(jnp.maximum(f, F_FLOOR))
    k_ref[...] = (1.0 - lb) * (1.0 - sig)
    v_ref[...] = seg(2)
    g = seg(3)
    g_ref[...] = g * _sigmoid(g)


def _hgrn_in_kernel(layer_j, x_ref, w_ref, lbl_ref, *out_refs):
    _hgrn_in_compute(layer_j, x_ref[...].astype(BF16), w_ref, lbl_ref, *out_refs)


def hgrn_in(x, w_in_bf, lb_logits, layer_j, row0, nrows, tm=256):
    base = row0 // tm
    out = jax.ShapeDtypeStruct((nrows, D_MODEL), F32)
    return pl.pallas_call(
        functools.partial(_hgrn_in_kernel, layer_j),
        grid=(nrows // tm,),
        in_specs=[pl.BlockSpec((tm, D_MODEL), lambda i: (base + i, 0)),
                  _layer_block(w_in_bf, layer_j, (D_MODEL, 4 * D_MODEL), lambda i: (0, 0)),
                  pl.BlockSpec(lb_logits.shape, lambda i: (0, 0))],
        out_specs=[pl.BlockSpec((tm, D_MODEL), lambda i: (i, 0))] * 5,
        out_shape=[out] * 5,
        compiler_params=_cparams(("arbitrary",)),
        name="hgrn_in",
    )(x, w_in_bf, lb_logits)


def _hgrn_prefix(C, lf):
    row = lax.broadcasted_iota(I32, (C, LANES), 0)
    G = lf
    d = 1
    while d < C:
        G = G + jnp.where(row >= d, pltpu.roll(G, d, 0), 0.0)
        d *= 2
    return G


def _hgrn_diag(C, G, q, k, v):
    row8 = lax.broadcasted_iota(I32, (SUBLANES, LANES), 0)
    parts = []
    for gi in range(C // SUBLANES):
        lo = gi * SUBLANES
        Gg = G[lo:lo + SUBLANES]
        qg = q[lo:lo + SUBLANES]
        acc = jnp.zeros((SUBLANES, LANES), F32)
        for s in range(lo + SUBLANES):
            dec = jnp.exp(Gg - G[s:s + 1])
            if s >= lo:
                dec = jnp.where(row8 >= (s - lo), dec, 0.0)
            w = jnp.sum(qg * dec * k[s:s + 1], axis=-1, keepdims=True)
            acc = acc + w * v[s:s + 1]
        parts.append(acc)
    return parts[0] if len(parts) == 1 else jnp.concatenate(parts, axis=0)


def _hgrn_state_update(C, G, k, v, S):
    gend = G[C - 1:C, :]
    kt = k * jnp.exp(gend - G)
    ecol = jnp.transpose(jnp.broadcast_to(jnp.exp(gend), (HG_DK, LANES)))
    upd = lax.dot_general(kt.astype(BF16), v.astype(BF16), (((0,), (0,)), ((), ())),
                          preferred_element_type=F32)
    return ecol * S + upd


def _hgrn_chunk_head(C, q, k, lf, v, S):
    G = _hgrn_prefix(C, lf)
    o = jnp.dot((q * jnp.exp(G)).astype(BF16), S.astype(BF16), preferred_element_type=F32)
    return o + _hgrn_diag(C, G, q, k, v), _hgrn_state_update(C, G, k, v, S)


def _hgrn_rec_kernel(C, q_ref, k_ref, lf_ref, v_ref, g_ref, ng_ref, s0_ref, o_ref, s_ref):
    heads = [slice(h * LANES, (h + 1) * LANES) for h in range(HG_HEADS)]

    def body(i, carry):
        rows = pl.ds(pl.multiple_of(i * C, C), C)
        gs, os_ = [], []
        for h, sl in enumerate(heads):
            G = _hgrn_prefix(C, lf_ref[rows, sl])
            gs.append(G)
            os_.append(jnp.dot((q_ref[rows, sl] * jnp.exp(G)).astype(BF16), s0_ref[i, h].astype(BF16),
                               preferred_element_type=F32))
        for h, sl in enumerate(heads):
            o = os_[h] + _hgrn_diag(C, gs[h], q_ref[rows, sl], k_ref[rows, sl], v_ref[rows, sl])
            ms = jnp.mean(o * o, axis=-1, keepdims=True)
            o_ref[rows, sl] = o * lax.rsqrt(ms + LN_EPS) * ng_ref[:, sl] * g_ref[rows, sl]
        for h, sl in enumerate(heads):
            s_ref[i, h] = _hgrn_state_update(C, gs[h], k_ref[rows, sl], v_ref[rows, sl], s0_ref[i, h])
        return carry

    lax.fori_loop(0, q_ref.shape[0] // C, body, 0)


HG_CHUNK = 64
HG_SUB = 16
HG_SAFE_RANGE = 60.0


def _hgrn_chunk_masks(tt):
    import numpy as np
    t = np.arange(tt)[:, None]
    s = np.arange(tt)[None, :]
    tri = (s <= t) & ((t // HG_CHUNK) == (s // HG_CHUNK))
    return jnp.asarray(tri.astype(np.float32), BF16)


def _hgrn_prompt_kernel(layer_j, x_ref, win_ref, lbl_ref, ng_ref, m_ref, wout_ref, lng_ref, lnb_ref,
                        xo_ref, s_ref,
                        q_ref, k_ref, lf_ref, v_ref, g_ref, o_ref,
                        g_s, qi_s, ks_s, qd_s, kd_s, q1_s, k1_s, q2_s, k2_s, a_s, flag_s):
    tt = x_ref.shape[0]

    @pl.when(pl.program_id(1) == 0)
    def _():
        s_ref[...] = jnp.zeros(s_ref.shape, F32)

    _hgrn_in_compute(layer_j, x_ref[...].astype(BF16), win_ref, lbl_ref, q_ref, k_ref, lf_ref, v_ref, g_ref)

    lf = lf_ref[...] * LOG2E
    hi = lf.astype(BF16)
    lo = (lf - hi.astype(F32)).astype(BF16)
    g_s[...] = (jnp.dot(m_ref[...], hi, preferred_element_type=F32)
                + jnp.dot(m_ref[...], lo, preferred_element_type=F32))

    C = HG_CHUNK
    SB = HG_SUB
    worst = jnp.zeros((SB, D_MODEL), F32)
    for blk in range(tt // SB):
        r0 = blk * SB
        c0 = (r0 // C) * C
        rows = slice(r0, r0 + SB)

        def grow(r):
            return g_s[r - 1:r, :] if r > c0 else jnp.zeros((1, D_MODEL), F32)

        G = g_s[rows, :]
        q = q_ref[rows, :]
        k = k_ref[rows, :]
        qi_s[rows, :] = (q * jnp.exp2(G)).astype(BF16)
        ks_s[rows, :] = (k * jnp.exp2(g_s[c0 + C - 1:c0 + C, :] - G)).astype(BF16)
        e0 = G - grow(r0)
        qd_s[rows, :] = (q * jnp.exp2(e0)).astype(BF16)
        kd_s[rows, :] = (k * jnp.exp2(-e0)).astype(BF16)
        worst = jnp.maximum(worst, -e0)
        e1 = G - grow((r0 // (2 * SB)) * (2 * SB) + SB)
        q1_s[rows, :] = (q * jnp.exp2(jnp.minimum(e1, 0.0))).astype(BF16)
        k1_s[rows, :] = (k * jnp.exp2(jnp.minimum(-e1, 0.0))).astype(BF16)
        e2 = G - grow(c0 + C // 2)
        q2_s[rows, :] = (q * jnp.exp2(jnp.minimum(e2, 0.0))).astype(BF16)
        k2_s[rows, :] = (k * jnp.exp2(jnp.minimum(-e2, 0.0))).astype(BF16)
    flag_s[0] = (jnp.max(worst) <= HG_SAFE_RANGE * LOG2E).astype(I32)

    def finish(rows, sl, o):
        ms = jnp.mean(o * o, axis=-1, keepdims=True)
        o_ref[rows, sl] = o * lax.rsqrt(ms + LN_EPS) * ng_ref[:, sl] * g_ref[rows, sl]

    C = HG_CHUNK
    ti = lax.broadcasted_iota(I32, (C, C), 0)
    si = lax.broadcasted_iota(I32, (C, C), 1)
    causal = si <= ti
    sub_bits = HG_SUB.bit_length() - 1
    m0 = causal & ((ti >> sub_bits) == (si >> sub_bits))
    m1 = causal & ((ti >> (sub_bits + 1)) == (si >> (sub_bits + 1))) & jnp.logical_not(m0)
    nt_dims = (((1,), (1,)), ((), ()))

    @pl.when(flag_s[0] == 1)
    def _():
        def body(c, carry):
            r0 = pl.multiple_of(c * C, C)
            rows = pl.ds(r0, C)
            heads = [slice(h * LANES, (h + 1) * LANES) for h in range(HG_HEADS)]
            for h, sl in enumerate(heads):
                a0 = lax.dot_general(qd_s[rows, sl], kd_s[rows, sl], nt_dims, preferred_element_type=F32)
                a1 = lax.dot_general(q1_s[rows, sl], k1_s[rows, sl], nt_dims, preferred_element_type=F32)
                a2 = lax.dot_general(q2_s[rows, sl], k2_s[rows, sl], nt_dims, preferred_element_type=F32)
                a = jnp.where(m0, a0, jnp.where(m1, a1, jnp.where(causal, a2, 0.0)))
                a_s[h] = a.astype(BF16)
            for h, sl in enumerate(heads):
                o = (jnp.dot(qi_s[rows, sl], s_ref[0, h].astype(BF16), preferred_element_type=F32)
                     + jnp.dot(a_s[h], v_ref[rows, sl].astype(BF16), preferred_element_type=F32))
                finish(rows, sl, o)
            for h, sl in enumerate(heads):
                gend = g_s[pl.ds(pl.multiple_of(r0 + C - SUBLANES, SUBLANES), SUBLANES), sl][SUBLANES - 1:]
                ecol = jnp.transpose(jnp.broadcast_to(jnp.exp2(gend), (HG_DK, LANES)))
                upd = lax.dot_general(ks_s[rows, sl], v_ref[rows, sl].astype(BF16), (((0,), (0,)), ((), ())),
                                      preferred_element_type=F32)
                s_ref[0, h] = ecol * s_ref[0, h] + upd
            return carry

        lax.fori_loop(0, tt // C, body, 0)

    @pl.when(flag_s[0] == 0)
    def _():
        Cs = HG_CHUNK_PROMPT

        def body(i, carry):
            r0 = pl.multiple_of(i * Cs, Cs)
            rows = pl.ds(r0, Cs)
            for h in range(HG_HEADS):
                sl = slice(h * LANES, (h + 1) * LANES)
                o, s_new = _hgrn_chunk_head(Cs, q_ref[rows, sl], k_ref[rows, sl], lf_ref[rows, sl],
                                            v_ref[rows, sl], s_ref[0, h])
                finish(rows, sl, o)
                s_ref[0, h] = s_new
            return carry

        lax.fori_loop(0, tt // Cs, body, 0)

    y = jnp.dot(o_ref[...].astype(BF16), wout_ref[...], preferred_element_type=F32)
    xo_ref[...] = _layer_norm_rows(ALPHA * x_ref[...] + y, lng_ref[...], lnb_ref[...])


def hgrn_prompt_layer(x, w_in_bf, lb_logits, norm_g, w_out_bf, ln_g, ln_b, s_all, layer_j, batch, seq, n,
                      tt=256):
    nt = seq // tt
    row = pl.BlockSpec((tt, D_MODEL), lambda b, t: (b * nt + t, 0))
    vec = pl.BlockSpec((1, D_MODEL), lambda b, t: (0, 0))
    s_spec = pl.BlockSpec((None, 1, HG_HEADS, HG_DK, HG_DV), lambda b, t: (layer_j, b, 0, 0, 0))
    in_specs = [row,
                _layer_block(w_in_bf, layer_j, (D_MODEL, 4 * D_MODEL), lambda b, t: (0, 0)),
                pl.BlockSpec(lb_logits.shape, lambda b, t: (0, 0)),
                vec,
                pl.BlockSpec((tt, tt), lambda b, t: (0, 0)),
                _layer_block(w_out_bf, layer_j, (D_MODEL, D_MODEL), lambda b, t: (0, 0)),
                vec, vec]
    args = [x, w_in_bf, lb_logits, norm_g, _hgrn_chunk_masks(tt), w_out_bf, ln_g, ln_b]
    aliases = {}
    kern = functools.partial(_hgrn_prompt_kernel, layer_j)
    if s_all is not None:
        def kern(*refs):
            _hgrn_prompt_kernel(layer_j, *refs[:8], *refs[9:])
        in_specs.append(pl.BlockSpec(memory_space=pl.ANY))
        args.append(s_all)
        aliases = {8: 1}
    f32_tile = pltpu.VMEM((tt, D_MODEL), F32)
    bf_tile = pltpu.VMEM((tt, D_MODEL), BF16)
    return pl.pallas_call(
        kern,
        grid=(batch, nt),
        in_specs=in_specs,
        out_specs=[row, s_spec],
        out_shape=[jax.ShapeDtypeStruct((n, D_MODEL), F32),
                   jax.ShapeDtypeStruct((2, batch, HG_HEADS, HG_DK, HG_DV), F32)],
        scratch_shapes=[f32_tile] * 7 + [bf_tile] * 8
                       + [pltpu.VMEM((HG_HEADS, HG_CHUNK, HG_CHUNK), BF16), pltpu.SMEM((1,), I32)],
        input_output_aliases=aliases,
        compiler_params=_cparams(("arbitrary", "arbitrary")),
        name="hgrn_prompt_layer",
    )(*args)


def hgrn_rec_sample(q, k, lf, v, g, norm_g, state, s_all, layer_j, nseq, seq, sg=8):
    n = q.shape[0]
    rt = sg * seq
    row = pl.BlockSpec((rt, D_MODEL), lambda i: (i, 0))
    s_in = pl.BlockSpec((None, sg, HG_HEADS, HG_DK, HG_DV), lambda i: (layer_j, i, 0, 0, 0))
    in_specs = [row] * 5 + [pl.BlockSpec((1, D_MODEL), lambda i: (0, 0)), s_in]
    args = [q, k, lf, v, g, norm_g, state]
    aliases = {}
    if s_all is not None:
        in_specs.append(pl.BlockSpec(memory_space=pl.ANY))
        args.append(s_all)
        aliases[7] = 1

    def kern(*refs):
        _hgrn_rec_kernel(seq, *refs[:7], *refs[len(args):])

    return pl.pallas_call(
        kern,
        grid=(nseq // sg,),
        in_specs=in_specs,
        out_specs=[row, s_in],
        out_shape=[jax.ShapeDtypeStruct((n, D_MODEL), F32),
                   jax.ShapeDtypeStruct((2, nseq, HG_HEADS, HG_DK, HG_DV), F32)],
        input_output_aliases=aliases,
        compiler_params=_cparams(("arbitrary",)),
        name="hgrn_rec_sample",
    )(*args)


def _proj_ln_kernel(a_ref, w_ref, x_ref, g_ref, b_ref, o_ref):
    y = jnp.dot(a_ref[...].astype(BF16), w_ref[...], preferred_element_type=F32)
    o_ref[...] = _layer_norm_rows(ALPHA * x_ref[...] + y, g_ref[...], b_ref[...])


def proj_ln(a, w_bf, x, x_row0, g, b, xo, row0, layer=0, tm=512):
    n = xo.shape[0]
    nrows = a.shape[0]
    base = row0 // tm
    xbase = x_row0 // tm
    row = pl.BlockSpec((tm, D_MODEL), lambda i: (base + i, 0))
    vec = pl.BlockSpec((1, D_MODEL), lambda i: (0, 0))

    def kern(a_ref, w_ref, x_ref, g_ref, b_ref, _, o_ref):
        _proj_ln_kernel(a_ref, w_ref, x_ref, g_ref, b_ref, o_ref)

    return pl.pallas_call(
        kern,
        grid=(nrows // tm,),
        in_specs=[pl.BlockSpec((tm, D_MODEL), lambda i: (i, 0)),
                  _layer_block(w_bf, layer, (D_MODEL, D_MODEL), lambda i: (0, 0)),
                  pl.BlockSpec((tm, D_MODEL), lambda i: (xbase + i, 0)), vec, vec,
                  pl.BlockSpec(memory_space=pl.ANY)],
        out_specs=row,
        out_shape=jax.ShapeDtypeStruct((n, D_MODEL), F32),
        input_output_aliases={5: 0},
        compiler_params=_cparams(("arbitrary",)),
        name="proj_ln",
    )(a, w_bf, x, g, b, xo)


def _ffn_kernel(x_ref, wg_ref, wu_ref, wd_ref, g_ref, b_ref, o_ref):
    xb = x_ref[...].astype(BF16)
    gg = jnp.dot(xb, wg_ref[...], preferred_element_type=F32)
    uu = jnp.dot(xb, wu_ref[...], preferred_element_type=F32)
    h = (gg * _sigmoid(gg) * uu).astype(BF16)
    y = jnp.dot(h, wd_ref[...], preferred_element_type=F32)
    o_ref[...] = _layer_norm_rows(ALPHA * x_ref[...] + y, g_ref[...], b_ref[...])


def ffn_ln(x, w_gu_bf, w_down_bf, g, b, layer=0, tm=512):
    n = x.shape[0]
    row = pl.BlockSpec((tm, D_MODEL), lambda i: (i, 0))
    vec = pl.BlockSpec((1, D_MODEL), lambda i: (0, 0))
    once = pl.Buffered(1)
    return pl.pallas_call(
        _ffn_kernel,
        grid=(n // tm,),
        in_specs=[row,
                  _layer_block(w_gu_bf, layer, (D_MODEL, D_FF), lambda i: (0, 0), pipeline_mode=once),
                  _layer_block(w_gu_bf, layer, (D_MODEL, D_FF), lambda i: (0, 1), pipeline_mode=once),
                  _layer_block(w_down_bf, layer, (D_FF, D_MODEL), lambda i: (0, 0), pipeline_mode=once),
                  vec, vec],
        out_specs=row,
        out_shape=jax.ShapeDtypeStruct((n, D_MODEL), F32),
        compiler_params=_cparams(("arbitrary",)),
        name="ffn_ln",
    )(x, w_gu_bf, w_gu_bf, w_down_bf, g, b)


def _lru_in_kernel(x_ref, w_ref, xb_ref, gate_ref):
    xb = x_ref[...].astype(BF16)
    xb_ref[...] = jnp.dot(xb, w_ref[:, :D_MODEL], preferred_element_type=F32)
    y = jnp.dot(xb, w_ref[:, D_MODEL:], preferred_element_type=F32)
    gate_ref[...] = _gelu_tanh(y)


def lru_in(x, w_in_bf, row0, nrows, layer=0, tm=512):
    base = row0 // tm
    row = pl.BlockSpec((tm, D_MODEL), lambda i: (i, 0))
    out = jax.ShapeDtypeStruct((nrows, D_MODEL), F32)
    return pl.pallas_call(
        _lru_in_kernel,
        grid=(nrows // tm,),
        in_specs=[pl.BlockSpec((tm, D_MODEL), lambda i: (base + i, 0)),
                  _layer_block(w_in_bf, layer, (D_MODEL, 2 * D_MODEL), lambda i: (0, 0))],
        out_specs=[row, row],
        out_shape=[out, out],
        compiler_params=_cparams(("arbitrary",)),
        name="lru_in",
    )(x, w_in_bf)


def _lru_rec_kernel(L, carry, *refs):
    if carry:
        (x_ref, win_ref, cw_ref, cb_ref, wr_ref, br_ref, wi_ref, bi_ref, lam_ref, wout_ref, lng_ref, lnb_ref,
         cgu_ref, cdn_ref, xo_ref, hn_ref, cn_ref, cgu_o, cdn_o, xp_ref, hs_ref, hc_ref, gate_ref, hg_ref) = refs
        R = x_ref.shape[0]
        cgu_o[...] = cgu_ref[...].astype(BF16)
        cdn_o[...] = cdn_ref[...].astype(BF16)
    else:
        (xb_ref, gate_ref, cc_ref, h0_ref, cw_ref, cb_ref, wr_ref, br_ref, wi_ref, bi_ref, lam_ref,
         hg_ref, hn_ref, cn_ref, xp_ref, hs_ref) = refs
        R = xb_ref.shape[0]
    S = R // L
    PADR = SUBLANES
    t = pl.program_id(1) if carry else 0

    if carry:
        @pl.when(t == 0)
        def _():
            xp_ref[:, 0:PADR, :] = jnp.zeros((S, PADR, D_MODEL), F32)
            hc_ref[...] = jnp.zeros(hc_ref.shape, F32)

        @pl.when(t > 0)
        def _():
            xp_ref[:, 0:PADR, :] = xp_ref[:, L:L + PADR, :]

        xin = x_ref[...].astype(BF16)
        xp_ref[:, PADR:PADR + L, :] = jnp.dot(xin, win_ref[:, :D_MODEL],
                                              preferred_element_type=F32).reshape(S, L, D_MODEL)
        gate_ref[...] = _gelu_tanh(jnp.dot(xin, win_ref[:, D_MODEL:], preferred_element_type=F32))
    else:
        xp_ref[:, 0:PADR, :] = cc_ref[...].reshape(S, PADR, D_MODEL)
        xp_ref[:, PADR:PADR + L, :] = xb_ref[...].reshape(S, L, D_MODEL)

    xc = cb_ref[...].reshape(1, 1, D_MODEL) + xp_ref[:, PADR:PADR + L, :] * cw_ref[CONV_W - 1:CONV_W, :].reshape(1, 1, D_MODEL)
    for j in range(CONV_W - 1):
        off = PADR - (CONV_W - 1) + j
        xc = xc + xp_ref[:, off:off + L, :] * cw_ref[j:j + 1, :].reshape(1, 1, D_MODEL)
    xc = xc.reshape(R, D_MODEL)
    cn_ref[...] = xp_ref[:, L + PADR - (CONV_W - 1):L + PADR, :].reshape(cn_ref.shape)

    assert L & (L - 1) == 0
    pos = jnp.bitwise_and(lax.broadcasted_iota(I32, (R, LRU_BW), 0), L - 1)
    for nb in range(LRU_BLOCKS):
        sl = slice(nb * LRU_BW, (nb + 1) * LRU_BW)
        xcb = xc[:, sl]
        xh = xcb.astype(BF16)
        r = _sigmoid(jnp.dot(xh, wr_ref[nb], preferred_element_type=F32) + br_ref[:, sl])
        ig = _sigmoid(jnp.dot(xh, wi_ref[nb], preferred_element_type=F32) + bi_ref[:, sl])
        z = -lam_ref[:, sl]
        softplus = jnp.maximum(z, 0.0) + jnp.log1p(jnp.exp(-jnp.abs(z)))
        log_a = (-RG_C) * r * softplus
        a = jnp.exp(log_a)
        u = jnp.sqrt(1.0 - a * a) * (ig * xcb)
        if carry:
            ng = R // SUBLANES
            u3 = u.reshape(ng, SUBLANES, LRU_BW)
            a3 = a.reshape(ng, SUBLANES, LRU_BW)
            pos8 = lax.broadcasted_iota(I32, (ng, SUBLANES, LRU_BW), 1)
            d = 1
            while d < SUBLANES:
                m = pos8 >= d
                u3 = jnp.where(m, a3 * pltpu.roll(u3, d, 1) + u3, u3)
                a3 = jnp.where(m, a3 * pltpu.roll(a3, d, 1), a3)
                d *= 2
            hprev = hc_ref[:, sl]
            for gi in range(ng):
                hgrp = u3[gi] + a3[gi] * hprev
                hs_ref[0, gi * SUBLANES:(gi + 1) * SUBLANES, sl] = hgrp
                hprev = hgrp[SUBLANES - 1:SUBLANES]
            hg_ref[:, sl] = hs_ref[0, :, sl] * gate_ref[:, sl]
        else:
            u = u + a * h0_ref[:, sl]
            d = 1
            while d < L:
                m = pos >= d
                u = jnp.where(m, a * pltpu.roll(u, d, 0) + u, u)
                a = jnp.where(m, a * pltpu.roll(a, d, 0), a)
                d *= 2
            hg_ref[:, sl] = u * gate_ref[:, sl]
            hs_ref[:, :, sl] = u.reshape(S, L, LRU_BW)

    hlast = hs_ref[:, L - 1:L, :]
    hn_ref[...] = hlast.reshape(hn_ref.shape)
    if carry:
        hc_ref[...] = hlast.reshape(1, D_MODEL)
        y = jnp.dot(hg_ref[...].astype(BF16), wout_ref[...], preferred_element_type=F32)
        xo_ref[...] = _layer_norm_rows(ALPHA * x_ref[...] + y, lng_ref[...], lnb_ref[...])


def _lru_weight_specs(wts, layer):
    def c2(*_):
        return (0, 0)

    def c3(*_):
        return (0, 0, 0)
    gate_shape = (LRU_BLOCKS, LRU_BW, LRU_BW)
    return [pl.BlockSpec((CONV_W, D_MODEL), c2), pl.BlockSpec((1, D_MODEL), c2),
            _layer_block(wts[2], layer, gate_shape, c3), pl.BlockSpec((1, D_MODEL), c2),
            _layer_block(wts[4], layer, gate_shape, c3), pl.BlockSpec((1, D_MODEL), c2),
            pl.BlockSpec((1, D_MODEL), c2)]


def lru_prompt_layer(x, w_in_bf, wts, w_out_bf, ln_g, ln_b, moe_gu, moe_down, layer_j, batch, seq, tt=256):
    n = x.shape[0]
    nt = seq // tt
    steps = batch * nt
    gu2 = moe_gu.reshape(moe_gu.shape[0], N_EXPERTS * D_MODEL, 2 * D_FF_EXPERT)
    dn2 = moe_down.reshape(moe_down.shape[0], N_EXPERTS * D_FF_EXPERT, D_MODEL)
    gu_rows = gu2.shape[1] // steps
    dn_rows = dn2.shape[1] // steps
    assert gu_rows * steps == gu2.shape[1] and dn_rows * steps == dn2.shape[1]
    row = pl.BlockSpec((tt, D_MODEL), lambda b, t: (b * nt + t, 0))
    vec = pl.BlockSpec((1, D_MODEL), lambda b, t: (0, 0))
    xo, hn, cn, gu_bf, dn_bf = pl.pallas_call(
        functools.partial(_lru_rec_kernel, tt, True),
        grid=(batch, nt),
        in_specs=[row, _layer_block(w_in_bf, layer_j, (D_MODEL, 2 * D_MODEL), lambda b, t: (0, 0))]
                 + _lru_weight_specs(wts, layer_j)
                 + [_layer_block(w_out_bf, layer_j, (D_MODEL, D_MODEL), lambda b, t: (0, 0)), vec, vec,
                    pl.BlockSpec((None, gu_rows, 2 * D_FF_EXPERT), lambda b, t: (layer_j, b * nt + t, 0)),
                    pl.BlockSpec((None, dn_rows, D_MODEL), lambda b, t: (layer_j, b * nt + t, 0))],
        out_specs=[row,
                   pl.BlockSpec((1, 1, D_MODEL), lambda b, t: (b, 0, 0)),
                   pl.BlockSpec((1, CONV_W - 1, D_MODEL), lambda b, t: (b, 0, 0)),
                   pl.BlockSpec((gu_rows, 2 * D_FF_EXPERT), lambda b, t: (b * nt + t, 0)),
                   pl.BlockSpec((dn_rows, D_MODEL), lambda b, t: (b * nt + t, 0))],
        out_shape=[jax.ShapeDtypeStruct((n, D_MODEL), F32),
                   jax.ShapeDtypeStruct((batch, 1, D_MODEL), F32),
                   jax.ShapeDtypeStruct((batch, CONV_W - 1, D_MODEL), F32),
                   jax.ShapeDtypeStruct(gu2.shape[1:], BF16),
                   jax.ShapeDtypeStruct(dn2.shape[1:], BF16)],
        scratch_shapes=[pltpu.VMEM((1, tt + SUBLANES, D_MODEL), F32),
                        pltpu.VMEM((1, tt, D_MODEL), F32),
                        pltpu.VMEM((1, D_MODEL), F32),
                        pltpu.VMEM((tt, D_MODEL), F32),
                        pltpu.VMEM((tt, D_MODEL), F32)],
        compiler_params=_cparams(("arbitrary", "arbitrary")),
        name="lru_prompt_layer",
    )(x, w_in_bf, *wts, w_out_bf, ln_g, ln_b, gu2, dn2)
    return (xo, hn, cn, gu_bf.reshape(N_EXPERTS, D_MODEL, 2 * D_FF_EXPERT),
            dn_bf.reshape(N_EXPERTS, D_FF_EXPERT, D_MODEL))


def lru_rec_sample(xb, gate, cc_rows, h0_rows, wts, nseq, seq, layer=0, sg=32):
    n = xb.shape[0]
    rt = sg * seq
    row = pl.BlockSpec((rt, D_MODEL), lambda i: (i, 0))
    return pl.pallas_call(
        functools.partial(_lru_rec_kernel, seq, False),
        grid=(nseq // sg,),
        in_specs=[row, row, pl.BlockSpec((sg * SUBLANES, D_MODEL), lambda i: (i, 0)), row]
                 + _lru_weight_specs(wts, layer),
        out_specs=[row,
                   pl.BlockSpec((sg, 1, D_MODEL), lambda i: (i, 0, 0)),
                   pl.BlockSpec((sg, CONV_W - 1, D_MODEL), lambda i: (i, 0, 0))],
        out_shape=[jax.ShapeDtypeStruct((n, D_MODEL), F32),
                   jax.ShapeDtypeStruct((nseq, 1, D_MODEL), F32),
                   jax.ShapeDtypeStruct((nseq, CONV_W - 1, D_MODEL), F32)],
        scratch_shapes=[pltpu.VMEM((sg, seq + SUBLANES, D_MODEL), F32),
                        pltpu.VMEM((sg, seq, D_MODEL), F32)],
        compiler_params=_cparams(("arbitrary",)),
        name="lru_rec_sample",
    )(xb, gate, cc_rows, h0_rows, *wts)


def _router_kernel(x_ref, w_ref, mi_ref, mf_ref, cnt_ref, run_ref):
    i = pl.program_id(0)

    @pl.when(i == 0)
    def _():
        run_ref[...] = jnp.zeros(run_ref.shape, F32)

    tm = x_ref.shape[0]
    x = x_ref[...]
    w = w_ref[...]
    xh = x.astype(BF16)
    xl = (x - xh.astype(F32)).astype(BF16)
    wh = w.astype(BF16)
    wl = (w - wh.astype(F32)).astype(BF16)
    hh_hl = jnp.dot(xh, jnp.concatenate([wh, wl], axis=1), preferred_element_type=F32)
    logits = hh_hl[:, :LANES] + hh_hl[:, LANES:] + jnp.dot(xl, wh, preferred_element_type=F32)
    lane_i = lax.broadcasted_iota(I32, (tm, LANES), 1)
    lane = lane_i.astype(F32)
    neg = jnp.float32(-jnp.inf)
    logits = jnp.where(lane_i < N_EXPERTS, logits, neg)
    m1 = jnp.max(logits, axis=-1, keepdims=True)
    i1 = jnp.min(jnp.where(logits == m1, lane, float(LANES)), axis=-1, keepdims=True)
    l2 = jnp.where(lane == i1, neg, logits)
    m2 = jnp.max(l2, axis=-1, keepdims=True)
    i2 = jnp.min(jnp.where(l2 == m2, lane, float(LANES)), axis=-1, keepdims=True)
    e2 = jnp.exp(m2 - m1)
    p1 = 1.0 / (1.0 + e2)
    p2 = e2 * p1

    hit1 = lane == i1
    hit2 = lane == i2
    onehot = jnp.where(hit1 | hit2, 1.0, 0.0)
    r_i = lax.broadcasted_iota(I32, (tm, tm), 0)
    c_i = lax.broadcasted_iota(I32, (tm, tm), 1)
    tri = jnp.where(c_i < r_i, 1.0, 0.0).astype(BF16)
    rank = jnp.dot(tri, onehot.astype(BF16), preferred_element_type=F32) + run_ref[0:1, :]
    r1 = jnp.sum(jnp.where(hit1, rank, 0.0), axis=-1, keepdims=True)
    r2 = jnp.sum(jnp.where(hit2, rank, 0.0), axis=-1, keepdims=True)
    total = run_ref[0:1, :] + jnp.sum(onehot, axis=0, keepdims=True)
    run_ref[...] = jnp.broadcast_to(total, run_ref.shape)
    cnt_ref[...] = jnp.broadcast_to(total, cnt_ref.shape).astype(I32)

    mi = jnp.where(lane_i == 0, i1, jnp.where(lane_i == 1, i2, 0.0))
    mi = jnp.where(lane_i == 2, r1, jnp.where(lane_i == 3, r2, mi))
    mi_ref[...] = mi.astype(I32)
    mf_ref[...] = jnp.where(lane_i == 0, p1, jnp.where(lane_i == 1, p2, 0.0))


def moe_router(x, w_router_pad, tm=512):
    n = x.shape[0]
    row = pl.BlockSpec((tm, LANES), lambda i: (i, 0))
    return pl.pallas_call(
        _router_kernel,
        grid=(n // tm,),
        in_specs=[pl.BlockSpec((tm, D_MODEL), lambda i: (i, 0)),
                  pl.BlockSpec((D_MODEL, LANES), lambda i: (0, 0))],
        out_specs=[row, row, pl.BlockSpec((SUBLANES, LANES), lambda i: (0, 0))],
        out_shape=[jax.ShapeDtypeStruct((n, LANES), I32),
                   jax.ShapeDtypeStruct((n, LANES), F32),
                   jax.ShapeDtypeStruct((SUBLANES, LANES), I32)],
        scratch_shapes=[pltpu.VMEM((SUBLANES, LANES), F32)],
        compiler_params=_cparams(("arbitrary",)),
        name="moe_router",
    )(x, w_router_pad)


ZERO_ROWS = 256


def _dispatch_kernel(dest_ref, ztile_ref, x_ref, xs_hbm, zero_ref, sem):
    i = pl.program_id(0)
    tm = x_ref.shape[0]

    @pl.when(i == 0)
    def _():
        zero_ref[...] = jnp.zeros(zero_ref.shape, F32)
        copies = [pltpu.make_async_copy(
                      zero_ref,
                      xs_hbm.at[pl.ds(pl.multiple_of(ztile_ref[e] + c * ZERO_ROWS, ZERO_ROWS), ZERO_ROWS)], sem)
                  for e in range(N_EXPERTS) for c in range(MOE_TILE // ZERO_ROWS)]
        for cp in copies:
            cp.start()
        for cp in copies:
            cp.wait()

    def group(jj, c):
        r8 = pl.multiple_of(jj * SUBLANES, SUBLANES)
        grp = x_ref.at[pl.ds(r8, SUBLANES)]
        t0 = 2 * (i * tm + r8)
        for u in range(SUBLANES):
            src = grp.at[pl.ds(u, 1)]
            pltpu.make_async_copy(src, xs_hbm.at[pl.ds(dest_ref[t0 + 2 * u], 1)], sem).start()
            pltpu.make_async_copy(src, xs_hbm.at[pl.ds(dest_ref[t0 + 2 * u + 1], 1)], sem).start()
        return c

    lax.fori_loop(0, tm // SUBLANES, group, 0)
    for _ in range(2):
        pltpu.make_async_copy(x_ref, xs_hbm.at[pl.ds(0, tm)], sem).wait()


def moe_dispatch(dest, ztile, x, n_rows, tm=512):
    n = x.shape[0]
    return pl.pallas_call(
        _dispatch_kernel,
        grid_spec=pltpu.PrefetchScalarGridSpec(
            num_scalar_prefetch=2, grid=(n // tm,),
            in_specs=[pl.BlockSpec((tm, D_MODEL), lambda i, d, z: (i, 0))],
            out_specs=pl.BlockSpec(memory_space=pl.ANY),
            scratch_shapes=[pltpu.VMEM((ZERO_ROWS, D_MODEL), F32), pltpu.SemaphoreType.DMA(())]),
        out_shape=jax.ShapeDtypeStruct((n_rows, D_MODEL), F32),
        compiler_params=pltpu.CompilerParams(dimension_semantics=("arbitrary",),
                                             has_side_effects=True),
        name="moe_dispatch",
    )(dest, ztile, x)


def _expert_kernel(te_ref, nu_ref, xs_ref, wg_ref, wu_ref, wd_ref, ys_ref, acc_ref):
    i = pl.program_id(0)
    f = pl.program_id(1)
    used = i < nu_ref[0]

    @pl.when(used)
    def _():
        xb = xs_ref[...].astype(BF16)
        gg = jnp.dot(xb, wg_ref[...], preferred_element_type=F32)
        uu = jnp.dot(xb, wu_ref[...], preferred_element_type=F32)
        h = (gg * _sigmoid(gg) * uu).astype(BF16)
        part = jnp.dot(h, wd_ref[...], preferred_element_type=F32)

        @pl.when(f == 0)
        def _():
            acc_ref[...] = part

        @pl.when(f > 0)
        def _():
            acc_ref[...] += part

    @pl.when(f == pl.num_programs(1) - 1)
    def _():
        @pl.when(used)
        def _():
            ys_ref[...] = acc_ref[...]

        @pl.when(jnp.logical_not(used))
        def _():
            ys_ref[...] = jnp.zeros(ys_ref.shape, F32)


def moe_experts(tile_expert, n_used, xs, w_gu_bf, w_down_bf, tf=1792):
    n_rows = xs.shape[0]
    tm = MOE_TILE
    nf = D_FF_EXPERT // tf

    def xrow(i, f, te, nu):
        return (jnp.minimum(i, nu[0] - 1), 0)

    def wg(i, f, te, nu):
        return (te[i], 0, jnp.where(i < nu[0], f, nf - 1))

    def wu(i, f, te, nu):
        return (te[i], 0, nf + jnp.where(i < nu[0], f, nf - 1))

    def wd(i, f, te, nu):
        return (te[i], jnp.where(i < nu[0], f, nf - 1), 0)

    return pl.pallas_call(
        _expert_kernel,
        grid_spec=pltpu.PrefetchScalarGridSpec(
            num_scalar_prefetch=2, grid=(n_rows // tm, nf),
            in_specs=[pl.BlockSpec((tm, D_MODEL), xrow),
                      pl.BlockSpec((None, D_MODEL, tf), wg),
                      pl.BlockSpec((None, D_MODEL, tf), wu),
                      pl.BlockSpec((None, tf, D_MODEL), wd)],
            out_specs=pl.BlockSpec((tm, D_MODEL), lambda i, f, te, nu: (i, 0)),
            scratch_shapes=[pltpu.VMEM((tm, D_MODEL), F32)]),
        out_shape=jax.ShapeDtypeStruct((n_rows, D_MODEL), F32),
        compiler_params=_cparams(("arbitrary", "arbitrary")),
        name="moe_experts",
    )(tile_expert, n_used, xs, w_gu_bf, w_gu_bf, w_down_bf)


def _combine_kernel(nsplit, dest_ref, ys_hbm, mf_ref, x_ref, g_ref, b_ref, *rest):
    if nsplit is None:
        o_ref, buf_ref, sem = rest
    else:
        oa_ref, ob_ref, buf_ref, sem = rest
    i = pl.program_id(0)
    n_steps = pl.num_programs(0)
    tm = x_ref.shape[0]

    def gather(step, slot):
        def group(jj, c):
            r8 = pl.multiple_of(jj * SUBLANES, SUBLANES)
            g0 = buf_ref.at[slot, 0, pl.ds(r8, SUBLANES)]
            g1 = buf_ref.at[slot, 1, pl.ds(r8, SUBLANES)]
            t0 = 2 * (step * tm + r8)
            for u in range(SUBLANES):
                pltpu.make_async_copy(ys_hbm.at[pl.ds(dest_ref[t0 + 2 * u], 1)], g0.at[pl.ds(u, 1)],
                                      sem.at[slot]).start()
                pltpu.make_async_copy(ys_hbm.at[pl.ds(dest_ref[t0 + 2 * u + 1], 1)], g1.at[pl.ds(u, 1)],
                                      sem.at[slot]).start()
            return c

        lax.fori_loop(0, tm // SUBLANES, group, 0)

    slot = lax.rem(i, 2)

    @pl.when(i == 0)
    def _():
        gather(i, slot)

    @pl.when(i + 1 < n_steps)
    def _():
        gather(i + 1, 1 - slot)

    for s in range(2):
        pltpu.make_async_copy(ys_hbm.at[pl.ds(0, tm)], buf_ref.at[slot, s], sem.at[slot]).wait()
    mf = mf_ref[...]
    y = buf_ref[slot, 0] * mf[:, 0:1] + buf_ref[slot, 1] * mf[:, 1:2]
    res = _layer_norm_rows(ALPHA * x_ref[...] + y, g_ref[...], b_ref[...])
    if nsplit is None:
        o_ref[...] = res
    else:
        @pl.when(i < nsplit)
        def _():
            oa_ref[...] = res

        @pl.when(i >= nsplit)
        def _():
            ob_ref[...] = res


def moe_combine_ln(dest, ys, mf, x, g, b, split_rows=None, tm=256):
    n = x.shape[0]
    row = pl.BlockSpec((tm, D_MODEL), lambda i, d: (i, 0))
    vec = pl.BlockSpec((1, D_MODEL), lambda i, d: (0, 0))
    if split_rows is None:
        nsplit = None
        out_specs = row
        out_shape = jax.ShapeDtypeStruct((n, D_MODEL), F32)
    else:
        nsplit = split_rows // tm
        out_specs = [pl.BlockSpec((tm, D_MODEL), lambda i, d: (jnp.minimum(i, nsplit - 1), 0)),
                     pl.BlockSpec((tm, D_MODEL), lambda i, d: (jnp.maximum(i - nsplit, 0), 0))]
        out_shape = [jax.ShapeDtypeStruct((split_rows, D_MODEL), F32),
                     jax.ShapeDtypeStruct((n - split_rows, D_MODEL), F32)]
    return pl.pallas_call(
        functools.partial(_combine_kernel, nsplit),
        grid_spec=pltpu.PrefetchScalarGridSpec(
            num_scalar_prefetch=1, grid=(n // tm,),
            in_specs=[pl.BlockSpec(memory_space=pl.ANY),
                      pl.BlockSpec((tm, LANES), lambda i, d: (i, 0)),
                      row, vec, vec],
            out_specs=out_specs,
            scratch_shapes=[pltpu.VMEM((2, 2, tm, D_MODEL), F32), pltpu.SemaphoreType.DMA((2,))]),
        out_shape=out_shape,
        compiler_params=_cparams(("arbitrary",)),
        name="moe_combine_ln",
    )(dest, ys, mf, x, g, b)


def moe_ffn_ln(x, w_router_pad, w_gu, w_down, g, b, split_rows=None):
    n = x.shape[0]
    tm = MOE_TILE
    mi, mf, cnt = moe_router(x, w_router_pad)
    counts = cnt[0, :N_EXPERTS]
    padded = (counts + tm - 1) // tm * tm
    ends = jnp.cumsum(padded)
    starts = ends - padded
    ids = mi[:, 0:2]
    ranks = mi[:, 2:4]
    dest = (starts[ids] + ranks).reshape(-1).astype(I32)
    n_tiles = (2 * n) // tm + N_EXPERTS
    tile_start = jnp.arange(n_tiles, dtype=I32) * tm
    tile_expert = jnp.minimum(jnp.sum(tile_start[:, None] >= ends[None, :], axis=1), N_EXPERTS - 1).astype(I32)
    n_used = (ends[-1] // tm).astype(I32).reshape(1)
    tile_expert = jnp.where(tile_start < ends[-1], tile_expert, tile_expert[jnp.maximum(n_used[0] - 1, 0)])
    ztile = jnp.maximum(ends - tm, 0).astype(I32)
    xs = moe_dispatch(dest, ztile, x, n_tiles * tm)
    ys = moe_experts(tile_expert, n_used, xs, w_gu, w_down)
    return moe_combine_ln(dest, ys, mf, x, g, b, split_rows)


def kernel(x_prompt, x_sample, state_hgrn, state_lru_h, state_lru_conv, ln_mix_g, ln_mix_b, ln_ffn_g, ln_ffn_b, w_hgrn_in, hgrn_lb_logits, hgrn_norm_g, w_hgrn_out, w_lru_in, lru_conv_w, lru_conv_b, w_lru_rgate, b_lru_rgate, w_lru_igate, b_lru_igate, lru_lambda, w_lru_out, w_ffn_gu, w_ffn_down, w_router, w_moe_gu, w_moe_down):
    bp, tp, _ = x_prompt.shape
    bs, ts, _ = x_sample.shape
    n_p = bp * tp
    n = n_p + bs * ts
    x = x_prompt.reshape(n_p, D_MODEL)
    x_smp, smp_row0 = x_sample.reshape(bs * ts, D_MODEL), 0

    def vec(a, i):
        return a[i].reshape(1, D_MODEL)

    w_hgrn_in_bf = w_hgrn_in.astype(BF16)
    w_hgrn_out_bf = w_hgrn_out.astype(BF16)
    w_lru_in_bf = w_lru_in.astype(BF16)
    w_lru_out_bf = w_lru_out.astype(BF16)
    w_r_bf = w_lru_rgate.astype(BF16)
    w_i_bf = w_lru_igate.astype(BF16)
    w_ffn_gu_bf = w_ffn_gu.astype(BF16)
    w_ffn_down_bf = w_ffn_down.astype(BF16)
    w_router_pad = jnp.pad(w_router, ((0, 0), (0, 0), (0, LANES - N_EXPERTS)))

    hg_p = hg_s = None
    h_p, h_s, c_p, c_s = [], [], [], []
    for layer in range(DEPTH):
        j = layer // 2
        lng, lnb = vec(ln_mix_g, layer), vec(ln_mix_b, layer)
        if layer % 2 == 0:
            ng = vec(hgrn_norm_g, j)
            xo, hg_p = hgrn_prompt_layer(x, w_hgrn_in_bf, hgrn_lb_logits, ng, w_hgrn_out_bf,
                                         lng, lnb, hg_p, j, bp, tp, n)
            q, k, lf, v, g = hgrn_in(x_smp, w_hgrn_in_bf, hgrn_lb_logits, j, smp_row0, bs * ts)
            o, hg_s = hgrn_rec_sample(q, k, lf, v, g, ng, state_hgrn, hg_s, j, bs, ts)
            x = proj_ln(o, w_hgrn_out_bf, x_smp, smp_row0, lng, lnb, xo, n_p, layer=j)
            x = ffn_ln(x, w_ffn_gu_bf, w_ffn_down_bf, vec(ln_ffn_g, layer), vec(ln_ffn_b, layer), layer=j)
            x_smp, smp_row0 = x, n_p
        else:
            wts = (lru_conv_w[j], vec(lru_conv_b, j), w_r_bf, vec(b_lru_rgate, j),
                   w_i_bf, vec(b_lru_igate, j), vec(lru_lambda, j))
            xo, hp, cp, moe_gu_bf, moe_down_bf = lru_prompt_layer(
                x, w_lru_in_bf, wts, w_lru_out_bf, lng, lnb, w_moe_gu, w_moe_down, j, bp, tp)
            xb, gate = lru_in(x_smp, w_lru_in_bf, smp_row0, bs * ts, layer=j)
            cc_rows = jnp.pad(state_lru_conv[j], ((0, 0), (SUBLANES - (CONV_W - 1), 0), (0, 0)))
            h0_rows = jnp.pad(state_lru_h[j][:, None, :], ((0, 0), (0, ts - 1), (0, 0)))
            hgate, hs, cs = lru_rec_sample(xb, gate, cc_rows.reshape(bs * SUBLANES, D_MODEL),
                                           h0_rows.reshape(bs * ts, D_MODEL), wts, bs, ts, layer=j)
            h_p.append(hp.reshape(bp, D_MODEL))
            h_s.append(hs.reshape(bs, D_MODEL))
            c_p.append(cp)
            c_s.append(cs)
            x = proj_ln(hgate, w_lru_out_bf, x_smp, smp_row0, lng, lnb, xo, n_p, layer=j)
            x = moe_ffn_ln(x, w_router_pad[j], moe_gu_bf, moe_down_bf,
                           vec(ln_ffn_g, layer), vec(ln_ffn_b, layer),
                           split_rows=n_p if layer == DEPTH - 1 else None)
            x_smp, smp_row0 = x, n_p

    y_prompt = x[0].reshape(bp, tp, D_MODEL)
    y_sample = x[1].reshape(bs, ts, D_MODEL)
    return (y_prompt, y_sample, hg_p, hg_s, jnp.stack(h_p), jnp.stack(h_s), jnp.stack(c_p), jnp.stack(c_s))
```

```python
import functools
import math

import jax
import jax.numpy as jnp
from jax import lax
from jax.experimental import pallas as pl
from jax.experimental.pallas import tpu as pltpu

F32 = jnp.float32
BF16 = jnp.bfloat16
I32 = jnp.int32

D_MODEL = 1024
DEPTH = 4
HG_DK = 128
HG_HEADS = 8
HG_DV = 128
CONV_W = 4
RG_C = 8.0
LRU_BLOCKS = 4
LRU_BW = 256
D_FF = 2816
N_EXPERTS = 8
D_FF_EXPERT = 3584
ALPHA = (2 * DEPTH) ** 0.25
LN_EPS = 1e-5
F_FLOOR = 1e-20
LOG2E = math.log2(math.e)

LANES = 128
SUBLANES = 8
VMEM_LIMIT = 56 * 1024 * 1024

HG_CHUNK_PROMPT = 16
MOE_TILE = 512


def _cparams(sem):
    return pltpu.CompilerParams(dimension_semantics=sem, vmem_limit_bytes=VMEM_LIMIT)


def _layer_block(w, layer, block_shape, index_map, **kw):
    if w.ndim == len(block_shape):
        return pl.BlockSpec(block_shape, index_map, **kw)
    return pl.BlockSpec((None,) + tuple(block_shape), lambda *a: (layer,) + tuple(index_map(*a)), **kw)


def _sigmoid(x):
    return 0.5 * jnp.tanh(0.5 * x) + 0.5


def _gelu_tanh(y):
    c = math.sqrt(2.0 / math.pi)
    return 0.5 * y * (1.0 + jnp.tanh(c * (y + 0.044715 * (y * y * y))))


def _layer_norm_rows(z, g, b):
    mu = jnp.mean(z, axis=-1, keepdims=True)
    zc = z - mu
    var = jnp.mean(zc * zc, axis=-1, keepdims=True)
    return zc * lax.rsqrt(var + LN_EPS) * g + b


def _hgrn_in_compute(layer_j, xb, w_ref, lbl_ref, q_ref, k_ref, lf_ref, v_ref, g_ref):
    l = lbl_ref[...]
    e = jnp.exp(l - jnp.max(l, axis=0, keepdims=True))
    p = e / jnp.sum(e, axis=0, keepdims=True)
    cs = p[0:1]
    for i in range(1, layer_j + 1):
        cs = cs + p[i:i + 1]
    lb = cs - p[0:1]

    def seg(i):
        return jnp.dot(xb, w_ref[:, i * D_MODEL:(i + 1) * D_MODEL], preferred_element_type=F32)

    q = seg(0)
    q_ref[...] = q * _sigmoid(q) * (HG_DK ** -0.5)
    fp = seg(1)
    sig = 1.0 / (1.0 + jnp.exp(-fp))
    f = lb + (1.0 - lb) * sig
    lf_ref[...] = jnp.log(jnp.maximum(f, F_FLOOR))
    k_ref[...] = (1.0 - lb) * (1.0 - sig)
    v_ref[...] = seg(2)
    g = seg(3)
    g_ref[...] = g * _sigmoid(g)


def _hgrn_in_kernel(layer_j, x_ref, w_ref, lbl_ref, *out_refs):
    _hgrn_in_compute(layer_j, x_ref[...].astype(BF16), w_ref, lbl_ref, *out_refs)


def hgrn_in(x, w_in_bf, lb_logits, layer_j, row0, nrows, tm=256):
    base = row0 // tm
    out = jax.ShapeDtypeStruct((nrows, D_MODEL), F32)
    return pl.pallas_call(
        functools.partial(_hgrn_in_kernel, layer_j),
        grid=(nrows // tm,),
        in_specs=[pl.BlockSpec((tm, D_MODEL), lambda i: (base + i, 0)),
                  _layer_block(w_in_bf, layer_j, (D_MODEL, 4 * D_MODEL), lambda i: (0, 0)),
                  pl.BlockSpec(lb_logits.shape, lambda i: (0, 0))],
        out_specs=[pl.BlockSpec((tm, D_MODEL), lambda i: (i, 0))] * 5,
        out_shape=[out] * 5,
        compiler_params=_cparams(("arbitrary",)),
        name="hgrn_in",
    )(x, w_in_bf, lb_logits)


def _hgrn_prefix(C, lf):
    row = lax.broadcasted_iota(I32, (C, LANES), 0)
    G = lf
    d = 1
    while d < C:
        G = G + jnp.where(row >= d, pltpu.roll(G, d, 0), 0.0)
        d *= 2
    return G


def _hgrn_diag(C, G, q, k, v):
    row8 = lax.broadcasted_iota(I32, (SUBLANES, LANES), 0)
    parts = []
    for gi in range(C // SUBLANES):
        lo = gi * SUBLANES
        Gg = G[lo:lo + SUBLANES]
        qg = q[lo:lo + SUBLANES]
        acc = jnp.zeros((SUBLANES, LANES), F32)
        for s in range(lo + SUBLANES):
            dec = jnp.exp(Gg - G[s:s + 1])
            if s >= lo:
                dec = jnp.where(row8 >= (s - lo), dec, 0.0)
            w = jnp.sum(qg * dec * k[s:s + 1], axis=-1, keepdims=True)
            acc = acc + w * v[s:s + 1]
        parts.append(acc)
    return parts[0] if len(parts) == 1 else jnp.concatenate(parts, axis=0)


def _hgrn_state_update(C, G, k, v, S):
    gend = G[C - 1:C, :]
    kt = k * jnp.exp(gend - G)
    ecol = jnp.transpose(jnp.broadcast_to(jnp.exp(gend), (HG_DK, LANES)))
    upd = lax.dot_general(kt.astype(BF16), v.astype(BF16), (((0,), (0,)), ((), ())),
                          preferred_element_type=F32)
    return ecol * S + upd


def _hgrn_chunk_head(C, q, k, lf, v, S):
    G = _hgrn_prefix(C, lf)
    o = jnp.dot((q * jnp.exp(G)).astype(BF16), S.astype(BF16), preferred_element_type=F32)
    return o + _hgrn_diag(C, G, q, k, v), _hgrn_state_update(C, G, k, v, S)


def _hgrn_rec_kernel(C, q_ref, k_ref, lf_ref, v_ref, g_ref, ng_ref, s0_ref, o_ref, s_ref):
    heads = [slice(h * LANES, (h + 1) * LANES) for h in range(HG_HEADS)]

    def body(i, carry):
        rows = pl.ds(pl.multiple_of(i * C, C), C)
        gs, os_ = [], []
        for h, sl in enumerate(heads):
            G = _hgrn_prefix(C, lf_ref[rows, sl])
            gs.append(G)
            os_.append(jnp.dot((q_ref[rows, sl] * jnp.exp(G)).astype(BF16), s0_ref[i, h].astype(BF16),
                               preferred_element_type=F32))
        for h, sl in enumerate(heads):
            o = os_[h] + _hgrn_diag(C, gs[h], q_ref[rows, sl], k_ref[rows, sl], v_ref[rows, sl])
            ms = jnp.mean(o * o, axis=-1, keepdims=True)
            o_ref[rows, sl] = o * lax.rsqrt(ms + LN_EPS) * ng_ref[:, sl] * g_ref[rows, sl]
        for h, sl in enumerate(heads):
            s_ref[i, h] = _hgrn_state_update(C, gs[h], k_ref[rows, sl], v_ref[rows, sl], s0_ref[i, h])
        return carry

    lax.fori_loop(0, q_ref.shape[0] // C, body, 0)


HG_CHUNK = 64
HG_SUB = 16
HG_SAFE_RANGE = 60.0


def _hgrn_chunk_masks(tt):
    import numpy as np
    t = np.arange(tt)[:, None]
    s = np.arange(tt)[None, :]
    tri = (s <= t) & ((t // HG_CHUNK) == (s // HG_CHUNK))
    return jnp.asarray(tri.astype(np.float32), BF16)


def _hgrn_prompt_kernel(layer_j, x_ref, win_ref, lbl_ref, ng_ref, m_ref, wout_ref, lng_ref, lnb_ref,
                        xo_ref, s_ref,
                        q_ref, k_ref, lf_ref, v_ref, g_ref, o_ref,
                        g_s, qi_s, ks_s, qd_s, kd_s, q1_s, k1_s, q2_s, k2_s, a_s, flag_s):
    tt = x_ref.shape[0]

    @pl.when(pl.program_id(1) == 0)
    def _():
        s_ref[...] = jnp.zeros(s_ref.shape, F32)

    _hgrn_in_compute(layer_j, x_ref[...].astype(BF16), win_ref, lbl_ref, q_ref, k_ref, lf_ref, v_ref, g_ref)

    lf = lf_ref[...] * LOG2E
    hi = lf.astype(BF16)
    lo = (lf - hi.astype(F32)).astype(BF16)
    g_s[...] = (jnp.dot(m_ref[...], hi, preferred_element_type=F32)
                + jnp.dot(m_ref[...], lo, preferred_element_type=F32))

    C = HG_CHUNK
    SB = HG_SUB
    worst = jnp.zeros((SB, D_MODEL), F32)
    for blk in range(tt // SB):
        r0 = blk * SB
        c0 = (r0 // C) * C
        rows = slice(r0, r0 + SB)

        def grow(r):
            return g_s[r - 1:r, :] if r > c0 else jnp.zeros((1, D_MODEL), F32)

        G = g_s[rows, :]
        q = q_ref[rows, :]
        k = k_ref[rows, :]
        qi_s[rows, :] = (q * jnp.exp2(G)).astype(BF16)
        ks_s[rows, :] = (k * jnp.exp2(g_s[c0 + C - 1:c0 + C, :] - G)).astype(BF16)
        e0 = G - grow(r0)
        qd_s[rows, :] = (q * jnp.exp2(e0)).astype(BF16)
        kd_s[rows, :] = (k * jnp.exp2(-e0)).astype(BF16)
        worst = jnp.maximum(worst, -e0)
        e1 = G - grow((r0 // (2 * SB)) * (2 * SB) + SB)
        q1_s[rows, :] = (q * jnp.exp2(jnp.minimum(e1, 0.0))).astype(BF16)
        k1_s[rows, :] = (k * jnp.exp2(jnp.minimum(-e1, 0.0))).astype(BF16)
        e2 = G - grow(c0 + C // 2)
        q2_s[rows, :] = (q * jnp.exp2(jnp.minimum(e2, 0.0))).astype(BF16)
        k2_s[rows, :] = (k * jnp.exp2(jnp.minimum(-e2, 0.0))).astype(BF16)
    flag_s[0] = (jnp.max(worst) <= HG_SAFE_RANGE * LOG2E).astype(I32)

    def finish(rows, sl, o):
        ms = jnp.mean(o * o, axis=-1, keepdims=True)
        o_ref[rows, sl] = o * lax.rsqrt(ms + LN_EPS) * ng_ref[:, sl] * g_ref[rows, sl]

    C = HG_CHUNK
    ti = lax.broadcasted_iota(I32, (C, C), 0)
    si = lax.broadcasted_iota(I32, (C, C), 1)
    causal = si <= ti
    sub_bits = HG_SUB.bit_length() - 1
    m0 = causal & ((ti >> sub_bits) == (si >> sub_bits))
    m1 = causal & ((ti >> (sub_bits + 1)) == (si >> (sub_bits + 1))) & jnp.logical_not(m0)
    nt_dims = (((1,), (1,)), ((), ()))

    @pl.when(flag_s[0] == 1)
    def _():
        def body(c, carry):
            r0 = pl.multiple_of(c * C, C)
            rows = pl.ds(r0, C)
            heads = [slice(h * LANES, (h + 1) * LANES) for h in range(HG_HEADS)]
            for h, sl in enumerate(heads):
                a0 = lax.dot_general(qd_s[rows, sl], kd_s[rows, sl], nt_dims, preferred_element_type=F32)
                a1 = lax.dot_general(q1_s[rows, sl], k1_s[rows, sl], nt_dims, preferred_element_type=F32)
                a2 = lax.dot_general(q2_s[rows, sl], k2_s[rows, sl], nt_dims, preferred_element_type=F32)
                a = jnp.where(m0, a0, jnp.where(m1, a1, jnp.where(causal, a2, 0.0)))
                a_s[h] = a.astype(BF16)
            for h, sl in enumerate(heads):
                o = (jnp.dot(qi_s[rows, sl], s_ref[0, h].astype(BF16), preferred_element_type=F32)
                     + jnp.dot(a_s[h], v_ref[rows, sl].astype(BF16), preferred_element_type=F32))
                finish(rows, sl, o)
            for h, sl in enumerate(heads):
                gend = g_s[pl.ds(pl.multiple_of(r0 + C - SUBLANES, SUBLANES), SUBLANES), sl][SUBLANES - 1:]
                ecol = jnp.transpose(jnp.broadcast_to(jnp.exp2(gend), (HG_DK, LANES)))
                upd = lax.dot_general(ks_s[rows, sl], v_ref[rows, sl].astype(BF16), (((0,), (0,)), ((), ())),
                                      preferred_element_type=F32)
                s_ref[0, h] = ecol * s_ref[0, h] + upd
            return carry

        lax.fori_loop(0, tt // C, body, 0)

    @pl.when(flag_s[0] == 0)
    def _():
        Cs = HG_CHUNK_PROMPT

        def body(i, carry):
            r0 = pl.multiple_of(i * Cs, Cs)
            rows = pl.ds(r0, Cs)
            for h in range(HG_HEADS):
                sl = slice(h * LANES, (h + 1) * LANES)
                o, s_new = _hgrn_chunk_head(Cs, q_ref[rows, sl], k_ref[rows, sl], lf_ref[rows, sl],
                                            v_ref[rows, sl], s_ref[0, h])
                finish(rows, sl, o)
                s_ref[0, h] = s_new
            return carry

        lax.fori_loop(0, tt // Cs, body, 0)

    y = jnp.dot(o_ref[...].astype(BF16), wout_ref[...], preferred_element_type=F32)
    xo_ref[...] = _layer_norm_rows(ALPHA * x_ref[...] + y, lng_ref[...], lnb_ref[...])


def hgrn_prompt_layer(x, w_in_bf, lb_logits, norm_g, w_out_bf, ln_g, ln_b, s_all, layer_j, batch, seq, n,
                      tt=256):
    nt = seq // tt
    row = pl.BlockSpec((tt, D_MODEL), lambda b, t: (b * nt + t, 0))
    vec = pl.BlockSpec((1, D_MODEL), lambda b, t: (0, 0))
    s_spec = pl.BlockSpec((None, 1, HG_HEADS, HG_DK, HG_DV), lambda b, t: (layer_j, b, 0, 0, 0))
    in_specs = [row,
                _layer_block(w_in_bf, layer_j, (D_MODEL, 4 * D_MODEL), lambda b, t: (0, 0)),
                pl.BlockSpec(lb_logits.shape, lambda b, t: (0, 0)),
                vec,
                pl.BlockSpec((tt, tt), lambda b, t: (0, 0)),
                _layer_block(w_out_bf, layer_j, (D_MODEL, D_MODEL), lambda b, t: (0, 0)),
                vec, vec]
    args = [x, w_in_bf, lb_logits, norm_g, _hgrn_chunk_masks(tt), w_out_bf, ln_g, ln_b]
    aliases = {}
    kern = functools.partial(_hgrn_prompt_kernel, layer_j)
    if s_all is not None:
        def kern(*refs):
            _hgrn_prompt_kernel(layer_j, *refs[:8], *refs[9:])
        in_specs.append(pl.BlockSpec(memory_space=pl.ANY))
        args.append(s_all)
        aliases = {8: 1}
    f32_tile = pltpu.VMEM((tt, D_MODEL), F32)
    bf_tile = pltpu.VMEM((tt, D_MODEL), BF16)
    return pl.pallas_call(
        kern,
        grid=(batch, nt),
        in_specs=in_specs,
        out_specs=[row, s_spec],
        out_shape=[jax.ShapeDtypeStruct((n, D_MODEL), F32),
                   jax.ShapeDtypeStruct((2, batch, HG_HEADS, HG_DK, HG_DV), F32)],
        scratch_shapes=[f32_tile] * 7 + [bf_tile] * 8
                       + [pltpu.VMEM((HG_HEADS, HG_CHUNK, HG_CHUNK), BF16), pltpu.SMEM((1,), I32)],
        input_output_aliases=aliases,
        compiler_params=_cparams(("arbitrary", "arbitrary")),
        name="hgrn_prompt_layer",
    )(*args)


def hgrn_rec_sample(q, k, lf, v, g, norm_g, state, s_all, layer_j, nseq, seq, sg=8):
    n = q.shape[0]
    rt = sg * seq
    row = pl.BlockSpec((rt, D_MODEL), lambda i: (i, 0))
    s_in = pl.BlockSpec((None, sg, HG_HEADS, HG_DK, HG_DV), lambda i: (layer_j, i, 0, 0, 0))
    in_specs = [row] * 5 + [pl.BlockSpec((1, D_MODEL), lambda i: (0, 0)), s_in]
    args = [q, k, lf, v, g, norm_g, state]
    aliases = {}
    if s_all is not None:
        in_specs.append(pl.BlockSpec(memory_space=pl.ANY))
        args.append(s_all)
        aliases[7] = 1

    def kern(*refs):
        _hgrn_rec_kernel(seq, *refs[:7], *refs[len(args):])

    return pl.pallas_call(
        kern,
        grid=(nseq // sg,),
        in_specs=in_specs,
        out_specs=[row, s_in],
        out_shape=[jax.ShapeDtypeStruct((n, D_MODEL), F32),
                   jax.ShapeDtypeStruct((2, nseq, HG_HEADS, HG_DK, HG_DV), F32)],
        input_output_aliases=aliases,
        compiler_params=_cparams(("arbitrary",)),
        name="hgrn_rec_sample",
    )(*args)


def _proj_ln_kernel(a_ref, w_ref, x_ref, g_ref, b_ref, o_ref):
    y = jnp.dot(a_ref[...].astype(BF16), w_ref[...], preferred_element_type=F32)
    o_ref[...] = _layer_norm_rows(ALPHA * x_ref[...] + y, g_ref[...], b_ref[...])


def proj_ln(a, w_bf, x, x_row0, g, b, xo, row0, layer=0, tm=512):
    n = xo.shape[0]
    nrows = a.shape[0]
    base = row0 // tm
    xbase = x_row0 // tm
    row = pl.BlockSpec((tm, D_MODEL), lambda i: (base + i, 0))
    vec = pl.BlockSpec((1, D_MODEL), lambda i: (0, 0))

    def kern(a_ref, w_ref, x_ref, g_ref, b_ref, _, o_ref):
        _proj_ln_kernel(a_ref, w_ref, x_ref, g_ref, b_ref, o_ref)

    return pl.pallas_call(
        kern,
        grid=(nrows // tm,),
        in_specs=[pl.BlockSpec((tm, D_MODEL), lambda i: (i, 0)),
                  _layer_block(w_bf, layer, (D_MODEL, D_MODEL), lambda i: (0, 0)),
                  pl.BlockSpec((tm, D_MODEL), lambda i: (xbase + i, 0)), vec, vec,
                  pl.BlockSpec(memory_space=pl.ANY)],
        out_specs=row,
        out_shape=jax.ShapeDtypeStruct((n, D_MODEL), F32),
        input_output_aliases={5: 0},
        compiler_params=_cparams(("arbitrary",)),
        name="proj_ln",
    )(a, w_bf, x, g, b, xo)


def _ffn_kernel(x_ref, wg_ref, wu_ref, wd_ref, g_ref, b_ref, o_ref):
    xb = x_ref[...].astype(BF16)
    gg = jnp.dot(xb, wg_ref[...], preferred_element_type=F32)
    uu = jnp.dot(xb, wu_ref[...], preferred_element_type=F32)
    h = (gg * _sigmoid(gg) * uu).astype(BF16)
    y = jnp.dot(h, wd_ref[...], preferred_element_type=F32)
    o_ref[...] = _layer_norm_rows(ALPHA * x_ref[...] + y, g_ref[...], b_ref[...])


def ffn_ln(x, w_gu_bf, w_down_bf, g, b, layer=0, tm=512):
    n = x.shape[0]
    row = pl.BlockSpec((tm, D_MODEL), lambda i: (i, 0))
    vec = pl.BlockSpec((1, D_MODEL), lambda i: (0, 0))
    once = pl.Buffered(1)
    return pl.pallas_call(
        _ffn_kernel,
        grid=(n // tm,),
        in_specs=[row,
                  _layer_block(w_gu_bf, layer, (D_MODEL, D_FF), lambda i: (0, 0), pipeline_mode=once),
                  _layer_block(w_gu_bf, layer, (D_MODEL, D_FF), lambda i: (0, 1), pipeline_mode=once),
                  _layer_block(w_down_bf, layer, (D_FF, D_MODEL), lambda i: (0, 0), pipeline_mode=once),
                  vec, vec],
        out_specs=row,
        out_shape=jax.ShapeDtypeStruct((n, D_MODEL), F32),
        compiler_params=_cparams(("arbitrary",)),
        name="ffn_ln",
    )(x, w_gu_bf, w_gu_bf, w_down_bf, g, b)


def _lru_in_kernel(x_ref, w_ref, xb_ref, gate_ref):
    xb = x_ref[...].astype(BF16)
    xb_ref[...] = jnp.dot(xb, w_ref[:, :D_MODEL], preferred_element_type=F32)
    y = jnp.dot(xb, w_ref[:, D_MODEL:], preferred_element_type=F32)
    gate_ref[...] = _gelu_tanh(y)


def lru_in(x, w_in_bf, row0, nrows, layer=0, tm=512):
    base = row0 // tm
    row = pl.BlockSpec((tm, D_MODEL), lambda i: (i, 0))
    out = jax.ShapeDtypeStruct((nrows, D_MODEL), F32)
    return pl.pallas_call(
        _lru_in_kernel,
        grid=(nrows // tm,),
        in_specs=[pl.BlockSpec((tm, D_MODEL), lambda i: (base + i, 0)),
                  _layer_block(w_in_bf, layer, (D_MODEL, 2 * D_MODEL), lambda i: (0, 0))],
        out_specs=[row, row],
        out_shape=[out, out],
        compiler_params=_cparams(("arbitrary",)),
        name="lru_in",
    )(x, w_in_bf)


def _lru_rec_kernel(L, carry, *refs):
    if carry:
        (x_ref, win_ref, cw_ref, cb_ref, wr_ref, br_ref, wi_ref, bi_ref, lam_ref, wout_ref, lng_ref, lnb_ref,
         cgu_ref, cdn_ref, xo_ref, hn_ref, cn_ref, cgu_o, cdn_o, xp_ref, hs_ref, hc_ref, gate_ref, hg_ref) = refs
        R = x_ref.shape[0]
        cgu_o[...] = cgu_ref[...].astype(BF16)
        cdn_o[...] = cdn_ref[...].astype(BF16)
    else:
        (xb_ref, gate_ref, cc_ref, h0_ref, cw_ref, cb_ref, wr_ref, br_ref, wi_ref, bi_ref, lam_ref,
         hg_ref, hn_ref, cn_ref, xp_ref, hs_ref) = refs
        R = xb_ref.shape[0]
    S = R // L
    PADR = SUBLANES
    t = pl.program_id(1) if carry else 0

    if carry:
        @pl.when(t == 0)
        def _():
            xp_ref[:, 0:PADR, :] = jnp.zeros((S, PADR, D_MODEL), F32)
            hc_ref[...] = jnp.zeros(hc_ref.shape, F32)

        @pl.when(t > 0)
        def _():
            xp_ref[:, 0:PADR, :] = xp_ref[:, L:L + PADR, :]

        xin = x_ref[...].astype(BF16)
        xp_ref[:, PADR:PADR + L, :] = jnp.dot(xin, win_ref[:, :D_MODEL],
                                              preferred_element_type=F32).reshape(S, L, D_MODEL)
        gate_ref[...] = _gelu_tanh(jnp.dot(xin, win_ref[:, D_MODEL:], preferred_element_type=F32))
    else:
        xp_ref[:, 0:PADR, :] = cc_ref[...].reshape(S, PADR, D_MODEL)
        xp_ref[:, PADR:PADR + L, :] = xb_ref[...].reshape(S, L, D_MODEL)

    xc = cb_ref[...].reshape(1, 1, D_MODEL) + xp_ref[:, PADR:PADR + L, :] * cw_ref[CONV_W - 1:CONV_W, :].reshape(1, 1, D_MODEL)
    for j in range(CONV_W - 1):
        off = PADR - (CONV_W - 1) + j
        xc = xc + xp_ref[:, off:off + L, :] * cw_ref[j:j + 1, :].reshape(1, 1, D_MODEL)
    xc = xc.reshape(R, D_MODEL)
    cn_ref[...] = xp_ref[:, L + PADR - (CONV_W - 1):L + PADR, :].reshape(cn_ref.shape)

    assert L & (L - 1) == 0
    pos = jnp.bitwise_and(lax.broadcasted_iota(I32, (R, LRU_BW), 0), L - 1)
    for nb in range(LRU_BLOCKS):
        sl = slice(nb * LRU_BW, (nb + 1) * LRU_BW)
        xcb = xc[:, sl]
        xh = xcb.astype(BF16)
        r = _sigmoid(jnp.dot(xh, wr_ref[nb], preferred_element_type=F32) + br_ref[:, sl])
        ig = _sigmoid(jnp.dot(xh, wi_ref[nb], preferred_element_type=F32) + bi_ref[:, sl])
        z = -lam_ref[:, sl]
        softplus = jnp.maximum(z, 0.0) + jnp.log1p(jnp.exp(-jnp.abs(z)))
        log_a = (-RG_C) * r * softplus
        a = jnp.exp(log_a)
        u = jnp.sqrt(1.0 - a * a) * (ig * xcb)
        if carry:
            ng = R // SUBLANES
            u3 = u.reshape(ng, SUBLANES, LRU_BW)
            a3 = a.reshape(ng, SUBLANES, LRU_BW)
            pos8 = lax.broadcasted_iota(I32, (ng, SUBLANES, LRU_BW), 1)
            d = 1
            while d < SUBLANES:
                m = pos8 >= d
                u3 = jnp.where(m, a3 * pltpu.roll(u3, d, 1) + u3, u3)
                a3 = jnp.where(m, a3 * pltpu.roll(a3, d, 1), a3)
                d *= 2
            hprev = hc_ref[:, sl]
            for gi in range(ng):
                hgrp = u3[gi] + a3[gi] * hprev
                hs_ref[0, gi * SUBLANES:(gi + 1) * SUBLANES, sl] = hgrp
                hprev = hgrp[SUBLANES - 1:SUBLANES]
            hg_ref[:, sl] = hs_ref[0, :, sl] * gate_ref[:, sl]
        else:
            u = u + a * h0_ref[:, sl]
            d = 1
            while d < L:
                m = pos >= d
                u = jnp.where(m, a * pltpu.roll(u, d, 0) + u, u)
                a = jnp.where(m, a * pltpu.roll(a, d, 0), a)
                d *= 2
            hg_ref[:, sl] = u * gate_ref[:, sl]
            hs_ref[:, :, sl] = u.reshape(S, L, LRU_BW)

    hlast = hs_ref[:, L - 1:L, :]
    hn_ref[...] = hlast.reshape(hn_ref.shape)
    if carry:
        hc_ref[...] = hlast.reshape(1, D_MODEL)
        y = jnp.dot(hg_ref[...].astype(BF16), wout_ref[...], preferred_element_type=F32)
        xo_ref[...] = _layer_norm_rows(ALPHA * x_ref[...] + y, lng_ref[...], lnb_ref[...])


def _lru_weight_specs(wts, layer):
    def c2(*_):
        return (0, 0)

    def c3(*_):
        return (0, 0, 0)
    gate_shape = (LRU_BLOCKS, LRU_BW, LRU_BW)
    return [pl.BlockSpec((CONV_W, D_MODEL), c2), pl.BlockSpec((1, D_MODEL), c2),
            _layer_block(wts[2], layer, gate_shape, c3), pl.BlockSpec((1, D_MODEL), c2),
            _layer_block(wts[4], layer, gate_shape, c3), pl.BlockSpec((1, D_MODEL), c2),
            pl.BlockSpec((1, D_MODEL), c2)]


def lru_prompt_layer(x, w_in_bf, wts, w_out_bf, ln_g, ln_b, moe_gu, moe_down, layer_j, batch, seq, tt=256):
    n = x.shape[0]
    nt = seq // tt
    steps = batch * nt
    gu2 = moe_gu.reshape(moe_gu.shape[0], N_EXPERTS * D_MODEL, 2 * D_FF_EXPERT)
    dn2 = moe_down.reshape(moe_down.shape[0], N_EXPERTS * D_FF_EXPERT, D_MODEL)
    gu_rows = gu2.shape[1] // steps
    dn_rows = dn2.shape[1] // steps
    assert gu_rows * steps == gu2.shape[1] and dn_rows * steps == dn2.shape[1]
    row = pl.BlockSpec((tt, D_MODEL), lambda b, t: (b * nt + t, 0))
    vec = pl.BlockSpec((1, D_MODEL), lambda b, t: (0, 0))
    xo, hn, cn, gu_bf, dn_bf = pl.pallas_call(
        functools.partial(_lru_rec_kernel, tt, True),
        grid=(batch, nt),
        in_specs=[row, _layer_block(w_in_bf, layer_j, (D_MODEL, 2 * D_MODEL), lambda b, t: (0, 0))]
                 + _lru_weight_specs(wts, layer_j)
                 + [_layer_block(w_out_bf, layer_j, (D_MODEL, D_MODEL), lambda b, t: (0, 0)), vec, vec,
                    pl.BlockSpec((None, gu_rows, 2 * D_FF_EXPERT), lambda b, t: (layer_j, b * nt + t, 0)),
                    pl.BlockSpec((None, dn_rows, D_MODEL), lambda b, t: (layer_j, b * nt + t, 0))],
        out_specs=[row,
                   pl.BlockSpec((1, 1, D_MODEL), lambda b, t: (b, 0, 0)),
                   pl.BlockSpec((1, CONV_W - 1, D_MODEL), lambda b, t: (b, 0, 0)),
                   pl.BlockSpec((gu_rows, 2 * D_FF_EXPERT), lambda b, t: (b * nt + t, 0)),
                   pl.BlockSpec((dn_rows, D_MODEL), lambda b, t: (b * nt + t, 0))],
        out_shape=[jax.ShapeDtypeStruct((n, D_MODEL), F32),
                   jax.ShapeDtypeStruct((batch, 1, D_MODEL), F32),
                   jax.ShapeDtypeStruct((batch, CONV_W - 1, D_MODEL), F32),
                   jax.ShapeDtypeStruct(gu2.shape[1:], BF16),
                   jax.ShapeDtypeStruct(dn2.shape[1:], BF16)],
        scratch_shapes=[pltpu.VMEM((1, tt + SUBLANES, D_MODEL), F32),
                        pltpu.VMEM((1, tt, D_MODEL), F32),
                        pltpu.VMEM((1, D_MODEL), F32),
                        pltpu.VMEM((tt, D_MODEL), F32),
                        pltpu.VMEM((tt, D_MODEL), F32)],
        compiler_params=_cparams(("arbitrary", "arbitrary")),
        name="lru_prompt_layer",
    )(x, w_in_bf, *wts, w_out_bf, ln_g, ln_b, gu2, dn2)
    return (xo, hn, cn, gu_bf.reshape(N_EXPERTS, D_MODEL, 2 * D_FF_EXPERT),
            dn_bf.reshape(N_EXPERTS, D_FF_EXPERT, D_MODEL))


def lru_rec_sample(xb, gate, cc_rows, h0_rows, wts, nseq, seq, layer=0, sg=32):
    n = xb.shape[0]
    rt = sg * seq
    row = pl.BlockSpec((rt, D_MODEL), lambda i: (i, 0))
    return pl.pallas_call(
        functools.partial(_lru_rec_kernel, seq, False),
        grid=(nseq // sg,),
        in_specs=[row, row, pl.BlockSpec((sg * SUBLANES, D_MODEL), lambda i: (i, 0)), row]
                 + _lru_weight_specs(wts, layer),
        out_specs=[row,
                   pl.BlockSpec((sg, 1, D_MODEL), lambda i: (i, 0, 0)),
                   pl.BlockSpec((sg, CONV_W - 1, D_MODEL), lambda i: (i, 0, 0))],
        out_shape=[jax.ShapeDtypeStruct((n, D_MODEL), F32),
                   jax.ShapeDtypeStruct((nseq, 1, D_MODEL), F32),
                   jax.ShapeDtypeStruct((nseq, CONV_W - 1, D_MODEL), F32)],
        scratch_shapes=[pltpu.VMEM((sg, seq + SUBLANES, D_MODEL), F32),
                        pltpu.VMEM((sg, seq, D_MODEL), F32)],
        compiler_params=_cparams(("arbitrary",)),
        name="lru_rec_sample",
    )(xb, gate, cc_rows, h0_rows, *wts)


def _router_kernel(x_ref, w_ref, mi_ref, mf_ref, cnt_ref, run_ref):
    i = pl.program_id(0)

    @pl.when(i == 0)
    def _():
        run_ref[...] = jnp.zeros(run_ref.shape, F32)

    tm = x_ref.shape[0]
    x = x_ref[...]
    w = w_ref[...]
    xh = x.astype(BF16)
    xl = (x - xh.astype(F32)).astype(BF16)
    wh = w.astype(BF16)
    wl = (w - wh.astype(F32)).astype(BF16)
    hh_hl = jnp.dot(xh, jnp.concatenate([wh, wl], axis=1), preferred_element_type=F32)
    logits = hh_hl[:, :LANES] + hh_hl[:, LANES:] + jnp.dot(xl, wh, preferred_element_type=F32)
    lane_i = lax.broadcasted_iota(I32, (tm, LANES), 1)
    lane = lane_i.astype(F32)
    neg = jnp.float32(-jnp.inf)
    logits = jnp.where(lane_i < N_EXPERTS, logits, neg)
    m1 = jnp.max(logits, axis=-1, keepdims=True)
    i1 = jnp.min(jnp.where(logits == m1, lane, float(LANES)), axis=-1, keepdims=True)
    l2 = jnp.where(lane == i1, neg, logits)
    m2 = jnp.max(l2, axis=-1, keepdims=True)
    i2 = jnp.min(jnp.where(l2 == m2, lane, float(LANES)), axis=-1, keepdims=True)
    e2 = jnp.exp(m2 - m1)
    p1 = 1.0 / (1.0 + e2)
    p2 = e2 * p1

    hit1 = lane == i1
    hit2 = lane == i2
    onehot = jnp.where(hit1 | hit2, 1.0, 0.0)
    r_i = lax.broadcasted_iota(I32, (tm, tm), 0)
    c_i = lax.broadcasted_iota(I32, (tm, tm), 1)
    tri = jnp.where(c_i < r_i, 1.0, 0.0).astype(BF16)
    rank = jnp.dot(tri, onehot.astype(BF16), preferred_element_type=F32) + run_ref[0:1, :]
    r1 = jnp.sum(jnp.where(hit1, rank, 0.0), axis=-1, keepdims=True)
    r2 = jnp.sum(jnp.where(hit2, rank, 0.0), axis=-1, keepdims=True)
    total = run_ref[0:1, :] + jnp.sum(onehot, axis=0, keepdims=True)
    run_ref[...] = jnp.broadcast_to(total, run_ref.shape)
    cnt_ref[...] = jnp.broadcast_to(total, cnt_ref.shape).astype(I32)

    mi = jnp.where(lane_i == 0, i1, jnp.where(lane_i == 1, i2, 0.0))
    mi = jnp.where(lane_i == 2, r1, jnp.where(lane_i == 3, r2, mi))
    mi_ref[...] = jnp.transpose(mi)[0:SUBLANES, :].astype(I32)
    mf_ref[...] = jnp.where(lane_i == 0, p1, jnp.where(lane_i == 1, p2, 0.0))


def moe_router(x, w_router_pad, tm=512):
    n = x.shape[0]
    row = pl.BlockSpec((tm, LANES), lambda i: (i, 0))
    return pl.pallas_call(
        _router_kernel,
        grid=(n // tm,),
        in_specs=[pl.BlockSpec((tm, D_MODEL), lambda i: (i, 0)),
                  pl.BlockSpec((D_MODEL, LANES), lambda i: (0, 0))],
        out_specs=[pl.BlockSpec((SUBLANES, tm), lambda i: (0, i)), row,
                   pl.BlockSpec((SUBLANES, LANES), lambda i: (0, 0))],
        out_shape=[jax.ShapeDtypeStruct((SUBLANES, n), I32),
                   jax.ShapeDtypeStruct((n, LANES), F32),
                   jax.ShapeDtypeStruct((SUBLANES, LANES), I32)],
        scratch_shapes=[pltpu.VMEM((SUBLANES, LANES), F32)],
        compiler_params=_cparams(("arbitrary",)),
        name="moe_router",
    )(x, w_router_pad)


ZERO_ROWS = 256


def _dispatch_kernel(dest_ref, ztile_ref, x_ref, xs_hbm, zero_ref, sem):
    i = pl.program_id(0)
    tm = x_ref.shape[0]
    n_tok = pl.num_programs(0) * tm

    @pl.when(i == 0)
    def _():
        zero_ref[...] = jnp.zeros(zero_ref.shape, F32)
        copies = [pltpu.make_async_copy(
                      zero_ref,
                      xs_hbm.at[pl.ds(pl.multiple_of(ztile_ref[e] + c * ZERO_ROWS, ZERO_ROWS), ZERO_ROWS)], sem)
                  for e in range(N_EXPERTS) for c in range(MOE_TILE // ZERO_ROWS)]
        for cp in copies:
            cp.start()
        for cp in copies:
            cp.wait()

    def group(jj, c):
        r8 = pl.multiple_of(jj * SUBLANES, SUBLANES)
        grp = x_ref.at[pl.ds(r8, SUBLANES)]
        t0 = i * tm + r8
        for u in range(SUBLANES):
            src = grp.at[pl.ds(u, 1)]
            pltpu.make_async_copy(src, xs_hbm.at[pl.ds(dest_ref[t0 + u], 1)], sem).start()
            pltpu.make_async_copy(src, xs_hbm.at[pl.ds(dest_ref[n_tok + t0 + u], 1)], sem).start()
        return c

    lax.fori_loop(0, tm // SUBLANES, group, 0)
    for _ in range(2):
        pltpu.make_async_copy(x_ref, xs_hbm.at[pl.ds(0, tm)], sem).wait()


def moe_dispatch(dest, ztile, x, n_rows, tm=512):
    n = x.shape[0]
    return pl.pallas_call(
        _dispatch_kernel,
        grid_spec=pltpu.PrefetchScalarGridSpec(
            num_scalar_prefetch=2, grid=(n // tm,),
            in_specs=[pl.BlockSpec((tm, D_MODEL), lambda i, d, z: (i, 0))],
            out_specs=pl.BlockSpec(memory_space=pl.ANY),
            scratch_shapes=[pltpu.VMEM((ZERO_ROWS, D_MODEL), F32), pltpu.SemaphoreType.DMA(())]),
        out_shape=jax.ShapeDtypeStruct((n_rows, D_MODEL), F32),
        compiler_params=pltpu.CompilerParams(dimension_semantics=("arbitrary",),
                                             has_side_effects=True),
        name="moe_dispatch",
    )(dest, ztile, x)


def _expert_kernel(te_ref, nu_ref, xs_ref, wg_ref, wu_ref, wd_ref, ys_ref, acc_ref):
    i = pl.program_id(0)
    f = pl.program_id(1)
    used = i < nu_ref[0]

    @pl.when(used)
    def _():
        xb = xs_ref[...].astype(BF16)
        gg = jnp.dot(xb, wg_ref[...], preferred_element_type=F32)
        uu = jnp.dot(xb, wu_ref[...], preferred_element_type=F32)
        h = (gg * _sigmoid(gg) * uu).astype(BF16)
        part = jnp.dot(h, wd_ref[...], preferred_element_type=F32)

        @pl.when(f == 0)
        def _():
            acc_ref[...] = part

        @pl.when(f > 0)
        def _():
            acc_ref[...] += part

    @pl.when(f == pl.num_programs(1) - 1)
    def _():
        @pl.when(used)
        def _():
            ys_ref[...] = acc_ref[...]

        @pl.when(jnp.logical_not(used))
        def _():
            ys_ref[...] = jnp.zeros(ys_ref.shape, F32)


def moe_experts(tile_expert, n_used, xs, w_gu_bf, w_down_bf, tf=1792):
    n_rows = xs.shape[0]
    tm = MOE_TILE
    nf = D_FF_EXPERT // tf

    def xrow(i, f, te, nu):
        return (jnp.minimum(i, nu[0] - 1), 0)

    def wg(i, f, te, nu):
        return (te[i], 0, jnp.where(i < nu[0], f, nf - 1))

    def wu(i, f, te, nu):
        return (te[i], 0, nf + jnp.where(i < nu[0], f, nf - 1))

    def wd(i, f, te, nu):
        return (te[i], jnp.where(i < nu[0], f, nf - 1), 0)

    return pl.pallas_call(
        _expert_kernel,
        grid_spec=pltpu.PrefetchScalarGridSpec(
            num_scalar_prefetch=2, grid=(n_rows // tm, nf),
            in_specs=[pl.BlockSpec((tm, D_MODEL), xrow),
                      pl.BlockSpec((None, D_MODEL, tf), wg),
                      pl.BlockSpec((None, D_MODEL, tf), wu),
                      pl.BlockSpec((None, tf, D_MODEL), wd)],
            out_specs=pl.BlockSpec((tm, D_MODEL), lambda i, f, te, nu: (i, 0)),
            scratch_shapes=[pltpu.VMEM((tm, D_MODEL), F32)]),
        out_shape=jax.ShapeDtypeStruct((n_rows, D_MODEL), F32),
        compiler_params=_cparams(("arbitrary", "arbitrary")),
        name="moe_experts",
    )(tile_expert, n_used, xs, w_gu_bf, w_gu_bf, w_down_bf)


def _combine_kernel(nsplit, dest_ref, ys_hbm, mf_ref, x_ref, g_ref, b_ref, *rest):
    if nsplit is None:
        o_ref, buf_ref, sem = rest
    else:
        oa_ref, ob_ref, buf_ref, sem = rest
    i = pl.program_id(0)
    n_steps = pl.num_programs(0)
    tm = x_ref.shape[0]
    n_tok = n_steps * tm

    def gather(step, slot):
        def group(jj, c):
            r8 = pl.multiple_of(jj * SUBLANES, SUBLANES)
            g0 = buf_ref.at[slot, 0, pl.ds(r8, SUBLANES)]
            g1 = buf_ref.at[slot, 1, pl.ds(r8, SUBLANES)]
            t0 = step * tm + r8
            for u in range(SUBLANES):
                pltpu.make_async_copy(ys_hbm.at[pl.ds(dest_ref[t0 + u], 1)], g0.at[pl.ds(u, 1)],
                                      sem.at[slot]).start()
                pltpu.make_async_copy(ys_hbm.at[pl.ds(dest_ref[n_tok + t0 + u], 1)], g1.at[pl.ds(u, 1)],
                                      sem.at[slot]).start()
            return c

        lax.fori_loop(0, tm // SUBLANES, group, 0)

    slot = lax.rem(i, 2)

    @pl.when(i == 0)
    def _():
        gather(i, slot)

    @pl.when(i + 1 < n_steps)
    def _():
        gather(i + 1, 1 - slot)

    for s in range(2):
        pltpu.make_async_copy(ys_hbm.at[pl.ds(0, tm)], buf_ref.at[slot, s], sem.at[slot]).wait()
    mf = mf_ref[...]
    y = buf_ref[slot, 0] * mf[:, 0:1] + buf_ref[slot, 1] * mf[:, 1:2]
    res = _layer_norm_rows(ALPHA * x_ref[...] + y, g_ref[...], b_ref[...])
    if nsplit is None:
        o_ref[...] = res
    else:
        @pl.when(i < nsplit)
        def _():
            oa_ref[...] = res

        @pl.when(i >= nsplit)
        def _():
            ob_ref[...] = res


def moe_combine_ln(dest, ys, mf, x, g, b, split_rows=None, tm=512):
    n = x.shape[0]
    row = pl.BlockSpec((tm, D_MODEL), lambda i, d: (i, 0))
    vec = pl.BlockSpec((1, D_MODEL), lambda i, d: (0, 0))
    if split_rows is None:
        nsplit = None
        out_specs = row
        out_shape = jax.ShapeDtypeStruct((n, D_MODEL), F32)
    else:
        nsplit = split_rows // tm
        out_specs = [pl.BlockSpec((tm, D_MODEL), lambda i, d: (jnp.minimum(i, nsplit - 1), 0)),
                     pl.BlockSpec((tm, D_MODEL), lambda i, d: (jnp.maximum(i - nsplit, 0), 0))]
        out_shape = [jax.ShapeDtypeStruct((split_rows, D_MODEL), F32),
                     jax.ShapeDtypeStruct((n - split_rows, D_MODEL), F32)]
    return pl.pallas_call(
        functools.partial(_combine_kernel, nsplit),
        grid_spec=pltpu.PrefetchScalarGridSpec(
            num_scalar_prefetch=1, grid=(n // tm,),
            in_specs=[pl.BlockSpec(memory_space=pl.ANY),
                      pl.BlockSpec((tm, LANES), lambda i, d: (i, 0)),
                      row, vec, vec],
            out_specs=out_specs,
            scratch_shapes=[pltpu.VMEM((2, 2, tm, D_MODEL), F32), pltpu.SemaphoreType.DMA((2,))]),
        out_shape=out_shape,
        compiler_params=_cparams(("arbitrary",)),
        name="moe_combine_ln",
    )(dest, ys, mf, x, g, b)


def moe_ffn_ln(x, w_router_pad, w_gu, w_down, g, b, split_rows=None):
    n = x.shape[0]
    tm = MOE_TILE
    mi, mf, cnt = moe_router(x, w_router_pad)
    counts = cnt[0, :N_EXPERTS]
    padded = (counts + tm - 1) // tm * tm
    ends = jnp.cumsum(padded)
    starts = ends - padded
    dest = (starts[mi[0:2]] + mi[2:4]).reshape(-1).astype(I32)
    n_tiles = (2 * n) // tm + N_EXPERTS
    tile_start = jnp.arange(n_tiles, dtype=I32) * tm
    tile_expert = jnp.minimum(jnp.sum(tile_start[:, None] >= ends[None, :], axis=1), N_EXPERTS - 1).astype(I32)
    n_used = (ends[-1] // tm).astype(I32).reshape(1)
    tile_expert = jnp.where(tile_start < ends[-1], tile_expert, tile_expert[jnp.maximum(n_used[0] - 1, 0)])
    ztile = jnp.maximum(ends - tm, 0).astype(I32)
    xs = moe_dispatch(dest, ztile, x, n_tiles * tm)
    ys = moe_experts(tile_expert, n_used, xs, w_gu, w_down)
    return moe_combine_ln(dest, ys, mf, x, g, b, split_rows)


def kernel(x_prompt, x_sample, state_hgrn, state_lru_h, state_lru_conv, ln_mix_g, ln_mix_b, ln_ffn_g, ln_ffn_b, w_hgrn_in, hgrn_lb_logits, hgrn_norm_g, w_hgrn_out, w_lru_in, lru_conv_w, lru_conv_b, w_lru_rgate, b_lru_rgate, w_lru_igate, b_lru_igate, lru_lambda, w_lru_out, w_ffn_gu, w_ffn_down, w_router, w_moe_gu, w_moe_down):
    bp, tp, _ = x_prompt.shape
    bs, ts, _ = x_sample.shape
    n_p = bp * tp
    n = n_p + bs * ts
    x = x_prompt.reshape(n_p, D_MODEL)
    x_smp, smp_row0 = x_sample.reshape(bs * ts, D_MODEL), 0

    def vec(a, i):
        return a[i].reshape(1, D_MODEL)

    w_hgrn_in_bf = w_hgrn_in.astype(BF16)
    w_hgrn_out_bf = w_hgrn_out.astype(BF16)
    w_lru_in_bf = w_lru_in.astype(BF16)
    w_lru_out_bf = w_lru_out.astype(BF16)
    w_r_bf = w_lru_rgate.astype(BF16)
    w_i_bf = w_lru_igate.astype(BF16)
    w_ffn_gu_bf = w_ffn_gu.astype(BF16)
    w_ffn_down_bf = w_ffn_down.astype(BF16)
    w_router_pad = jnp.pad(w_router, ((0, 0), (0, 0), (0, LANES - N_EXPERTS)))

    hg_p = hg_s = None
    h_p, h_s, c_p, c_s = [], [], [], []
    for layer in range(DEPTH):
        j = layer // 2
        lng, lnb = vec(ln_mix_g, layer), vec(ln_mix_b, layer)
        if layer % 2 == 0:
            ng = vec(hgrn_norm_g, j)
            xo, hg_p = hgrn_prompt_layer(x, w_hgrn_in_bf, hgrn_lb_logits, ng, w_hgrn_out_bf,
                                         lng, lnb, hg_p, j, bp, tp, n)
            q, k, lf, v, g = hgrn_in(x_smp, w_hgrn_in_bf, hgrn_lb_logits, j, smp_row0, bs * ts)
            o, hg_s = hgrn_rec_sample(q, k, lf, v, g, ng, state_hgrn, hg_s, j, bs, ts)
            x = proj_ln(o, w_hgrn_out_bf, x_smp, smp_row0, lng, lnb, xo, n_p, layer=j)
            x = ffn_ln(x, w_ffn_gu_bf, w_ffn_down_bf, vec(ln_ffn_g, layer), vec(ln_ffn_b, layer), layer=j)
            x_smp, smp_row0 = x, n_p
        else:
            wts = (lru_conv_w[j], vec(lru_conv_b, j), w_r_bf, vec(b_lru_rgate, j),
                   w_i_bf, vec(b_lru_igate, j), vec(lru_lambda, j))
            xo, hp, cp, moe_gu_bf, moe_down_bf = lru_prompt_layer(
                x, w_lru_in_bf, wts, w_lru_out_bf, lng, lnb, w_moe_gu, w_moe_down, j, bp, tp)
            xb, gate = lru_in(x_smp, w_lru_in_bf, smp_row0, bs * ts, layer=j)
            cc_rows = jnp.pad(state_lru_conv[j], ((0, 0), (SUBLANES - (CONV_W - 1), 0), (0, 0)))
            h0_rows = jnp.pad(state_lru_h[j][:, None, :], ((0, 0), (0, ts - 1), (0, 0)))
            hgate, hs, cs = lru_rec_sample(xb, gate, cc_rows.reshape(bs * SUBLANES, D_MODEL),
                                           h0_rows.reshape(bs * ts, D_MODEL), wts, bs, ts, layer=j)
            h_p.append(hp.reshape(bp, D_MODEL))
            h_s.append(hs.reshape(bs, D_MODEL))
            c_p.append(cp)
            c_s.append(cs)
            x = proj_ln(hgate, w_lru_out_bf, x_smp, smp_row0, lng, lnb, xo, n_p, layer=j)
            x = moe_ffn_ln(x, w_router_pad[j], moe_gu_bf, moe_down_bf,
                           vec(ln_ffn_g, layer), vec(ln_ffn_b, layer),
                           split_rows=n_p if layer == DEPTH - 1 else None)
            x_smp, smp_row0 = x, n_p

    y_prompt = x[0].reshape(bp, tp, D_MODEL)
    y_sample = x[1].reshape(bs, ts, D_MODEL)
    return (y_prompt, y_sample, hg_p, hg_s, jnp.stack(h_p), jnp.stack(h_s), jnp.stack(c_p), jnp.stack(c_s))
```

```python
import functools
import math

import jax
import jax.numpy as jnp
from jax import lax
from jax.experimental import pallas as pl
from jax.experimental.pallas import tpu as pltpu

F32 = jnp.float32
BF16 = jnp.bfloat16
I32 = jnp.int32

D_MODEL = 1024
DEPTH = 4
HG_DK = 128
HG_HEADS = 8
HG_DV = 128
CONV_W = 4
RG_C = 8.0
LRU_BLOCKS = 4
LRU_BW = 256
D_FF = 2816
N_EXPERTS = 8
D_FF_EXPERT = 3584
ALPHA = (2 * DEPTH) ** 0.25
LN_EPS = 1e-5
F_FLOOR = 1e-20
LOG2E = math.log2(math.e)

LANES = 128
SUBLANES = 8
VMEM_LIMIT = 56 * 1024 * 1024

HG_CHUNK_PROMPT = 16
MOE_TILE = 512


def _cparams(sem):
    return pltpu.CompilerParams(dimension_semantics=sem, vmem_limit_bytes=VMEM_LIMIT)


def _layer_block(w, layer, block_shape, index_map, **kw):
    if w.ndim == len(block_shape):
        return pl.BlockSpec(block_shape, index_map, **kw)
    return pl.BlockSpec((None,) + tuple(block_shape), lambda *a: (layer,) + tuple(index_map(*a)), **kw)


def _sigmoid(x):
    return 0.5 * jnp.tanh(0.5 * x) + 0.5


def _gelu_tanh(y):
    c = math.sqrt(2.0 / math.pi)
    return 0.5 * y * (1.0 + jnp.tanh(c * (y + 0.044715 * (y * y * y))))


def _layer_norm_rows(z, g, b):
    mu = jnp.mean(z, axis=-1, keepdims=True)
    zc = z - mu
    var = jnp.mean(zc * zc, axis=-1, keepdims=True)
    return zc * lax.rsqrt(var + LN_EPS) * g + b


def _hgrn_in_compute(layer_j, xb, w_ref, lbl_ref, q_ref, k_ref, lf_ref, v_ref, g_ref):
    l = lbl_ref[...]
    e = jnp.exp(l - jnp.max(l, axis=0, keepdims=True))
    p = e / jnp.sum(e, axis=0, keepdims=True)
    cs = p[0:1]
    for i in range(1, layer_j + 1):
        cs = cs + p[i:i + 1]
    lb = cs - p[0:1]

    def seg(i):
        return jnp.dot(xb, w_ref[:, i * D_MODEL:(i + 1) * D_MODEL], preferred_element_type=F32)

    q = seg(0)
    q_ref[...] = q * _sigmoid(q) * (HG_DK ** -0.5)
    fp = seg(1)
    sig = 1.0 / (1.0 + jnp.exp(-fp))
    f = lb + (1.0 - lb) * sig
    lf_ref[...] = jnp.log(jnp.maximum(f, F_FLOOR))
    k_ref[...] = (1.0 - lb) * (1.0 - sig)
    v_ref[...] = seg(2)
    g = seg(3)
    g_ref[...] = g * _sigmoid(g)


def _hgrn_in_kernel(layer_j, x_ref, w_ref, lbl_ref, *out_refs):
    _hgrn_in_compute(layer_j, x_ref[...].astype(BF16), w_ref, lbl_ref, *out_refs)


def hgrn_in(x, w_in_bf, lb_logits, layer_j, row0, nrows, tm=256):
    base = row0 // tm
    out = jax.ShapeDtypeStruct((nrows, D_MODEL), F32)
    return pl.pallas_call(
        functools.partial(_hgrn_in_kernel, layer_j),
        grid=(nrows // tm,),
        in_specs=[pl.BlockSpec((tm, D_MODEL), lambda i: (base + i, 0)),
                  _layer_block(w_in_bf, layer_j, (D_MODEL, 4 * D_MODEL), lambda i: (0, 0)),
                  pl.BlockSpec(lb_logits.shape, lambda i: (0, 0))],
        out_specs=[pl.BlockSpec((tm, D_MODEL), lambda i: (i, 0))] * 5,
        out_shape=[out] * 5,
        compiler_params=_cparams(("arbitrary",)),
        name="hgrn_in",
    )(x, w_in_bf, lb_logits)


def _hgrn_prefix(C, lf):
    row = lax.broadcasted_iota(I32, (C, LANES), 0)
    G = lf
    d = 1
    while d < C:
        G = G + jnp.where(row >= d, pltpu.roll(G, d, 0), 0.0)
        d *= 2
    return G


def _hgrn_diag(C, G, q, k, v):
    row8 = lax.broadcasted_iota(I32, (SUBLANES, LANES), 0)
    parts = []
    for gi in range(C // SUBLANES):
        lo = gi * SUBLANES
        Gg = G[lo:lo + SUBLANES]
        qg = q[lo:lo + SUBLANES]
        acc = jnp.zeros((SUBLANES, LANES), F32)
        for s in range(lo + SUBLANES):
            dec = jnp.exp(Gg - G[s:s + 1])
            if s >= lo:
                dec = jnp.where(row8 >= (s - lo), dec, 0.0)
            w = jnp.sum(qg * dec * k[s:s + 1], axis=-1, keepdims=True)
            acc = acc + w * v[s:s + 1]
        parts.append(acc)
    return parts[0] if len(parts) == 1 else jnp.concatenate(parts, axis=0)


def _hgrn_state_update(C, G, k, v, S):
    gend = G[C - 1:C, :]
    kt = k * jnp.exp(gend - G)
    ecol = jnp.transpose(jnp.broadcast_to(jnp.exp(gend), (HG_DK, LANES)))
    upd = lax.dot_general(kt.astype(BF16), v.astype(BF16), (((0,), (0,)), ((), ())),
                          preferred_element_type=F32)
    return ecol * S + upd


def _hgrn_chunk_head(C, q, k, lf, v, S):
    G = _hgrn_prefix(C, lf)
    o = jnp.dot((q * jnp.exp(G)).astype(BF16), S.astype(BF16), preferred_element_type=F32)
    return o + _hgrn_diag(C, G, q, k, v), _hgrn_state_update(C, G, k, v, S)


def _hgrn_rec_kernel(C, q_ref, k_ref, lf_ref, v_ref, g_ref, ng_ref, s0_ref, o_ref, s_ref):
    heads = [slice(h * LANES, (h + 1) * LANES) for h in range(HG_HEADS)]

    def body(i, carry):
        rows = pl.ds(pl.multiple_of(i * C, C), C)
        gs, os_ = [], []
        for h, sl in enumerate(heads):
            G = _hgrn_prefix(C, lf_ref[rows, sl])
            gs.append(G)
            os_.append(jnp.dot((q_ref[rows, sl] * jnp.exp(G)).astype(BF16), s0_ref[i, h].astype(BF16),
                               preferred_element_type=F32))
        for h, sl in enumerate(heads):
            o = os_[h] + _hgrn_diag(C, gs[h], q_ref[rows, sl], k_ref[rows, sl], v_ref[rows, sl])
            ms = jnp.mean(o * o, axis=-1, keepdims=True)
            o_ref[rows, sl] = o * lax.rsqrt(ms + LN_EPS) * ng_ref[:, sl] * g_ref[rows, sl]
        for h, sl in enumerate(heads):
            s_ref[i, h] = _hgrn_state_update(C, gs[h], k_ref[rows, sl], v_ref[rows, sl], s0_ref[i, h])
        return carry

    lax.fori_loop(0, q_ref.shape[0] // C, body, 0)


HG_CHUNK = 64
HG_SUB = 16
HG_SAFE_RANGE = 60.0


def _hgrn_chunk_masks(tt):
    import numpy as np
    t = np.arange(tt)[:, None]
    s = np.arange(tt)[None, :]
    tri = (s <= t) & ((t // HG_CHUNK) == (s // HG_CHUNK))
    return jnp.asarray(tri.astype(np.float32), BF16)


def _hgrn_prompt_kernel(layer_j, x_ref, win_ref, lbl_ref, ng_ref, m_ref, wout_ref, lng_ref, lnb_ref,
                        xo_ref, s_ref,
                        q_ref, k_ref, lf_ref, v_ref, g_ref, o_ref,
                        g_s, qi_s, ks_s, qd_s, kd_s, q1_s, k1_s, q2_s, k2_s, a_s, flag_s):
    tt = x_ref.shape[0]

    @pl.when(pl.program_id(1) == 0)
    def _():
        s_ref[...] = jnp.zeros(s_ref.shape, F32)

    _hgrn_in_compute(layer_j, x_ref[...].astype(BF16), win_ref, lbl_ref, q_ref, k_ref, lf_ref, v_ref, g_ref)

    lf = lf_ref[...] * LOG2E
    hi = lf.astype(BF16)
    lo = (lf - hi.astype(F32)).astype(BF16)
    g_s[...] = (jnp.dot(m_ref[...], hi, preferred_element_type=F32)
                + jnp.dot(m_ref[...], lo, preferred_element_type=F32))

    C = HG_CHUNK
    SB = HG_SUB
    worst = jnp.zeros((SB, D_MODEL), F32)
    for blk in range(tt // SB):
        r0 = blk * SB
        c0 = (r0 // C) * C
        rows = slice(r0, r0 + SB)

        def grow(r):
            return g_s[r - 1:r, :] if r > c0 else jnp.zeros((1, D_MODEL), F32)

        G = g_s[rows, :]
        q = q_ref[rows, :]
        k = k_ref[rows, :]
        qi_s[rows, :] = (q * jnp.exp2(G)).astype(BF16)
        ks_s[rows, :] = (k * jnp.exp2(g_s[c0 + C - 1:c0 + C, :] - G)).astype(BF16)
        e0 = G - grow(r0)
        qd_s[rows, :] = (q * jnp.exp2(e0)).astype(BF16)
        kd_s[rows, :] = (k * jnp.exp2(-e0)).astype(BF16)
        worst = jnp.maximum(worst, -e0)
        e1 = G - grow((r0 // (2 * SB)) * (2 * SB) + SB)
        q1_s[rows, :] = (q * jnp.exp2(jnp.minimum(e1, 0.0))).astype(BF16)
        k1_s[rows, :] = (k * jnp.exp2(jnp.minimum(-e1, 0.0))).astype(BF16)
        e2 = G - grow(c0 + C // 2)
        q2_s[rows, :] = (q * jnp.exp2(jnp.minimum(e2, 0.0))).astype(BF16)
        k2_s[rows, :] = (k * jnp.exp2(jnp.minimum(-e2, 0.0))).astype(BF16)
    flag_s[0] = (jnp.max(worst) <= HG_SAFE_RANGE * LOG2E).astype(I32)

    def finish(rows, sl, o):
        ms = jnp.mean(o * o, axis=-1, keepdims=True)
        o_ref[rows, sl] = o * lax.rsqrt(ms + LN_EPS) * ng_ref[:, sl] * g_ref[rows, sl]

    C = HG_CHUNK
    ti = lax.broadcasted_iota(I32, (C, C), 0)
    si = lax.broadcasted_iota(I32, (C, C), 1)
    causal = si <= ti
    sub_bits = HG_SUB.bit_length() - 1
    m0 = causal & ((ti >> sub_bits) == (si >> sub_bits))
    m1 = causal & ((ti >> (sub_bits + 1)) == (si >> (sub_bits + 1))) & jnp.logical_not(m0)
    nt_dims = (((1,), (1,)), ((), ()))

    @pl.when(flag_s[0] == 1)
    def _():
        def body(c, carry):
            r0 = pl.multiple_of(c * C, C)
            rows = pl.ds(r0, C)
            heads = [slice(h * LANES, (h + 1) * LANES) for h in range(HG_HEADS)]
            for h, sl in enumerate(heads):
                a0 = lax.dot_general(qd_s[rows, sl], kd_s[rows, sl], nt_dims, preferred_element_type=F32)
                a1 = lax.dot_general(q1_s[rows, sl], k1_s[rows, sl], nt_dims, preferred_element_type=F32)
                a2 = lax.dot_general(q2_s[rows, sl], k2_s[rows, sl], nt_dims, preferred_element_type=F32)
                a = jnp.where(m0, a0, jnp.where(m1, a1, jnp.where(causal, a2, 0.0)))
                a_s[h] = a.astype(BF16)
            for h, sl in enumerate(heads):
                o = (jnp.dot(qi_s[rows, sl], s_ref[0, h].astype(BF16), preferred_element_type=F32)
                     + jnp.dot(a_s[h], v_ref[rows, sl].astype(BF16), preferred_element_type=F32))
                finish(rows, sl, o)
            for h, sl in enumerate(heads):
                gend = g_s[pl.ds(pl.multiple_of(r0 + C - SUBLANES, SUBLANES), SUBLANES), sl][SUBLANES - 1:]
                ecol = jnp.transpose(jnp.broadcast_to(jnp.exp2(gend), (HG_DK, LANES)))
                upd = lax.dot_general(ks_s[rows, sl], v_ref[rows, sl].astype(BF16), (((0,), (0,)), ((), ())),
                                      preferred_element_type=F32)
                s_ref[0, h] = ecol * s_ref[0, h] + upd
            return carry

        lax.fori_loop(0, tt // C, body, 0)

    @pl.when(flag_s[0] == 0)
    def _():
        Cs = HG_CHUNK_PROMPT

        def body(i, carry):
            r0 = pl.multiple_of(i * Cs, Cs)
            rows = pl.ds(r0, Cs)
            for h in range(HG_HEADS):
                sl = slice(h * LANES, (h + 1) * LANES)
                o, s_new = _hgrn_chunk_head(Cs, q_ref[rows, sl], k_ref[rows, sl], lf_ref[rows, sl],
                                            v_ref[rows, sl], s_ref[0, h])
                finish(rows, sl, o)
                s_ref[0, h] = s_new
            return carry

        lax.fori_loop(0, tt // Cs, body, 0)

    y = jnp.dot(o_ref[...].astype(BF16), wout_ref[...], preferred_element_type=F32)
    xo_ref[...] = _layer_norm_rows(ALPHA * x_ref[...] + y, lng_ref[...], lnb_ref[...])


def hgrn_prompt_layer(x, w_in_bf, lb_logits, norm_g, w_out_bf, ln_g, ln_b, s_all, layer_j, batch, seq, n,
                      tt=256):
    nt = seq // tt
    row = pl.BlockSpec((tt, D_MODEL), lambda b, t: (b * nt + t, 0))
    vec = pl.BlockSpec((1, D_MODEL), lambda b, t: (0, 0))
    s_spec = pl.BlockSpec((None, 1, HG_HEADS, HG_DK, HG_DV), lambda b, t: (layer_j, b, 0, 0, 0))
    in_specs = [row,
                _layer_block(w_in_bf, layer_j, (D_MODEL, 4 * D_MODEL), lambda b, t: (0, 0)),
                pl.BlockSpec(lb_logits.shape, lambda b, t: (0, 0)),
                vec,
                pl.BlockSpec((tt, tt), lambda b, t: (0, 0)),
                _layer_block(w_out_bf, layer_j, (D_MODEL, D_MODEL), lambda b, t: (0, 0)),
                vec, vec]
    args = [x, w_in_bf, lb_logits, norm_g, _hgrn_chunk_masks(tt), w_out_bf, ln_g, ln_b]
    aliases = {}
    kern = functools.partial(_hgrn_prompt_kernel, layer_j)
    if s_all is not None:
        def kern(*refs):
            _hgrn_prompt_kernel(layer_j, *refs[:8], *refs[9:])
        in_specs.append(pl.BlockSpec(memory_space=pl.ANY))
        args.append(s_all)
        aliases = {8: 1}
    f32_tile = pltpu.VMEM((tt, D_MODEL), F32)
    bf_tile = pltpu.VMEM((tt, D_MODEL), BF16)
    return pl.pallas_call(
        kern,
        grid=(batch, nt),
        in_specs=in_specs,
        out_specs=[row, s_spec],
        out_shape=[jax.ShapeDtypeStruct((n, D_MODEL), F32),
                   jax.ShapeDtypeStruct((2, batch, HG_HEADS, HG_DK, HG_DV), F32)],
        scratch_shapes=[f32_tile] * 7 + [bf_tile] * 8
                       + [pltpu.VMEM((HG_HEADS, HG_CHUNK, HG_CHUNK), BF16), pltpu.SMEM((1,), I32)],
        input_output_aliases=aliases,
        compiler_params=_cparams(("arbitrary", "arbitrary")),
        name="hgrn_prompt_layer",
    )(*args)


def hgrn_rec_sample(q, k, lf, v, g, norm_g, state, s_all, layer_j, nseq, seq, sg=8):
    n = q.shape[0]
    rt = sg * seq
    row = pl.BlockSpec((rt, D_MODEL), lambda i: (i, 0))
    s_in = pl.BlockSpec((None, sg, HG_HEADS, HG_DK, HG_DV), lambda i: (layer_j, i, 0, 0, 0))
    in_specs = [row] * 5 + [pl.BlockSpec((1, D_MODEL), lambda i: (0, 0)), s_in]
    args = [q, k, lf, v, g, norm_g, state]
    aliases = {}
    if s_all is not None:
        in_specs.append(pl.BlockSpec(memory_space=pl.ANY))
        args.append(s_all)
        aliases[7] = 1

    def kern(*refs):
        _hgrn_rec_kernel(seq, *refs[:7], *refs[len(args):])

    return pl.pallas_call(
        kern,
        grid=(nseq // sg,),
        in_specs=in_specs,
        out_specs=[row, s_in],
        out_shape=[jax.ShapeDtypeStruct((n, D_MODEL), F32),
                   jax.ShapeDtypeStruct((2, nseq, HG_HEADS, HG_DK, HG_DV), F32)],
        input_output_aliases=aliases,
        compiler_params=_cparams(("arbitrary",)),
        name="hgrn_rec_sample",
    )(*args)


def _proj_ln_kernel(a_ref, w_ref, x_ref, g_ref, b_ref, o_ref):
    y = jnp.dot(a_ref[...].astype(BF16), w_ref[...], preferred_element_type=F32)
    o_ref[...] = _layer_norm_rows(ALPHA * x_ref[...] + y, g_ref[...], b_ref[...])


def proj_ln(a, w_bf, x, x_row0, g, b, xo, row0, layer=0, tm=512):
    n = xo.shape[0]
    nrows = a.shape[0]
    base = row0 // tm
    xbase = x_row0 // tm
    row = pl.BlockSpec((tm, D_MODEL), lambda i: (base + i, 0))
    vec = pl.BlockSpec((1, D_MODEL), lambda i: (0, 0))

    def kern(a_ref, w_ref, x_ref, g_ref, b_ref, _, o_ref):
        _proj_ln_kernel(a_ref, w_ref, x_ref, g_ref, b_ref, o_ref)

    return pl.pallas_call(
        kern,
        grid=(nrows // tm,),
        in_specs=[pl.BlockSpec((tm, D_MODEL), lambda i: (i, 0)),
                  _layer_block(w_bf, layer, (D_MODEL, D_MODEL), lambda i: (0, 0)),
                  pl.BlockSpec((tm, D_MODEL), lambda i: (xbase + i, 0)), vec, vec,
                  pl.BlockSpec(memory_space=pl.ANY)],
        out_specs=row,
        out_shape=jax.ShapeDtypeStruct((n, D_MODEL), F32),
        input_output_aliases={5: 0},
        compiler_params=_cparams(("arbitrary",)),
        name="proj_ln",
    )(a, w_bf, x, g, b, xo)


def _ffn_kernel(x_ref, wg_ref, wu_ref, wd_ref, g_ref, b_ref, o_ref):
    xb = x_ref[...].astype(BF16)
    gg = jnp.dot(xb, wg_ref[...], preferred_element_type=F32)
    uu = jnp.dot(xb, wu_ref[...], preferred_element_type=F32)
    h = (gg * _sigmoid(gg) * uu).astype(BF16)
    y = jnp.dot(h, wd_ref[...], preferred_element_type=F32)
    o_ref[...] = _layer_norm_rows(ALPHA * x_ref[...] + y, g_ref[...], b_ref[...])


def ffn_ln(x, w_gu_bf, w_down_bf, g, b, layer=0, tm=512):
    n = x.shape[0]
    row = pl.BlockSpec((tm, D_MODEL), lambda i: (i, 0))
    vec = pl.BlockSpec((1, D_MODEL), lambda i: (0, 0))
    once = pl.Buffered(1)
    return pl.pallas_call(
        _ffn_kernel,
        grid=(n // tm,),
        in_specs=[row,
                  _layer_block(w_gu_bf, layer, (D_MODEL, D_FF), lambda i: (0, 0), pipeline_mode=once),
                  _layer_block(w_gu_bf, layer, (D_MODEL, D_FF), lambda i: (0, 1), pipeline_mode=once),
                  _layer_block(w_down_bf, layer, (D_FF, D_MODEL), lambda i: (0, 0), pipeline_mode=once),
                  vec, vec],
        out_specs=row,
        out_shape=jax.ShapeDtypeStruct((n, D_MODEL), F32),
        compiler_params=_cparams(("arbitrary",)),
        name="ffn_ln",
    )(x, w_gu_bf, w_gu_bf, w_down_bf, g, b)


def _lru_in_kernel(x_ref, w_ref, xb_ref, gate_ref):
    xb = x_ref[...].astype(BF16)
    xb_ref[...] = jnp.dot(xb, w_ref[:, :D_MODEL], preferred_element_type=F32)
    y = jnp.dot(xb, w_ref[:, D_MODEL:], preferred_element_type=F32)
    gate_ref[...] = _gelu_tanh(y)


def lru_in(x, w_in_bf, row0, nrows, layer=0, tm=512):
    base = row0 // tm
    row = pl.BlockSpec((tm, D_MODEL), lambda i: (i, 0))
    out = jax.ShapeDtypeStruct((nrows, D_MODEL), F32)
    return pl.pallas_call(
        _lru_in_kernel,
        grid=(nrows // tm,),
        in_specs=[pl.BlockSpec((tm, D_MODEL), lambda i: (base + i, 0)),
                  _layer_block(w_in_bf, layer, (D_MODEL, 2 * D_MODEL), lambda i: (0, 0))],
        out_specs=[row, row],
        out_shape=[out, out],
        compiler_params=_cparams(("arbitrary",)),
        name="lru_in",
    )(x, w_in_bf)


def _lru_rec_kernel(L, carry, *refs):
    if carry:
        (x_ref, win_ref, cw_ref, cb_ref, wr_ref, br_ref, wi_ref, bi_ref, lam_ref, wout_ref, lng_ref, lnb_ref,
         cgu_ref, cdn_ref, xo_ref, hn_ref, cn_ref, cgu_o, cdn_o, xp_ref, hs_ref, hc_ref, gate_ref, hg_ref) = refs
        R = x_ref.shape[0]
        cgu_o[...] = cgu_ref[...].astype(BF16)
        cdn_o[...] = cdn_ref[...].astype(BF16)
    else:
        (xb_ref, gate_ref, cc_ref, h0_ref, cw_ref, cb_ref, wr_ref, br_ref, wi_ref, bi_ref, lam_ref,
         hg_ref, hn_ref, cn_ref, xp_ref, hs_ref) = refs
        R = xb_ref.shape[0]
    S = R // L
    PADR = SUBLANES
    t = pl.program_id(1) if carry else 0

    if carry:
        @pl.when(t == 0)
        def _():
            xp_ref[:, 0:PADR, :] = jnp.zeros((S, PADR, D_MODEL), F32)
            hc_ref[...] = jnp.zeros(hc_ref.shape, F32)

        @pl.when(t > 0)
        def _():
            xp_ref[:, 0:PADR, :] = xp_ref[:, L:L + PADR, :]

        xin = x_ref[...].astype(BF16)
        xp_ref[:, PADR:PADR + L, :] = jnp.dot(xin, win_ref[:, :D_MODEL],
                                              preferred_element_type=F32).reshape(S, L, D_MODEL)
        gate_ref[...] = _gelu_tanh(jnp.dot(xin, win_ref[:, D_MODEL:], preferred_element_type=F32))
    else:
        xp_ref[:, 0:PADR, :] = cc_ref[...].reshape(S, PADR, D_MODEL)
        xp_ref[:, PADR:PADR + L, :] = xb_ref[...].reshape(S, L, D_MODEL)

    xc = cb_ref[...].reshape(1, 1, D_MODEL) + xp_ref[:, PADR:PADR + L, :] * cw_ref[CONV_W - 1:CONV_W, :].reshape(1, 1, D_MODEL)
    for j in range(CONV_W - 1):
        off = PADR - (CONV_W - 1) + j
        xc = xc + xp_ref[:, off:off + L, :] * cw_ref[j:j + 1, :].reshape(1, 1, D_MODEL)
    xc = xc.reshape(R, D_MODEL)
    cn_ref[...] = xp_ref[:, L + PADR - (CONV_W - 1):L + PADR, :].reshape(cn_ref.shape)

    assert L & (L - 1) == 0
    pos = jnp.bitwise_and(lax.broadcasted_iota(I32, (R, LRU_BW), 0), L - 1)
    for nb in range(LRU_BLOCKS):
        sl = slice(nb * LRU_BW, (nb + 1) * LRU_BW)
        xcb = xc[:, sl]
        xh = xcb.astype(BF16)
        r = _sigmoid(jnp.dot(xh, wr_ref[nb], preferred_element_type=F32) + br_ref[:, sl])
        ig = _sigmoid(jnp.dot(xh, wi_ref[nb], preferred_element_type=F32) + bi_ref[:, sl])
        z = -lam_ref[:, sl]
        softplus = jnp.maximum(z, 0.0) + jnp.log1p(jnp.exp(-jnp.abs(z)))
        log_a = (-RG_C) * r * softplus
        a = jnp.exp(log_a)
        u = jnp.sqrt(1.0 - a * a) * (ig * xcb)
        if carry:
            ng = R // SUBLANES
            u3 = u.reshape(ng, SUBLANES, LRU_BW)
            a3 = a.reshape(ng, SUBLANES, LRU_BW)
            pos8 = lax.broadcasted_iota(I32, (ng, SUBLANES, LRU_BW), 1)
            d = 1
            while d < SUBLANES:
                m = pos8 >= d
                u3 = jnp.where(m, a3 * pltpu.roll(u3, d, 1) + u3, u3)
                a3 = jnp.where(m, a3 * pltpu.roll(a3, d, 1), a3)
                d *= 2
            hprev = hc_ref[:, sl]
            for gi in range(ng):
                hgrp = u3[gi] + a3[gi] * hprev
                hs_ref[0, gi * SUBLANES:(gi + 1) * SUBLANES, sl] = hgrp
                hprev = hgrp[SUBLANES - 1:SUBLANES]
            hg_ref[:, sl] = hs_ref[0, :, sl] * gate_ref[:, sl]
        else:
            u = u + a * h0_ref[:, sl]
            d = 1
            while d < L:
                m = pos >= d
                u = jnp.where(m, a * pltpu.roll(u, d, 0) + u, u)
                a = jnp.where(m, a * pltpu.roll(a, d, 0), a)
                d *= 2
            hg_ref[:, sl] = u * gate_ref[:, sl]
            hs_ref[:, :, sl] = u.reshape(S, L, LRU_BW)

    hlast = hs_ref[:, L - 1:L, :]
    hn_ref[...] = hlast.reshape(hn_ref.shape)
    if carry:
        hc_ref[...] = hlast.reshape(1, D_MODEL)
        y = jnp.dot(hg_ref[...].astype(BF16), wout_ref[...], preferred_element_type=F32)
        xo_ref[...] = _layer_norm_rows(ALPHA * x_ref[...] + y, lng_ref[...], lnb_ref[...])


def _lru_weight_specs(wts, layer):
    def c2(*_):
        return (0, 0)

    def c3(*_):
        return (0, 0, 0)
    gate_shape = (LRU_BLOCKS, LRU_BW, LRU_BW)
    return [pl.BlockSpec((CONV_W, D_MODEL), c2), pl.BlockSpec((1, D_MODEL), c2),
            _layer_block(wts[2], layer, gate_shape, c3), pl.BlockSpec((1, D_MODEL), c2),
            _layer_block(wts[4], layer, gate_shape, c3), pl.BlockSpec((1, D_MODEL), c2),
            pl.BlockSpec((1, D_MODEL), c2)]


def lru_prompt_layer(x, w_in_bf, wts, w_out_bf, ln_g, ln_b, moe_gu, moe_down, layer_j, batch, seq, tt=256):
    n = x.shape[0]
    nt = seq // tt
    steps = batch * nt
    gu2 = moe_gu.reshape(moe_gu.shape[0], N_EXPERTS * D_MODEL, 2 * D_FF_EXPERT)
    dn2 = moe_down.reshape(moe_down.shape[0], N_EXPERTS * D_FF_EXPERT, D_MODEL)
    gu_rows = gu2.shape[1] // steps
    dn_rows = dn2.shape[1] // steps
    assert gu_rows * steps == gu2.shape[1] and dn_rows * steps == dn2.shape[1]
    row = pl.BlockSpec((tt, D_MODEL), lambda b, t: (b * nt + t, 0))
    vec = pl.BlockSpec((1, D_MODEL), lambda b, t: (0, 0))
    xo, hn, cn, gu_bf, dn_bf = pl.pallas_call(
        functools.partial(_lru_rec_kernel, tt, True),
        grid=(batch, nt),
        in_specs=[row, _layer_block(w_in_bf, layer_j, (D_MODEL, 2 * D_MODEL), lambda b, t: (0, 0))]
                 + _lru_weight_specs(wts, layer_j)
                 + [_layer_block(w_out_bf, layer_j, (D_MODEL, D_MODEL), lambda b, t: (0, 0)), vec, vec,
                    pl.BlockSpec((None, gu_rows, 2 * D_FF_EXPERT), lambda b, t: (layer_j, b * nt + t, 0)),
                    pl.BlockSpec((None, dn_rows, D_MODEL), lambda b, t: (layer_j, b * nt + t, 0))],
        out_specs=[row,
                   pl.BlockSpec((1, 1, D_MODEL), lambda b, t: (b, 0, 0)),
                   pl.BlockSpec((1, CONV_W - 1, D_MODEL), lambda b, t: (b, 0, 0)),
                   pl.BlockSpec((gu_rows, 2 * D_FF_EXPERT), lambda b, t: (b * nt + t, 0)),
                   pl.BlockSpec((dn_rows, D_MODEL), lambda b, t: (b * nt + t, 0))],
        out_shape=[jax.ShapeDtypeStruct((n, D_MODEL), F32),
                   jax.ShapeDtypeStruct((batch, 1, D_MODEL), F32),
                   jax.ShapeDtypeStruct((batch, CONV_W - 1, D_MODEL), F32),
                   jax.ShapeDtypeStruct(gu2.shape[1:], BF16),
                   jax.ShapeDtypeStruct(dn2.shape[1:], BF16)],
        scratch_shapes=[pltpu.VMEM((1, tt + SUBLANES, D_MODEL), F32),
                        pltpu.VMEM((1, tt, D_MODEL), F32),
                        pltpu.VMEM((1, D_MODEL), F32),
                        pltpu.VMEM((tt, D_MODEL), F32),
                        pltpu.VMEM((tt, D_MODEL), F32)],
        compiler_params=_cparams(("arbitrary", "arbitrary")),
        name="lru_prompt_layer",
    )(x, w_in_bf, *wts, w_out_bf, ln_g, ln_b, gu2, dn2)
    return (xo, hn, cn, gu_bf.reshape(N_EXPERTS, D_MODEL, 2 * D_FF_EXPERT),
            dn_bf.reshape(N_EXPERTS, D_FF_EXPERT, D_MODEL))


def lru_rec_sample(xb, gate, cc_rows, h0_rows, wts, nseq, seq, layer=0, sg=32):
    n = xb.shape[0]
    rt = sg * seq
    row = pl.BlockSpec((rt, D_MODEL), lambda i: (i, 0))
    return pl.pallas_call(
        functools.partial(_lru_rec_kernel, seq, False),
        grid=(nseq // sg,),
        in_specs=[row, row, pl.BlockSpec((sg * SUBLANES, D_MODEL), lambda i: (i, 0)), row]
                 + _lru_weight_specs(wts, layer),
        out_specs=[row,
                   pl.BlockSpec((sg, 1, D_MODEL), lambda i: (i, 0, 0)),
                   pl.BlockSpec((sg, CONV_W - 1, D_MODEL), lambda i: (i, 0, 0))],
        out_shape=[jax.ShapeDtypeStruct((n, D_MODEL), F32),
                   jax.ShapeDtypeStruct((nseq, 1, D_MODEL), F32),
                   jax.ShapeDtypeStruct((nseq, CONV_W - 1, D_MODEL), F32)],
        scratch_shapes=[pltpu.VMEM((sg, seq + SUBLANES, D_MODEL), F32),
                        pltpu.VMEM((sg, seq, D_MODEL), F32)],
        compiler_params=_cparams(("arbitrary",)),
        name="lru_rec_sample",
    )(xb, gate, cc_rows, h0_rows, *wts)


def _router_kernel(x_ref, w_ref, mi_ref, mf_ref, cnt_ref, run_ref):
    i = pl.program_id(0)

    @pl.when(i == 0)
    def _():
        run_ref[...] = jnp.zeros(run_ref.shape, F32)

    tm = x_ref.shape[0]
    x = x_ref[...]
    w = w_ref[...]
    xh = x.astype(BF16)
    xl = (x - xh.astype(F32)).astype(BF16)
    wh = w.astype(BF16)
    wl = (w - wh.astype(F32)).astype(BF16)
    hh_hl = jnp.dot(xh, jnp.concatenate([wh, wl], axis=1), preferred_element_type=F32)
    logits = hh_hl[:, :LANES] + hh_hl[:, LANES:] + jnp.dot(xl, wh, preferred_element_type=F32)
    lane_i = lax.broadcasted_iota(I32, (tm, LANES), 1)
    lane = lane_i.astype(F32)
    neg = jnp.float32(-jnp.inf)
    logits = jnp.where(lane_i < N_EXPERTS, logits, neg)
    m1 = jnp.max(logits, axis=-1, keepdims=True)
    i1 = jnp.min(jnp.where(logits == m1, lane, float(LANES)), axis=-1, keepdims=True)
    l2 = jnp.where(lane == i1, neg, logits)
    m2 = jnp.max(l2, axis=-1, keepdims=True)
    i2 = jnp.min(jnp.where(l2 == m2, lane, float(LANES)), axis=-1, keepdims=True)
    e2 = jnp.exp(m2 - m1)
    p1 = 1.0 / (1.0 + e2)
    p2 = e2 * p1

    hit1 = lane == i1
    hit2 = lane == i2
    onehot = jnp.where(hit1 | hit2, 1.0, 0.0)
    r_i = lax.broadcasted_iota(I32, (tm, tm), 0)
    c_i = lax.broadcasted_iota(I32, (tm, tm), 1)
    tri = jnp.where(c_i < r_i, 1.0, 0.0).astype(BF16)
    rank = jnp.dot(tri, onehot.astype(BF16), preferred_element_type=F32) + run_ref[0:1, :]
    r1 = jnp.sum(jnp.where(hit1, rank, 0.0), axis=-1, keepdims=True)
    r2 = jnp.sum(jnp.where(hit2, rank, 0.0), axis=-1, keepdims=True)
    total = run_ref[0:1, :] + jnp.sum(onehot, axis=0, keepdims=True)
    run_ref[...] = jnp.broadcast_to(total, run_ref.shape)
    cnt_ref[...] = jnp.broadcast_to(total, cnt_ref.shape).astype(I32)

    mi = jnp.where(lane_i == 0, i1, jnp.where(lane_i == 1, i2, 0.0))
    mi = jnp.where(lane_i == 2, r1, jnp.where(lane_i == 3, r2, mi))
    mi_ref[...] = jnp.transpose(mi)[0:SUBLANES, :].astype(I32)
    mf_ref[...] = jnp.where(lane_i == 0, p1, jnp.where(lane_i == 1, p2, 0.0))


def moe_router(x, w_router_pad, tm=512):
    n = x.shape[0]
    row = pl.BlockSpec((tm, LANES), lambda i: (i, 0))
    return pl.pallas_call(
        _router_kernel,
        grid=(n // tm,),
        in_specs=[pl.BlockSpec((tm, D_MODEL), lambda i: (i, 0)),
                  pl.BlockSpec((D_MODEL, LANES), lambda i: (0, 0))],
        out_specs=[pl.BlockSpec((SUBLANES, tm), lambda i: (0, i)), row,
                   pl.BlockSpec((SUBLANES, LANES), lambda i: (0, 0))],
        out_shape=[jax.ShapeDtypeStruct((SUBLANES, n), I32),
                   jax.ShapeDtypeStruct((n, LANES), F32),
                   jax.ShapeDtypeStruct((SUBLANES, LANES), I32)],
        scratch_shapes=[pltpu.VMEM((SUBLANES, LANES), F32)],
        compiler_params=_cparams(("arbitrary",)),
        name="moe_router",
    )(x, w_router_pad)


ZERO_ROWS = 256


def _dispatch_kernel(dest_ref, ztile_ref, x_ref, xs_hbm, zero_ref, sem):
    i = pl.program_id(0)
    tm = x_ref.shape[0]
    n_tok = pl.num_programs(0) * tm

    @pl.when(i == 0)
    def _():
        zero_ref[...] = jnp.zeros(zero_ref.shape, F32)
        copies = [pltpu.make_async_copy(
                      zero_ref,
                      xs_hbm.at[pl.ds(pl.multiple_of(ztile_ref[e] + c * ZERO_ROWS, ZERO_ROWS), ZERO_ROWS)], sem)
                  for e in range(N_EXPERTS) for c in range(MOE_TILE // ZERO_ROWS)]
        for cp in copies:
            cp.start()
        for cp in copies:
            cp.wait()

    def group(jj, c):
        r8 = pl.multiple_of(jj * SUBLANES, SUBLANES)
        grp = x_ref.at[pl.ds(r8, SUBLANES)]
        t0 = i * tm + r8
        for u in range(SUBLANES):
            src = grp.at[pl.ds(u, 1)]
            pltpu.make_async_copy(src, xs_hbm.at[pl.ds(dest_ref[t0 + u], 1)], sem).start()
            pltpu.make_async_copy(src, xs_hbm.at[pl.ds(dest_ref[n_tok + t0 + u], 1)], sem).start()
        return c

    lax.fori_loop(0, tm // SUBLANES, group, 0)
    for _ in range(2):
        pltpu.make_async_copy(x_ref, xs_hbm.at[pl.ds(0, tm)], sem).wait()


def moe_dispatch(dest, ztile, x, n_rows, tm=512):
    n = x.shape[0]
    return pl.pallas_call(
        _dispatch_kernel,
        grid_spec=pltpu.PrefetchScalarGridSpec(
            num_scalar_prefetch=2, grid=(n // tm,),
            in_specs=[pl.BlockSpec((tm, D_MODEL), lambda i, d, z: (i, 0))],
            out_specs=pl.BlockSpec(memory_space=pl.ANY),
            scratch_shapes=[pltpu.VMEM((ZERO_ROWS, D_MODEL), F32), pltpu.SemaphoreType.DMA(())]),
        out_shape=jax.ShapeDtypeStruct((n_rows, D_MODEL), F32),
        compiler_params=pltpu.CompilerParams(dimension_semantics=("arbitrary",),
                                             has_side_effects=True),
        name="moe_dispatch",
    )(dest, ztile, x)


def _expert_kernel(te_ref, nu_ref, xs_ref, wg_ref, wu_ref, wd_ref, ys_ref, acc_ref):
    i = pl.program_id(0)
    f = pl.program_id(1)
    used = i < nu_ref[0]

    @pl.when(used)
    def _():
        xb = xs_ref[...].astype(BF16)
        gg = jnp.dot(xb, wg_ref[...], preferred_element_type=F32)
        uu = jnp.dot(xb, wu_ref[...], preferred_element_type=F32)
        h = (gg * _sigmoid(gg) * uu).astype(BF16)
        part = jnp.dot(h, wd_ref[...], preferred_element_type=F32)

        @pl.when(f == 0)
        def _():
            acc_ref[...] = part

        @pl.when(f > 0)
        def _():
            acc_ref[...] += part

    @pl.when(f == pl.num_programs(1) - 1)
    def _():
        @pl.when(used)
        def _():
            ys_ref[...] = acc_ref[...]

        @pl.when(jnp.logical_not(used))
        def _():
            ys_ref[...] = jnp.zeros(ys_ref.shape, F32)


def moe_experts(tile_expert, n_used, xs, w_gu_bf, w_down_bf, tf=1792):
    n_rows = xs.shape[0]
    tm = MOE_TILE
    nf = D_FF_EXPERT // tf

    def xrow(i, f, te, nu):
        return (jnp.minimum(i, nu[0] - 1), 0)

    def wg(i, f, te, nu):
        return (te[i], 0, jnp.where(i < nu[0], f, nf - 1))

    def wu(i, f, te, nu):
        return (te[i], 0, nf + jnp.where(i < nu[0], f, nf - 1))

    def wd(i, f, te, nu):
        return (te[i], jnp.where(i < nu[0], f, nf - 1), 0)

    return pl.pallas_call(
        _expert_kernel,
        grid_spec=pltpu.PrefetchScalarGridSpec(
            num_scalar_prefetch=2, grid=(n_rows // tm, nf),
            in_specs=[pl.BlockSpec((tm, D_MODEL), xrow),
                      pl.BlockSpec((None, D_MODEL, tf), wg),
                      pl.BlockSpec((None, D_MODEL, tf), wu),
                      pl.BlockSpec((None, tf, D_MODEL), wd)],
            out_specs=pl.BlockSpec((tm, D_MODEL), lambda i, f, te, nu: (i, 0)),
            scratch_shapes=[pltpu.VMEM((tm, D_MODEL), F32)]),
        out_shape=jax.ShapeDtypeStruct((n_rows, D_MODEL), F32),
        compiler_params=_cparams(("arbitrary", "arbitrary")),
        name="moe_experts",
    )(tile_expert, n_used, xs, w_gu_bf, w_gu_bf, w_down_bf)


def _combine_kernel(nsplit, dest_ref, ys_hbm, mf_ref, x_ref, g_ref, b_ref, *rest):
    if nsplit is None:
        o_ref, buf_ref, sem = rest
    else:
        oa_ref, ob_ref, buf_ref, sem = rest
    i = pl.program_id(0)
    n_steps = pl.num_programs(0)
    tm = x_ref.shape[0]
    n_tok = n_steps * tm

    def gather(step, slot):
        def group(jj, c):
            r8 = pl.multiple_of(jj * SUBLANES, SUBLANES)
            g0 = buf_ref.at[slot, 0, pl.ds(r8, SUBLANES)]
            g1 = buf_ref.at[slot, 1, pl.ds(r8, SUBLANES)]
            t0 = step * tm + r8
            for u in range(SUBLANES):
                pltpu.make_async_copy(ys_hbm.at[pl.ds(dest_ref[t0 + u], 1)], g0.at[pl.ds(u, 1)],
                                      sem.at[slot]).start()
                pltpu.make_async_copy(ys_hbm.at[pl.ds(dest_ref[n_tok + t0 + u], 1)], g1.at[pl.ds(u, 1)],
                                      sem.at[slot]).start()
            return c

        lax.fori_loop(0, tm // SUBLANES, group, 0)

    slot = lax.rem(i, 2)

    @pl.when(i == 0)
    def _():
        gather(i, slot)

    @pl.when(i + 1 < n_steps)
    def _():
        gather(i + 1, 1 - slot)

    for s in range(2):
        pltpu.make_async_copy(ys_hbm.at[pl.ds(0, tm)], buf_ref.at[slot, s], sem.at[slot]).wait()
    mf = mf_ref[...]
    y = buf_ref[slot, 0] * mf[:, 0:1] + buf_ref[slot, 1] * mf[:, 1:2]
    res = _layer_norm_rows(ALPHA * x_ref[...] + y, g_ref[...], b_ref[...])
    if nsplit is None:
        o_ref[...] = res
    else:
        @pl.when(i < nsplit)
        def _():
            oa_ref[...] = res

        @pl.when(i >= nsplit)
        def _():
            ob_ref[...] = res


def moe_combine_ln(dest, ys, mf, x, g, b, split_rows=None, tm=512):
    n = x.shape[0]
    row = pl.BlockSpec((tm, D_MODEL), lambda i, d: (i, 0))
    vec = pl.BlockSpec((1, D_MODEL), lambda i, d: (0, 0))
    if split_rows is None:
        nsplit = None
        out_specs = row
        out_shape = jax.ShapeDtypeStruct((n, D_MODEL), F32)
    else:
        nsplit = split_rows // tm
        out_specs = [pl.BlockSpec((tm, D_MODEL), lambda i, d: (jnp.minimum(i, nsplit - 1), 0)),
                     pl.BlockSpec((tm, D_MODEL), lambda i, d: (jnp.maximum(i - nsplit, 0), 0))]
        out_shape = [jax.ShapeDtypeStruct((split_rows, D_MODEL), F32),
                     jax.ShapeDtypeStruct((n - split_rows, D_MODEL), F32)]
    return pl.pallas_call(
        functools.partial(_combine_kernel, nsplit),
        grid_spec=pltpu.PrefetchScalarGridSpec(
            num_scalar_prefetch=1, grid=(n // tm,),
            in_specs=[pl.BlockSpec(memory_space=pl.ANY),
                      pl.BlockSpec((tm, LANES), lambda i, d: (i, 0)),
                      row, vec, vec],
            out_specs=out_specs,
            scratch_shapes=[pltpu.VMEM((2, 2, tm, D_MODEL), F32), pltpu.SemaphoreType.DMA((2,))]),
        out_shape=out_shape,
        compiler_params=_cparams(("arbitrary",)),
        name="moe_combine_ln",
    )(dest, ys, mf, x, g, b)


def moe_ffn_ln(x, w_router_pad, w_gu, w_down, g, b, split_rows=None):
    n = x.shape[0]
    tm = MOE_TILE
    mi, mf, cnt = moe_router(x, w_router_pad)
    counts = cnt[0, :N_EXPERTS]
    padded = (counts + tm - 1) // tm * tm
    ends = jnp.cumsum(padded)
    starts = ends - padded
    ids = mi[0:2]
    dest = mi[2:4]
    for e in range(N_EXPERTS):
        dest = dest + jnp.where(ids == e, starts[e], 0)
    dest = dest.reshape(-1).astype(I32)
    n_tiles = (2 * n) // tm + N_EXPERTS
    tile_start = jnp.arange(n_tiles, dtype=I32) * tm
    tile_expert = jnp.minimum(jnp.sum(tile_start[:, None] >= ends[None, :], axis=1), N_EXPERTS - 1).astype(I32)
    n_used = (ends[-1] // tm).astype(I32).reshape(1)
    tile_expert = jnp.where(tile_start < ends[-1], tile_expert, tile_expert[jnp.maximum(n_used[0] - 1, 0)])
    ztile = jnp.maximum(ends - tm, 0).astype(I32)
    xs = moe_dispatch(dest, ztile, x, n_tiles * tm)
    ys = moe_experts(tile_expert, n_used, xs, w_gu, w_down)
    return moe_combine_ln(dest, ys, mf, x, g, b, split_rows)


def kernel(x_prompt, x_sample, state_hgrn, state_lru_h, state_lru_conv, ln_mix_g, ln_mix_b, ln_ffn_g, ln_ffn_b, w_hgrn_in, hgrn_lb_logits, hgrn_norm_g, w_hgrn_out, w_lru_in, lru_conv_w, lru_conv_b, w_lru_rgate, b_lru_rgate, w_lru_igate, b_lru_igate, lru_lambda, w_lru_out, w_ffn_gu, w_ffn_down, w_router, w_moe_gu, w_moe_down):
    bp, tp, _ = x_prompt.shape
    bs, ts, _ = x_sample.shape
    n_p = bp * tp
    n = n_p + bs * ts
    x = x_prompt.reshape(n_p, D_MODEL)
    x_smp, smp_row0 = x_sample.reshape(bs * ts, D_MODEL), 0

    def vec(a, i):
        return a[i].reshape(1, D_MODEL)

    w_hgrn_in_bf = w_hgrn_in.astype(BF16)
    w_hgrn_out_bf = w_hgrn_out.astype(BF16)
    w_lru_in_bf = w_lru_in.astype(BF16)
    w_lru_out_bf = w_lru_out.astype(BF16)
    w_r_bf = w_lru_rgate.astype(BF16)
    w_i_bf = w_lru_igate.astype(BF16)
    w_ffn_gu_bf = w_ffn_gu.astype(BF16)
    w_ffn_down_bf = w_ffn_down.astype(BF16)
    w_router_pad = jnp.pad(w_router, ((0, 0), (0, 0), (0, LANES - N_EXPERTS)))

    hg_p = hg_s = None
    h_p, h_s, c_p, c_s = [], [], [], []
    for layer in range(DEPTH):
        j = layer // 2
        lng, lnb = vec(ln_mix_g, layer), vec(ln_mix_b, layer)
        if layer % 2 == 0:
            ng = vec(hgrn_norm_g, j)
            xo, hg_p = hgrn_prompt_layer(x, w_hgrn_in_bf, hgrn_lb_logits, ng, w_hgrn_out_bf,
                                         lng, lnb, hg_p, j, bp, tp, n)
            q, k, lf, v, g = hgrn_in(x_smp, w_hgrn_in_bf, hgrn_lb_logits, j, smp_row0, bs * ts)
            o, hg_s = hgrn_rec_sample(q, k, lf, v, g, ng, state_hgrn, hg_s, j, bs, ts)
            x = proj_ln(o, w_hgrn_out_bf, x_smp, smp_row0, lng, lnb, xo, n_p, layer=j)
            x = ffn_ln(x, w_ffn_gu_bf, w_ffn_down_bf, vec(ln_ffn_g, layer), vec(ln_ffn_b, layer), layer=j)
            x_smp, smp_row0 = x, n_p
        else:
            wts = (lru_conv_w[j], vec(lru_conv_b, j), w_r_bf, vec(b_lru_rgate, j),
                   w_i_bf, vec(b_lru_igate, j), vec(lru_lambda, j))
            xo, hp, cp, moe_gu_bf, moe_down_bf = lru_prompt_layer(
                x, w_lru_in_bf, wts, w_lru_out_bf, lng, lnb, w_moe_gu, w_moe_down, j, bp, tp)
            xb, gate = lru_in(x_smp, w_lru_in_bf, smp_row0, bs * ts, layer=j)
            cc_rows = jnp.pad(state_lru_conv[j], ((0, 0), (SUBLANES - (CONV_W - 1), 0), (0, 0)))
            h0_rows = jnp.pad(state_lru_h[j][:, None, :], ((0, 0), (0, ts - 1), (0, 0)))
            hgate, hs, cs = lru_rec_sample(xb, gate, cc_rows.reshape(bs * SUBLANES, D_MODEL),
                                           h0_rows.reshape(bs * ts, D_MODEL), wts, bs, ts, layer=j)
            h_p.append(hp.reshape(bp, D_MODEL))
            h_s.append(hs.reshape(bs, D_MODEL))
            c_p.append(cp)
            c_s.append(cs)
            x = proj_ln(hgate, w_lru_out_bf, x_smp, smp_row0, lng, lnb, xo, n_p, layer=j)
            x = moe_ffn_ln(x, w_router_pad[j], moe_gu_bf, moe_down_bf,
                           vec(ln_ffn_g, layer), vec(ln_ffn_b, layer),
                           split_rows=n_p if layer == DEPTH - 1 else None)
            x_smp, smp_row0 = x, n_p

    y_prompt = x[0].reshape(bp, tp, D_MODEL)
    y_sample = x[1].reshape(bs, ts, D_MODEL)
    return (y_prompt, y_sample, hg_p, hg_s, jnp.stack(h_p), jnp.stack(h_s), jnp.stack(c_p), jnp.stack(c_s))
```

```python
import functools
import math

import jax
import jax.numpy as jnp
from jax import lax
from jax.experimental import pallas as pl
from jax.experimental.pallas import tpu as pltpu

F32 = jnp.float32
BF16 = jnp.bfloat16
I32 = jnp.int32

D_MODEL = 1024
DEPTH = 4
HG_DK = 128
HG_HEADS = 8
HG_DV = 128
CONV_W = 4
RG_C = 8.0
LRU_BLOCKS = 4
LRU_BW = 256
D_FF = 2816
N_EXPERTS = 8
D_FF_EXPERT = 3584
ALPHA = (2 * DEPTH) ** 0.25
LN_EPS = 1e-5
F_FLOOR = 1e-20
LOG2E = math.log2(math.e)

LANES = 128
SUBLANES = 8
VMEM_LIMIT = 56 * 1024 * 1024

HG_CHUNK_PROMPT = 16
MOE_TILE = 512


def _cparams(sem):
    return pltpu.CompilerParams(dimension_semantics=sem, vmem_limit_bytes=VMEM_LIMIT)


def _layer_block(w, layer, block_shape, index_map, **kw):
    if w.ndim == len(block_shape):
        return pl.BlockSpec(block_shape, index_map, **kw)
    return pl.BlockSpec((None,) + tuple(block_shape), lambda *a: (layer,) + tuple(index_map(*a)), **kw)


def _sigmoid(x):
    return 0.5 * jnp.tanh(0.5 * x) + 0.5


def _gelu_tanh(y):
    c = math.sqrt(2.0 / math.pi)
    return 0.5 * y * (1.0 + jnp.tanh(c * (y + 0.044715 * (y * y * y))))


def _layer_norm_rows(z, g, b):
    mu = jnp.mean(z, axis=-1, keepdims=True)
    zc = z - mu
    var = jnp.mean(zc * zc, axis=-1, keepdims=True)
    return zc * lax.rsqrt(var + LN_EPS) * g + b


def _hgrn_in_compute(layer_j, xb, w_ref, lbl_ref, q_ref, k_ref, lf_ref, v_ref, g_ref):
    l = lbl_ref[...]
    e = jnp.exp(l - jnp.max(l, axis=0, keepdims=True))
    p = e / jnp.sum(e, axis=0, keepdims=True)
    cs = p[0:1]
    for i in range(1, layer_j + 1):
        cs = cs + p[i:i + 1]
    lb = cs - p[0:1]

    def seg(i):
        return jnp.dot(xb, w_ref[:, i * D_MODEL:(i + 1) * D_MODEL], preferred_element_type=F32)

    q = seg(0)
    q_ref[...] = q * _sigmoid(q) * (HG_DK ** -0.5)
    fp = seg(1)
    sig = 1.0 / (1.0 + jnp.exp(-fp))
    f = lb + (1.0 - lb) * sig
    lf_ref[...] = jnp.log(jnp.maximum(f, F_FLOOR))
    k_ref[...] = (1.0 - lb) * (1.0 - sig)
    v_ref[...] = seg(2)
    g = seg(3)
    g_ref[...] = g * _sigmoid(g)


def _hgrn_in_kernel(layer_j, x_ref, w_ref, lbl_ref, *out_refs):
    _hgrn_in_compute(layer_j, x_ref[...].astype(BF16), w_ref, lbl_ref, *out_refs)


def hgrn_in(x, w_in_bf, lb_logits, layer_j, row0, nrows, tm=256):
    base = row0 // tm
    out = jax.ShapeDtypeStruct((nrows, D_MODEL), F32)
    return pl.pallas_call(
        functools.partial(_hgrn_in_kernel, layer_j),
        grid=(nrows // tm,),
        in_specs=[pl.BlockSpec((tm, D_MODEL), lambda i: (base + i, 0)),
                  _layer_block(w_in_bf, layer_j, (D_MODEL, 4 * D_MODEL), lambda i: (0, 0)),
                  pl.BlockSpec(lb_logits.shape, lambda i: (0, 0))],
        out_specs=[pl.BlockSpec((tm, D_MODEL), lambda i: (i, 0))] * 5,
        out_shape=[out] * 5,
        compiler_params=_cparams(("arbitrary",)),
        name="hgrn_in",
    )(x, w_in_bf, lb_logits)


def _hgrn_prefix(C, lf):
    row = lax.broadcasted_iota(I32, (C, LANES), 0)
    G = lf
    d = 1
    while d < C:
        G = G + jnp.where(row >= d, pltpu.roll(G, d, 0), 0.0)
        d *= 2
    return G


def _hgrn_diag(C, G, q, k, v):
    row8 = lax.broadcasted_iota(I32, (SUBLANES, LANES), 0)
    parts = []
    for gi in range(C // SUBLANES):
        lo = gi * SUBLANES
        Gg = G[lo:lo + SUBLANES]
        qg = q[lo:lo + SUBLANES]
        acc = jnp.zeros((SUBLANES, LANES), F32)
        for s in range(lo + SUBLANES):
            dec = jnp.exp(Gg - G[s:s + 1])
            if s >= lo:
                dec = jnp.where(row8 >= (s - lo), dec, 0.0)
            w = jnp.sum(qg * dec * k[s:s + 1], axis=-1, keepdims=True)
            acc = acc + w * v[s:s + 1]
        parts.append(acc)
    return parts[0] if len(parts) == 1 else jnp.concatenate(parts, axis=0)


def _hgrn_state_update(C, G, k, v, S):
    gend = G[C - 1:C, :]
    kt = k * jnp.exp(gend - G)
    ecol = jnp.transpose(jnp.broadcast_to(jnp.exp(gend), (HG_DK, LANES)))
    upd = lax.dot_general(kt.astype(BF16), v.astype(BF16), (((0,), (0,)), ((), ())),
                          preferred_element_type=F32)
    return ecol * S + upd


def _hgrn_chunk_head(C, q, k, lf, v, S):
    G = _hgrn_prefix(C, lf)
    o = jnp.dot((q * jnp.exp(G)).astype(BF16), S.astype(BF16), preferred_element_type=F32)
    return o + _hgrn_diag(C, G, q, k, v), _hgrn_state_update(C, G, k, v, S)


def _hgrn_rec_kernel(C, q_ref, k_ref, lf_ref, v_ref, g_ref, ng_ref, s0_ref, o_ref, s_ref):
    heads = [slice(h * LANES, (h + 1) * LANES) for h in range(HG_HEADS)]

    def body(i, carry):
        rows = pl.ds(pl.multiple_of(i * C, C), C)
        gs, os_ = [], []
        for h, sl in enumerate(heads):
            G = _hgrn_prefix(C, lf_ref[rows, sl])
            gs.append(G)
            os_.append(jnp.dot((q_ref[rows, sl] * jnp.exp(G)).astype(BF16), s0_ref[i, h].astype(BF16),
                               preferred_element_type=F32))
        for h, sl in enumerate(heads):
            o = os_[h] + _hgrn_diag(C, gs[h], q_ref[rows, sl], k_ref[rows, sl], v_ref[rows, sl])
            ms = jnp.mean(o * o, axis=-1, keepdims=True)
            o_ref[rows, sl] = o * lax.rsqrt(ms + LN_EPS) * ng_ref[:, sl] * g_ref[rows, sl]
        for h, sl in enumerate(heads):
            s_ref[i, h] = _hgrn_state_update(C, gs[h], k_ref[rows, sl], v_ref[rows, sl], s0_ref[i, h])
        return carry

    lax.fori_loop(0, q_ref.shape[0] // C, body, 0)


HG_CHUNK = 64
HG_SUB = 16
HG_SAFE_RANGE = 60.0


def _hgrn_chunk_masks(tt):
    import numpy as np
    t = np.arange(tt)[:, None]
    s = np.arange(tt)[None, :]
    tri = (s <= t) & ((t // HG_CHUNK) == (s // HG_CHUNK))
    return jnp.asarray(tri.astype(np.float32), BF16)


def _hgrn_prompt_kernel(layer_j, x_ref, win_ref, lbl_ref, ng_ref, m_ref, wout_ref, lng_ref, lnb_ref,
                        xo_ref, s_ref,
                        q_ref, k_ref, lf_ref, v_ref, g_ref, o_ref,
                        g_s, qi_s, ks_s, qd_s, kd_s, q1_s, k1_s, q2_s, k2_s, a_s, flag_s):
    tt = x_ref.shape[0]

    @pl.when(pl.program_id(1) == 0)
    def _():
        s_ref[...] = jnp.zeros(s_ref.shape, F32)

    _hgrn_in_compute(layer_j, x_ref[...].astype(BF16), win_ref, lbl_ref, q_ref, k_ref, lf_ref, v_ref, g_ref)

    lf = lf_ref[...] * LOG2E
    hi = lf.astype(BF16)
    lo = (lf - hi.astype(F32)).astype(BF16)
    g_s[...] = (jnp.dot(m_ref[...], hi, preferred_element_type=F32)
                + jnp.dot(m_ref[...], lo, preferred_element_type=F32))

    C = HG_CHUNK
    SB = HG_SUB
    worst = jnp.zeros((SB, D_MODEL), F32)
    for blk in range(tt // SB):
        r0 = blk * SB
        c0 = (r0 // C) * C
        rows = slice(r0, r0 + SB)

        def grow(r):
            return g_s[r - 1:r, :] if r > c0 else jnp.zeros((1, D_MODEL), F32)

        G = g_s[rows, :]
        q = q_ref[rows, :]
        k = k_ref[rows, :]
        qi_s[rows, :] = (q * jnp.exp2(G)).astype(BF16)
        ks_s[rows, :] = (k * jnp.exp2(g_s[c0 + C - 1:c0 + C, :] - G)).astype(BF16)
        e0 = G - grow(r0)
        qd_s[rows, :] = (q * jnp.exp2(e0)).astype(BF16)
        kd_s[rows, :] = (k * jnp.exp2(-e0)).astype(BF16)
        worst = jnp.maximum(worst, -e0)
        e1 = G - grow((r0 // (2 * SB)) * (2 * SB) + SB)
        q1_s[rows, :] = (q * jnp.exp2(jnp.minimum(e1, 0.0))).astype(BF16)
        k1_s[rows, :] = (k * jnp.exp2(jnp.minimum(-e1, 0.0))).astype(BF16)
        e2 = G - grow(c0 + C // 2)
        q2_s[rows, :] = (q * jnp.exp2(jnp.minimum(e2, 0.0))).astype(BF16)
        k2_s[rows, :] = (k * jnp.exp2(jnp.minimum(-e2, 0.0))).astype(BF16)
    flag_s[0] = (jnp.max(worst) <= HG_SAFE_RANGE * LOG2E).astype(I32)

    def finish(rows, sl, o):
        ms = jnp.mean(o * o, axis=-1, keepdims=True)
        o_ref[rows, sl] = o * lax.rsqrt(ms + LN_EPS) * ng_ref[:, sl] * g_ref[rows, sl]

    C = HG_CHUNK
    ti = lax.broadcasted_iota(I32, (C, C), 0)
    si = lax.broadcasted_iota(I32, (C, C), 1)
    causal = si <= ti
    sub_bits = HG_SUB.bit_length() - 1
    m0 = causal & ((ti >> sub_bits) == (si >> sub_bits))
    m1 = causal & ((ti >> (sub_bits + 1)) == (si >> (sub_bits + 1))) & jnp.logical_not(m0)
    nt_dims = (((1,), (1,)), ((), ()))

    @pl.when(flag_s[0] == 1)
    def _():
        def body(c, carry):
            r0 = pl.multiple_of(c * C, C)
            rows = pl.ds(r0, C)
            heads = [slice(h * LANES, (h + 1) * LANES) for h in range(HG_HEADS)]
            for h, sl in enumerate(heads):
                a0 = lax.dot_general(qd_s[rows, sl], kd_s[rows, sl], nt_dims, preferred_element_type=F32)
                a1 = lax.dot_general(q1_s[rows, sl], k1_s[rows, sl], nt_dims, preferred_element_type=F32)
                a2 = lax.dot_general(q2_s[rows, sl], k2_s[rows, sl], nt_dims, preferred_element_type=F32)
                a = jnp.where(m0, a0, jnp.where(m1, a1, jnp.where(causal, a2, 0.0)))
                a_s[h] = a.astype(BF16)
            for h, sl in enumerate(heads):
                o = (jnp.dot(qi_s[rows, sl], s_ref[0, h].astype(BF16), preferred_element_type=F32)
                     + jnp.dot(a_s[h], v_ref[rows, sl].astype(BF16), preferred_element_type=F32))
                finish(rows, sl, o)
            for h, sl in enumerate(heads):
                gend = g_s[pl.ds(pl.multiple_of(r0 + C - SUBLANES, SUBLANES), SUBLANES), sl][SUBLANES - 1:]
                ecol = jnp.transpose(jnp.broadcast_to(jnp.exp2(gend), (HG_DK, LANES)))
                upd = lax.dot_general(ks_s[rows, sl], v_ref[rows, sl].astype(BF16), (((0,), (0,)), ((), ())),
                                      preferred_element_type=F32)
                s_ref[0, h] = ecol * s_ref[0, h] + upd
            return carry

        lax.fori_loop(0, tt // C, body, 0)

    @pl.when(flag_s[0] == 0)
    def _():
        Cs = HG_CHUNK_PROMPT

        def body(i, carry):
            r0 = pl.multiple_of(i * Cs, Cs)
            rows = pl.ds(r0, Cs)
            for h in range(HG_HEADS):
                sl = slice(h * LANES, (h + 1) * LANES)
                o, s_new = _hgrn_chunk_head(Cs, q_ref[rows, sl], k_ref[rows, sl], lf_ref[rows, sl],
                                            v_ref[rows, sl], s_ref[0, h])
                finish(rows, sl, o)
                s_ref[0, h] = s_new
            return carry

        lax.fori_loop(0, tt // Cs, body, 0)

    y = jnp.dot(o_ref[...].astype(BF16), wout_ref[...], preferred_element_type=F32)
    xo_ref[...] = _layer_norm_rows(ALPHA * x_ref[...] + y, lng_ref[...], lnb_ref[...])


def hgrn_prompt_layer(x, w_in_bf, lb_logits, norm_g, w_out_bf, ln_g, ln_b, s_all, layer_j, batch, seq, n,
                      tt=256):
    nt = seq // tt
    row = pl.BlockSpec((tt, D_MODEL), lambda b, t: (b * nt + t, 0))
    vec = pl.BlockSpec((1, D_MODEL), lambda b, t: (0, 0))
    s_spec = pl.BlockSpec((None, 1, HG_HEADS, HG_DK, HG_DV), lambda b, t: (layer_j, b, 0, 0, 0))
    in_specs = [row,
                _layer_block(w_in_bf, layer_j, (D_MODEL, 4 * D_MODEL), lambda b, t: (0, 0)),
                pl.BlockSpec(lb_logits.shape, lambda b, t: (0, 0)),
                vec,
                pl.BlockSpec((tt, tt), lambda b, t: (0, 0)),
                _layer_block(w_out_bf, layer_j, (D_MODEL, D_MODEL), lambda b, t: (0, 0)),
                vec, vec]
    args = [x, w_in_bf, lb_logits, norm_g, _hgrn_chunk_masks(tt), w_out_bf, ln_g, ln_b]
    aliases = {}
    kern = functools.partial(_hgrn_prompt_kernel, layer_j)
    if s_all is not None:
        def kern(*refs):
            _hgrn_prompt_kernel(layer_j, *refs[:8], *refs[9:])
        in_specs.append(pl.BlockSpec(memory_space=pl.ANY))
        args.append(s_all)
        aliases = {8: 1}
    f32_tile = pltpu.VMEM((tt, D_MODEL), F32)
    bf_tile = pltpu.VMEM((tt, D_MODEL), BF16)
    return pl.pallas_call(
        kern,
        grid=(batch, nt),
        in_specs=in_specs,
        out_specs=[row, s_spec],
        out_shape=[jax.ShapeDtypeStruct((n, D_MODEL), F32),
                   jax.ShapeDtypeStruct((2, batch, HG_HEADS, HG_DK, HG_DV), F32)],
        scratch_shapes=[f32_tile] * 7 + [bf_tile] * 8
                       + [pltpu.VMEM((HG_HEADS, HG_CHUNK, HG_CHUNK), BF16), pltpu.SMEM((1,), I32)],
        input_output_aliases=aliases,
        compiler_params=_cparams(("arbitrary", "arbitrary")),
        name="hgrn_prompt_layer",
    )(*args)


def hgrn_rec_sample(q, k, lf, v, g, norm_g, state, s_all, layer_j, nseq, seq, sg=8):
    n = q.shape[0]
    rt = sg * seq
    row = pl.BlockSpec((rt, D_MODEL), lambda i: (i, 0))
    s_in = pl.BlockSpec((None, sg, HG_HEADS, HG_DK, HG_DV), lambda i: (layer_j, i, 0, 0, 0))
    in_specs = [row] * 5 + [pl.BlockSpec((1, D_MODEL), lambda i: (0, 0)), s_in]
    args = [q, k, lf, v, g, norm_g, state]
    aliases = {}
    if s_all is not None:
        in_specs.append(pl.BlockSpec(memory_space=pl.ANY))
        args.append(s_all)
        aliases[7] = 1

    def kern(*refs):
        _hgrn_rec_kernel(seq, *refs[:7], *refs[len(args):])

    return pl.pallas_call(
        kern,
        grid=(nseq // sg,),
        in_specs=in_specs,
        out_specs=[row, s_in],
        out_shape=[jax.ShapeDtypeStruct((n, D_MODEL), F32),
                   jax.ShapeDtypeStruct((2, nseq, HG_HEADS, HG_DK, HG_DV), F32)],
        input_output_aliases=aliases,
        compiler_params=_cparams(("arbitrary",)),
        name="hgrn_rec_sample",
    )(*args)


def _proj_ln_kernel(a_ref, w_ref, x_ref, g_ref, b_ref, o_ref):
    y = jnp.dot(a_ref[...].astype(BF16), w_ref[...], preferred_element_type=F32)
    o_ref[...] = _layer_norm_rows(ALPHA * x_ref[...] + y, g_ref[...], b_ref[...])


def proj_ln(a, w_bf, x, x_row0, g, b, xo, row0, layer=0, tm=512):
    n = xo.shape[0]
    nrows = a.shape[0]
    base = row0 // tm
    xbase = x_row0 // tm
    row = pl.BlockSpec((tm, D_MODEL), lambda i: (base + i, 0))
    vec = pl.BlockSpec((1, D_MODEL), lambda i: (0, 0))

    def kern(a_ref, w_ref, x_ref, g_ref, b_ref, _, o_ref):
        _proj_ln_kernel(a_ref, w_ref, x_ref, g_ref, b_ref, o_ref)

    return pl.pallas_call(
        kern,
        grid=(nrows // tm,),
        in_specs=[pl.BlockSpec((tm, D_MODEL), lambda i: (i, 0)),
                  _layer_block(w_bf, layer, (D_MODEL, D_MODEL), lambda i: (0, 0)),
                  pl.BlockSpec((tm, D_MODEL), lambda i: (xbase + i, 0)), vec, vec,
                  pl.BlockSpec(memory_space=pl.ANY)],
        out_specs=row,
        out_shape=jax.ShapeDtypeStruct((n, D_MODEL), F32),
        input_output_aliases={5: 0},
        compiler_params=_cparams(("arbitrary",)),
        name="proj_ln",
    )(a, w_bf, x, g, b, xo)


def _ffn_kernel(x_ref, wg_ref, wu_ref, wd_ref, g_ref, b_ref, o_ref):
    xb = x_ref[...].astype(BF16)
    gg = jnp.dot(xb, wg_ref[...], preferred_element_type=F32)
    uu = jnp.dot(xb, wu_ref[...], preferred_element_type=F32)
    h = (gg * _sigmoid(gg) * uu).astype(BF16)
    y = jnp.dot(h, wd_ref[...], preferred_element_type=F32)
    o_ref[...] = _layer_norm_rows(ALPHA * x_ref[...] + y, g_ref[...], b_ref[...])


def ffn_ln(x, w_gu_bf, w_down_bf, g, b, layer=0, tm=512):
    n = x.shape[0]
    row = pl.BlockSpec((tm, D_MODEL), lambda i: (i, 0))
    vec = pl.BlockSpec((1, D_MODEL), lambda i: (0, 0))
    once = pl.Buffered(1)
    return pl.pallas_call(
        _ffn_kernel,
        grid=(n // tm,),
        in_specs=[row,
                  _layer_block(w_gu_bf, layer, (D_MODEL, D_FF), lambda i: (0, 0), pipeline_mode=once),
                  _layer_block(w_gu_bf, layer, (D_MODEL, D_FF), lambda i: (0, 1), pipeline_mode=once),
                  _layer_block(w_down_bf, layer, (D_FF, D_MODEL), lambda i: (0, 0), pipeline_mode=once),
                  vec, vec],
        out_specs=row,
        out_shape=jax.ShapeDtypeStruct((n, D_MODEL), F32),
        compiler_params=_cparams(("arbitrary",)),
        name="ffn_ln",
    )(x, w_gu_bf, w_gu_bf, w_down_bf, g, b)


def _lru_in_kernel(x_ref, w_ref, xb_ref, gate_ref):
    xb = x_ref[...].astype(BF16)
    xb_ref[...] = jnp.dot(xb, w_ref[:, :D_MODEL], preferred_element_type=F32)
    y = jnp.dot(xb, w_ref[:, D_MODEL:], preferred_element_type=F32)
    gate_ref[...] = _gelu_tanh(y)


def lru_in(x, w_in_bf, row0, nrows, layer=0, tm=512):
    base = row0 // tm
    row = pl.BlockSpec((tm, D_MODEL), lambda i: (i, 0))
    out = jax.ShapeDtypeStruct((nrows, D_MODEL), F32)
    return pl.pallas_call(
        _lru_in_kernel,
        grid=(nrows // tm,),
        in_specs=[pl.BlockSpec((tm, D_MODEL), lambda i: (base + i, 0)),
                  _layer_block(w_in_bf, layer, (D_MODEL, 2 * D_MODEL), lambda i: (0, 0))],
        out_specs=[row, row],
        out_shape=[out, out],
        compiler_params=_cparams(("arbitrary",)),
        name="lru_in",
    )(x, w_in_bf)


def _lru_rec_kernel(L, carry, *refs):
    if carry:
        (x_ref, win_ref, cw_ref, cb_ref, wr_ref, br_ref, wi_ref, bi_ref, lam_ref, wout_ref, lng_ref, lnb_ref,
         cgu_ref, cdn_ref, xo_ref, hn_ref, cn_ref, cgu_o, cdn_o, xp_ref, hs_ref, hc_ref, gate_ref, hg_ref) = refs
        R = x_ref.shape[0]
        cgu_o[...] = cgu_ref[...].astype(BF16)
        cdn_o[...] = cdn_ref[...].astype(BF16)
    else:
        (xb_ref, gate_ref, cc_ref, h0_ref, cw_ref, cb_ref, wr_ref, br_ref, wi_ref, bi_ref, lam_ref,
         hg_ref, hn_ref, cn_ref, xp_ref, hs_ref) = refs
        R = xb_ref.shape[0]
    S = R // L
    PADR = SUBLANES
    t = pl.program_id(1) if carry else 0

    if carry:
        @pl.when(t == 0)
        def _():
            xp_ref[:, 0:PADR, :] = jnp.zeros((S, PADR, D_MODEL), F32)
            hc_ref[...] = jnp.zeros(hc_ref.shape, F32)

        @pl.when(t > 0)
        def _():
            xp_ref[:, 0:PADR, :] = xp_ref[:, L:L + PADR, :]

        xin = x_ref[...].astype(BF16)
        xp_ref[:, PADR:PADR + L, :] = jnp.dot(xin, win_ref[:, :D_MODEL],
                                              preferred_element_type=F32).reshape(S, L, D_MODEL)
        gate_ref[...] = _gelu_tanh(jnp.dot(xin, win_ref[:, D_MODEL:], preferred_element_type=F32))
    else:
        xp_ref[:, 0:PADR, :] = cc_ref[...].reshape(S, PADR, D_MODEL)
        xp_ref[:, PADR:PADR + L, :] = xb_ref[...].reshape(S, L, D_MODEL)

    xc = cb_ref[...].reshape(1, 1, D_MODEL) + xp_ref[:, PADR:PADR + L, :] * cw_ref[CONV_W - 1:CONV_W, :].reshape(1, 1, D_MODEL)
    for j in range(CONV_W - 1):
        off = PADR - (CONV_W - 1) + j
        xc = xc + xp_ref[:, off:off + L, :] * cw_ref[j:j + 1, :].reshape(1, 1, D_MODEL)
    xc = xc.reshape(R, D_MODEL)
    cn_ref[...] = xp_ref[:, L + PADR - (CONV_W - 1):L + PADR, :].reshape(cn_ref.shape)

    assert L & (L - 1) == 0
    pos = jnp.bitwise_and(lax.broadcasted_iota(I32, (R, LRU_BW), 0), L - 1)
    for nb in range(LRU_BLOCKS):
        sl = slice(nb * LRU_BW, (nb + 1) * LRU_BW)
        xcb = xc[:, sl]
        xh = xcb.astype(BF16)
        r = _sigmoid(jnp.dot(xh, wr_ref[nb], preferred_element_type=F32) + br_ref[:, sl])
        ig = _sigmoid(jnp.dot(xh, wi_ref[nb], preferred_element_type=F32) + bi_ref[:, sl])
        z = -lam_ref[:, sl]
        softplus = jnp.maximum(z, 0.0) + jnp.log1p(jnp.exp(-jnp.abs(z)))
        log_a = (-RG_C) * r * softplus
        a = jnp.exp(log_a)
        u = jnp.sqrt(1.0 - a * a) * (ig * xcb)
        if carry:
            ng = R // SUBLANES
            u3 = u.reshape(ng, SUBLANES, LRU_BW)
            a3 = a.reshape(ng, SUBLANES, LRU_BW)
            pos8 = lax.broadcasted_iota(I32, (ng, SUBLANES, LRU_BW), 1)
            d = 1
            while d < SUBLANES:
                m = pos8 >= d
                u3 = jnp.where(m, a3 * pltpu.roll(u3, d, 1) + u3, u3)
                a3 = jnp.where(m, a3 * pltpu.roll(a3, d, 1), a3)
                d *= 2
            hprev = hc_ref[:, sl]
            for gi in range(ng):
                hgrp = u3[gi] + a3[gi] * hprev
                hs_ref[0, gi * SUBLANES:(gi + 1) * SUBLANES, sl] = hgrp
                hprev = hgrp[SUBLANES - 1:SUBLANES]
            hg_ref[:, sl] = hs_ref[0, :, sl] * gate_ref[:, sl]
        else:
            u = u + a * h0_ref[:, sl]
            d = 1
            while d < L:
                m = pos >= d
                u = jnp.where(m, a * pltpu.roll(u, d, 0) + u, u)
                a = jnp.where(m, a * pltpu.roll(a, d, 0), a)
                d *= 2
            hg_ref[:, sl] = u * gate_ref[:, sl]
            hs_ref[:, :, sl] = u.reshape(S, L, LRU_BW)

    hlast = hs_ref[:, L - 1:L, :]
    hn_ref[...] = hlast.reshape(hn_ref.shape)
    if carry:
        hc_ref[...] = hlast.reshape(1, D_MODEL)
        y = jnp.dot(hg_ref[...].astype(BF16), wout_ref[...], preferred_element_type=F32)
        xo_ref[...] = _layer_norm_rows(ALPHA * x_ref[...] + y, lng_ref[...], lnb_ref[...])


def _lru_weight_specs(wts, layer):
    def c2(*_):
        return (0, 0)

    def c3(*_):
        return (0, 0, 0)
    gate_shape = (LRU_BLOCKS, LRU_BW, LRU_BW)
    return [pl.BlockSpec((CONV_W, D_MODEL), c2), pl.BlockSpec((1, D_MODEL), c2),
            _layer_block(wts[2], layer, gate_shape, c3), pl.BlockSpec((1, D_MODEL), c2),
            _layer_block(wts[4], layer, gate_shape, c3), pl.BlockSpec((1, D_MODEL), c2),
            pl.BlockSpec((1, D_MODEL), c2)]


def lru_prompt_layer(x, w_in_bf, wts, w_out_bf, ln_g, ln_b, moe_gu, moe_down, layer_j, batch, seq, tt=256):
    n = x.shape[0]
    nt = seq // tt
    steps = batch * nt
    gu2 = moe_gu.reshape(moe_gu.shape[0], N_EXPERTS * D_MODEL, 2 * D_FF_EXPERT)
    dn2 = moe_down.reshape(moe_down.shape[0], N_EXPERTS * D_FF_EXPERT, D_MODEL)
    gu_rows = gu2.shape[1] // steps
    dn_rows = dn2.shape[1] // steps
    assert gu_rows * steps == gu2.shape[1] and dn_rows * steps == dn2.shape[1]
    row = pl.BlockSpec((tt, D_MODEL), lambda b, t: (b * nt + t, 0))
    vec = pl.BlockSpec((1, D_MODEL), lambda b, t: (0, 0))
    xo, hn, cn, gu_bf, dn_bf = pl.pallas_call(
        functools.partial(_lru_rec_kernel, tt, True),
        grid=(batch, nt),
        in_specs=[row, _layer_block(w_in_bf, layer_j, (D_MODEL, 2 * D_MODEL), lambda b, t: (0, 0))]
                 + _lru_weight_specs(wts, layer_j)
                 + [_layer_block(w_out_bf, layer_j, (D_MODEL, D_MODEL), lambda b, t: (0, 0)), vec, vec,
                    pl.BlockSpec((None, gu_rows, 2 * D_FF_EXPERT), lambda b, t: (layer_j, b * nt + t, 0)),
                    pl.BlockSpec((None, dn_rows, D_MODEL), lambda b, t: (layer_j, b * nt + t, 0))],
        out_specs=[row,
                   pl.BlockSpec((1, 1, D_MODEL), lambda b, t: (b, 0, 0)),
                   pl.BlockSpec((1, CONV_W - 1, D_MODEL), lambda b, t: (b, 0, 0)),
                   pl.BlockSpec((gu_rows, 2 * D_FF_EXPERT), lambda b, t: (b * nt + t, 0)),
                   pl.BlockSpec((dn_rows, D_MODEL), lambda b, t: (b * nt + t, 0))],
        out_shape=[jax.ShapeDtypeStruct((n, D_MODEL), F32),
                   jax.ShapeDtypeStruct((batch, 1, D_MODEL), F32),
                   jax.ShapeDtypeStruct((batch, CONV_W - 1, D_MODEL), F32),
                   jax.ShapeDtypeStruct(gu2.shape[1:], BF16),
                   jax.ShapeDtypeStruct(dn2.shape[1:], BF16)],
        scratch_shapes=[pltpu.VMEM((1, tt + SUBLANES, D_MODEL), F32),
                        pltpu.VMEM((1, tt, D_MODEL), F32),
                        pltpu.VMEM((1, D_MODEL), F32),
                        pltpu.VMEM((tt, D_MODEL), F32),
                        pltpu.VMEM((tt, D_MODEL), F32)],
        compiler_params=_cparams(("arbitrary", "arbitrary")),
        name="lru_prompt_layer",
    )(x, w_in_bf, *wts, w_out_bf, ln_g, ln_b, gu2, dn2)
    return (xo, hn, cn, gu_bf.reshape(N_EXPERTS, D_MODEL, 2 * D_FF_EXPERT),
            dn_bf.reshape(N_EXPERTS, D_FF_EXPERT, D_MODEL))


def lru_rec_sample(xb, gate, cc_rows, h0_rows, wts, nseq, seq, layer=0, sg=32):
    n = xb.shape[0]
    rt = sg * seq
    row = pl.BlockSpec((rt, D_MODEL), lambda i: (i, 0))
    return pl.pallas_call(
        functools.partial(_lru_rec_kernel, seq, False),
        grid=(nseq // sg,),
        in_specs=[row, row, pl.BlockSpec((sg * SUBLANES, D_MODEL), lambda i: (i, 0)), row]
                 + _lru_weight_specs(wts, layer),
        out_specs=[row,
                   pl.BlockSpec((sg, 1, D_MODEL), lambda i: (i, 0, 0)),
                   pl.BlockSpec((sg, CONV_W - 1, D_MODEL), lambda i: (i, 0, 0))],
        out_shape=[jax.ShapeDtypeStruct((n, D_MODEL), F32),
                   jax.ShapeDtypeStruct((nseq, 1, D_MODEL), F32),
                   jax.ShapeDtypeStruct((nseq, CONV_W - 1, D_MODEL), F32)],
        scratch_shapes=[pltpu.VMEM((sg, seq + SUBLANES, D_MODEL), F32),
                        pltpu.VMEM((sg, seq, D_MODEL), F32)],
        compiler_params=_cparams(("arbitrary",)),
        name="lru_rec_sample",
    )(xb, gate, cc_rows, h0_rows, *wts)


def _router_kernel(x_ref, w_ref, mi_ref, mf_ref, cnt_ref, run_ref):
    i = pl.program_id(0)

    @pl.when(i == 0)
    def _():
        run_ref[...] = jnp.zeros(run_ref.shape, F32)

    tm = x_ref.shape[0]
    x = x_ref[...]
    w = w_ref[...]
    xh = x.astype(BF16)
    xl = (x - xh.astype(F32)).astype(BF16)
    wh = w.astype(BF16)
    wl = (w - wh.astype(F32)).astype(BF16)
    hh_hl = jnp.dot(xh, jnp.concatenate([wh, wl], axis=1), preferred_element_type=F32)
    logits = hh_hl[:, :LANES] + hh_hl[:, LANES:] + jnp.dot(xl, wh, preferred_element_type=F32)
    lane_i = lax.broadcasted_iota(I32, (tm, LANES), 1)
    lane = lane_i.astype(F32)
    neg = jnp.float32(-jnp.inf)
    logits = jnp.where(lane_i < N_EXPERTS, logits, neg)
    m1 = jnp.max(logits, axis=-1, keepdims=True)
    i1 = jnp.min(jnp.where(logits == m1, lane, float(LANES)), axis=-1, keepdims=True)
    l2 = jnp.where(lane == i1, neg, logits)
    m2 = jnp.max(l2, axis=-1, keepdims=True)
    i2 = jnp.min(jnp.where(l2 == m2, lane, float(LANES)), axis=-1, keepdims=True)
    e2 = jnp.exp(m2 - m1)
    p1 = 1.0 / (1.0 + e2)
    p2 = e2 * p1

    hit1 = lane == i1
    hit2 = lane == i2
    onehot = jnp.where(hit1 | hit2, 1.0, 0.0)
    r_i = lax.broadcasted_iota(I32, (tm, tm), 0)
    c_i = lax.broadcasted_iota(I32, (tm, tm), 1)
    tri = jnp.where(c_i < r_i, 1.0, 0.0).astype(BF16)
    rank = jnp.dot(tri, onehot.astype(BF16), preferred_element_type=F32) + run_ref[0:1, :]
    r1 = jnp.sum(jnp.where(hit1, rank, 0.0), axis=-1, keepdims=True)
    r2 = jnp.sum(jnp.where(hit2, rank, 0.0), axis=-1, keepdims=True)
    total = run_ref[0:1, :] + jnp.sum(onehot, axis=0, keepdims=True)
    run_ref[...] = jnp.broadcast_to(total, run_ref.shape)
    cnt_ref[...] = jnp.broadcast_to(total, cnt_ref.shape).astype(I32)

    mi = jnp.where(lane_i == 0, i1, jnp.where(lane_i == 1, i2, 0.0))
    mi = jnp.where(lane_i == 2, r1, jnp.where(lane_i == 3, r2, mi))
    mi_ref[...] = jnp.transpose(mi)[0:SUBLANES, :].astype(I32)
    mf_ref[...] = jnp.where(lane_i == 0, p1, jnp.where(lane_i == 1, p2, 0.0))


def moe_router(x, w_router_pad, tm=512):
    n = x.shape[0]
    row = pl.BlockSpec((tm, LANES), lambda i: (i, 0))
    return pl.pallas_call(
        _router_kernel,
        grid=(n // tm,),
        in_specs=[pl.BlockSpec((tm, D_MODEL), lambda i: (i, 0)),
                  pl.BlockSpec((D_MODEL, LANES), lambda i: (0, 0))],
        out_specs=[pl.BlockSpec((SUBLANES, tm), lambda i: (0, i)), row,
                   pl.BlockSpec((SUBLANES, LANES), lambda i: (0, 0))],
        out_shape=[jax.ShapeDtypeStruct((SUBLANES, n), I32),
                   jax.ShapeDtypeStruct((n, LANES), F32),
                   jax.ShapeDtypeStruct((SUBLANES, LANES), I32)],
        scratch_shapes=[pltpu.VMEM((SUBLANES, LANES), F32)],
        compiler_params=_cparams(("arbitrary",)),
        name="moe_router",
    )(x, w_router_pad)


ZERO_ROWS = 256


def _dispatch_kernel(dest_ref, ztile_ref, x_ref, xs_hbm, zero_ref, sem):
    i = pl.program_id(0)
    tm = x_ref.shape[0]
    n_tok = pl.num_programs(0) * tm

    @pl.when(i == 0)
    def _():
        zero_ref[...] = jnp.zeros(zero_ref.shape, F32)
        copies = [pltpu.make_async_copy(
                      zero_ref,
                      xs_hbm.at[pl.ds(pl.multiple_of(ztile_ref[e] + c * ZERO_ROWS, ZERO_ROWS), ZERO_ROWS)], sem)
                  for e in range(N_EXPERTS) for c in range(MOE_TILE // ZERO_ROWS)]
        for cp in copies:
            cp.start()
        for cp in copies:
            cp.wait()

    def group(jj, c):
        r8 = pl.multiple_of(jj * SUBLANES, SUBLANES)
        grp = x_ref.at[pl.ds(r8, SUBLANES)]
        t0 = i * tm + r8
        for u in range(SUBLANES):
            src = grp.at[pl.ds(u, 1)]
            pltpu.make_async_copy(src, xs_hbm.at[pl.ds(dest_ref[t0 + u], 1)], sem).start(priority=0)
            pltpu.make_async_copy(src, xs_hbm.at[pl.ds(dest_ref[n_tok + t0 + u], 1)], sem).start(priority=1)
        return c

    lax.fori_loop(0, tm // SUBLANES, group, 0)
    for _ in range(2):
        pltpu.make_async_copy(x_ref, xs_hbm.at[pl.ds(0, tm)], sem).wait()


def moe_dispatch(dest, ztile, x, n_rows, tm=512):
    n = x.shape[0]
    return pl.pallas_call(
        _dispatch_kernel,
        grid_spec=pltpu.PrefetchScalarGridSpec(
            num_scalar_prefetch=2, grid=(n // tm,),
            in_specs=[pl.BlockSpec((tm, D_MODEL), lambda i, d, z: (i, 0))],
            out_specs=pl.BlockSpec(memory_space=pl.ANY),
            scratch_shapes=[pltpu.VMEM((ZERO_ROWS, D_MODEL), F32), pltpu.SemaphoreType.DMA(())]),
        out_shape=jax.ShapeDtypeStruct((n_rows, D_MODEL), F32),
        compiler_params=pltpu.CompilerParams(dimension_semantics=("arbitrary",),
                                             has_side_effects=True),
        name="moe_dispatch",
    )(dest, ztile, x)


def _expert_kernel(te_ref, nu_ref, xs_ref, wg_ref, wu_ref, wd_ref, ys_ref, acc_ref):
    i = pl.program_id(0)
    f = pl.program_id(1)
    used = i < nu_ref[0]

    @pl.when(used)
    def _():
        xb = xs_ref[...].astype(BF16)
        gg = jnp.dot(xb, wg_ref[...], preferred_element_type=F32)
        uu = jnp.dot(xb, wu_ref[...], preferred_element_type=F32)
        h = (gg * _sigmoid(gg) * uu).astype(BF16)
        part = jnp.dot(h, wd_ref[...], preferred_element_type=F32)

        @pl.when(f == 0)
        def _():
            acc_ref[...] = part

        @pl.when(f > 0)
        def _():
            acc_ref[...] += part

    @pl.when(f == pl.num_programs(1) - 1)
    def _():
        @pl.when(used)
        def _():
            ys_ref[...] = acc_ref[...]

        @pl.when(jnp.logical_not(used))
        def _():
            ys_ref[...] = jnp.zeros(ys_ref.shape, F32)


def moe_experts(tile_expert, n_used, xs, w_gu_bf, w_down_bf, tf=1792):
    n_rows = xs.shape[0]
    tm = MOE_TILE
    nf = D_FF_EXPERT // tf

    def xrow(i, f, te, nu):
        return (jnp.minimum(i, nu[0] - 1), 0)

    def wg(i, f, te, nu):
        return (te[i], 0, jnp.where(i < nu[0], f, nf - 1))

    def wu(i, f, te, nu):
        return (te[i], 0, nf + jnp.where(i < nu[0], f, nf - 1))

    def wd(i, f, te, nu):
        return (te[i], jnp.where(i < nu[0], f, nf - 1), 0)

    return pl.pallas_call(
        _expert_kernel,
        grid_spec=pltpu.PrefetchScalarGridSpec(
            num_scalar_prefetch=2, grid=(n_rows // tm, nf),
            in_specs=[pl.BlockSpec((tm, D_MODEL), xrow),
                      pl.BlockSpec((None, D_MODEL, tf), wg),
                      pl.BlockSpec((None, D_MODEL, tf), wu),
                      pl.BlockSpec((None, tf, D_MODEL), wd)],
            out_specs=pl.BlockSpec((tm, D_MODEL), lambda i, f, te, nu: (i, 0)),
            scratch_shapes=[pltpu.VMEM((tm, D_MODEL), F32)]),
        out_shape=jax.ShapeDtypeStruct((n_rows, D_MODEL), F32),
        compiler_params=_cparams(("arbitrary", "arbitrary")),
        name="moe_experts",
    )(tile_expert, n_used, xs, w_gu_bf, w_gu_bf, w_down_bf)


def _combine_kernel(nsplit, dest_ref, ys_hbm, mf_ref, x_ref, g_ref, b_ref, *rest):
    if nsplit is None:
        o_ref, buf_ref, sem = rest
    else:
        oa_ref, ob_ref, buf_ref, sem = rest
    i = pl.program_id(0)
    n_steps = pl.num_programs(0)
    tm = x_ref.shape[0]
    n_tok = n_steps * tm

    def gather(step, slot):
        def group(jj, c):
            r8 = pl.multiple_of(jj * SUBLANES, SUBLANES)
            g0 = buf_ref.at[slot, 0, pl.ds(r8, SUBLANES)]
            g1 = buf_ref.at[slot, 1, pl.ds(r8, SUBLANES)]
            t0 = step * tm + r8
            for u in range(SUBLANES):
                pltpu.make_async_copy(ys_hbm.at[pl.ds(dest_ref[t0 + u], 1)], g0.at[pl.ds(u, 1)],
                                      sem.at[slot]).start(priority=0)
                pltpu.make_async_copy(ys_hbm.at[pl.ds(dest_ref[n_tok + t0 + u], 1)], g1.at[pl.ds(u, 1)],
                                      sem.at[slot]).start(priority=1)
            return c

        lax.fori_loop(0, tm // SUBLANES, group, 0)

    slot = lax.rem(i, 2)

    @pl.when(i == 0)
    def _():
        gather(i, slot)

    @pl.when(i + 1 < n_steps)
    def _():
        gather(i + 1, 1 - slot)

    for s in range(2):
        pltpu.make_async_copy(ys_hbm.at[pl.ds(0, tm)], buf_ref.at[slot, s], sem.at[slot]).wait()
    mf = mf_ref[...]
    y = buf_ref[slot, 0] * mf[:, 0:1] + buf_ref[slot, 1] * mf[:, 1:2]
    res = _layer_norm_rows(ALPHA * x_ref[...] + y, g_ref[...], b_ref[...])
    if nsplit is None:
        o_ref[...] = res
    else:
        @pl.when(i < nsplit)
        def _():
            oa_ref[...] = res

        @pl.when(i >= nsplit)
        def _():
            ob_ref[...] = res


def moe_combine_ln(dest, ys, mf, x, g, b, split_rows=None, tm=512):
    n = x.shape[0]
    row = pl.BlockSpec((tm, D_MODEL), lambda i, d: (i, 0))
    vec = pl.BlockSpec((1, D_MODEL), lambda i, d: (0, 0))
    if split_rows is None:
        nsplit = None
        out_specs = row
        out_shape = jax.ShapeDtypeStruct((n, D_MODEL), F32)
    else:
        nsplit = split_rows // tm
        out_specs = [pl.BlockSpec((tm, D_MODEL), lambda i, d: (jnp.minimum(i, nsplit - 1), 0)),
                     pl.BlockSpec((tm, D_MODEL), lambda i, d: (jnp.maximum(i - nsplit, 0), 0))]
        out_shape = [jax.ShapeDtypeStruct((split_rows, D_MODEL), F32),
                     jax.ShapeDtypeStruct((n - split_rows, D_MODEL), F32)]
    return pl.pallas_call(
        functools.partial(_combine_kernel, nsplit),
        grid_spec=pltpu.PrefetchScalarGridSpec(
            num_scalar_prefetch=1, grid=(n // tm,),
            in_specs=[pl.BlockSpec(memory_space=pl.ANY),
                      pl.BlockSpec((tm, LANES), lambda i, d: (i, 0)),
                      row, vec, vec],
            out_specs=out_specs,
            scratch_shapes=[pltpu.VMEM((2, 2, tm, D_MODEL), F32), pltpu.SemaphoreType.DMA((2,))]),
        out_shape=out_shape,
        compiler_params=_cparams(("arbitrary",)),
        name="moe_combine_ln",
    )(dest, ys, mf, x, g, b)


def moe_ffn_ln(x, w_router_pad, w_gu, w_down, g, b, split_rows=None):
    n = x.shape[0]
    tm = MOE_TILE
    mi, mf, cnt = moe_router(x, w_router_pad)
    counts = cnt[0, :N_EXPERTS]
    padded = (counts + tm - 1) // tm * tm
    ends = jnp.cumsum(padded)
    starts = ends - padded
    ids = mi[0:2]
    dest = mi[2:4]
    for e in range(N_EXPERTS):
        dest = dest + jnp.where(ids == e, starts[e], 0)
    dest = dest.reshape(-1).astype(I32)
    n_tiles = (2 * n) // tm + N_EXPERTS
    tile_start = jnp.arange(n_tiles, dtype=I32) * tm
    tile_expert = jnp.minimum(jnp.sum(tile_start[:, None] >= ends[None, :], axis=1), N_EXPERTS - 1).astype(I32)
    n_used = (ends[-1] // tm).astype(I32).reshape(1)
    tile_expert = jnp.where(tile_start < ends[-1], tile_expert, tile_expert[jnp.maximum(n_used[0] - 1, 0)])
    ztile = jnp.maximum(ends - tm, 0).astype(I32)
    xs = moe_dispatch(dest, ztile, x, n_tiles * tm)
    ys = moe_experts(tile_expert, n_used, xs, w_gu, w_down)
    return moe_combine_ln(dest, ys, mf, x, g, b, split_rows)


def kernel(x_prompt, x_sample, state_hgrn, state_lru_h, state_lru_conv, ln_mix_g, ln_mix_b, ln_ffn_g, ln_ffn_b, w_hgrn_in, hgrn_lb_logits, hgrn_norm_g, w_hgrn_out, w_lru_in, lru_conv_w, lru_conv_b, w_lru_rgate, b_lru_rgate, w_lru_igate, b_lru_igate, lru_lambda, w_lru_out, w_ffn_gu, w_ffn_down, w_router, w_moe_gu, w_moe_down):
    bp, tp, _ = x_prompt.shape
    bs, ts, _ = x_sample.shape
    n_p = bp * tp
    n = n_p + bs * ts
    x = x_prompt.reshape(n_p, D_MODEL)
    x_smp, smp_row0 = x_sample.reshape(bs * ts, D_MODEL), 0

    def vec(a, i):
        return a[i].reshape(1, D_MODEL)

    w_hgrn_in_bf = w_hgrn_in.astype(BF16)
    w_hgrn_out_bf = w_hgrn_out.astype(BF16)
    w_lru_in_bf = w_lru_in.astype(BF16)
    w_lru_out_bf = w_lru_out.astype(BF16)
    w_r_bf = w_lru_rgate.astype(BF16)
    w_i_bf = w_lru_igate.astype(BF16)
    w_ffn_gu_bf = w_ffn_gu.astype(BF16)
    w_ffn_down_bf = w_ffn_down.astype(BF16)
    w_router_pad = jnp.pad(w_router, ((0, 0), (0, 0), (0, LANES - N_EXPERTS)))

    hg_p = hg_s = None
    h_p, h_s, c_p, c_s = [], [], [], []
    for layer in range(DEPTH):
        j = layer // 2
        lng, lnb = vec(ln_mix_g, layer), vec(ln_mix_b, layer)
        if layer % 2 == 0:
            ng = vec(hgrn_norm_g, j)
            xo, hg_p = hgrn_prompt_layer(x, w_hgrn_in_bf, hgrn_lb_logits, ng, w_hgrn_out_bf,
                                         lng, lnb, hg_p, j, bp, tp, n)
            q, k, lf, v, g = hgrn_in(x_smp, w_hgrn_in_bf, hgrn_lb_logits, j, smp_row0, bs * ts)
            o, hg_s = hgrn_rec_sample(q, k, lf, v, g, ng, state_hgrn, hg_s, j, bs, ts)
            x = proj_ln(o, w_hgrn_out_bf, x_smp, smp_row0, lng, lnb, xo, n_p, layer=j)
            x = ffn_ln(x, w_ffn_gu_bf, w_ffn_down_bf, vec(ln_ffn_g, layer), vec(ln_ffn_b, layer), layer=j)
            x_smp, smp_row0 = x, n_p
        else:
            wts = (lru_conv_w[j], vec(lru_conv_b, j), w_r_bf, vec(b_lru_rgate, j),
                   w_i_bf, vec(b_lru_igate, j), vec(lru_lambda, j))
            xo, hp, cp, moe_gu_bf, moe_down_bf = lru_prompt_layer(
                x, w_lru_in_bf, wts, w_lru_out_bf, lng, lnb, w_moe_gu, w_moe_down, j, bp, tp)
            xb, gate = lru_in(x_smp, w_lru_in_bf, smp_row0, bs * ts, layer=j)
            cc_rows = jnp.pad(state_lru_conv[j], ((0, 0), (SUBLANES - (CONV_W - 1), 0), (0, 0)))
            h0_rows = jnp.pad(state_lru_h[j][:, None, :], ((0, 0), (0, ts - 1), (0, 0)))
            hgate, hs, cs = lru_rec_sample(xb, gate, cc_rows.reshape(bs * SUBLANES, D_MODEL),
                                           h0_rows.reshape(bs * ts, D_MODEL), wts, bs, ts, layer=j)
            h_p.append(hp.reshape(bp, D_MODEL))
            h_s.append(hs.reshape(bs, D_MODEL))
            c_p.append(cp)
            c_s.append(cs)
            x = proj_ln(hgate, w_lru_out_bf, x_smp, smp_row0, lng, lnb, xo, n_p, layer=j)
            x = moe_ffn_ln(x, w_router_pad[j], moe_gu_bf, moe_down_bf,
                           vec(ln_ffn_g, layer), vec(ln_ffn_b, layer),
                           split_rows=n_p if layer == DEPTH - 1 else None)
            x_smp, smp_row0 = x, n_p

    y_prompt = x[0].reshape(bp, tp, D_MODEL)
    y_sample = x[1].reshape(bs, ts, D_MODEL)
    return (y_prompt, y_sample, hg_p, hg_s, jnp.stack(h_p), jnp.stack(h_s), jnp.stack(c_p), jnp.stack(c_s))
```
